```python
import jax, jax.numpy as jnp
from jax import lax
import numpy as np

D_MODEL = 4096
BATCH = 8
SEQ = 4096
DEPTH = 1

GRID_W = 64
CTX_LEN = 256
GLA_VAL_W = D_MODEL // 2
GLA_HEADS = 8
GLA_DV = GLA_VAL_W // GLA_HEADS
GLA_DK = GLA_DV // 2
GLA_KEY_W = GLA_HEADS * GLA_DK
GLA_CHUNK = 64
GLA_LOWRANK = 16
GLA_TAU = 16.0
ROPE_BASE = 10000.0
SG_WIDTH = D_MODEL - GLA_VAL_W
SG_GROUPS = 4
SG_GROUP_W = SG_WIDTH // SG_GROUPS
SG_CHUNK = 128
MIX_W = GLA_VAL_W + SG_WIDTH
D_FF = 4 * D_MODEL
N_MOD = 6
EPS = 1e-6
Q0 = 0
K0 = Q0 + GLA_KEY_W
V0 = K0 + GLA_KEY_W
R0 = V0 + GLA_VAL_W
LF0 = R0 + GLA_VAL_W
LB0 = LF0 + GLA_LOWRANK
SG0 = LB0 + GLA_LOWRANK
IN_COLS = SG0 + 2 * SG_WIDTH

kernel_name = "hybrid_gla_gmlp_prefix_dit_block"


def rmsnorm(t, g):
    tf = t.astype(jnp.float32)
    y = tf * lax.rsqrt(jnp.mean(tf * tf, axis=-1, keepdims=True) + EPS)
    return (y * g.astype(jnp.float32)).astype(t.dtype)


def layernorm(t, g, b):
    tf = t.astype(jnp.float32)
    mu = jnp.mean(tf, axis=-1, keepdims=True)
    var = jnp.mean(jnp.square(tf - mu), axis=-1, keepdims=True)
    y = (tf - mu) * lax.rsqrt(var + EPS)
    return (y * g.astype(jnp.float32) + b.astype(jnp.float32)).astype(t.dtype)


def modulate(h, shift, scale):
    return h * (1.0 + scale) + shift


def split_heads(t, d):
    return t.reshape(t.shape[:-1] + (GLA_HEADS, d))


def flip_seq(t):
    return jnp.flip(t, axis=1)


def rope_axis(t, pos):
    m = t.shape[-1] // 2
    inv_freq = ROPE_BASE ** (-jnp.arange(m, dtype=jnp.float32) / m)
    ang = pos.astype(jnp.float32)[:, None] * inv_freq[None, :]
    cos = jnp.cos(ang)[:, None, :]
    sin = jnp.sin(ang)[:, None, :]
    t1 = t[..., :m].astype(jnp.float32)
    t2 = t[..., m:].astype(jnp.float32)
    return jnp.concatenate([t1 * cos - t2 * sin, t1 * sin + t2 * cos], axis=-1).astype(t.dtype)


def rope2d(t, row_pos, col_pos):
    half = t.shape[-1] // 2
    return jnp.concatenate([rope_axis(t[..., :half], row_pos), rope_axis(t[..., half:], col_pos)], axis=-1)


def gla_qk(z):
    q = split_heads(z[..., Q0:K0], GLA_DK) * (GLA_DK ** -0.5)
    k = split_heads(z[..., K0:V0], GLA_DK)
    return q, k


def gla_log_decay(lr, w_dec, b_dec):
    a = (lr @ w_dec + b_dec).astype(jnp.float32)
    return split_heads(jax.nn.log_sigmoid(a) / GLA_TAU, GLA_DK)


def gla_chunked(q, k, v, log_a, s0):
    bsz, n, h, dk = q.shape
    dv = v.shape[-1]
    nc = n // GLA_CHUNK

    def to_chunks(t):
        return t.astype(jnp.float32).reshape(bsz, nc, GLA_CHUNK, h, t.shape[-1]).transpose(1, 0, 3, 2, 4)

    mask = jnp.tril(jnp.ones((GLA_CHUNK, GLA_CHUNK), dtype=bool))[None, None, :, :, None]

    def step(s, inp):
        qi, ki, vi, ai = inp
        b = jnp.cumsum(ai, axis=2)
        inter = jnp.einsum('bhck,bhkv->bhcv', qi * jnp.exp(b), s)
        diff = b[:, :, :, None, :] - b[:, :, None, :, :]
        decay = jnp.exp(jnp.where(mask, diff, -jnp.inf))
        att = jnp.einsum('bhik,bhjk,bhijk->bhij', qi, ki, decay)
        intra = jnp.einsum('bhij,bhjv->bhiv', att, vi)
        b_last = b[:, :, -1, :]
        s_new = jnp.exp(b_last)[..., None] * s + jnp.einsum(
            'bhck,bhcv->bhkv', ki * jnp.exp(b_last[:, :, None, :] - b), vi)
        return s_new, inter + intra

    s_fin, o = lax.scan(step, s0.astype(jnp.float32), (to_chunks(q), to_chunks(k), to_chunks(v), to_chunks(log_a)))
    o = o.transpose(1, 0, 3, 2, 4).reshape(bsz, n, h, dv)
    return o, s_fin


def gla_final_state(k, v, log_a):
    b = jnp.cumsum(log_a.astype(jnp.float32), axis=1)
    w = jnp.exp(b[:, -1:] - b)
    return jnp.einsum('bnhk,bnhv->bhkv', k.astype(jnp.float32) * w, v.astype(jnp.float32))


def gla_bidir(q, k, v, la_f, la_b, s_f0, s_b0):
    o_f, s_f = gla_chunked(q, k, v, la_f, s_f0)
    o_b, s_b = gla_chunked(flip_seq(q), flip_seq(k), flip_seq(v), flip_seq(la_b), s_b0)
    return o_f + flip_seq(o_b), s_f, s_b


def gla_readout(o, r, g):
    bsz, n = o.shape[0], o.shape[1]
    y = rmsnorm(o, g).astype(r.dtype).reshape(bsz, n, GLA_VAL_W)
    return y * jax.nn.silu(r)


def spatial_gating(zs, ln_g, ln_b, w_s, b_s):
    zs = jax.nn.gelu(zs, approximate=False)
    u, vv = zs[..., :SG_WIDTH], zs[..., SG_WIDTH:]
    vv = layernorm(vv, ln_g, ln_b)
    bsz, n = vv.shape[0], vv.shape[1]
    vv = vv.reshape(bsz, n // SG_CHUNK, SG_CHUNK, SG_GROUPS, SG_GROUP_W)
    s = jnp.einsum('gij,bnjgc->bnigc', w_s, vv) + b_s.T[:, :, None]
    return u * s.reshape(bsz, n, SG_WIDTH)


def token_mix(z, q, k, s_f0, s_b0, w_dec_f, b_dec_f, w_dec_b, b_dec_b,
              gla_norm_g, sg_ln_g, sg_ln_b, w_s, b_s, w_o):
    v = split_heads(z[..., V0:R0], GLA_DV)
    la_f = gla_log_decay(z[..., LF0:LB0], w_dec_f, b_dec_f)
    la_b = gla_log_decay(z[..., LB0:SG0], w_dec_b, b_dec_b)
    o, s_f, s_b = gla_bidir(q, k, v, la_f, la_b, s_f0, s_b0)
    y_gla = gla_readout(o, z[..., R0:LF0], gla_norm_g)
    y_sg = spatial_gating(z[..., SG0:], sg_ln_g, sg_ln_b, w_s, b_s)
    y = jnp.concatenate([y_gla, y_sg], axis=-1) @ w_o
    return y, s_f, s_b


def ctx_states(hc, w_in, w_dec_f, b_dec_f, w_dec_b, b_dec_b):
    k = split_heads(hc @ w_in[:, K0:V0], GLA_DK)
    v = split_heads(hc @ w_in[:, V0:R0], GLA_DV)
    lr = hc @ w_in[:, LF0:SG0]
    la_f = gla_log_decay(lr[..., :GLA_LOWRANK], w_dec_f, b_dec_f)
    la_b = gla_log_decay(lr[..., GLA_LOWRANK:], w_dec_b, b_dec_b)
    s_f = gla_final_state(k, v, la_f)
    s_b = gla_final_state(flip_seq(k), flip_seq(v), flip_seq(la_b))
    return s_f, s_b


def sq_relu_mlp(h, w_1, w_2):
    return jnp.square(jax.nn.relu(h @ w_1)) @ w_2


def _fwd_setup_inputs(seed: int = 0) -> dict:
    key = jax.random.key(seed)
    ks = jax.random.split(key, 24)

    def nrm(k, shape, scale):
        return jax.random.normal(k, shape, jnp.float32) * scale

    def gain(k, shape):
        return 1.0 + nrm(k, shape, 0.01)

    L = DEPTH
    return {
        "x": nrm(ks[0], (BATCH, SEQ, D_MODEL), 1.0),
        "c": nrm(ks[1], (BATCH, D_MODEL), 1.0),
        "ctx": nrm(ks[2], (BATCH, CTX_LEN, D_MODEL), 1.0),
        "c_ctx": nrm(ks[3], (D_MODEL,), 1.0),
        "w_ada": nrm(ks[4], (L, D_MODEL, N_MOD * D_MODEL), D_MODEL ** -0.5),
        "b_ada": nrm(ks[5], (L, N_MOD * D_MODEL), 0.01),
        "pre1_g": gain(ks[6], (L, D_MODEL)),
        "post1_g": gain(ks[7], (L, D_MODEL)),
        "pre2_g": gain(ks[8], (L, D_MODEL)),
        "post2_g": gain(ks[9], (L, D_MODEL)),
        "w_in": nrm(ks[10], (L, D_MODEL, IN_COLS), D_MODEL ** -0.5),
        "w_dec_f": nrm(ks[11], (L, GLA_LOWRANK, GLA_KEY_W), GLA_LOWRANK ** -0.5),
        "b_dec_f": nrm(ks[12], (L, GLA_KEY_W), 0.1),
        "w_dec_b": nrm(ks[13], (L, GLA_LOWRANK, GLA_KEY_W), GLA_LOWRANK ** -0.5),
        "b_dec_b": nrm(ks[14], (L, GLA_KEY_W), 0.1),
        "gla_norm_g": gain(ks[15], (L, GLA_HEADS, GLA_DV)),
        "sg_ln_g": gain(ks[16], (L, SG_WIDTH)),
        "sg_ln_b": nrm(ks[17], (L, SG_WIDTH), 0.01),
        "w_s": nrm(ks[18], (L, SG_GROUPS, SG_CHUNK, SG_CHUNK), SG_CHUNK ** -0.5),
        "b_s": gain(ks[19], (L, SG_GROUPS, SG_CHUNK)),
        "w_o": nrm(ks[20], (L, MIX_W, D_MODEL), MIX_W ** -0.5),
        "w_1": nrm(ks[21], (L, D_MODEL, D_FF), D_MODEL ** -0.5),
        "w_2": nrm(ks[22], (L, D_FF, D_MODEL), D_FF ** -0.5),
    }


def _fwd_reference(x, c, ctx, c_ctx, w_ada, b_ada, pre1_g, post1_g, pre2_g, post2_g, w_in,
              w_dec_f, b_dec_f, w_dec_b, b_dec_b, gla_norm_g, sg_ln_g, sg_ln_b, w_s, b_s,
              w_o, w_1, w_2):
    bsz, n = x.shape[0], x.shape[1]
    ROWS = n // GRID_W
    pos = jnp.arange(ROWS * GRID_W)
    row_pos = pos // GRID_W
    col_pos = pos % GRID_W
    zero_state = jnp.zeros((bsz, GLA_HEADS, GLA_DK, GLA_DV), jnp.float32)
    cond_x = jax.nn.silu(c)[:, None, :]
    cond_c = jax.nn.silu(c_ctx)

    for l in range(DEPTH):
        mod_x = cond_x @ w_ada[l] + b_ada[l]
        sh1, sc1, g1, sh2, sc2, g2 = jnp.split(mod_x, N_MOD, axis=-1)
        hx = modulate(rmsnorm(x, pre1_g[l]), sh1, sc1)

        if l == DEPTH - 1:
            mod_c = cond_c @ w_ada[l][:, :2 * D_MODEL] + b_ada[l][:2 * D_MODEL]
            csh1, csc1 = jnp.split(mod_c, 2, axis=-1)
            hc = modulate(rmsnorm(ctx, pre1_g[l]), csh1, csc1)
            s_f, s_b = ctx_states(hc, w_in[l], w_dec_f[l], b_dec_f[l], w_dec_b[l], b_dec_b[l])
        else:
            mod_c = cond_c @ w_ada[l] + b_ada[l]
            csh1, csc1, cg1, csh2, csc2, cg2 = jnp.split(mod_c, N_MOD, axis=-1)
            hc = modulate(rmsnorm(ctx, pre1_g[l]), csh1, csc1)
            zc = hc @ w_in[l]
            qc, kc = gla_qk(zc)
            mix_c, s_f, s_b = token_mix(zc, qc, kc, zero_state, zero_state,
                                        w_dec_f[l], b_dec_f[l], w_dec_b[l], b_dec_b[l],
                                        gla_norm_g[l], sg_ln_g[l], sg_ln_b[l], w_s[l], b_s[l], w_o[l])
            ctx = ctx + cg1 * rmsnorm(mix_c, post1_g[l])
            hc2 = modulate(rmsnorm(ctx, pre2_g[l]), csh2, csc2)
            ctx = ctx + cg2 * rmsnorm(sq_relu_mlp(hc2, w_1[l], w_2[l]), post2_g[l])

        zx = hx @ w_in[l]
        qx, kx = gla_qk(zx)
        qx = rope2d(qx, row_pos, col_pos)
        kx = rope2d(kx, row_pos, col_pos)
        mix_x, _, _ = token_mix(zx, qx, kx, s_f, s_b,
                                w_dec_f[l], b_dec_f[l], w_dec_b[l], b_dec_b[l],
                                gla_norm_g[l], sg_ln_g[l], sg_ln_b[l], w_s[l], b_s[l], w_o[l])
        x = x + g1 * rmsnorm(mix_x, post1_g[l])
        h2 = modulate(rmsnorm(x, pre2_g[l]), sh2, sc2)
        x = x + g2 * rmsnorm(sq_relu_mlp(h2, w_1[l], w_2[l]), post2_g[l])
    return x


import jax as _jax
import jax.numpy as _jnp

TWIN_FORMAT = 'train_step'
FWD_PARAMS = ['x', 'c', 'ctx', 'c_ctx', 'w_ada', 'b_ada', 'pre1_g', 'post1_g', 'pre2_g', 'post2_g', 'w_in', 'w_dec_f', 'b_dec_f', 'w_dec_b', 'b_dec_b', 'gla_norm_g', 'sg_ln_g', 'sg_ln_b', 'w_s', 'b_s', 'w_o', 'w_1', 'w_2']
TWIN_WEIGHTS = ['c_ctx', 'w_ada', 'b_ada', 'pre1_g', 'post1_g', 'pre2_g', 'post2_g', 'w_in', 'w_dec_f', 'b_dec_f', 'w_dec_b', 'b_dec_b', 'gla_norm_g', 'sg_ln_g', 'sg_ln_b', 'w_s', 'b_s', 'w_o', 'w_1', 'w_2']
TWIN_DIFF_INPUT = 'x'
TWIN_INPUTS = ['x', 'c', 'ctx', 'c_ctx', 'w_ada', 'b_ada', 'pre1_g', 'post1_g', 'pre2_g', 'post2_g', 'w_in', 'w_dec_f', 'b_dec_f', 'w_dec_b', 'b_dec_b', 'gla_norm_g', 'sg_ln_g', 'sg_ln_b', 'w_s', 'b_s', 'w_o', 'w_1', 'w_2', 'loss_target', 'm_c_ctx', 'm_w_ada', 'm_b_ada', 'm_pre1_g', 'm_post1_g', 'm_pre2_g', 'm_post2_g', 'm_w_in', 'm_w_dec_f', 'm_b_dec_f', 'm_w_dec_b', 'm_b_dec_b', 'm_gla_norm_g', 'm_sg_ln_g', 'm_sg_ln_b', 'm_w_s', 'm_b_s', 'm_w_o', 'm_w_1', 'm_w_2', 'v_c_ctx', 'v_w_ada', 'v_b_ada', 'v_pre1_g', 'v_post1_g', 'v_pre2_g', 'v_post2_g', 'v_w_in', 'v_w_dec_f', 'v_b_dec_f', 'v_w_dec_b', 'v_b_dec_b', 'v_gla_norm_g', 'v_sg_ln_g', 'v_sg_ln_b', 'v_w_s', 'v_b_s', 'v_w_o', 'v_w_1', 'v_w_2']
TWIN_OUTPUTS = ['loss', 'grad_x', 'grad_c_ctx', 'grad_w_ada', 'grad_b_ada', 'grad_pre1_g', 'grad_post1_g', 'grad_pre2_g', 'grad_post2_g', 'grad_w_in', 'grad_w_dec_f', 'grad_b_dec_f', 'grad_w_dec_b', 'grad_b_dec_b', 'grad_gla_norm_g', 'grad_sg_ln_g', 'grad_sg_ln_b', 'grad_w_s', 'grad_b_s', 'grad_w_o', 'grad_w_1', 'grad_w_2', 'delta_c_ctx', 'delta_w_ada', 'delta_b_ada', 'delta_pre1_g', 'delta_post1_g', 'delta_pre2_g', 'delta_post2_g', 'delta_w_in', 'delta_w_dec_f', 'delta_b_dec_f', 'delta_w_dec_b', 'delta_b_dec_b', 'delta_gla_norm_g', 'delta_sg_ln_g', 'delta_sg_ln_b', 'delta_w_s', 'delta_b_s', 'delta_w_o', 'delta_w_1', 'delta_w_2', 'new_m_c_ctx', 'new_m_w_ada', 'new_m_b_ada', 'new_m_pre1_g', 'new_m_post1_g', 'new_m_pre2_g', 'new_m_post2_g', 'new_m_w_in', 'new_m_w_dec_f', 'new_m_b_dec_f', 'new_m_w_dec_b', 'new_m_b_dec_b', 'new_m_gla_norm_g', 'new_m_sg_ln_g', 'new_m_sg_ln_b', 'new_m_w_s', 'new_m_b_s', 'new_m_w_o', 'new_m_w_1', 'new_m_w_2', 'new_v_c_ctx', 'new_v_w_ada', 'new_v_b_ada', 'new_v_pre1_g', 'new_v_post1_g', 'new_v_pre2_g', 'new_v_post2_g', 'new_v_w_in', 'new_v_w_dec_f', 'new_v_b_dec_f', 'new_v_w_dec_b', 'new_v_b_dec_b', 'new_v_gla_norm_g', 'new_v_sg_ln_g', 'new_v_sg_ln_b', 'new_v_w_s', 'new_v_b_s', 'new_v_w_o', 'new_v_w_1', 'new_v_w_2']
TWIN_LEAF_KINDS = {'loss': 'loss', 'grad_x': 'grad_x', 'grad_c_ctx': 'grad_w', 'grad_w_ada': 'grad_w', 'grad_b_ada': 'grad_w', 'grad_pre1_g': 'grad_w', 'grad_post1_g': 'grad_w', 'grad_pre2_g': 'grad_w', 'grad_post2_g': 'grad_w', 'grad_w_in': 'grad_w', 'grad_w_dec_f': 'grad_w', 'grad_b_dec_f': 'grad_w', 'grad_w_dec_b': 'grad_w', 'grad_b_dec_b': 'grad_w', 'grad_gla_norm_g': 'grad_w', 'grad_sg_ln_g': 'grad_w', 'grad_sg_ln_b': 'grad_w', 'grad_w_s': 'grad_w', 'grad_b_s': 'grad_w', 'grad_w_o': 'grad_w', 'grad_w_1': 'grad_w', 'grad_w_2': 'grad_w', 'delta_c_ctx': 'delta_w', 'delta_w_ada': 'delta_w', 'delta_b_ada': 'delta_w', 'delta_pre1_g': 'delta_w', 'delta_post1_g': 'delta_w', 'delta_pre2_g': 'delta_w', 'delta_post2_g': 'delta_w', 'delta_w_in': 'delta_w', 'delta_w_dec_f': 'delta_w', 'delta_b_dec_f': 'delta_w', 'delta_w_dec_b': 'delta_w', 'delta_b_dec_b': 'delta_w', 'delta_gla_norm_g': 'delta_w', 'delta_sg_ln_g': 'delta_w', 'delta_sg_ln_b': 'delta_w', 'delta_w_s': 'delta_w', 'delta_b_s': 'delta_w', 'delta_w_o': 'delta_w', 'delta_w_1': 'delta_w', 'delta_w_2': 'delta_w', 'new_m_c_ctx': 'new_m', 'new_m_w_ada': 'new_m', 'new_m_b_ada': 'new_m', 'new_m_pre1_g': 'new_m', 'new_m_post1_g': 'new_m', 'new_m_pre2_g': 'new_m', 'new_m_post2_g': 'new_m', 'new_m_w_in': 'new_m', 'new_m_w_dec_f': 'new_m', 'new_m_b_dec_f': 'new_m', 'new_m_w_dec_b': 'new_m', 'new_m_b_dec_b': 'new_m', 'new_m_gla_norm_g': 'new_m', 'new_m_sg_ln_g': 'new_m', 'new_m_sg_ln_b': 'new_m', 'new_m_w_s': 'new_m', 'new_m_b_s': 'new_m', 'new_m_w_o': 'new_m', 'new_m_w_1': 'new_m', 'new_m_w_2': 'new_m', 'new_v_c_ctx': 'new_v', 'new_v_w_ada': 'new_v', 'new_v_b_ada': 'new_v', 'new_v_pre1_g': 'new_v', 'new_v_post1_g': 'new_v', 'new_v_pre2_g': 'new_v', 'new_v_post2_g': 'new_v', 'new_v_w_in': 'new_v', 'new_v_w_dec_f': 'new_v', 'new_v_b_dec_f': 'new_v', 'new_v_w_dec_b': 'new_v', 'new_v_b_dec_b': 'new_v', 'new_v_gla_norm_g': 'new_v', 'new_v_sg_ln_g': 'new_v', 'new_v_sg_ln_b': 'new_v', 'new_v_w_s': 'new_v', 'new_v_b_s': 'new_v', 'new_v_w_o': 'new_v', 'new_v_w_1': 'new_v', 'new_v_w_2': 'new_v'}


def _forward(args):
    return _fwd_reference(*[args[k] for k in FWD_PARAMS])


def _output_shape():
    out = _jax.eval_shape(lambda: _forward(_fwd_setup_inputs(0)))
    return out.shape, out.dtype

N_MICROBATCH = 1
ADAM_LR = 0.001
ADAM_B1 = 0.9
ADAM_B2 = 0.999
ADAM_EPS = 1e-08
ADAM_WD = 0.01
ADAM_STEP = 10
PER_EXAMPLE_BATCH_AXIS = {'x': 0, 'c': 0, 'ctx': 0, 'loss_target': 0}
SHARED_INPUTS = []
_WEIGHT_DTYPES = {'c_ctx': _jnp.float32, 'w_ada': _jnp.float32, 'b_ada': _jnp.float32, 'pre1_g': _jnp.float32, 'post1_g': _jnp.float32, 'pre2_g': _jnp.float32, 'post2_g': _jnp.float32, 'w_in': _jnp.float32, 'w_dec_f': _jnp.float32, 'b_dec_f': _jnp.float32, 'w_dec_b': _jnp.float32, 'b_dec_b': _jnp.float32, 'gla_norm_g': _jnp.float32, 'sg_ln_g': _jnp.float32, 'sg_ln_b': _jnp.float32, 'w_s': _jnp.float32, 'b_s': _jnp.float32, 'w_o': _jnp.float32, 'w_1': _jnp.float32, 'w_2': _jnp.float32}
MOMENT_SCALE = {'c_ctx': 1.089548e-02, 'w_ada': 9.069433e-01, 'b_ada': 1.714213e+00, 'pre1_g': 7.875813e-02, 'post1_g': 3.607700e+00, 'pre2_g': 1.204889e-01, 'post2_g': 3.727060e+00, 'w_in': 1.483801e-01, 'w_dec_f': 1.746300e-02, 'b_dec_f': 4.189421e-02, 'w_dec_b': 2.899554e-02, 'b_dec_b': 4.846012e-02, 'gla_norm_g': 1.125814e-01, 'sg_ln_g': 5.753310e-02, 'sg_ln_b': 7.275942e-02, 'w_s': 9.337228e-02, 'b_s': 9.579037e-02, 'w_o': 3.524086e-01, 'w_1': 1.969250e-01, 'w_2': 6.110684e-01}


def _to_microbatches(a, axis):
    t = _jnp.moveaxis(a, axis, 0)
    t = t.reshape((N_MICROBATCH, t.shape[0] // N_MICROBATCH) + t.shape[1:])
    return _jnp.moveaxis(t, 1, axis + 1)


def setup_inputs(seed: int = 0) -> dict:
    inp = _fwd_setup_inputs(seed)
    key = _jax.random.fold_in(_jax.random.key(seed), 7919)
    shape, _ = _output_shape()
    out = dict(inp)
    out["loss_target"] = _jax.random.normal(_jax.random.fold_in(key, 0), shape, _jnp.float32)
    for i, name in enumerate(TWIN_WEIGHTS):
        w = inp[name].astype(_jnp.float32)
        if MOMENT_SCALE is None:
            s = _jnp.sqrt(_jnp.mean(_jnp.square(w)) + 1e-30)
        else:
            s = MOMENT_SCALE[name]
        km, kv = _jax.random.split(_jax.random.fold_in(key, i + 1))
        out[name] = w
        out["m_" + name] = s * _jax.random.normal(km, w.shape, _jnp.float32)
        out["v_" + name] = (s * s) * _jax.random.uniform(kv, w.shape, _jnp.float32, 0.5, 1.5)
    if N_MICROBATCH > 1:
        for name, axis in PER_EXAMPLE_BATCH_AXIS.items():
            out[name] = _to_microbatches(out[name], axis)
    return {'x': out['x'], 'c': out['c'], 'ctx': out['ctx'], 'c_ctx': out['c_ctx'], 'w_ada': out['w_ada'], 'b_ada': out['b_ada'], 'pre1_g': out['pre1_g'], 'post1_g': out['post1_g'], 'pre2_g': out['pre2_g'], 'post2_g': out['post2_g'], 'w_in': out['w_in'], 'w_dec_f': out['w_dec_f'], 'b_dec_f': out['b_dec_f'], 'w_dec_b': out['w_dec_b'], 'b_dec_b': out['b_dec_b'], 'gla_norm_g': out['gla_norm_g'], 'sg_ln_g': out['sg_ln_g'], 'sg_ln_b': out['sg_ln_b'], 'w_s': out['w_s'], 'b_s': out['b_s'], 'w_o': out['w_o'], 'w_1': out['w_1'], 'w_2': out['w_2'], 'loss_target': out['loss_target'], 'm_c_ctx': out['m_c_ctx'], 'm_w_ada': out['m_w_ada'], 'm_b_ada': out['m_b_ada'], 'm_pre1_g': out['m_pre1_g'], 'm_post1_g': out['m_post1_g'], 'm_pre2_g': out['m_pre2_g'], 'm_post2_g': out['m_post2_g'], 'm_w_in': out['m_w_in'], 'm_w_dec_f': out['m_w_dec_f'], 'm_b_dec_f': out['m_b_dec_f'], 'm_w_dec_b': out['m_w_dec_b'], 'm_b_dec_b': out['m_b_dec_b'], 'm_gla_norm_g': out['m_gla_norm_g'], 'm_sg_ln_g': out['m_sg_ln_g'], 'm_sg_ln_b': out['m_sg_ln_b'], 'm_w_s': out['m_w_s'], 'm_b_s': out['m_b_s'], 'm_w_o': out['m_w_o'], 'm_w_1': out['m_w_1'], 'm_w_2': out['m_w_2'], 'v_c_ctx': out['v_c_ctx'], 'v_w_ada': out['v_w_ada'], 'v_b_ada': out['v_b_ada'], 'v_pre1_g': out['v_pre1_g'], 'v_post1_g': out['v_post1_g'], 'v_pre2_g': out['v_pre2_g'], 'v_post2_g': out['v_post2_g'], 'v_w_in': out['v_w_in'], 'v_w_dec_f': out['v_w_dec_f'], 'v_b_dec_f': out['v_b_dec_f'], 'v_w_dec_b': out['v_w_dec_b'], 'v_b_dec_b': out['v_b_dec_b'], 'v_gla_norm_g': out['v_gla_norm_g'], 'v_sg_ln_g': out['v_sg_ln_g'], 'v_sg_ln_b': out['v_sg_ln_b'], 'v_w_s': out['v_w_s'], 'v_b_s': out['v_b_s'], 'v_w_o': out['v_w_o'], 'v_w_1': out['v_w_1'], 'v_w_2': out['v_w_2']}


def _loss(weights, diff, rest, loss_target):
    with _jax.named_scope("forward"):
        args = {**rest, TWIN_DIFF_INPUT: diff, **{k: w.astype(_WEIGHT_DTYPES[k]) for k, w in weights.items()}}
        y = _forward(args)
    with _jax.named_scope("loss_head"):
        err = _jnp.square(y.astype(_jnp.float32) - loss_target)
        return 0.5 * _jnp.sum(_jnp.mean(err, axis=-1)) if err.ndim else 0.5 * err


def _adamw(w, g, m, v):
    m = ADAM_B1 * m + (1.0 - ADAM_B1) * g
    v = ADAM_B2 * v + (1.0 - ADAM_B2) * _jnp.square(g)
    m_hat = m / (1.0 - ADAM_B1 ** ADAM_STEP)
    v_hat = v / (1.0 - ADAM_B2 ** ADAM_STEP)
    delta = -ADAM_LR * (m_hat / (_jnp.sqrt(v_hat) + ADAM_EPS) + ADAM_WD * w)
    return delta, m, v


def reference(x, c, ctx, c_ctx, w_ada, b_ada, pre1_g, post1_g, pre2_g, post2_g, w_in, w_dec_f, b_dec_f, w_dec_b, b_dec_b, gla_norm_g, sg_ln_g, sg_ln_b, w_s, b_s, w_o, w_1, w_2, loss_target, m_c_ctx, m_w_ada, m_b_ada, m_pre1_g, m_post1_g, m_pre2_g, m_post2_g, m_w_in, m_w_dec_f, m_b_dec_f, m_w_dec_b, m_b_dec_b, m_gla_norm_g, m_sg_ln_g, m_sg_ln_b, m_w_s, m_b_s, m_w_o, m_w_1, m_w_2, v_c_ctx, v_w_ada, v_b_ada, v_pre1_g, v_post1_g, v_pre2_g, v_post2_g, v_w_in, v_w_dec_f, v_b_dec_f, v_w_dec_b, v_b_dec_b, v_gla_norm_g, v_sg_ln_g, v_sg_ln_b, v_w_s, v_b_s, v_w_o, v_w_1, v_w_2):
    given = dict(x=x, c=c, ctx=ctx, c_ctx=c_ctx, w_ada=w_ada, b_ada=b_ada, pre1_g=pre1_g, post1_g=post1_g, pre2_g=pre2_g, post2_g=post2_g, w_in=w_in, w_dec_f=w_dec_f, b_dec_f=b_dec_f, w_dec_b=w_dec_b, b_dec_b=b_dec_b, gla_norm_g=gla_norm_g, sg_ln_g=sg_ln_g, sg_ln_b=sg_ln_b, w_s=w_s, b_s=b_s, w_o=w_o, w_1=w_1, w_2=w_2, loss_target=loss_target, m_c_ctx=m_c_ctx, m_w_ada=m_w_ada, m_b_ada=m_b_ada, m_pre1_g=m_pre1_g, m_post1_g=m_post1_g, m_pre2_g=m_pre2_g, m_post2_g=m_post2_g, m_w_in=m_w_in, m_w_dec_f=m_w_dec_f, m_b_dec_f=m_b_dec_f, m_w_dec_b=m_w_dec_b, m_b_dec_b=m_b_dec_b, m_gla_norm_g=m_gla_norm_g, m_sg_ln_g=m_sg_ln_g, m_sg_ln_b=m_sg_ln_b, m_w_s=m_w_s, m_b_s=m_b_s, m_w_o=m_w_o, m_w_1=m_w_1, m_w_2=m_w_2, v_c_ctx=v_c_ctx, v_w_ada=v_w_ada, v_b_ada=v_b_ada, v_pre1_g=v_pre1_g, v_post1_g=v_post1_g, v_pre2_g=v_pre2_g, v_post2_g=v_post2_g, v_w_in=v_w_in, v_w_dec_f=v_w_dec_f, v_b_dec_f=v_b_dec_f, v_w_dec_b=v_w_dec_b, v_b_dec_b=v_b_dec_b, v_gla_norm_g=v_gla_norm_g, v_sg_ln_g=v_sg_ln_g, v_sg_ln_b=v_sg_ln_b, v_w_s=v_w_s, v_b_s=v_b_s, v_w_o=v_w_o, v_w_1=v_w_1, v_w_2=v_w_2)
    weights = {n: given[n] for n in TWIN_WEIGHTS}
    shared = {n: given[n] for n in SHARED_INPUTS}
    per_example = {n: given[n] for n in ['x', 'c', 'ctx']}
    grad_fn = _jax.value_and_grad(_loss, argnums=(0, 1))

    def one_microbatch(ex, loss_target):
        ex = dict(ex)
        diff = ex.pop(TWIN_DIFF_INPUT)
        return grad_fn(weights, diff, {**shared, **ex}, loss_target)

    if N_MICROBATCH == 1:
        loss, (grad_w, grad_x) = one_microbatch(per_example, given["loss_target"])
    else:
        def body(carry, xs):
            loss_sum, grad_sum = carry
            l_k, (gw_k, gx_k) = one_microbatch(xs[0], xs[1])
            with _jax.named_scope("update"):
                return (loss_sum + l_k, _jax.tree.map(_jnp.add, grad_sum, gw_k)), gx_k

        init = (_jnp.zeros((), _jnp.float32), _jax.tree.map(_jnp.zeros_like, weights))
        (loss, grad_w), grad_x = _jax.lax.scan(body, init, (per_example, given["loss_target"]))
    with _jax.named_scope("update"):
        delta_w, new_m, new_v = {}, {}, {}
        for n in TWIN_WEIGHTS:
            delta_w[n], new_m[n], new_v[n] = _adamw(weights[n], grad_w[n], given["m_" + n], given["v_" + n])
    return (loss, grad_x, *[grad_w[n] for n in TWIN_WEIGHTS], *[delta_w[n] for n in TWIN_WEIGHTS],
            *[new_m[n] for n in TWIN_WEIGHTS], *[new_v[n] for n in TWIN_WEIGHTS])
```

```python
import functools
import math

import jax
import jax.numpy as jnp
from jax import lax
from jax.experimental import pallas as pl
from jax.experimental.pallas import tpu as pltpu

F32 = jnp.float32
BF16 = jnp.bfloat16
MESH = pl.DeviceIdType.MESH
ANY = pl.BlockSpec(memory_space=pl.ANY)

GLA_HEADS = 8
GLA_CHUNK = 64
GLA_LOWRANK = 16
GLA_TAU = 16.0
ROPE_BASE = 10000.0
GRID_W = 64
SG_GROUPS = 4
SG_CHUNK = 128
N_MOD = 6
EPS = 1e-6
ADAM_LR = 0.001
ADAM_B1 = 0.9
ADAM_B2 = 0.999
ADAM_EPS = 1e-08
ADAM_WD = 0.01
ADAM_STEP = 10

LANES = 128
VMEM_LIMIT = 56 << 20
N_DEV = 8
N_CHIP = 4


def _params(sem=None):
    return pltpu.CompilerParams(dimension_semantics=sem, vmem_limit_bytes=VMEM_LIMIT)


def _pick(dim, target, unit=LANES):
    best = None
    for t in range(unit, min(dim, target) + 1, unit):
        if dim % t == 0:
            best = t
    return dim if best is None else best


def _dg(a, b, ca, cb, precision=None):
    return lax.dot_general(a, b, (((ca,), (cb,)), ((), ())), preferred_element_type=F32,
                           precision=precision)


def _place():
    return lax.axis_index("x"), lax.axis_index("y"), lax.axis_index("c")


def _allgather_small(v, name):
    m_per, n = v.shape

    def body(x_ref, out_ref, send_sems, recv_sems, local_sem):
        x, y, c = _place()
        me, sibling = (x, y, c), (x, y, 1 - c)
        chips = [(1 - x, y), (x, 1 - y), (1 - x, 1 - y)]

        def rows(px, py, pc):
            return out_ref.at[pl.ds((4 * px + 2 * py + pc) * m_per, m_per), :]

        def copy(k, block, to, src=None):
            return pltpu.make_async_remote_copy(
                src_ref=rows(*block) if src is None else src, dst_ref=rows(*block),
                send_sem=send_sems.at[k], recv_sem=recv_sems.at[k],
                device_id=to, device_id_type=MESH)

        mine = pltpu.make_async_copy(x_ref, rows(*me), local_sem)
        mine.start()
        first = [copy(0, me, sibling, src=x_ref)]
        first += [copy(1 + j, me, (*chip, c), src=x_ref) for j, chip in enumerate(chips)]
        for cp in first:
            cp.start()
        passed = [copy(4 + j, (*chip, c), sibling) for j, chip in enumerate(chips)]
        for j, chip in enumerate(chips):
            copy(1 + j, (*chip, c), me).wait_recv()
            passed[j].start()
        copy(0, sibling, me).wait_recv()
        for j, chip in enumerate(chips):
            copy(4 + j, (*chip, 1 - c), me).wait_recv()
        for cp in first + passed:
            cp.wait_send()
        mine.wait()

    return pl.pallas_call(
        body, name=name,
        out_shape=jax.ShapeDtypeStruct((N_DEV * m_per, n), v.dtype),
        in_specs=[pl.BlockSpec(memory_space=pltpu.VMEM)],
        out_specs=pl.BlockSpec(memory_space=pltpu.VMEM),
        scratch_shapes=[pltpu.SemaphoreType.DMA((7,)), pltpu.SemaphoreType.DMA((7,)),
                        pltpu.SemaphoreType.DMA],
        compiler_params=pltpu.CompilerParams(vmem_limit_bytes=VMEM_LIMIT),
    )(v)


def _cast_blocks(w, s_me, name):
    _, r, cols = w.shape
    tr = _row_tile(r, cols, 4)

    def body(s_ref, w_ref, o_ref):
        o_ref[...] = w_ref[...].astype(BF16)

    return pl.pallas_call(
        body, name=name,
        out_shape=jax.ShapeDtypeStruct((N_DEV, r, cols), BF16),
        grid_spec=pltpu.PrefetchScalarGridSpec(
            num_scalar_prefetch=1, grid=(2, r // tr),
            in_specs=[pl.BlockSpec((None, tr, cols), lambda h, i, s: (h, i, 0))],
            out_specs=pl.BlockSpec((None, tr, cols), lambda h, i, s: (2 * s[0] + h, i, 0))),
        compiler_params=_params(("arbitrary", "arbitrary")),
    )(s_me, w)


def _gather_big(ws, name):
    nw = len(ws)

    def body(*refs):
        outs = refs[nw:2 * nw]
        send_sems, recv_sems = refs[2 * nw:]
        x, y, c = _place()
        me, sibling = (x, y, c), (x, y, 1 - c)
        chips = [(1 - x, y), (x, 1 - y), (1 - x, 1 - y)]

        def blk(px, py, pc):
            return 4 * px + 2 * py + pc

        def copy(w, k, block, to):
            return pltpu.make_async_remote_copy(
                src_ref=outs[w].at[block], dst_ref=outs[w].at[block],
                send_sem=send_sems.at[6 * w + k], recv_sem=recv_sems.at[6 * w + k],
                device_id=to, device_id_type=MESH)

        first = []
        for w in range(nw):
            for j, chip in enumerate(chips):
                cp = copy(w, j, blk(x, y, c), (*chip, c))
                cp.start()
                first.append(cp)
        passed = []
        for w in range(nw):
            for j, chip in enumerate(chips):
                copy(w, j, blk(*chip, c), me).wait_recv()
                cp = copy(w, 3 + j, blk(*chip, c), sibling)
                cp.start()
                passed.append(cp)
        for w in range(nw):
            for j, chip in enumerate(chips):
                copy(w, 3 + j, blk(*chip, 1 - c), me).wait_recv()
        for cp in first + passed:
            cp.wait_send()

    return pl.pallas_call(
        body, name=name,
        out_shape=[jax.ShapeDtypeStruct(w.shape, w.dtype) for w in ws],
        in_specs=[ANY] * nw, out_specs=[ANY] * nw,
        input_output_aliases={w: w for w in range(nw)},
        scratch_shapes=[pltpu.SemaphoreType.DMA((6 * nw,)), pltpu.SemaphoreType.DMA((6 * nw,))],
    )(*ws)


def _rs_sibling(gs, name):
    nw = len(gs)

    def body(*refs):
        ins, outs = refs[:nw], refs[nw:2 * nw]
        send_sems, recv_sems = refs[2 * nw:]
        x, y, c = _place()
        cps = []
        for w in range(nw):
            for s in range(N_CHIP):
                cp = pltpu.make_async_remote_copy(
                    src_ref=ins[w].at[2 * s + (1 - c)], dst_ref=outs[w].at[s],
                    send_sem=send_sems.at[N_CHIP * w + s], recv_sem=recv_sems.at[N_CHIP * w + s],
                    device_id=(x, y, 1 - c), device_id_type=MESH)
                cp.start()
                cps.append(cp)
        for cp in cps:
            cp.wait()

    return pl.pallas_call(
        body, name=name,
        out_shape=[jax.ShapeDtypeStruct((N_CHIP,) + g.shape[1:], g.dtype) for g in gs],
        in_specs=[ANY] * nw, out_specs=[ANY] * nw,
        scratch_shapes=[pltpu.SemaphoreType.DMA((N_CHIP * nw,)), pltpu.SemaphoreType.DMA((N_CHIP * nw,))],
    )(*gs)


def _rs_chips(ps, name):
    nw = len(ps)

    def body(*refs):
        ins, outs = refs[:nw], refs[nw:2 * nw]
        send_sems, recv_sems = refs[2 * nw:]
        x, y, c = _place()
        chips = [(1 - x, y), (x, 1 - y), (1 - x, 1 - y)]
        cps = []
        for w in range(nw):
            for j, chip in enumerate(chips):
                cp = pltpu.make_async_remote_copy(
                    src_ref=ins[w].at[2 * chip[0] + chip[1]], dst_ref=outs[w].at[j],
                    send_sem=send_sems.at[3 * w + j], recv_sem=recv_sems.at[3 * w + j],
                    device_id=(*chip, c), device_id_type=MESH)
                cp.start()
                cps.append(cp)
        for cp in cps:
            cp.wait()

    return pl.pallas_call(
        body, name=name,
        out_shape=[jax.ShapeDtypeStruct((3,) + p.shape[1:], p.dtype) for p in ps],
        in_specs=[ANY] * nw, out_specs=[ANY] * nw,
        scratch_shapes=[pltpu.SemaphoreType.DMA((3 * nw,)), pltpu.SemaphoreType.DMA((3 * nw,))],
    )(*ps)


def _rs_final(fs, name):
    nw = len(fs)

    def body(*refs):
        outs = refs[nw:2 * nw]
        send_sems, recv_sems = refs[2 * nw:]
        x, y, c = _place()
        cps = []
        for w in range(nw):
            cp = pltpu.make_async_remote_copy(
                src_ref=outs[w].at[c], dst_ref=outs[w].at[c],
                send_sem=send_sems.at[w], recv_sem=recv_sems.at[w],
                device_id=(x, y, 1 - c), device_id_type=MESH)
            cp.start()
            cps.append(cp)
        for cp in cps:
            cp.wait()

    return pl.pallas_call(
        body, name=name,
        out_shape=[jax.ShapeDtypeStruct(f.shape, f.dtype) for f in fs],
        in_specs=[ANY] * nw, out_specs=[ANY] * nw,
        input_output_aliases={w: w for w in range(nw)},
        scratch_shapes=[pltpu.SemaphoreType.DMA((nw,)), pltpu.SemaphoreType.DMA((nw,))],
    )(*fs)


def _row_tile(r, cols, itemsize):
    t = r
    while t * cols * itemsize > (2 << 20) and t % 16 == 0:
        t //= 2
    return t


def _sum_sibling(g, r1, c_me, name):
    _, r, cols = g.shape
    tr = _row_tile(r, cols, 4)

    def body(c_ref, g_ref, r_ref, o_ref):
        o_ref[...] = (g_ref[...].astype(F32) + r_ref[...].astype(F32)).astype(o_ref.dtype)

    return pl.pallas_call(
        body, name=name,
        out_shape=jax.ShapeDtypeStruct((N_CHIP, r, cols), g.dtype),
        grid_spec=pltpu.PrefetchScalarGridSpec(
            num_scalar_prefetch=1, grid=(N_CHIP, r // tr),
            in_specs=[pl.BlockSpec((None, tr, cols), lambda s, i, c: (2 * s + c[0], i, 0)),
                      pl.BlockSpec((None, tr, cols), lambda s, i, c: (s, i, 0))],
            out_specs=pl.BlockSpec((None, tr, cols), lambda s, i, c: (s, i, 0))),
        compiler_params=_params(("arbitrary", "arbitrary")),
    )(c_me, g, r1)


def _sum_chips(p, r2, sc_me, name):
    _, r, cols = p.shape
    tr = _row_tile(r, cols, 4)

    def body(s_ref, p_ref, a_ref, b_ref, c_ref, o_ref):
        o_ref[...] = ((p_ref[...].astype(F32) + a_ref[...].astype(F32)) + b_ref[...].astype(F32)) + c_ref[...].astype(F32)

    return pl.pallas_call(
        body, name=name,
        out_shape=jax.ShapeDtypeStruct((2, r, cols), F32),
        grid_spec=pltpu.PrefetchScalarGridSpec(
            num_scalar_prefetch=1, grid=(r // tr,),
            in_specs=[pl.BlockSpec((None, tr, cols), lambda i, s: (s[0], i, 0)),
                      pl.BlockSpec((None, tr, cols), lambda i, s: (0, i, 0)),
                      pl.BlockSpec((None, tr, cols), lambda i, s: (1, i, 0)),
                      pl.BlockSpec((None, tr, cols), lambda i, s: (2, i, 0))],
            out_specs=pl.BlockSpec((None, tr, cols), lambda i, s: (s[1], i, 0))),
        compiler_params=_params(("arbitrary",)),
    )(sc_me, p, r2, r2, r2)


def _shard_columns(g_al, shard_cols, bound, off_lo, off_hi, rh, name):
    d, acols = g_al.shape
    wp = -(-shard_cols // LANES) * LANES
    tr = min(LANES, rh)
    nt = acols // LANES

    def body(x_ref, o_ref):
        s = pl.program_id(0)
        lane = lax.broadcasted_iota(jnp.int32, (tr, LANES), 1)

        def tile(q):
            q = max(0, min(nt - 1, q))
            return x_ref[:, q * LANES:(q + 1) * LANES].astype(F32)

        def read(start):
            q, sh = divmod(start, LANES)
            if sh == 0:
                return tile(q)
            return jnp.where(lane < LANES - sh, pltpu.roll(tile(q), LANES - sh, 1), pltpu.roll(tile(q + 1), LANES - sh, 1))

        for k in range(N_CHIP):
            @pl.when(s == k)
            def _(k=k):
                for t in range(wp // LANES):
                    n0 = k * shard_cols + t * LANES
                    if n0 + LANES <= bound:
                        v = read(n0 + off_lo)
                    elif n0 >= bound:
                        v = read(n0 + off_hi)
                    else:
                        v = jnp.where(lane < bound - n0, read(n0 + off_lo), read(n0 + off_hi))
                    o_ref[:, t * LANES:(t + 1) * LANES] = v.astype(o_ref.dtype)

    return pl.pallas_call(
        body, name=name, grid=(N_CHIP, d // tr),
        out_shape=jax.ShapeDtypeStruct((N_DEV, rh, wp), BF16),
        in_specs=[pl.BlockSpec((tr, acols), lambda s, i: (i, 0))],
        out_specs=pl.BlockSpec((None, tr, wp), lambda s, i: (2 * s + (i * tr) // rh, ((i * tr) % rh) // tr, 0)),
        compiler_params=_params(("arbitrary", "arbitrary")),
    )(g_al)


def _mm(a, b, M, N, K, *, name, ta=False, tb=False, out_dtypes=(F32,), tm=1024, tn=1024, tk=512,
        a_spec=None, b_spec=None, out_specs=None, out_shapes=None, epi=None, epi_in=(), epi_specs=()):
    tm, tn, tk = min(tm, M), min(tn, N), min(tk, K)
    assert M % tm == 0 and N % tn == 0 and K % tk == 0, (name, M, N, K, tm, tn, tk)
    nk = K // tk
    n_epi, n_out = len(epi_in), len(out_dtypes)
    if a_spec is None:
        a_spec = pl.BlockSpec((tk, tm), lambda i, j, k: (k, i)) if ta else pl.BlockSpec((tm, tk), lambda i, j, k: (i, k))
    if b_spec is None:
        b_spec = pl.BlockSpec((tn, tk), lambda i, j, k: (j, k)) if tb else pl.BlockSpec((tk, tn), lambda i, j, k: (k, j))
    if out_specs is None:
        out_specs = [pl.BlockSpec((tm, tn), lambda i, j, k: (i, j))] * n_out
        out_shapes = [jax.ShapeDtypeStruct((M, N), dt) for dt in out_dtypes]

    def body(a_ref, b_ref, *rest):
        epi_refs, o_refs, acc = rest[:n_epi], rest[n_epi:n_epi + n_out], rest[-1]
        k = pl.program_id(2)

        @pl.when(k == 0)
        def _():
            acc[...] = jnp.zeros_like(acc)

        acc[...] += _dg(a_ref[...].astype(BF16), b_ref[...].astype(BF16), 0 if ta else 1, 1 if tb else 0)

        @pl.when(k == nk - 1)
        def _():
            r = acc[...]
            vals = (r,) if epi is None else epi(r, *[e[...] for e in epi_refs])
            for o_ref, v in zip(o_refs, vals):
                o_ref[...] = v.astype(o_ref.dtype)

    outs = pl.pallas_call(
        body, name=name, out_shape=out_shapes, grid=(M // tm, N // tn, nk),
        in_specs=[a_spec, b_spec, *epi_specs], out_specs=out_specs,
        scratch_shapes=[pltpu.VMEM((tm, tn), F32)],
        compiler_params=_params(("parallel", "parallel", "arbitrary")),
    )(a, b, *epi_in)
    return outs[0] if n_out == 1 else outs


def _T(arr, width, col=0, lead=None):
    return ("tile", arr, width, col, lead)


def _W(arr):
    return ("whole", arr)


def _rows(fn, n_rows, tr, ins, tile_outs, acc_outs, name):
    tr = min(tr, n_rows)
    assert n_rows % tr == 0, (name, n_rows, tr)
    in_specs, args = [], []
    for d in ins:
        if d[0] == "tile":
            _, arr, width, col, lead = d
            if lead is None:
                in_specs.append(pl.BlockSpec((tr, width), lambda i, col=col: (i, col)))
            else:
                in_specs.append(pl.BlockSpec((None, tr, width), lambda i, col=col, lead=lead: (lead, i, col)))
            args.append(arr)
        else:
            arr = d[1]
            in_specs.append(pl.BlockSpec(arr.shape, lambda i, nd=arr.ndim: (0,) * nd))
            args.append(arr)
    n_in, n_t = len(ins), len(tile_outs)
    out_shape = [jax.ShapeDtypeStruct((n_rows, w), dt) for w, dt in tile_outs]
    out_specs = [pl.BlockSpec((tr, w), lambda i: (i, 0)) for w, _ in tile_outs]
    out_shape += [jax.ShapeDtypeStruct(s, F32) for s in acc_outs]
    out_specs += [pl.BlockSpec(s, lambda i, nd=len(s): (0,) * nd) for s in acc_outs]

    def body(*refs):
        in_refs, t_refs, a_refs = refs[:n_in], refs[n_in:n_in + n_t], refs[n_in + n_t:]
        vals = fn(*[r[...] for r in in_refs])
        for r, v in zip(t_refs, vals[:n_t]):
            r[...] = v.astype(r.dtype)
        first = pl.program_id(0) == 0
        for r, v in zip(a_refs, vals[n_t:]):
            @pl.when(first)
            def _(r=r, v=v):
                r[...] = v

            @pl.when(jnp.logical_not(first))
            def _(r=r, v=v):
                r[...] += v

    return pl.pallas_call(
        body, name=name, out_shape=out_shape, grid=(n_rows // tr,),
        in_specs=in_specs, out_specs=out_specs,
        compiler_params=_params(("arbitrary",)),
    )(*args)


def _colsum(t):
    return jnp.sum(t, axis=0, keepdims=True)


def _prenorm(x, g, sc, sh):
    xf = x.astype(F32)
    return xf * lax.rsqrt(jnp.mean(xf * xf, axis=-1, keepdims=True) + EPS) * g * (1.0 + sc) + sh


def _postnorm(x, y, gate, pg):
    return x + gate * (y * lax.rsqrt(jnp.mean(y * y, axis=-1, keepdims=True) + EPS) * pg)


def _gelu(t):
    return 0.5 * t * (1.0 + lax.erf(t * (2.0 ** -0.5)))


def _sg_pre(zu, zv, lng, lnb):
    u, vr = _gelu(zu), _gelu(zv)
    mu = jnp.mean(vr, axis=-1, keepdims=True)
    var = jnp.mean(jnp.square(vr - mu), axis=-1, keepdims=True)
    return u, (vr - mu) * lax.rsqrt(var + EPS) * lng + lnb


def _sg_mix(vv, ws_ref_vals, bs_vals, gw):
    parts = []
    for g in range(SG_GROUPS):
        s = _dg(ws_ref_vals[g].astype(BF16), vv[:, g * gw:(g + 1) * gw].astype(BF16), 1, 0)
        parts.append(s + bs_vals[g])
    return jnp.concatenate(parts, axis=1)


def _readout(o, r, g, heads, dv):
    parts = []
    for h in range(heads):
        oh = o[:, h * dv:(h + 1) * dv]
        parts.append(oh * lax.rsqrt(jnp.mean(oh * oh, axis=-1, keepdims=True) + EPS))
    return jnp.concatenate(parts, axis=1) * g * (r * jax.nn.sigmoid(r))


def _log_sigmoid(a):
    return jnp.minimum(a, 0.0) - jnp.log(1.0 + jnp.exp(-jnp.abs(a)))


def _rope_swap(t, m):
    lane = lax.broadcasted_iota(jnp.int32, t.shape, 1)
    return jnp.where((lane % (2 * m)) < m, pltpu.roll(t, 3 * m, 1), pltpu.roll(t, m, 1))


def _rope(t, cos, sin, heads, dk):
    parts = []
    for h in range(heads):
        th = t[:, h * dk:(h + 1) * dk]
        parts.append(th * cos + _rope_swap(th, dk // 4) * sin)
    return jnp.concatenate(parts, axis=1)


def _rope_t(dt, cos, sin, heads, dk):
    parts = []
    for h in range(heads):
        dh = dt[:, h * dk:(h + 1) * dk]
        parts.append(dh * cos + _rope_swap(dh * sin, dk // 4))
    return jnp.concatenate(parts, axis=1)


def _chunk_terms(d, qv, kv, lav, C):
    row = lax.broadcasted_iota(jnp.int32, (C, C), 0)
    col = lax.broadcasted_iota(jnp.int32, (C, C), 1)
    tri = jnp.where(d == 0, row - col, col - row) >= 0
    b = _dg(tri.astype(F32), lav, 1, 0, precision=lax.Precision.HIGHEST)
    btot = _colsum(lav)
    eb, enb, ebt = jnp.exp(b), jnp.exp(-b), jnp.exp(btot - b)
    return tri, btot, eb, enb, ebt, qv * eb, kv * enb, kv * ebt


def _gla_fwd(q, k, zv, v_col0, la, st0, heads, dk, dv, name):
    n, C = q.shape[0], GLA_CHUNK
    nc = n // C

    def chunk(d, i):
        return jnp.where(d == 0, i, nc - 1 - i)

    def body(q_ref, k_ref, v_ref, la_ref, st0_ref, o_ref, save_ref, fin_ref, st):
        d, i = pl.program_id(0), pl.program_id(2)

        @pl.when(i == 0)
        def _():
            st[...] = st0_ref[...]

        tri, btot, _, _, _, qt, kt, kh = _chunk_terms(d, q_ref[...], k_ref[...], la_ref[...], C)
        s = st[...]
        vb = v_ref[...].astype(BF16)
        qtb = qt.astype(BF16)
        att = jnp.where(tri, _dg(qtb, kt.astype(BF16), 1, 1), 0.0)
        o_ref[...] = _dg(qtb, s.astype(BF16), 1, 1) + _dg(att.astype(BF16), vb, 1, 0)
        save_ref[...] = s
        s_new = s * jnp.exp(btot) + _dg(vb, kh.astype(BF16), 0, 0)
        st[...] = s_new

        @pl.when(i == nc - 1)
        def _():
            fin_ref[...] = s_new

    H = heads
    return pl.pallas_call(
        body, name=name, grid=(2, H, nc),
        out_shape=[jax.ShapeDtypeStruct((2, n, H * dv), F32),
                   jax.ShapeDtypeStruct((2, H, nc, dv, dk), F32),
                   jax.ShapeDtypeStruct((2, H, dv, dk), F32)],
        in_specs=[pl.BlockSpec((C, dk), lambda d, h, i: (chunk(d, i), h)),
                  pl.BlockSpec((C, dk), lambda d, h, i: (chunk(d, i), h)),
                  pl.BlockSpec((C, dv), lambda d, h, i: (chunk(d, i), v_col0 + h)),
                  pl.BlockSpec((C, dk), lambda d, h, i: (chunk(d, i), d * H + h)),
                  pl.BlockSpec((None, None, dv, dk), lambda d, h, i: (d, h, 0, 0))],
        out_specs=[pl.BlockSpec((None, C, dv), lambda d, h, i: (d, chunk(d, i), h)),
                   pl.BlockSpec((None, None, None, dv, dk), lambda d, h, i: (d, h, chunk(d, i), 0, 0)),
                   pl.BlockSpec((None, None, dv, dk), lambda d, h, i: (d, h, 0, 0))],
        scratch_shapes=[pltpu.VMEM((dv, dk), F32)],
        compiler_params=_params(("arbitrary", "arbitrary", "arbitrary")),
    )(q, k, zv, la, st0)


def _gla_bwd(q, k, zv, v_col0, la, saved, do, dfin, heads, dk, dv, name):
    n, C = q.shape[0], GLA_CHUNK
    nc = n // C

    def chunk(d, i):
        return jnp.where(d == 0, nc - 1 - i, i)

    def body(q_ref, k_ref, v_ref, la_ref, save_ref, do_ref, dfin_ref, dq_ref, dk_ref, dv_ref, dla_ref, d0_ref, dst):
        d, i = pl.program_id(0), pl.program_id(2)

        @pl.when(i == 0)
        def _():
            dst[...] = dfin_ref[...]

        tri, btot, eb, enb, ebt, qt, kt, kh = _chunk_terms(d, q_ref[...], k_ref[...], la_ref[...], C)
        s, dsn = save_ref[...], dst[...]
        vb, dob = v_ref[...].astype(BF16), do_ref[...].astype(BF16)
        qtb, ktb, khb, dsnb = qt.astype(BF16), kt.astype(BF16), kh.astype(BF16), dsn.astype(BF16)
        att = jnp.where(tri, _dg(qtb, ktb, 1, 1), 0.0).astype(BF16)
        datt = jnp.where(tri, _dg(dob, vb, 1, 1), 0.0).astype(BF16)
        dqt = _dg(dob, s.astype(BF16), 1, 0) + _dg(datt, ktb, 1, 0)
        dkt = _dg(datt, qtb, 0, 0)
        dkh = _dg(vb, dsnb, 1, 0)
        dv_ref[...] = _dg(att, dob, 0, 0) + _dg(khb, dsnb, 1, 1)
        ebtot = jnp.exp(btot)
        dbtot = ebtot * _colsum(s * dsn) + _colsum(dkh * kh)
        s0 = dsn * ebtot + _dg(dob, qtb, 0, 0)
        dst[...] = s0
        db = dqt * qt - dkt * kt - dkh * kh
        dq_ref[...] = dqt * eb
        dk_ref[...] = dkt * enb + dkh * ebt
        row = lax.broadcasted_iota(jnp.int32, (C, C), 0)
        col = lax.broadcasted_iota(jnp.int32, (C, C), 1)
        tri_t = (jnp.where(d == 0, col - row, row - col) >= 0).astype(F32)
        dla_ref[...] = _dg(tri_t, db, 1, 0, precision=lax.Precision.HIGHEST) + dbtot

        @pl.when(i == nc - 1)
        def _():
            d0_ref[...] = s0

    H = heads
    return pl.pallas_call(
        body, name=name, grid=(2, H, nc),
        out_shape=[jax.ShapeDtypeStruct((2, n, H * dk), F32),
                   jax.ShapeDtypeStruct((2, n, H * dk), F32),
                   jax.ShapeDtypeStruct((2, n, H * dv), F32),
                   jax.ShapeDtypeStruct((n, 2 * H * dk), F32),
                   jax.ShapeDtypeStruct((2, H, dv, dk), F32)],
        in_specs=[pl.BlockSpec((C, dk), lambda d, h, i: (chunk(d, i), h)),
                  pl.BlockSpec((C, dk), lambda d, h, i: (chunk(d, i), h)),
                  pl.BlockSpec((C, dv), lambda d, h, i: (chunk(d, i), v_col0 + h)),
                  pl.BlockSpec((C, dk), lambda d, h, i: (chunk(d, i), d * H + h)),
                  pl.BlockSpec((None, None, None, dv, dk), lambda d, h, i: (d, h, chunk(d, i), 0, 0)),
                  pl.BlockSpec((C, dv), lambda d, h, i: (chunk(d, i), h)),
                  pl.BlockSpec((None, None, dv, dk), lambda d, h, i: (d, h, 0, 0))],
        out_specs=[pl.BlockSpec((None, C, dk), lambda d, h, i: (d, chunk(d, i), h)),
                   pl.BlockSpec((None, C, dk), lambda d, h, i: (d, chunk(d, i), h)),
                   pl.BlockSpec((None, C, dv), lambda d, h, i: (d, chunk(d, i), h)),
                   pl.BlockSpec((C, dk), lambda d, h, i: (chunk(d, i), d * H + h)),
                   pl.BlockSpec((None, None, dv, dk), lambda d, h, i: (d, h, 0, 0))],
        scratch_shapes=[pltpu.VMEM((dv, dk), F32)],
        compiler_params=_params(("arbitrary", "arbitrary", "arbitrary")),
    )(q, k, zv, la, saved, do, dfin)


def _adamw_math(w, g, m, v):
    m2 = ADAM_B1 * m + (1.0 - ADAM_B1) * g
    v2 = ADAM_B2 * v + (1.0 - ADAM_B2) * jnp.square(g)
    m_hat = m2 / (1.0 - ADAM_B1 ** ADAM_STEP)
    v_hat = v2 / (1.0 - ADAM_B2 ** ADAM_STEP)
    delta = -ADAM_LR * (m_hat / (jnp.sqrt(v_hat) + ADAM_EPS) + ADAM_WD * w)
    return delta, m2, v2


def _adamw(w, g, m, v, name):
    r, cols = w.shape
    tr = _row_tile(r, cols, 4 * 4)
    spec = pl.BlockSpec((tr, cols), lambda i: (i, 0))

    def body(w_ref, g_ref, m_ref, v_ref, d_ref, m2_ref, v2_ref):
        d_ref[...], m2_ref[...], v2_ref[...] = _adamw_math(w_ref[...], g_ref[...], m_ref[...], v_ref[...])

    return pl.pallas_call(
        body, name=name, grid=(r // tr,), out_shape=[jax.ShapeDtypeStruct((r, cols), F32)] * 3,
        in_specs=[spec] * 4, out_specs=[spec] * 3, compiler_params=_params(("parallel",)),
    )(w, g, m, v)


def _ada_update(cond_t, dmod, w, m, v, name):
    r, cols = w.shape
    tr, tc = _pick(r, 512, 8), _pick(cols, 1024)
    spec = pl.BlockSpec((tr, tc), lambda i, j: (i, j))

    def body(c_ref, d_ref, w_ref, m_ref, v_ref, g_ref, dl_ref, m2_ref, v2_ref):
        g = _dg(c_ref[...].astype(BF16), d_ref[...].astype(BF16), 1, 0)
        g_ref[...] = g
        dl_ref[...], m2_ref[...], v2_ref[...] = _adamw_math(w_ref[...], g, m_ref[...], v_ref[...])

    return pl.pallas_call(
        body, name=name, grid=(r // tr, cols // tc), out_shape=[jax.ShapeDtypeStruct((r, cols), F32)] * 4,
        in_specs=[pl.BlockSpec((tr, cond_t.shape[1]), lambda i, j: (i, 0)),
                  pl.BlockSpec((dmod.shape[0], tc), lambda i, j: (0, j)), spec, spec, spec],
        out_specs=[spec] * 4, compiler_params=_params(("parallel", "parallel")),
    )(cond_t, dmod, w, m, v)


def _pack(parts, rows=8):
    flat = jnp.concatenate([p.reshape(-1).astype(F32) for p in parts])
    n = -(-flat.shape[0] // (rows * LANES)) * LANES
    return jnp.pad(flat, (0, rows * n - flat.shape[0])).reshape(rows, n)


def _unpack(flat, shapes):
    out, off = [], 0
    for s in shapes:
        size = math.prod(s)
        out.append(flat[off:off + size].reshape(s))
        off += size
    return out


def kernel(x, c, ctx, c_ctx, w_ada, b_ada, pre1_g, post1_g, pre2_g, post2_g, w_in, w_dec_f, b_dec_f, w_dec_b, b_dec_b, gla_norm_g, sg_ln_g, sg_ln_b, w_s, b_s, w_o, w_1, w_2, loss_target, m_c_ctx, m_w_ada, m_b_ada, m_pre1_g, m_post1_g, m_pre2_g, m_post2_g, m_w_in, m_w_dec_f, m_b_dec_f, m_w_dec_b, m_b_dec_b, m_gla_norm_g, m_sg_ln_g, m_sg_ln_b, m_w_s, m_b_s, m_w_o, m_w_1, m_w_2, v_c_ctx, v_w_ada, v_b_ada, v_pre1_g, v_post1_g, v_pre2_g, v_post2_g, v_w_in, v_w_dec_f, v_b_dec_f, v_w_dec_b, v_b_dec_b, v_gla_norm_g, v_sg_ln_g, v_sg_ln_b, v_w_s, v_b_s, v_w_o, v_w_1, v_w_2):
    N, D = x.shape[1], x.shape[2]
    NC = ctx.shape[1]
    H = GLA_HEADS
    VALW = D // 2
    DV = VALW // H
    DK = DV // 2
    KEYW = H * DK
    SGW = D - VALW
    GW = SGW // SG_GROUPS
    LR = 2 * GLA_LOWRANK
    F = w_1.shape[2] * N_CHIP
    FS = F // N_CHIP
    RH = D // 2
    MS = w_ada.shape[2]
    IN_COLS = w_in.shape[2] * N_CHIP
    K0, V0, R0, LF0 = KEYW, 2 * KEYW, 2 * KEYW + VALW, 2 * KEYW + 2 * VALW
    SG0 = LF0 + LR
    AQ, AK, AV, AR, ALR = 2 * SGW, 2 * SGW + KEYW, 2 * SGW + 2 * KEYW, 2 * SGW + 2 * KEYW + VALW, 2 * SGW + 2 * KEYW + 2 * VALW
    ACOLS = ALR + LANES
    assert IN_COLS == SG0 + 2 * SGW and N % SG_CHUNK == 0 and N % GLA_CHUNK == 0 and NC % GLA_CHUNK == 0

    ax, ay, ac = _place()
    s_me = (2 * ax + ay).astype(jnp.int32)
    b_me = (4 * ax + 2 * ay + ac).astype(jnp.int32)
    s_arr, c_arr = s_me.reshape(1), ac.astype(jnp.int32).reshape(1)
    sc_arr = jnp.concatenate([s_arr, c_arr])
    CS = IN_COLS // N_CHIP

    shards = [_cast_blocks(w_in[0].reshape(2, RH, CS), s_arr, "cast_w_in"), _cast_blocks(w_o[0].reshape(2, D // N_DEV, D), s_arr, "cast_w_o"),
              _cast_blocks(w_1[0].reshape(2, RH, FS), s_arr, "cast_w_1"), _cast_blocks(w_2[0].reshape(2, F // N_DEV, D), s_arr, "cast_w_2")]
    win_g, wo_g, w1_g, w2_g = _gather_big(shards, "gather_weights")
    w_in_nat = win_g.reshape(N_CHIP, 2, RH, IN_COLS // N_CHIP).transpose(1, 2, 0, 3).reshape(D, IN_COLS)
    w_al = jnp.concatenate([w_in_nat[:, SG0:], w_in_nat[:, :LF0], w_in_nat[:, LF0:SG0],
                            jnp.zeros((D, LANES - LR), BF16)], axis=1)
    w_o_f = wo_g.reshape(D, D)
    w_2_f = w2_g.reshape(F, D)

    n_dec = GLA_LOWRANK * (KEYW // N_CHIP)
    g0 = _allgather_small(_pack([c, w_dec_f, w_dec_b, gla_norm_g]), "gather_small0").reshape(N_DEV, -1)
    c_all = g0[:, :D]
    per_chip = g0[0::2]
    wdf = per_chip[:, D:D + n_dec].reshape(N_CHIP, GLA_LOWRANK, KEYW // N_CHIP).transpose(1, 0, 2).reshape(GLA_LOWRANK, KEYW)
    wdb = per_chip[:, D + n_dec:D + 2 * n_dec].reshape(N_CHIP, GLA_LOWRANK, KEYW // N_CHIP).transpose(1, 0, 2).reshape(GLA_LOWRANK, KEYW)
    gn_full = per_chip[:, D + 2 * n_dec:D + 2 * n_dec + H * (DV // N_CHIP)].reshape(N_CHIP, H, DV // N_CHIP).transpose(1, 0, 2).reshape(1, VALW)
    wd_f = jnp.zeros((LANES, KEYW), F32).at[:GLA_LOWRANK].set(wdf)
    wd_b = jnp.zeros((LANES, KEYW), F32).at[GLA_LOWRANK:LR].set(wdb)

    cond_in = jnp.zeros((16, D), F32).at[:N_DEV].set(c_all).at[N_DEV].set(c_ctx)
    b_ada_sh = lax.dynamic_slice(b_ada, (0, s_me * MS), (1, MS))

    def mod_epi(r, bias):
        return (r + bias,)

    def silu_rows(t):
        return (t * jax.nn.sigmoid(t),)

    cond = _rows(silu_rows, 16, 16, [_W(cond_in)], [(D, F32)], [], "cond_silu")[0]
    mod_sh = _mm(cond, w_ada[0], 16, MS, D, name="mod_matmul", tn=512, tk=D, epi=mod_epi, epi_in=(b_ada_sh,),
                 epi_specs=(pl.BlockSpec((1, min(512, MS)), lambda i, j, k: (0, j)),))
    g1m = _allgather_small(mod_sh, "gather_mod").reshape(N_DEV, 16, MS)[0::2]
    mod_all = g1m.transpose(1, 0, 2).reshape(16, N_CHIP * MS)
    mod_me = lax.dynamic_slice(mod_all, (b_me, 0), (1, N_MOD * D))
    sh1, sc1, gt1, sh2, sc2, gt2 = [mod_me[:, i * D:(i + 1) * D] for i in range(N_MOD)]
    csh1, csc1 = mod_all[N_DEV:N_DEV + 1, :D], mod_all[N_DEV:N_DEV + 1, D:2 * D]

    mq = DK // 4
    pos = jnp.arange(N)
    inv_freq = ROPE_BASE ** (-jnp.arange(mq, dtype=F32) / mq)
    ang_r = (pos // GRID_W).astype(F32)[:, None] * inv_freq[None, :]
    ang_c = (pos % GRID_W).astype(F32)[:, None] * inv_freq[None, :]
    cos_t = jnp.concatenate([jnp.cos(ang_r), jnp.cos(ang_r), jnp.cos(ang_c), jnp.cos(ang_c)], axis=1)
    sin_t = jnp.concatenate([-jnp.sin(ang_r), jnp.sin(ang_r), -jnp.sin(ang_c), jnp.sin(ang_c)], axis=1)

    x2, tgt, ctx2 = x[0], loss_target[0], ctx[0]
    TR = 128
    qscale = DK ** -0.5

    def prenorm_fwd(xa, g, sc, sh, n_rows, name):
        return _rows(lambda xv, gv, scv, shv: (_prenorm(xv, gv, scv, shv),), n_rows, TR,
                     [_T(xa, D), _W(g), _W(sc), _W(sh)], [(D, BF16)], [], name)[0]

    hx = prenorm_fwd(x2, pre1_g, sc1, sh1, N, "prenorm1_x")
    hc = prenorm_fwd(ctx2, pre1_g, csc1, csh1, NC, "prenorm1_ctx")
    tka = _pick(ACOLS, 1152)
    z_al = _mm(hx, w_al, N, ACOLS, D, name="in_proj_x", tn=tka)
    zc_al = _mm(hc, w_al, NC, ACOLS, D, name="in_proj_ctx", tn=tka)

    def decay(lr, wdf_v, wdb_v, bf_v, bb_v):
        lrb = lr.astype(BF16)
        a_f = _dg(lrb, wdf_v.astype(BF16), 1, 0) + bf_v
        a_b = _dg(lrb, wdb_v.astype(BF16), 1, 0) + bb_v
        return a_f, a_b

    def prep_x(zq, zk, lr, cs, sn, wdf_v, wdb_v, bf_v, bb_v):
        a_f, a_b = decay(lr, wdf_v, wdb_v, bf_v, bb_v)
        la = jnp.concatenate([_log_sigmoid(a_f), _log_sigmoid(a_b)], axis=1) / GLA_TAU
        return _rope(zq * qscale, cs, sn, H, DK), _rope(zk, cs, sn, H, DK), la

    def prep_c(zk, lr, wdf_v, wdb_v, bf_v, bb_v):
        a_f, a_b = decay(lr, wdf_v, wdb_v, bf_v, bb_v)
        return zk, jnp.concatenate([_log_sigmoid(a_f), _log_sigmoid(a_b)], axis=1) / GLA_TAU

    dec_w = [_W(wd_f), _W(wd_b), _W(b_dec_f), _W(b_dec_b)]
    q_r, k_r, la_x = _rows(prep_x, N, TR, [_T(z_al, KEYW, AQ // KEYW), _T(z_al, KEYW, AK // KEYW), _T(z_al, LANES, ALR // LANES),
                                           _T(cos_t, DK), _T(sin_t, DK)] + dec_w,
                           [(KEYW, F32), (KEYW, F32), (2 * KEYW, F32)], [], "gla_prep_x")
    k_c, la_c = _rows(prep_c, NC, TR, [_T(zc_al, KEYW, AK // KEYW), _T(zc_al, LANES, ALR // LANES)] + dec_w,
                      [(KEYW, F32), (2 * KEYW, F32)], [], "gla_prep_ctx")

    zero_state = jnp.zeros((2, H, DV, DK), F32)
    q_c = jnp.zeros((NC, KEYW), F32)
    _, saved_c, st_c = _gla_fwd(q_c, k_c, zc_al, AV // DV, la_c, zero_state, H, DK, DV, "gla_fwd_ctx")
    o_x, saved_x, _ = _gla_fwd(q_r, k_r, z_al, AV // DV, la_x, st_c, H, DK, DV, "gla_fwd_x")

    def readout_fwd(of, ob, r, g):
        return (_readout(of + ob, r, g, H, DV),)

    y_gla = _rows(readout_fwd, N, TR, [_T(o_x, VALW, 0, 0), _T(o_x, VALW, 0, 1), _T(z_al, VALW, AR // VALW), _W(gn_full)],
                  [(VALW, BF16)], [], "gla_readout")[0]

    bs_col = b_s[0].reshape(SG_GROUPS, SG_CHUNK, 1)

    def sg_fwd(zu, zv, lng, lnb, ws, bs):
        u, vv = _sg_pre(zu, zv, lng, lnb)
        return (u * _sg_mix(vv, ws, bs, GW),)

    y_sg = _rows(sg_fwd, N, SG_CHUNK, [_T(z_al, SGW, 0), _T(z_al, SGW, 1), _W(sg_ln_g), _W(sg_ln_b), _W(w_s[0]), _W(bs_col)],
                 [(SGW, BF16)], [], "sg_fwd")[0]
    ycat = jnp.concatenate([y_gla, y_sg], axis=1)
    y = _mm(ycat, w_o_f, N, D, D, name="out_proj")
    x1 = _rows(lambda xv, yv, gv, pv: (_postnorm(xv, yv, gv, pv),), N, TR,
               [_T(x2, D), _T(y, D), _W(gt1), _W(post1_g)], [(D, F32)], [], "postnorm1")[0]
    h2 = prenorm_fwd(x1, pre2_g, sc2, sh2, N, "prenorm2")

    tm1, tn1, tk1 = min(1024, N), min(1024, FS), min(512, RH)
    w1_fwd_spec = pl.BlockSpec((None, tk1, tn1), lambda i, j, k: (2 * ((j * tn1) // FS) + (k * tk1) // RH, ((k * tk1) % RH) // tk1, ((j * tn1) % FS) // tn1))

    def relu2_epi(r):
        rf = jnp.maximum(r, 0.0)
        return rf * rf, rf

    act, rf = _mm(h2, w1_g, N, F, D, name="mlp_up", out_dtypes=(BF16, BF16), tm=tm1, tn=tn1, tk=tk1, b_spec=w1_fwd_spec, epi=relu2_epi)
    y2 = _mm(act, w_2_f, N, D, F, name="mlp_down")

    def final(x1v, y2v, gv, pv, tv):
        def loss_fn(x1a, y2a, ga, pa):
            err = _postnorm(x1a, y2a, ga, pa) - tv
            return 0.5 * jnp.sum(jnp.mean(err * err, axis=-1))
        loss, grads = jax.value_and_grad(loss_fn, argnums=(0, 1, 2, 3))(x1v, y2v, gv, pv)
        return grads[0], grads[1], jnp.full((1, LANES), loss, F32), _colsum(grads[2]), _colsum(grads[3])

    dx2, dy2, loss_acc, dgt2, dpost2 = _rows(final, N, TR, [_T(x1, D), _T(y2, D), _W(gt2), _W(post2_g), _T(tgt, D)],
                                             [(D, F32), (D, BF16)], [(1, LANES), (1, D), (1, D)], "loss_postnorm2_bwd")

    df = _mm(dy2, w_2_f, N, F, D, name="mlp_down_dx", tb=True, out_dtypes=(BF16,), epi=lambda r, rfv: (r * (2.0 * rfv.astype(F32)),),
             epi_in=(rf,), epi_specs=(pl.BlockSpec((min(1024, N), min(1024, F)), lambda i, j, k: (i, j)),))
    dw2 = _mm(act, dy2, F, D, N, name="mlp_down_dw", ta=True, out_dtypes=(BF16,))
    tnb, tkb = min(1024, D), min(512, FS)
    tnb = min(tnb, RH)
    w1_bwd_spec = pl.BlockSpec((None, tnb, tkb), lambda i, j, k: (2 * ((k * tkb) // FS) + (j * tnb) // RH, ((j * tnb) % RH) // tnb, ((k * tkb) % FS) // tkb))
    dh2 = _mm(df, w1_g, N, D, F, name="mlp_up_dx", tb=True, tn=tnb, tk=tkb, b_spec=w1_bwd_spec)
    tmw, tnw = min(1024, RH), min(1024, FS)
    dw1_spec = pl.BlockSpec((None, tmw, tnw), lambda i, j, k: (2 * ((j * tnw) // FS) + (i * tmw) // RH, ((i * tmw) % RH) // tmw, ((j * tnw) % FS) // tnw))
    dw1 = _mm(h2, df, D, F, N, name="mlp_up_dw", ta=True, tm=tmw, tn=tnw, out_dtypes=(BF16,), out_specs=[dw1_spec],
              out_shapes=[jax.ShapeDtypeStruct((N_DEV, RH, FS), BF16)])

    def prenorm_bwd(xv, gv, scv, shv, dh, dres):
        _, vjp = jax.vjp(_prenorm, xv, gv, scv, shv)
        dx, dg, dsc, dsh = vjp(dh)
        return dx + dres, _colsum(dg), _colsum(dsc), _colsum(dsh)

    dx1, dpre2, dsc2, dsh2 = _rows(prenorm_bwd, N, TR, [_T(x1, D), _W(pre2_g), _W(sc2), _W(sh2), _T(dh2, D), _T(dx2, D)],
                                   [(D, F32)], [(1, D)] * 3, "prenorm2_bwd")

    def postnorm_bwd(yv, gv, pv, dxv):
        _, vjp = jax.vjp(lambda ya, ga, pa: _postnorm(0.0, ya, ga, pa), yv, gv, pv)
        dy_, dg_, dp_ = vjp(dxv)
        return dy_, _colsum(dg_), _colsum(dp_)

    dy, dgt1, dpost1 = _rows(postnorm_bwd, N, TR, [_T(y, D), _W(gt1), _W(post1_g), _T(dx1, D)], [(D, BF16)], [(1, D)] * 2, "postnorm1_bwd")
    dycat = _mm(dy, w_o_f, N, D, D, name="out_proj_dx", tb=True)
    dwo = _mm(ycat, dy, D, D, N, name="out_proj_dw", ta=True, out_dtypes=(BF16,))

    def readout_bwd(of, ob, r, g, dyv):
        _, vjp = jax.vjp(lambda o_, r_, g_: _readout(o_, r_, g_, H, DV), of + ob, r, g)
        do_, dr_, dg_ = vjp(dyv)
        return do_, dr_, _colsum(dg_)

    do_x, dz_r, dgn = _rows(readout_bwd, N, TR, [_T(o_x, VALW, 0, 0), _T(o_x, VALW, 0, 1), _T(z_al, VALW, AR // VALW), _W(gn_full), _T(dycat, VALW, 0)],
                            [(VALW, F32), (VALW, BF16)], [(1, VALW)], "gla_readout_bwd")

    def sg_bwd(zu, zv, lng, lnb, ws, bs, dyv):
        (u, vv), vjp = jax.vjp(_sg_pre, zu, zv, lng, lnb)
        s = _sg_mix(vv, ws, bs, GW)
        du, ds = dyv * s, dyv * u
        dws, dbs, dvv = [], [], []
        for g in range(SG_GROUPS):
            dsg = ds[:, g * GW:(g + 1) * GW]
            dsb = dsg.astype(BF16)
            dws.append(_dg(dsb, vv[:, g * GW:(g + 1) * GW].astype(BF16), 1, 1))
            dbs.append(jnp.sum(dsg, axis=1, keepdims=True))
            dvv.append(_dg(ws[g].astype(BF16), dsb, 0, 0))
        dzu, dzv, dlng, dlnb = vjp((du, jnp.concatenate(dvv, axis=1)))
        return jnp.concatenate([dzu, dzv], axis=1), _colsum(dlng), _colsum(dlnb), jnp.concatenate(dws, axis=0), jnp.concatenate(dbs, axis=0)

    dz_sg, dlng, dlnb, dws, dbs = _rows(sg_bwd, N, SG_CHUNK, [_T(z_al, SGW, 0), _T(z_al, SGW, 1), _W(sg_ln_g), _W(sg_ln_b), _W(w_s[0]), _W(bs_col), _T(dycat, SGW, VALW // SGW)],
                                        [(2 * SGW, BF16)], [(1, SGW), (1, SGW), (SG_GROUPS * SG_CHUNK, SG_CHUNK), (SG_GROUPS * SG_CHUNK, 1)], "sg_bwd")

    dq_x, dk_x, dv_x, dla_x, dst0 = _gla_bwd(q_r, k_r, z_al, AV // DV, la_x, saved_x, do_x, zero_state, H, DK, DV, "gla_bwd_x")
    _, dk_c, dv_c, dla_c, _ = _gla_bwd(q_c, k_c, zc_al, AV // DV, la_c, saved_c, jnp.zeros((NC, VALW), F32), dst0, H, DK, DV, "gla_bwd_ctx")

    def decay_bwd(lr, dla, wdf_v, wdb_v, bf_v, bb_v):
        a_f, a_b = decay(lr, wdf_v, wdb_v, bf_v, bb_v)
        da_f = dla[:, :KEYW] * jax.nn.sigmoid(-a_f) / GLA_TAU
        da_b = dla[:, KEYW:] * jax.nn.sigmoid(-a_b) / GLA_TAU
        lrb, dfb, dbb = lr.astype(BF16), da_f.astype(BF16), da_b.astype(BF16)
        dlr = _dg(dfb, wdf_v.astype(BF16), 1, 1) + _dg(dbb, wdb_v.astype(BF16), 1, 1)
        return dlr, _dg(lrb, dfb, 0, 0), _dg(lrb, dbb, 0, 0), _colsum(da_f), _colsum(da_b)

    def prep_x_bwd(dq0, dq1, dk0, dk1, dv0, dv1, lr, dla, cs, sn, wdf_v, wdb_v, bf_v, bb_v):
        dlr, dwf, dwb, dbf, dbb = decay_bwd(lr, dla, wdf_v, wdb_v, bf_v, bb_v)
        return (_rope_t(dq0 + dq1, cs, sn, H, DK) * qscale, _rope_t(dk0 + dk1, cs, sn, H, DK), dv0 + dv1, dlr, dwf, dwb, dbf, dbb)

    def prep_c_bwd(dk0, dk1, dv0, dv1, lr, dla, wdf_v, wdb_v, bf_v, bb_v):
        dlr, dwf, dwb, dbf, dbb = decay_bwd(lr, dla, wdf_v, wdb_v, bf_v, bb_v)
        return dk0 + dk1, dv0 + dv1, dlr, dwf, dwb, dbf, dbb

    dec_acc = [(LANES, KEYW), (LANES, KEYW), (1, KEYW), (1, KEYW)]
    dz_q, dz_k, dz_v, dz_lr, dwdf_x, dwdb_x, dbdf_x, dbdb_x = _rows(
        prep_x_bwd, N, TR, [_T(dq_x, KEYW, 0, 0), _T(dq_x, KEYW, 0, 1), _T(dk_x, KEYW, 0, 0), _T(dk_x, KEYW, 0, 1), _T(dv_x, VALW, 0, 0), _T(dv_x, VALW, 0, 1),
                            _T(z_al, LANES, ALR // LANES), _T(dla_x, 2 * KEYW), _T(cos_t, DK), _T(sin_t, DK)] + dec_w,
        [(KEYW, BF16), (KEYW, BF16), (VALW, BF16), (LANES, BF16)], dec_acc, "gla_prep_x_bwd")
    dzc_k, dzc_v, dzc_lr, dwdf_c, dwdb_c, dbdf_c, dbdb_c = _rows(
        prep_c_bwd, NC, TR, [_T(dk_c, KEYW, 0, 0), _T(dk_c, KEYW, 0, 1), _T(dv_c, VALW, 0, 0), _T(dv_c, VALW, 0, 1),
                             _T(zc_al, LANES, ALR // LANES), _T(dla_c, 2 * KEYW)] + dec_w,
        [(KEYW, BF16), (VALW, BF16), (LANES, BF16)], dec_acc, "gla_prep_ctx_bwd")

    dz_al = jnp.concatenate([dz_sg, dz_q, dz_k, dz_v, dz_r, dz_lr], axis=1)
    dzc_al = jnp.concatenate([jnp.zeros((NC, 2 * SGW + KEYW), BF16), dzc_k, dzc_v, jnp.zeros((NC, VALW), BF16), dzc_lr], axis=1)
    dhx = _mm(dz_al, w_al, N, D, ACOLS, name="in_proj_dx", tb=True, tk=tka)
    dhc = _mm(dzc_al, w_al, NC, D, ACOLS, name="in_proj_dctx", tb=True, tk=tka)
    h_cat = jnp.concatenate([hx, hc], axis=0)
    dz_cat = jnp.concatenate([dz_al, dzc_al], axis=0)
    tkt = _pick(N + NC, 2304)
    dw_al = _mm(h_cat, dz_cat, D, ACOLS, N + NC, name="in_proj_dw", ta=True, tn=tka, tk=tkt, out_dtypes=(BF16,))

    grad_x, dpre1_x, dsc1, dsh1 = _rows(prenorm_bwd, N, TR, [_T(x2, D), _W(pre1_g), _W(sc1), _W(sh1), _T(dhx, D), _T(dx1, D)],
                                        [(D, F32)], [(1, D)] * 3, "prenorm1_x_bwd")

    def prenorm_bwd_ctx(xv, gv, scv, shv, dh):
        _, vjp = jax.vjp(_prenorm, xv, gv, scv, shv)
        _, dg, dsc, dsh = vjp(dh)
        return _colsum(dg), _colsum(dsc), _colsum(dsh)

    dpre1_c, dcsc1, dcsh1 = _rows(prenorm_bwd_ctx, NC, TR, [_T(ctx2, D), _W(pre1_g), _W(csc1), _W(csh1), _T(dhc, D)],
                                  [], [(1, D)] * 3, "prenorm1_ctx_bwd")

    g_in = _shard_columns(dw_al, CS, SG0, 2 * SGW, -SG0, RH, "w_in_grad_blocks")
    grads_big = [g_in, dwo.reshape(N_DEV, D // N_DEV, D), dw1, dw2.reshape(N_DEV, F // N_DEV, D)]
    names = ["w_in", "w_o", "w_1", "w_2"]
    recv1 = _rs_sibling(grads_big, "rs_sibling")
    part = [_sum_sibling(g, r, c_arr, "rs_sum_sibling_" + nm) for g, r, nm in zip(grads_big, recv1, names)]
    recv2 = _rs_chips(part, "rs_chips")
    half = [_sum_chips(p, r, sc_arr, "rs_sum_chips_" + nm) for p, r, nm in zip(part, recv2, names)]
    g_in_pad, g_w_o, g_w_1, g_w_2 = _rs_final(half, "rs_final")
    g_w_in = g_in_pad.reshape(D, -1)[:, :CS]
    g_w_o, g_w_1, g_w_2 = [g.reshape(w.shape[1:]) for g, w in zip((g_w_o, g_w_1, g_w_2), (w_o, w_1, w_2))]

    dmod_x = jnp.concatenate([dsh1, dsc1, dgt1, dsh2, dsc2, dgt2], axis=1)
    dmodc = jnp.concatenate([dcsh1, dcsc1], axis=1)
    small_parts = [loss_acc, dmod_x, dmodc, dpre1_x + dpre1_c, dpost1, dpre2, dpost2, dwdf_x + dwdf_c, dbdf_x + dbdf_c,
                   dwdb_x + dwdb_c, dbdb_x + dbdb_c, dgn, dlng, dlnb, dws, dbs]
    small_shapes = [p.shape for p in small_parts]
    packed = _pack(small_parts)
    n_sm = packed.shape[1]
    gathered = _allgather_small(packed, "gather_small_grads")

    def sum_devices(g):
        tot = g[0:8]
        for dev in range(1, N_DEV):
            tot = tot + g[8 * dev:8 * dev + 8]
        return (tot,)

    summed = _rows(sum_devices, N_DEV * 8, N_DEV * 8, [_W(gathered)], [], [(8, n_sm)], "sum_small_grads")[0]
    (loss_s, dmod_sum, dmodc_sum, g_pre1, g_post1, g_pre2, g_post2, g_wdf_pad, g_bdf, g_wdb_pad, g_bdb, g_gn, g_lng, g_lnb,
     g_ws, g_bs) = _unpack(summed.reshape(-1), small_shapes)
    loss = loss_s[0, 0]
    dmod_rows = gathered.reshape(N_DEV, -1)[:, LANES:LANES + N_MOD * D]
    g_b_ada = dmod_sum + jnp.pad(dmodc_sum, ((0, 0), (0, (N_MOD - 2) * D)))
    dmod16 = jnp.zeros((16, N_MOD * D), F32).at[:N_DEV].set(dmod_rows).at[N_DEV, :2 * D].set(dmodc_sum[0])
    dmod16_sh = lax.dynamic_slice(dmod16, (0, s_me * MS), (16, MS))
    g_w_ada, d_w_ada, nm_w_ada, nv_w_ada = _ada_update(cond.T, dmod16_sh, w_ada[0], m_w_ada[0], v_w_ada[0], "w_ada_update")

    dcond = _mm(dmod16_sh, w_ada[0], 16, D, MS, name="cond_bwd", tb=True, tk=min(512, MS))
    part_c = _allgather_small(dcond[N_DEV].reshape(8, D // 8), "gather_dcond").reshape(N_DEV, D)

    def cctx_grad(p, cv):
        sg = jax.nn.sigmoid(cv)
        tot = ((p[0:1] + p[2:3]) + p[4:5]) + p[6:7]
        return (jnp.broadcast_to(tot * (sg * (1.0 + cv * (1.0 - sg))), p.shape),)

    g_c_ctx = _rows(cctx_grad, N_DEV, N_DEV, [_W(part_c), _W(c_ctx.reshape(1, D))], [(D, F32)], [], "c_ctx_grad")[0][0:1]

    def col_shard(g_full, width):
        return lax.dynamic_slice_in_dim(g_full, s_me * width, width, axis=g_full.ndim - 1)

    g_w_dec_f = col_shard(g_wdf_pad[:GLA_LOWRANK], KEYW // N_CHIP)
    g_w_dec_b = col_shard(g_wdb_pad[GLA_LOWRANK:LR], KEYW // N_CHIP)
    g_gla_norm = col_shard(g_gn.reshape(H, DV), DV // N_CHIP)
    small_w = [c_ctx, b_ada, pre1_g, post1_g, pre2_g, post2_g, w_dec_f, b_dec_f, w_dec_b, b_dec_b, gla_norm_g, sg_ln_g, sg_ln_b, w_s, b_s]
    small_m = [m_c_ctx, m_b_ada, m_pre1_g, m_post1_g, m_pre2_g, m_post2_g, m_w_dec_f, m_b_dec_f, m_w_dec_b, m_b_dec_b, m_gla_norm_g, m_sg_ln_g, m_sg_ln_b, m_w_s, m_b_s]
    small_v = [v_c_ctx, v_b_ada, v_pre1_g, v_post1_g, v_pre2_g, v_post2_g, v_w_dec_f, v_b_dec_f, v_w_dec_b, v_b_dec_b, v_gla_norm_g, v_sg_ln_g, v_sg_ln_b, v_w_s, v_b_s]
    small_g = [g_c_ctx, g_b_ada, g_pre1, g_post1, g_pre2, g_post2, g_w_dec_f, g_bdf, g_w_dec_b, g_bdb, g_gla_norm, g_lng, g_lnb, g_ws, g_bs]
    small_g = [g.reshape(w.shape) for g, w in zip(small_g, small_w)]
    shapes_w = [w.shape for w in small_w]
    upd = _adamw(_pack(small_w), _pack(small_g), _pack(small_m), _pack(small_v), "adamw_small")
    d_small, m_small, v_small = [_unpack(u.reshape(-1), shapes_w) for u in upd]

    def big(w, g, m, v, name):
        shp = w.shape
        res = _adamw(w.reshape(shp[-2:]), g.reshape(shp[-2:]), m.reshape(shp[-2:]), v.reshape(shp[-2:]), name)
        return [g.reshape(shp)] + [r.reshape(shp) for r in res]

    r_in = big(w_in, g_w_in, m_w_in, v_w_in, "adamw_w_in")
    r_o = big(w_o, g_w_o, m_w_o, v_w_o, "adamw_w_o")
    r_1 = big(w_1, g_w_1, m_w_1, v_w_1, "adamw_w_1")
    r_2 = big(w_2, g_w_2, m_w_2, v_w_2, "adamw_w_2")
    r_ada = [t.reshape(w_ada.shape) for t in (g_w_ada, d_w_ada, nm_w_ada, nv_w_ada)]

    def ordered(k):
        sm = [small_g, d_small, m_small, v_small][k]
        return [sm[0], r_ada[k], *sm[1:6], r_in[k], *sm[6:15], r_o[k], r_1[k], r_2[k]]

    return (loss, grad_x.reshape(x.shape), *ordered(0), *ordered(1), *ordered(2), *ordered(3))
```

```python
import functools
import math

import jax
import jax.numpy as jnp
from jax import lax
from jax.experimental import pallas as pl
from jax.experimental.pallas import tpu as pltpu

F32 = jnp.float32
BF16 = jnp.bfloat16
MESH = pl.DeviceIdType.MESH
ANY = pl.BlockSpec(memory_space=pl.ANY)

GLA_HEADS = 8
GLA_CHUNK = 64
GLA_LOWRANK = 16
GLA_TAU = 16.0
ROPE_BASE = 10000.0
GRID_W = 64
SG_GROUPS = 4
SG_CHUNK = 128
N_MOD = 6
EPS = 1e-6
ADAM_LR = 0.001
ADAM_B1 = 0.9
ADAM_B2 = 0.999
ADAM_EPS = 1e-08
ADAM_WD = 0.01
ADAM_STEP = 10

LANES = 128
VMEM_LIMIT = 56 << 20
N_DEV = 8
N_CHIP = 4


def _params(sem=None):
    return pltpu.CompilerParams(dimension_semantics=sem, vmem_limit_bytes=VMEM_LIMIT)


def _pick(dim, target, unit=LANES):
    best = None
    for t in range(unit, min(dim, target) + 1, unit):
        if dim % t == 0:
            best = t
    return dim if best is None else best


def _dg(a, b, ca, cb, precision=None):
    return lax.dot_general(a, b, (((ca,), (cb,)), ((), ())), preferred_element_type=F32,
                           precision=precision)


def _place():
    return lax.axis_index("x"), lax.axis_index("y"), lax.axis_index("c")


def _allgather_small(v, name):
    m_per, n = v.shape

    def body(x_ref, out_ref, send_sems, recv_sems, local_sem):
        x, y, c = _place()
        me, sibling = (x, y, c), (x, y, 1 - c)
        chips = [(1 - x, y), (x, 1 - y), (1 - x, 1 - y)]

        def rows(px, py, pc):
            return out_ref.at[pl.ds((4 * px + 2 * py + pc) * m_per, m_per), :]

        def copy(k, block, to, src=None):
            return pltpu.make_async_remote_copy(
                src_ref=rows(*block) if src is None else src, dst_ref=rows(*block),
                send_sem=send_sems.at[k], recv_sem=recv_sems.at[k],
                device_id=to, device_id_type=MESH)

        mine = pltpu.make_async_copy(x_ref, rows(*me), local_sem)
        mine.start()
        first = [copy(0, me, sibling, src=x_ref)]
        first += [copy(1 + j, me, (*chip, c), src=x_ref) for j, chip in enumerate(chips)]
        for cp in first:
            cp.start()
        passed = [copy(4 + j, (*chip, c), sibling) for j, chip in enumerate(chips)]
        for j, chip in enumerate(chips):
            copy(1 + j, (*chip, c), me).wait_recv()
            passed[j].start()
        copy(0, sibling, me).wait_recv()
        for j, chip in enumerate(chips):
            copy(4 + j, (*chip, 1 - c), me).wait_recv()
        for cp in first + passed:
            cp.wait_send()
        mine.wait()

    return pl.pallas_call(
        body, name=name,
        out_shape=jax.ShapeDtypeStruct((N_DEV * m_per, n), v.dtype),
        in_specs=[pl.BlockSpec(memory_space=pltpu.VMEM)],
        out_specs=pl.BlockSpec(memory_space=pltpu.VMEM),
        scratch_shapes=[pltpu.SemaphoreType.DMA((7,)), pltpu.SemaphoreType.DMA((7,)),
                        pltpu.SemaphoreType.DMA],
        compiler_params=pltpu.CompilerParams(vmem_limit_bytes=VMEM_LIMIT),
    )(v)


def _cast_blocks(w, s_me, name):
    _, r, cols = w.shape
    tr = _row_tile(r, cols, 4)

    def body(s_ref, w_ref, o_ref):
        o_ref[...] = w_ref[...].astype(BF16)

    return pl.pallas_call(
        body, name=name,
        out_shape=jax.ShapeDtypeStruct((N_DEV, r, cols), BF16),
        grid_spec=pltpu.PrefetchScalarGridSpec(
            num_scalar_prefetch=1, grid=(2, r // tr),
            in_specs=[pl.BlockSpec((None, tr, cols), lambda h, i, s: (h, i, 0))],
            out_specs=pl.BlockSpec((None, tr, cols), lambda h, i, s: (2 * s[0] + h, i, 0))),
        compiler_params=_params(("arbitrary", "arbitrary")),
    )(s_me, w)


def _gather_big(ws, name):
    nw = len(ws)

    def body(*refs):
        outs = refs[nw:2 * nw]
        send_sems, recv_sems = refs[2 * nw:]
        x, y, c = _place()
        me, sibling = (x, y, c), (x, y, 1 - c)
        chips = [(1 - x, y), (x, 1 - y), (1 - x, 1 - y)]

        def blk(px, py, pc):
            return 4 * px + 2 * py + pc

        def copy(w, k, block, to):
            return pltpu.make_async_remote_copy(
                src_ref=outs[w].at[block], dst_ref=outs[w].at[block],
                send_sem=send_sems.at[6 * w + k], recv_sem=recv_sems.at[6 * w + k],
                device_id=to, device_id_type=MESH)

        first = []
        for w in range(nw):
            for j, chip in enumerate(chips):
                cp = copy(w, j, blk(x, y, c), (*chip, c))
                cp.start()
                first.append(cp)
        passed = []
        for w in range(nw):
            for j, chip in enumerate(chips):
                copy(w, j, blk(*chip, c), me).wait_recv()
                cp = copy(w, 3 + j, blk(*chip, c), sibling)
                cp.start()
                passed.append(cp)
        for w in range(nw):
            for j, chip in enumerate(chips):
                copy(w, 3 + j, blk(*chip, 1 - c), me).wait_recv()
        for cp in first + passed:
            cp.wait_send()

    return pl.pallas_call(
        body, name=name,
        out_shape=[jax.ShapeDtypeStruct(w.shape, w.dtype) for w in ws],
        in_specs=[ANY] * nw, out_specs=[ANY] * nw,
        input_output_aliases={w: w for w in range(nw)},
        scratch_shapes=[pltpu.SemaphoreType.DMA((6 * nw,)), pltpu.SemaphoreType.DMA((6 * nw,))],
    )(*ws)


def _rs_sibling(gs, name):
    nw = len(gs)

    def body(*refs):
        ins, outs = refs[:nw], refs[nw:2 * nw]
        send_sems, recv_sems = refs[2 * nw:]
        x, y, c = _place()
        cps = []
        for w in range(nw):
            for s in range(N_CHIP):
                cp = pltpu.make_async_remote_copy(
                    src_ref=ins[w].at[2 * s + (1 - c)], dst_ref=outs[w].at[s],
                    send_sem=send_sems.at[N_CHIP * w + s], recv_sem=recv_sems.at[N_CHIP * w + s],
                    device_id=(x, y, 1 - c), device_id_type=MESH)
                cp.start()
                cps.append(cp)
        for cp in cps:
            cp.wait()

    return pl.pallas_call(
        body, name=name,
        out_shape=[jax.ShapeDtypeStruct((N_CHIP,) + g.shape[1:], g.dtype) for g in gs],
        in_specs=[ANY] * nw, out_specs=[ANY] * nw,
        scratch_shapes=[pltpu.SemaphoreType.DMA((N_CHIP * nw,)), pltpu.SemaphoreType.DMA((N_CHIP * nw,))],
    )(*gs)


def _rs_chips(ps, name):
    nw = len(ps)

    def body(*refs):
        ins, outs = refs[:nw], refs[nw:2 * nw]
        send_sems, recv_sems = refs[2 * nw:]
        x, y, c = _place()
        chips = [(1 - x, y), (x, 1 - y), (1 - x, 1 - y)]
        cps = []
        for w in range(nw):
            for j, chip in enumerate(chips):
                cp = pltpu.make_async_remote_copy(
                    src_ref=ins[w].at[2 * chip[0] + chip[1]], dst_ref=outs[w].at[j],
                    send_sem=send_sems.at[3 * w + j], recv_sem=recv_sems.at[3 * w + j],
                    device_id=(*chip, c), device_id_type=MESH)
                cp.start()
                cps.append(cp)
        for cp in cps:
            cp.wait()

    return pl.pallas_call(
        body, name=name,
        out_shape=[jax.ShapeDtypeStruct((3,) + p.shape[1:], p.dtype) for p in ps],
        in_specs=[ANY] * nw, out_specs=[ANY] * nw,
        scratch_shapes=[pltpu.SemaphoreType.DMA((3 * nw,)), pltpu.SemaphoreType.DMA((3 * nw,))],
    )(*ps)


def _rs_final(fs, name):
    nw = len(fs)

    def body(*refs):
        outs = refs[nw:2 * nw]
        send_sems, recv_sems = refs[2 * nw:]
        x, y, c = _place()
        cps = []
        for w in range(nw):
            cp = pltpu.make_async_remote_copy(
                src_ref=outs[w].at[c], dst_ref=outs[w].at[c],
                send_sem=send_sems.at[w], recv_sem=recv_sems.at[w],
                device_id=(x, y, 1 - c), device_id_type=MESH)
            cp.start()
            cps.append(cp)
        for cp in cps:
            cp.wait()

    return pl.pallas_call(
        body, name=name,
        out_shape=[jax.ShapeDtypeStruct(f.shape, f.dtype) for f in fs],
        in_specs=[ANY] * nw, out_specs=[ANY] * nw,
        input_output_aliases={w: w for w in range(nw)},
        scratch_shapes=[pltpu.SemaphoreType.DMA((nw,)), pltpu.SemaphoreType.DMA((nw,))],
    )(*fs)


_PHASE_COPIES = {"gather_ici": 3, "gather_d2d": 3, "rs_sibling": N_CHIP, "rs_chips": 3}
_PHASE_IN_PLACE = ("gather_ici", "gather_d2d")


def _phase_out(kind, arr):
    lead = {"rs_sibling": N_CHIP, "rs_chips": 3}.get(kind, arr.shape[0])
    return jax.ShapeDtypeStruct((lead,) + arr.shape[1:], arr.dtype)


def _phase_copies(kind, src, dst, send_sems, recv_sems, base):
    x, y, c = _place()
    sibling = (x, y, 1 - c)
    chips = [(1 - x, y), (x, 1 - y), (1 - x, 1 - y)]
    if kind == "gather_ici":
        mine = dst.at[4 * x + 2 * y + c]
        trips = [(mine, mine, (*chip, c)) for chip in chips]
    elif kind == "gather_d2d":
        trips = [(dst.at[4 * chip[0] + 2 * chip[1] + c], dst.at[4 * chip[0] + 2 * chip[1] + c], sibling) for chip in chips]
    elif kind == "rs_sibling":
        trips = [(src.at[2 * s + (1 - c)], dst.at[s], sibling) for s in range(N_CHIP)]
    else:
        trips = [(src.at[2 * chip[0] + chip[1]], dst.at[j], (*chip, c)) for j, chip in enumerate(chips)]
    return [pltpu.make_async_remote_copy(src_ref=a, dst_ref=b, send_sem=send_sems.at[base + k], recv_sem=recv_sems.at[base + k],
                                         device_id=dev, device_id_type=MESH) for k, (a, b, dev) in enumerate(trips)]


def _side_call(inner, grid, in_specs, out_specs, out_shape, scratch, args, phases, name, semantics):
    n_in, n_out, n_ph = len(in_specs), len(out_specs), len(phases)
    if n_ph == 0:
        outs = pl.pallas_call(inner, name=name, grid=grid, in_specs=in_specs, out_specs=out_specs, out_shape=out_shape,
                              scratch_shapes=scratch, compiler_params=_params(semantics))(*args)
        return list(outs), []
    n_cp = sum(_PHASE_COPIES[k] for k, _ in phases)

    def body(*refs):
        b_in, s_in = refs[:n_in], refs[n_in:n_in + n_ph]
        b_out, s_out = refs[n_in + n_ph:n_in + n_ph + n_out], refs[n_in + n_ph + n_out:n_in + 2 * n_ph + n_out]
        rest = refs[n_in + 2 * n_ph + n_out:]
        send_sems, recv_sems = rest[-2:]
        first = functools.reduce(jnp.logical_and, [pl.program_id(a) == 0 for a in range(len(grid))])
        last = functools.reduce(jnp.logical_and, [pl.program_id(a) == g - 1 for a, g in enumerate(grid)])

        def copies():
            out, base = [], 0
            for (kind, _), si, so in zip(phases, s_in, s_out):
                out += _phase_copies(kind, si, so, send_sems, recv_sems, base)
                base += _PHASE_COPIES[kind]
            return out

        @pl.when(first)
        def _():
            for cp in copies():
                cp.start()

        inner(*b_in, *b_out, *rest[:-2])

        @pl.when(last)
        def _():
            for cp in copies():
                cp.wait()

    outs = pl.pallas_call(
        body, name=name, grid=grid,
        in_specs=list(in_specs) + [ANY] * n_ph, out_specs=list(out_specs) + [ANY] * n_ph,
        out_shape=list(out_shape) + [_phase_out(k, a) for k, a in phases],
        input_output_aliases={n_in + p: n_out + p for p, (k, _) in enumerate(phases) if k in _PHASE_IN_PLACE},
        scratch_shapes=list(scratch) + [pltpu.SemaphoreType.DMA((n_cp,)), pltpu.SemaphoreType.DMA((n_cp,))],
        compiler_params=_params(("arbitrary",) * len(grid)),
    )(*args, *[a for _, a in phases])
    return list(outs[:n_out]), list(outs[n_out:])


def _row_tile(r, cols, itemsize):
    t = r
    while t * cols * itemsize > (2 << 20) and t % 16 == 0:
        t //= 2
    return t


def _sum_sibling(g, r1, c_me, name):
    _, r, cols = g.shape
    tr = _row_tile(r, cols, 4)

    def body(c_ref, g_ref, r_ref, o_ref):
        o_ref[...] = (g_ref[...].astype(F32) + r_ref[...].astype(F32)).astype(o_ref.dtype)

    return pl.pallas_call(
        body, name=name,
        out_shape=jax.ShapeDtypeStruct((N_CHIP, r, cols), g.dtype),
        grid_spec=pltpu.PrefetchScalarGridSpec(
            num_scalar_prefetch=1, grid=(N_CHIP, r // tr),
            in_specs=[pl.BlockSpec((None, tr, cols), lambda s, i, c: (2 * s + c[0], i, 0)),
                      pl.BlockSpec((None, tr, cols), lambda s, i, c: (s, i, 0))],
            out_specs=pl.BlockSpec((None, tr, cols), lambda s, i, c: (s, i, 0))),
        compiler_params=_params(("arbitrary", "arbitrary")),
    )(c_me, g, r1)


def _sum_chips(p, r2, sc_me, name):
    _, r, cols = p.shape
    tr = _row_tile(r, cols, 4)

    def body(s_ref, p_ref, a_ref, b_ref, c_ref, o_ref):
        o_ref[...] = ((p_ref[...].astype(F32) + a_ref[...].astype(F32)) + b_ref[...].astype(F32)) + c_ref[...].astype(F32)

    return pl.pallas_call(
        body, name=name,
        out_shape=jax.ShapeDtypeStruct((2, r, cols), F32),
        grid_spec=pltpu.PrefetchScalarGridSpec(
            num_scalar_prefetch=1, grid=(r // tr,),
            in_specs=[pl.BlockSpec((None, tr, cols), lambda i, s: (s[0], i, 0)),
                      pl.BlockSpec((None, tr, cols), lambda i, s: (0, i, 0)),
                      pl.BlockSpec((None, tr, cols), lambda i, s: (1, i, 0)),
                      pl.BlockSpec((None, tr, cols), lambda i, s: (2, i, 0))],
            out_specs=pl.BlockSpec((None, tr, cols), lambda i, s: (s[1], i, 0))),
        compiler_params=_params(("arbitrary",)),
    )(sc_me, p, r2, r2, r2)


def _shard_columns(g_al, shard_cols, bound, off_lo, off_hi, rh, name):
    d, acols = g_al.shape
    wp = -(-shard_cols // LANES) * LANES
    tr = min(LANES, rh)
    nt = acols // LANES

    def body(x_ref, o_ref):
        s = pl.program_id(0)
        lane = lax.broadcasted_iota(jnp.int32, (tr, LANES), 1)

        def tile(q):
            q = max(0, min(nt - 1, q))
            return x_ref[:, q * LANES:(q + 1) * LANES].astype(F32)

        def read(start):
            q, sh = divmod(start, LANES)
            if sh == 0:
                return tile(q)
            return jnp.where(lane < LANES - sh, pltpu.roll(tile(q), LANES - sh, 1), pltpu.roll(tile(q + 1), LANES - sh, 1))

        for k in range(N_CHIP):
            @pl.when(s == k)
            def _(k=k):
                for t in range(wp // LANES):
                    n0 = k * shard_cols + t * LANES
                    if n0 + LANES <= bound:
                        v = read(n0 + off_lo)
                    elif n0 >= bound:
                        v = read(n0 + off_hi)
                    else:
                        v = jnp.where(lane < bound - n0, read(n0 + off_lo), read(n0 + off_hi))
                    o_ref[:, t * LANES:(t + 1) * LANES] = v.astype(o_ref.dtype)

    return pl.pallas_call(
        body, name=name, grid=(N_CHIP, d // tr),
        out_shape=jax.ShapeDtypeStruct((N_DEV, rh, wp), BF16),
        in_specs=[pl.BlockSpec((tr, acols), lambda s, i: (i, 0))],
        out_specs=pl.BlockSpec((None, tr, wp), lambda s, i: (2 * s + (i * tr) // rh, ((i * tr) % rh) // tr, 0)),
        compiler_params=_params(("arbitrary", "arbitrary")),
    )(g_al)


def _mm(a, b, M, N, K, *, name, ta=False, tb=False, out_dtypes=(F32,), tm=1024, tn=1024, tk=2048,
        a_spec=None, b_spec=None, out_specs=None, out_shapes=None, epi=None, epi_in=(), epi_specs=(), side=()):
    tm, tn, tk = min(tm, M), min(tn, N), min(tk, K)
    assert M % tm == 0 and N % tn == 0 and K % tk == 0, (name, M, N, K, tm, tn, tk)
    nk = K // tk
    n_epi, n_out = len(epi_in), len(out_dtypes)
    if a_spec is None:
        a_spec = pl.BlockSpec((tk, tm), lambda i, j, k: (k, i)) if ta else pl.BlockSpec((tm, tk), lambda i, j, k: (i, k))
    if b_spec is None:
        b_spec = pl.BlockSpec((tn, tk), lambda i, j, k: (j, k)) if tb else pl.BlockSpec((tk, tn), lambda i, j, k: (k, j))
    if out_specs is None:
        out_specs = [pl.BlockSpec((tm, tn), lambda i, j, k: (i, j))] * n_out
        out_shapes = [jax.ShapeDtypeStruct((M, N), dt) for dt in out_dtypes]

    def body(a_ref, b_ref, *rest):
        epi_refs, o_refs, acc = rest[:n_epi], rest[n_epi:n_epi + n_out], rest[-1]
        k = pl.program_id(2)

        @pl.when(k == 0)
        def _():
            acc[...] = jnp.zeros_like(acc)

        acc[...] += _dg(a_ref[...].astype(BF16), b_ref[...].astype(BF16), 0 if ta else 1, 1 if tb else 0)

        @pl.when(k == nk - 1)
        def _():
            r = acc[...]
            vals = (r,) if epi is None else epi(r, *[e[...] for e in epi_refs])
            for o_ref, v in zip(o_refs, vals):
                o_ref[...] = v.astype(o_ref.dtype)

    outs, side_outs = _side_call(body, (M // tm, N // tn, nk), [a_spec, b_spec, *epi_specs], out_specs, out_shapes,
                                 [pltpu.VMEM((tm, tn), F32)], (a, b, *epi_in), list(side), name,
                                 ("parallel", "parallel", "arbitrary"))
    if side:
        return outs + side_outs
    return outs[0] if n_out == 1 else outs


def _T(arr, width, col=0, lead=None):
    return ("tile", arr, width, col, lead)


def _W(arr):
    return ("whole", arr)


def _rows(fn, n_rows, tr, ins, tile_outs, acc_outs, name):
    tr = min(tr, n_rows)
    assert n_rows % tr == 0, (name, n_rows, tr)
    in_specs, args = [], []
    for d in ins:
        if d[0] == "tile":
            _, arr, width, col, lead = d
            if lead is None:
                in_specs.append(pl.BlockSpec((tr, width), lambda i, col=col: (i, col)))
            else:
                in_specs.append(pl.BlockSpec((None, tr, width), lambda i, col=col, lead=lead: (lead, i, col)))
            args.append(arr)
        else:
            arr = d[1]
            in_specs.append(pl.BlockSpec(arr.shape, lambda i, nd=arr.ndim: (0,) * nd))
            args.append(arr)
    n_in, n_t = len(ins), len(tile_outs)
    out_shape = [jax.ShapeDtypeStruct((n_rows, w), dt) for w, dt in tile_outs]
    out_specs = [pl.BlockSpec((tr, w), lambda i: (i, 0)) for w, _ in tile_outs]
    out_shape += [jax.ShapeDtypeStruct(s, F32) for s in acc_outs]
    out_specs += [pl.BlockSpec(s, lambda i, nd=len(s): (0,) * nd) for s in acc_outs]

    def body(*refs):
        in_refs, t_refs, a_refs = refs[:n_in], refs[n_in:n_in + n_t], refs[n_in + n_t:]
        vals = fn(*[r[...] for r in in_refs])
        for r, v in zip(t_refs, vals[:n_t]):
            r[...] = v.astype(r.dtype)
        first = pl.program_id(0) == 0
        for r, v in zip(a_refs, vals[n_t:]):
            @pl.when(first)
            def _(r=r, v=v):
                r[...] = v

            @pl.when(jnp.logical_not(first))
            def _(r=r, v=v):
                r[...] += v

    return pl.pallas_call(
        body, name=name, out_shape=out_shape, grid=(n_rows // tr,),
        in_specs=in_specs, out_specs=out_specs,
        compiler_params=_params(("arbitrary",)),
    )(*args)


def _colsum(t):
    return jnp.sum(t, axis=0, keepdims=True)


def _prenorm(x, g, sc, sh):
    xf = x.astype(F32)
    return xf * lax.rsqrt(jnp.mean(xf * xf, axis=-1, keepdims=True) + EPS) * g * (1.0 + sc) + sh


def _postnorm(x, y, gate, pg):
    return x + gate * (y * lax.rsqrt(jnp.mean(y * y, axis=-1, keepdims=True) + EPS) * pg)


def _gelu(t):
    return 0.5 * t * (1.0 + lax.erf(t * (2.0 ** -0.5)))


def _sg_pre(zu, zv, lng, lnb):
    u, vr = _gelu(zu), _gelu(zv)
    mu = jnp.mean(vr, axis=-1, keepdims=True)
    var = jnp.mean(jnp.square(vr - mu), axis=-1, keepdims=True)
    return u, (vr - mu) * lax.rsqrt(var + EPS) * lng + lnb


def _sg_mix(vv, ws_ref_vals, bs_vals, gw):
    parts = []
    for g in range(SG_GROUPS):
        s = _dg(ws_ref_vals[g].astype(BF16), vv[:, g * gw:(g + 1) * gw].astype(BF16), 1, 0)
        parts.append(s + bs_vals[g])
    return jnp.concatenate(parts, axis=1)


def _readout(o, r, g, heads, dv):
    parts = []
    for h in range(heads):
        oh = o[:, h * dv:(h + 1) * dv]
        parts.append(oh * lax.rsqrt(jnp.mean(oh * oh, axis=-1, keepdims=True) + EPS))
    return jnp.concatenate(parts, axis=1) * g * (r * jax.nn.sigmoid(r))


def _log_sigmoid(a):
    return jnp.minimum(a, 0.0) - jnp.log(1.0 + jnp.exp(-jnp.abs(a)))


def _rope_swap(t, m):
    lane = lax.broadcasted_iota(jnp.int32, t.shape, 1)
    return jnp.where((lane % (2 * m)) < m, pltpu.roll(t, 3 * m, 1), pltpu.roll(t, m, 1))


def _rope(t, cos, sin, heads, dk):
    parts = []
    for h in range(heads):
        th = t[:, h * dk:(h + 1) * dk]
        parts.append(th * cos + _rope_swap(th, dk // 4) * sin)
    return jnp.concatenate(parts, axis=1)


def _rope_t(dt, cos, sin, heads, dk):
    parts = []
    for h in range(heads):
        dh = dt[:, h * dk:(h + 1) * dk]
        parts.append(dh * cos + _rope_swap(dh * sin, dk // 4))
    return jnp.concatenate(parts, axis=1)


def _chunk_terms(d, qv, kv, lav, C):
    row = lax.broadcasted_iota(jnp.int32, (C, C), 0)
    col = lax.broadcasted_iota(jnp.int32, (C, C), 1)
    tri = row >= col if d == 0 else row <= col
    b = _dg(tri.astype(F32), lav, 1, 0, precision=lax.Precision.HIGHEST)
    btot = _colsum(lav)
    eb, enb, ebt = jnp.exp(b), jnp.exp(-b), jnp.exp(btot - b)
    return tri, btot, eb, enb, ebt, qv * eb, kv * enb, kv * ebt


def _gla_fwd(q, k, zv, v_col0, la, st0, heads, dk, dv, name, side=()):
    n, C, H = q.shape[0], GLA_CHUNK, heads
    nc = n // C

    def body(qf, kf, vf, laf, qb, kb, vb_, lab, st0_ref, of_ref, ob_ref, sf_ref, sb_ref, fin_ref, st):
        i = pl.program_id(1)

        @pl.when(i == 0)
        def _():
            st[...] = st0_ref[...]

        for d, (q_ref, k_ref, v_ref, la_ref, o_ref, save_ref) in enumerate(((qf, kf, vf, laf, of_ref, sf_ref), (qb, kb, vb_, lab, ob_ref, sb_ref))):
            tri, btot, _, _, _, qt, kt, kh = _chunk_terms(d, q_ref[...], k_ref[...], la_ref[...], C)
            s = st[d]
            vb = v_ref[...].astype(BF16)
            qtb = qt.astype(BF16)
            att = jnp.where(tri, _dg(qtb, kt.astype(BF16), 1, 1), 0.0)
            o_ref[...] = _dg(qtb, s.astype(BF16), 1, 1) + _dg(att.astype(BF16), vb, 1, 0)
            save_ref[...] = s
            s_new = s * jnp.exp(btot) + _dg(vb, kh.astype(BF16), 0, 0)
            st[d] = s_new

            @pl.when(i == nc - 1)
            def _(d=d, s_new=s_new):
                fin_ref[d] = s_new

    def seq(width, col0, rev, dir_cols=0):
        if rev:
            return pl.BlockSpec((C, width), lambda h, i: (nc - 1 - i, col0 + dir_cols + h))
        return pl.BlockSpec((C, width), lambda h, i: (i, col0 + h))

    both = pl.BlockSpec((2, None, dv, dk), lambda h, i: (0, h, 0, 0))
    outs, side_outs = _side_call(
        body, (H, nc),
        [seq(dk, 0, False), seq(dk, 0, False), seq(dv, v_col0, False), seq(dk, 0, False),
         seq(dk, 0, True), seq(dk, 0, True), seq(dv, v_col0, True), seq(dk, 0, True, H), both],
        [seq(dv, 0, False), seq(dv, 0, True),
         pl.BlockSpec((None, None, dv, dk), lambda h, i: (h, i, 0, 0)),
         pl.BlockSpec((None, None, dv, dk), lambda h, i: (h, nc - 1 - i, 0, 0)), both],
        [jax.ShapeDtypeStruct((n, H * dv), F32)] * 2 + [jax.ShapeDtypeStruct((H, nc, dv, dk), F32)] * 2
        + [jax.ShapeDtypeStruct((2, H, dv, dk), F32)],
        [pltpu.VMEM((2, dv, dk), F32)], (q, k, zv, la, q, k, zv, la, st0), list(side), name, ("arbitrary", "arbitrary"))
    return outs + side_outs


def _gla_bwd(q, k, zv, v_col0, la, saved_f, saved_b, do, dfin, heads, dk, dv, name, side=()):
    n, C, H = q.shape[0], GLA_CHUNK, heads
    nc = n // C

    def body(qf, kf, vf, laf, sf, dof, qb, kb, vb_, lab, sb, dob_, dfin_ref,
             dqf, dqb, dkf, dkb, dvf, dvb, dlaf, dlab, d0_ref, dst):
        i = pl.program_id(1)

        @pl.when(i == 0)
        def _():
            dst[...] = dfin_ref[...]

        dirs = ((qf, kf, vf, laf, sf, dof, dqf, dkf, dvf, dlaf), (qb, kb, vb_, lab, sb, dob_, dqb, dkb, dvb, dlab))
        for d, (q_ref, k_ref, v_ref, la_ref, save_ref, do_ref, dq_ref, dk_ref, dv_ref, dla_ref) in enumerate(dirs):
            tri, btot, eb, enb, ebt, qt, kt, kh = _chunk_terms(d, q_ref[...], k_ref[...], la_ref[...], C)
            s, dsn = save_ref[...], dst[d]
            vb, dob = v_ref[...].astype(BF16), do_ref[...].astype(BF16)
            qtb, ktb, khb, dsnb = qt.astype(BF16), kt.astype(BF16), kh.astype(BF16), dsn.astype(BF16)
            att = jnp.where(tri, _dg(qtb, ktb, 1, 1), 0.0).astype(BF16)
            datt = jnp.where(tri, _dg(dob, vb, 1, 1), 0.0).astype(BF16)
            dqt = _dg(dob, s.astype(BF16), 1, 0) + _dg(datt, ktb, 1, 0)
            dkt = _dg(datt, qtb, 0, 0)
            dkh = _dg(vb, dsnb, 1, 0)
            dv_ref[...] = _dg(att, dob, 0, 0) + _dg(khb, dsnb, 1, 1)
            ebtot = jnp.exp(btot)
            dbtot = ebtot * _colsum(s * dsn) + _colsum(dkh * kh)
            s0 = dsn * ebtot + _dg(dob, qtb, 0, 0)
            dst[d] = s0
            db = dqt * qt - dkt * kt - dkh * kh
            dq_ref[...] = dqt * eb
            dk_ref[...] = dkt * enb + dkh * ebt
            row = lax.broadcasted_iota(jnp.int32, (C, C), 0)
            col = lax.broadcasted_iota(jnp.int32, (C, C), 1)
            tri_t = (col >= row if d == 0 else col <= row).astype(F32)
            dla_ref[...] = _dg(tri_t, db, 1, 0, precision=lax.Precision.HIGHEST) + dbtot

            @pl.when(i == nc - 1)
            def _(d=d, s0=s0):
                d0_ref[d] = s0

    def seq(width, col0, fwd_dir, dir_cols=0):
        if fwd_dir:
            return pl.BlockSpec((C, width), lambda h, i: (nc - 1 - i, col0 + h))
        return pl.BlockSpec((C, width), lambda h, i: (i, col0 + dir_cols + h))

    both = pl.BlockSpec((2, None, dv, dk), lambda h, i: (0, h, 0, 0))
    sav_f = pl.BlockSpec((None, None, dv, dk), lambda h, i: (h, nc - 1 - i, 0, 0))
    sav_b = pl.BlockSpec((None, None, dv, dk), lambda h, i: (h, i, 0, 0))
    outs, side_outs = _side_call(
        body, (H, nc),
        [seq(dk, 0, True), seq(dk, 0, True), seq(dv, v_col0, True), seq(dk, 0, True), sav_f, seq(dv, 0, True),
         seq(dk, 0, False), seq(dk, 0, False), seq(dv, v_col0, False), seq(dk, 0, False, H), sav_b, seq(dv, 0, False), both],
        [seq(dk, 0, True), seq(dk, 0, False), seq(dk, 0, True), seq(dk, 0, False), seq(dv, 0, True), seq(dv, 0, False),
         seq(dk, 0, True), seq(dk, 0, False), both],
        [jax.ShapeDtypeStruct((n, H * dk), F32)] * 4 + [jax.ShapeDtypeStruct((n, H * dv), F32)] * 2
        + [jax.ShapeDtypeStruct((n, H * dk), F32)] * 2 + [jax.ShapeDtypeStruct((2, H, dv, dk), F32)],
        [pltpu.VMEM((2, dv, dk), F32)], (q, k, zv, la, saved_f, do, q, k, zv, la, saved_b, do, dfin), list(side), name,
        ("arbitrary", "arbitrary"))
    return outs + side_outs


def _adamw_math(w, g, m, v):
    m2 = ADAM_B1 * m + (1.0 - ADAM_B1) * g
    v2 = ADAM_B2 * v + (1.0 - ADAM_B2) * jnp.square(g)
    m_hat = m2 / (1.0 - ADAM_B1 ** ADAM_STEP)
    v_hat = v2 / (1.0 - ADAM_B2 ** ADAM_STEP)
    delta = -ADAM_LR * (m_hat / (jnp.sqrt(v_hat) + ADAM_EPS) + ADAM_WD * w)
    return delta, m2, v2


def _adamw(w, g, m, v, name):
    r, cols = w.shape
    tr = _row_tile(r, cols, 4 * 4)
    spec = pl.BlockSpec((tr, cols), lambda i: (i, 0))

    def body(w_ref, g_ref, m_ref, v_ref, d_ref, m2_ref, v2_ref):
        d_ref[...], m2_ref[...], v2_ref[...] = _adamw_math(w_ref[...], g_ref[...], m_ref[...], v_ref[...])

    return pl.pallas_call(
        body, name=name, grid=(r // tr,), out_shape=[jax.ShapeDtypeStruct((r, cols), F32)] * 3,
        in_specs=[spec] * 4, out_specs=[spec] * 3, compiler_params=_params(("parallel",)),
    )(w, g, m, v)


def _ada_update(cond_t, dmod, w, m, v, name):
    r, cols = w.shape
    tr, tc = _pick(r, 512, 8), _pick(cols, 1024)
    spec = pl.BlockSpec((tr, tc), lambda i, j: (i, j))

    def body(c_ref, d_ref, w_ref, m_ref, v_ref, g_ref, dl_ref, m2_ref, v2_ref):
        g = _dg(c_ref[...].astype(BF16), d_ref[...].astype(BF16), 1, 0)
        g_ref[...] = g
        dl_ref[...], m2_ref[...], v2_ref[...] = _adamw_math(w_ref[...], g, m_ref[...], v_ref[...])

    return pl.pallas_call(
        body, name=name, grid=(r // tr, cols // tc), out_shape=[jax.ShapeDtypeStruct((r, cols), F32)] * 4,
        in_specs=[pl.BlockSpec((tr, cond_t.shape[1]), lambda i, j: (i, 0)),
                  pl.BlockSpec((dmod.shape[0], tc), lambda i, j: (0, j)), spec, spec, spec],
        out_specs=[spec] * 4, compiler_params=_params(("parallel", "parallel")),
    )(cond_t, dmod, w, m, v)


def _pack(parts, rows=8):
    flat = jnp.concatenate([p.reshape(-1).astype(F32) for p in parts])
    n = -(-flat.shape[0] // (rows * LANES)) * LANES
    return jnp.pad(flat, (0, rows * n - flat.shape[0])).reshape(rows, n)


def _unpack(flat, shapes):
    out, off = [], 0
    for s in shapes:
        size = math.prod(s)
        out.append(flat[off:off + size].reshape(s))
        off += size
    return out


def kernel(x, c, ctx, c_ctx, w_ada, b_ada, pre1_g, post1_g, pre2_g, post2_g, w_in, w_dec_f, b_dec_f, w_dec_b, b_dec_b, gla_norm_g, sg_ln_g, sg_ln_b, w_s, b_s, w_o, w_1, w_2, loss_target, m_c_ctx, m_w_ada, m_b_ada, m_pre1_g, m_post1_g, m_pre2_g, m_post2_g, m_w_in, m_w_dec_f, m_b_dec_f, m_w_dec_b, m_b_dec_b, m_gla_norm_g, m_sg_ln_g, m_sg_ln_b, m_w_s, m_b_s, m_w_o, m_w_1, m_w_2, v_c_ctx, v_w_ada, v_b_ada, v_pre1_g, v_post1_g, v_pre2_g, v_post2_g, v_w_in, v_w_dec_f, v_b_dec_f, v_w_dec_b, v_b_dec_b, v_gla_norm_g, v_sg_ln_g, v_sg_ln_b, v_w_s, v_b_s, v_w_o, v_w_1, v_w_2):
    N, D = x.shape[1], x.shape[2]
    NC = ctx.shape[1]
    H = GLA_HEADS
    VALW = D // 2
    DV = VALW // H
    DK = DV // 2
    KEYW = H * DK
    SGW = D - VALW
    GW = SGW // SG_GROUPS
    LR = 2 * GLA_LOWRANK
    F = w_1.shape[2] * N_CHIP
    FS = F // N_CHIP
    RH = D // 2
    MS = w_ada.shape[2]
    IN_COLS = w_in.shape[2] * N_CHIP
    K0, V0, R0, LF0 = KEYW, 2 * KEYW, 2 * KEYW + VALW, 2 * KEYW + 2 * VALW
    SG0 = LF0 + LR
    AQ, AK, AV, AR, ALR = 2 * SGW, 2 * SGW + KEYW, 2 * SGW + 2 * KEYW, 2 * SGW + 2 * KEYW + VALW, 2 * SGW + 2 * KEYW + 2 * VALW
    ACOLS = ALR + LANES
    assert IN_COLS == SG0 + 2 * SGW and N % SG_CHUNK == 0 and N % GLA_CHUNK == 0 and NC % GLA_CHUNK == 0

    ax, ay, ac = _place()
    s_me = (2 * ax + ay).astype(jnp.int32)
    b_me = (4 * ax + 2 * ay + ac).astype(jnp.int32)
    s_arr, c_arr = s_me.reshape(1), ac.astype(jnp.int32).reshape(1)
    sc_arr = jnp.concatenate([s_arr, c_arr])
    CS = IN_COLS // N_CHIP

    shards = [_cast_blocks(w_in[0].reshape(2, RH, CS), s_arr, "cast_w_in"), _cast_blocks(w_o[0].reshape(2, D // N_DEV, D), s_arr, "cast_w_o"),
              _cast_blocks(w_1[0].reshape(2, RH, FS), s_arr, "cast_w_1"), _cast_blocks(w_2[0].reshape(2, F // N_DEV, D), s_arr, "cast_w_2")]
    win_g, wo_g = _gather_big(shards[:2], "gather_weights")
    w1_buf, w2_buf = shards[2], shards[3]
    w_in_nat = win_g.reshape(N_CHIP, 2, RH, IN_COLS // N_CHIP).transpose(1, 2, 0, 3).reshape(D, IN_COLS)
    w_al = jnp.concatenate([w_in_nat[:, SG0:], w_in_nat[:, :LF0], w_in_nat[:, LF0:SG0],
                            jnp.zeros((D, LANES - LR), BF16)], axis=1)
    w_o_f = wo_g.reshape(D, D)

    n_dec = GLA_LOWRANK * (KEYW // N_CHIP)
    g0 = _allgather_small(_pack([c, w_dec_f, w_dec_b, gla_norm_g]), "gather_small0").reshape(N_DEV, -1)
    c_all = g0[:, :D]
    per_chip = g0[0::2]
    wdf = per_chip[:, D:D + n_dec].reshape(N_CHIP, GLA_LOWRANK, KEYW // N_CHIP).transpose(1, 0, 2).reshape(GLA_LOWRANK, KEYW)
    wdb = per_chip[:, D + n_dec:D + 2 * n_dec].reshape(N_CHIP, GLA_LOWRANK, KEYW // N_CHIP).transpose(1, 0, 2).reshape(GLA_LOWRANK, KEYW)
    gn_full = per_chip[:, D + 2 * n_dec:D + 2 * n_dec + H * (DV // N_CHIP)].reshape(N_CHIP, H, DV // N_CHIP).transpose(1, 0, 2).reshape(1, VALW)
    wd_f = jnp.zeros((LANES, KEYW), F32).at[:GLA_LOWRANK].set(wdf)
    wd_b = jnp.zeros((LANES, KEYW), F32).at[GLA_LOWRANK:LR].set(wdb)

    cond_in = jnp.zeros((16, D), F32).at[:N_DEV].set(c_all).at[N_DEV].set(c_ctx)
    b_ada_sh = lax.dynamic_slice(b_ada, (0, s_me * MS), (1, MS))

    def mod_epi(r, bias):
        return (r + bias,)

    def silu_rows(t):
        return (t * jax.nn.sigmoid(t),)

    cond = _rows(silu_rows, 16, 16, [_W(cond_in)], [(D, F32)], [], "cond_silu")[0]
    mod_sh = _mm(cond, w_ada[0], 16, MS, D, name="mod_matmul", tn=512, tk=D, epi=mod_epi, epi_in=(b_ada_sh,),
                 epi_specs=(pl.BlockSpec((1, min(512, MS)), lambda i, j, k: (0, j)),))
    g1m = _allgather_small(mod_sh, "gather_mod").reshape(N_DEV, 16, MS)[0::2]
    mod_all = g1m.transpose(1, 0, 2).reshape(16, N_CHIP * MS)
    mod_me = lax.dynamic_slice(mod_all, (b_me, 0), (1, N_MOD * D))
    sh1, sc1, gt1, sh2, sc2, gt2 = [mod_me[:, i * D:(i + 1) * D] for i in range(N_MOD)]
    csh1, csc1 = mod_all[N_DEV:N_DEV + 1, :D], mod_all[N_DEV:N_DEV + 1, D:2 * D]

    mq = DK // 4
    pos = jnp.arange(N)
    inv_freq = ROPE_BASE ** (-jnp.arange(mq, dtype=F32) / mq)
    ang_r = (pos // GRID_W).astype(F32)[:, None] * inv_freq[None, :]
    ang_c = (pos % GRID_W).astype(F32)[:, None] * inv_freq[None, :]
    cos_t = jnp.concatenate([jnp.cos(ang_r), jnp.cos(ang_r), jnp.cos(ang_c), jnp.cos(ang_c)], axis=1)
    sin_t = jnp.concatenate([-jnp.sin(ang_r), jnp.sin(ang_r), -jnp.sin(ang_c), jnp.sin(ang_c)], axis=1)

    x2, tgt, ctx2 = x[0], loss_target[0], ctx[0]
    TR = 128
    qscale = DK ** -0.5

    def prenorm_fwd(xa, g, sc, sh, n_rows, name):
        return _rows(lambda xv, gv, scv, shv: (_prenorm(xv, gv, scv, shv),), n_rows, TR,
                     [_T(xa, D), _W(g), _W(sc), _W(sh)], [(D, BF16)], [], name)[0]

    hx = prenorm_fwd(x2, pre1_g, sc1, sh1, N, "prenorm1_x")
    hc = prenorm_fwd(ctx2, pre1_g, csc1, csh1, NC, "prenorm1_ctx")
    tka = _pick(ACOLS, 1152)
    z_al, w1_buf = _mm(hx, w_al, N, ACOLS, D, name="in_proj_x", tn=tka, side=[("gather_ici", w1_buf)])
    zc_al = _mm(hc, w_al, NC, ACOLS, D, name="in_proj_ctx", tn=tka)

    def decay(lr, wdf_v, wdb_v, bf_v, bb_v):
        lrb = lr.astype(BF16)
        a_f = _dg(lrb, wdf_v.astype(BF16), 1, 0) + bf_v
        a_b = _dg(lrb, wdb_v.astype(BF16), 1, 0) + bb_v
        return a_f, a_b

    def prep_x(zq, zk, lr, cs, sn, wdf_v, wdb_v, bf_v, bb_v):
        a_f, a_b = decay(lr, wdf_v, wdb_v, bf_v, bb_v)
        la = jnp.concatenate([_log_sigmoid(a_f), _log_sigmoid(a_b)], axis=1) / GLA_TAU
        return _rope(zq * qscale, cs, sn, H, DK), _rope(zk, cs, sn, H, DK), la

    def prep_c(zk, lr, wdf_v, wdb_v, bf_v, bb_v):
        a_f, a_b = decay(lr, wdf_v, wdb_v, bf_v, bb_v)
        return zk, jnp.concatenate([_log_sigmoid(a_f), _log_sigmoid(a_b)], axis=1) / GLA_TAU

    dec_w = [_W(wd_f), _W(wd_b), _W(b_dec_f), _W(b_dec_b)]
    q_r, k_r, la_x = _rows(prep_x, N, TR, [_T(z_al, KEYW, AQ // KEYW), _T(z_al, KEYW, AK // KEYW), _T(z_al, LANES, ALR // LANES),
                                           _T(cos_t, DK), _T(sin_t, DK)] + dec_w,
                           [(KEYW, F32), (KEYW, F32), (2 * KEYW, F32)], [], "gla_prep_x")
    k_c, la_c = _rows(prep_c, NC, TR, [_T(zc_al, KEYW, AK // KEYW), _T(zc_al, LANES, ALR // LANES)] + dec_w,
                      [(KEYW, F32), (2 * KEYW, F32)], [], "gla_prep_ctx")

    zero_state = jnp.zeros((2, H, DV, DK), F32)
    q_c = jnp.zeros((NC, KEYW), F32)
    _, _, savf_c, savb_c, st_c = _gla_fwd(q_c, k_c, zc_al, AV // DV, la_c, zero_state, H, DK, DV, "gla_fwd_ctx")
    o_f, o_b, savf_x, savb_x, _, w1_g, w2_buf = _gla_fwd(q_r, k_r, z_al, AV // DV, la_x, st_c, H, DK, DV, "gla_fwd_x",
                                                          side=[("gather_d2d", w1_buf), ("gather_ici", w2_buf)])

    def readout_fwd(of, ob, r, g):
        return (_readout(of + ob, r, g, H, DV),)

    y_gla = _rows(readout_fwd, N, TR, [_T(o_f, VALW), _T(o_b, VALW), _T(z_al, VALW, AR // VALW), _W(gn_full)],
                  [(VALW, BF16)], [], "gla_readout")[0]

    bs_col = b_s[0].reshape(SG_GROUPS, SG_CHUNK, 1)

    def sg_fwd(zu, zv, lng, lnb, ws, bs):
        u, vv = _sg_pre(zu, zv, lng, lnb)
        return (u * _sg_mix(vv, ws, bs, GW),)

    y_sg = _rows(sg_fwd, N, SG_CHUNK, [_T(z_al, SGW, 0), _T(z_al, SGW, 1), _W(sg_ln_g), _W(sg_ln_b), _W(w_s[0]), _W(bs_col)],
                 [(SGW, BF16)], [], "sg_fwd")[0]
    ycat = jnp.concatenate([y_gla, y_sg], axis=1)
    y, w2_g = _mm(ycat, w_o_f, N, D, D, name="out_proj", side=[("gather_d2d", w2_buf)])
    w_2_f = w2_g.reshape(F, D)
    x1 = _rows(lambda xv, yv, gv, pv: (_postnorm(xv, yv, gv, pv),), N, TR,
               [_T(x2, D), _T(y, D), _W(gt1), _W(post1_g)], [(D, F32)], [], "postnorm1")[0]
    h2 = prenorm_fwd(x1, pre2_g, sc2, sh2, N, "prenorm2")

    tm1, tn1, tk1 = min(1024, N), min(1024, FS), min(2048, RH)
    w1_fwd_spec = pl.BlockSpec((None, tk1, tn1), lambda i, j, k: (2 * ((j * tn1) // FS) + (k * tk1) // RH, ((k * tk1) % RH) // tk1, ((j * tn1) % FS) // tn1))

    def relu2_epi(r):
        rf = jnp.maximum(r, 0.0)
        return rf * rf, rf

    act, rf = _mm(h2, w1_g, N, F, D, name="mlp_up", out_dtypes=(BF16, BF16), tm=tm1, tn=tn1, tk=tk1, b_spec=w1_fwd_spec, epi=relu2_epi)
    y2 = _mm(act, w_2_f, N, D, F, name="mlp_down")

    def final(x1v, y2v, gv, pv, tv):
        def loss_fn(x1a, y2a, ga, pa):
            err = _postnorm(x1a, y2a, ga, pa) - tv
            return 0.5 * jnp.sum(jnp.mean(err * err, axis=-1))
        loss, grads = jax.value_and_grad(loss_fn, argnums=(0, 1, 2, 3))(x1v, y2v, gv, pv)
        return grads[0], grads[1], jnp.full((1, LANES), loss, F32), _colsum(grads[2]), _colsum(grads[3])

    dx2, dy2, loss_acc, dgt2, dpost2 = _rows(final, N, TR, [_T(x1, D), _T(y2, D), _W(gt2), _W(post2_g), _T(tgt, D)],
                                             [(D, F32), (D, BF16)], [(1, LANES), (1, D), (1, D)], "loss_postnorm2_bwd")

    df = _mm(dy2, w_2_f, N, F, D, name="mlp_down_dx", tb=True, out_dtypes=(BF16,), epi=lambda r, rfv: (r * (2.0 * rfv.astype(F32)),),
             epi_in=(rf,), epi_specs=(pl.BlockSpec((min(1024, N), min(1024, F)), lambda i, j, k: (i, j)),))
    dw2 = _mm(act, dy2, F, D, N, name="mlp_down_dw", ta=True, out_dtypes=(BF16,)).reshape(N_DEV, F // N_DEV, D)
    tnb, tkb = min(1024, D, RH), min(2048, FS)
    w1_bwd_spec = pl.BlockSpec((None, tnb, tkb), lambda i, j, k: (2 * ((k * tkb) // FS) + (j * tnb) // RH, ((j * tnb) % RH) // tnb, ((k * tkb) % FS) // tkb))
    dh2, recv1_w2 = _mm(df, w1_g, N, D, F, name="mlp_up_dx", tb=True, tn=tnb, tk=tkb, b_spec=w1_bwd_spec, side=[("rs_sibling", dw2)])
    part_w2 = _sum_sibling(dw2, recv1_w2, c_arr, "rs_sum_sibling_w_2")
    tmw, tnw = min(1024, RH), min(1024, FS)
    dw1_spec = pl.BlockSpec((None, tmw, tnw), lambda i, j, k: (2 * ((j * tnw) // FS) + (i * tmw) // RH, ((i * tmw) % RH) // tmw, ((j * tnw) % FS) // tnw))
    dw1, recv2_w2 = _mm(h2, df, D, F, N, name="mlp_up_dw", ta=True, tm=tmw, tn=tnw, out_dtypes=(BF16,), out_specs=[dw1_spec],
                        out_shapes=[jax.ShapeDtypeStruct((N_DEV, RH, FS), BF16)], side=[("rs_chips", part_w2)])

    def prenorm_bwd(xv, gv, scv, shv, dh, dres):
        _, vjp = jax.vjp(_prenorm, xv, gv, scv, shv)
        dx, dg, dsc, dsh = vjp(dh)
        return dx + dres, _colsum(dg), _colsum(dsc), _colsum(dsh)

    dx1, dpre2, dsc2, dsh2 = _rows(prenorm_bwd, N, TR, [_T(x1, D), _W(pre2_g), _W(sc2), _W(sh2), _T(dh2, D), _T(dx2, D)],
                                   [(D, F32)], [(1, D)] * 3, "prenorm2_bwd")

    def postnorm_bwd(yv, gv, pv, dxv):
        _, vjp = jax.vjp(lambda ya, ga, pa: _postnorm(0.0, ya, ga, pa), yv, gv, pv)
        dy_, dg_, dp_ = vjp(dxv)
        return dy_, _colsum(dg_), _colsum(dp_)

    dy, dgt1, dpost1 = _rows(postnorm_bwd, N, TR, [_T(y, D), _W(gt1), _W(post1_g), _T(dx1, D)], [(D, BF16)], [(1, D)] * 2, "postnorm1_bwd")
    dycat, recv1_w1 = _mm(dy, w_o_f, N, D, D, name="out_proj_dx", tb=True, side=[("rs_sibling", dw1)])
    part_w1 = _sum_sibling(dw1, recv1_w1, c_arr, "rs_sum_sibling_w_1")
    dwo = _mm(ycat, dy, D, D, N, name="out_proj_dw", ta=True, out_dtypes=(BF16,)).reshape(N_DEV, D // N_DEV, D)

    def readout_bwd(of, ob, r, g, dyv):
        _, vjp = jax.vjp(lambda o_, r_, g_: _readout(o_, r_, g_, H, DV), of + ob, r, g)
        do_, dr_, dg_ = vjp(dyv)
        return do_, dr_, _colsum(dg_)

    do_x, dz_r, dgn = _rows(readout_bwd, N, TR, [_T(o_f, VALW), _T(o_b, VALW), _T(z_al, VALW, AR // VALW), _W(gn_full), _T(dycat, VALW, 0)],
                            [(VALW, F32), (VALW, BF16)], [(1, VALW)], "gla_readout_bwd")

    def sg_bwd(zu, zv, lng, lnb, ws, bs, dyv):
        (u, vv), vjp = jax.vjp(_sg_pre, zu, zv, lng, lnb)
        s = _sg_mix(vv, ws, bs, GW)
        du, ds = dyv * s, dyv * u
        dws, dbs, dvv = [], [], []
        for g in range(SG_GROUPS):
            dsg = ds[:, g * GW:(g + 1) * GW]
            dsb = dsg.astype(BF16)
            dws.append(_dg(dsb, vv[:, g * GW:(g + 1) * GW].astype(BF16), 1, 1))
            dbs.append(jnp.sum(dsg, axis=1, keepdims=True))
            dvv.append(_dg(ws[g].astype(BF16), dsb, 0, 0))
        dzu, dzv, dlng, dlnb = vjp((du, jnp.concatenate(dvv, axis=1)))
        return jnp.concatenate([dzu, dzv], axis=1), _colsum(dlng), _colsum(dlnb), jnp.concatenate(dws, axis=0), jnp.concatenate(dbs, axis=0)

    dz_sg, dlng, dlnb, dws, dbs = _rows(sg_bwd, N, SG_CHUNK, [_T(z_al, SGW, 0), _T(z_al, SGW, 1), _W(sg_ln_g), _W(sg_ln_b), _W(w_s[0]), _W(bs_col), _T(dycat, SGW, VALW // SGW)],
                                        [(2 * SGW, BF16)], [(1, SGW), (1, SGW), (SG_GROUPS * SG_CHUNK, SG_CHUNK), (SG_GROUPS * SG_CHUNK, 1)], "sg_bwd")

    dq_f, dq_b, dk_f, dk_b, dv_f, dv_b, dla_f, dla_b, dst0, recv2_w1 = _gla_bwd(
        q_r, k_r, z_al, AV // DV, la_x, savf_x, savb_x, do_x, zero_state, H, DK, DV, "gla_bwd_x", side=[("rs_chips", part_w1)])
    _, _, dkc_f, dkc_b, dvc_f, dvc_b, dlac_f, dlac_b, _ = _gla_bwd(
        q_c, k_c, zc_al, AV // DV, la_c, savf_c, savb_c, jnp.zeros((NC, VALW), F32), dst0, H, DK, DV, "gla_bwd_ctx")

    def decay_bwd(lr, dla_f_v, dla_b_v, wdf_v, wdb_v, bf_v, bb_v):
        a_f, a_b = decay(lr, wdf_v, wdb_v, bf_v, bb_v)
        da_f = dla_f_v * jax.nn.sigmoid(-a_f) / GLA_TAU
        da_b = dla_b_v * jax.nn.sigmoid(-a_b) / GLA_TAU
        lrb, dfb, dbb = lr.astype(BF16), da_f.astype(BF16), da_b.astype(BF16)
        dlr = _dg(dfb, wdf_v.astype(BF16), 1, 1) + _dg(dbb, wdb_v.astype(BF16), 1, 1)
        return dlr, _dg(lrb, dfb, 0, 0), _dg(lrb, dbb, 0, 0), _colsum(da_f), _colsum(da_b)

    def prep_x_bwd(dq0, dq1, dk0, dk1, dv0, dv1, lr, dla0, dla1, cs, sn, wdf_v, wdb_v, bf_v, bb_v):
        dlr, dwf, dwb, dbf, dbb = decay_bwd(lr, dla0, dla1, wdf_v, wdb_v, bf_v, bb_v)
        return (_rope_t(dq0 + dq1, cs, sn, H, DK) * qscale, _rope_t(dk0 + dk1, cs, sn, H, DK), dv0 + dv1, dlr, dwf, dwb, dbf, dbb)

    def prep_c_bwd(dk0, dk1, dv0, dv1, lr, dla0, dla1, wdf_v, wdb_v, bf_v, bb_v):
        dlr, dwf, dwb, dbf, dbb = decay_bwd(lr, dla0, dla1, wdf_v, wdb_v, bf_v, bb_v)
        return dk0 + dk1, dv0 + dv1, dlr, dwf, dwb, dbf, dbb

    dec_acc = [(LANES, KEYW), (LANES, KEYW), (1, KEYW), (1, KEYW)]
    dz_q, dz_k, dz_v, dz_lr, dwdf_x, dwdb_x, dbdf_x, dbdb_x = _rows(
        prep_x_bwd, N, TR, [_T(dq_f, KEYW), _T(dq_b, KEYW), _T(dk_f, KEYW), _T(dk_b, KEYW), _T(dv_f, VALW), _T(dv_b, VALW),
                            _T(z_al, LANES, ALR // LANES), _T(dla_f, KEYW), _T(dla_b, KEYW), _T(cos_t, DK), _T(sin_t, DK)] + dec_w,
        [(KEYW, BF16), (KEYW, BF16), (VALW, BF16), (LANES, BF16)], dec_acc, "gla_prep_x_bwd")
    dzc_k, dzc_v, dzc_lr, dwdf_c, dwdb_c, dbdf_c, dbdb_c = _rows(
        prep_c_bwd, NC, TR, [_T(dkc_f, KEYW), _T(dkc_b, KEYW), _T(dvc_f, VALW), _T(dvc_b, VALW),
                             _T(zc_al, LANES, ALR // LANES), _T(dlac_f, KEYW), _T(dlac_b, KEYW)] + dec_w,
        [(KEYW, BF16), (VALW, BF16), (LANES, BF16)], dec_acc, "gla_prep_ctx_bwd")

    dz_al = jnp.concatenate([dz_sg, dz_q, dz_k, dz_v, dz_r, dz_lr], axis=1)
    dzc_al = jnp.concatenate([jnp.zeros((NC, 2 * SGW + KEYW), BF16), dzc_k, dzc_v, jnp.zeros((NC, VALW), BF16), dzc_lr], axis=1)
    tkd = _pick(ACOLS, 3456)
    dhx, recv1_wo = _mm(dz_al, w_al, N, D, ACOLS, name="in_proj_dx", tb=True, tk=tkd, side=[("rs_sibling", dwo)])
    part_wo = _sum_sibling(dwo, recv1_wo, c_arr, "rs_sum_sibling_w_o")
    dhc = _mm(dzc_al, w_al, NC, D, ACOLS, name="in_proj_dctx", tb=True, tk=tkd)
    h_cat = jnp.concatenate([hx, hc], axis=0)
    dz_cat = jnp.concatenate([dz_al, dzc_al], axis=0)
    tkt = _pick(N + NC, 2304)
    dw_al, recv2_wo = _mm(h_cat, dz_cat, D, ACOLS, N + NC, name="in_proj_dw", ta=True, tn=tka, tk=tkt, out_dtypes=(BF16,),
                          side=[("rs_chips", part_wo)])

    grad_x, dpre1_x, dsc1, dsh1 = _rows(prenorm_bwd, N, TR, [_T(x2, D), _W(pre1_g), _W(sc1), _W(sh1), _T(dhx, D), _T(dx1, D)],
                                        [(D, F32)], [(1, D)] * 3, "prenorm1_x_bwd")

    def prenorm_bwd_ctx(xv, gv, scv, shv, dh):
        _, vjp = jax.vjp(_prenorm, xv, gv, scv, shv)
        _, dg, dsc, dsh = vjp(dh)
        return _colsum(dg), _colsum(dsc), _colsum(dsh)

    dpre1_c, dcsc1, dcsh1 = _rows(prenorm_bwd_ctx, NC, TR, [_T(ctx2, D), _W(pre1_g), _W(csc1), _W(csh1), _T(dhc, D)],
                                  [], [(1, D)] * 3, "prenorm1_ctx_bwd")

    g_in = _shard_columns(dw_al, CS, SG0, 2 * SGW, -SG0, RH, "w_in_grad_blocks")
    recv1_in = _rs_sibling([g_in], "rs_sibling_w_in")[0]
    part_in = _sum_sibling(g_in, recv1_in, c_arr, "rs_sum_sibling_w_in")
    recv2_in = _rs_chips([part_in], "rs_chips_w_in")[0]
    half = [_sum_chips(p, r, sc_arr, "rs_sum_chips_" + nm)
            for p, r, nm in zip((part_in, part_wo, part_w1, part_w2), (recv2_in, recv2_wo, recv2_w1, recv2_w2), ("w_in", "w_o", "w_1", "w_2"))]
    g_in_pad, g_w_o, g_w_1, g_w_2 = _rs_final(half, "rs_final")
    g_w_in = g_in_pad.reshape(D, -1)[:, :CS]
    g_w_o, g_w_1, g_w_2 = [g.reshape(w.shape[1:]) for g, w in zip((g_w_o, g_w_1, g_w_2), (w_o, w_1, w_2))]

    dmod_x = jnp.concatenate([dsh1, dsc1, dgt1, dsh2, dsc2, dgt2], axis=1)
    dmodc = jnp.concatenate([dcsh1, dcsc1], axis=1)
    small_parts = [loss_acc, dmod_x, dmodc, dpre1_x + dpre1_c, dpost1, dpre2, dpost2, dwdf_x + dwdf_c, dbdf_x + dbdf_c,
                   dwdb_x + dwdb_c, dbdb_x + dbdb_c, dgn, dlng, dlnb, dws, dbs]
    small_shapes = [p.shape for p in small_parts]
    packed = _pack(small_parts)
    n_sm = packed.shape[1]
    gathered = _allgather_small(packed, "gather_small_grads")

    def sum_devices(g):
        tot = g[0:8]
        for dev in range(1, N_DEV):
            tot = tot + g[8 * dev:8 * dev + 8]
        return (tot,)

    summed = _rows(sum_devices, N_DEV * 8, N_DEV * 8, [_W(gathered)], [], [(8, n_sm)], "sum_small_grads")[0]
    (loss_s, dmod_sum, dmodc_sum, g_pre1, g_post1, g_pre2, g_post2, g_wdf_pad, g_bdf, g_wdb_pad, g_bdb, g_gn, g_lng, g_lnb,
     g_ws, g_bs) = _unpack(summed.reshape(-1), small_shapes)
    loss = loss_s[0, 0]
    dmod_rows = gathered.reshape(N_DEV, -1)[:, LANES:LANES + N_MOD * D]
    g_b_ada = dmod_sum + jnp.pad(dmodc_sum, ((0, 0), (0, (N_MOD - 2) * D)))
    dmod16 = jnp.zeros((16, N_MOD * D), F32).at[:N_DEV].set(dmod_rows).at[N_DEV, :2 * D].set(dmodc_sum[0])
    dmod16_sh = lax.dynamic_slice(dmod16, (0, s_me * MS), (16, MS))
    g_w_ada, d_w_ada, nm_w_ada, nv_w_ada = _ada_update(cond.T, dmod16_sh, w_ada[0], m_w_ada[0], v_w_ada[0], "w_ada_update")

    dcond = _mm(dmod16_sh, w_ada[0], 16, D, MS, name="cond_bwd", tb=True, tk=min(512, MS))
    part_c = _allgather_small(dcond[N_DEV].reshape(8, D // 8), "gather_dcond").reshape(N_DEV, D)

    def cctx_grad(p, cv):
        sg = jax.nn.sigmoid(cv)
        tot = ((p[0:1] + p[2:3]) + p[4:5]) + p[6:7]
        return (jnp.broadcast_to(tot * (sg * (1.0 + cv * (1.0 - sg))), p.shape),)

    g_c_ctx = _rows(cctx_grad, N_DEV, N_DEV, [_W(part_c), _W(c_ctx.reshape(1, D))], [(D, F32)], [], "c_ctx_grad")[0][0:1]

    def col_shard(g_full, width):
        return lax.dynamic_slice_in_dim(g_full, s_me * width, width, axis=g_full.ndim - 1)

    g_w_dec_f = col_shard(g_wdf_pad[:GLA_LOWRANK], KEYW // N_CHIP)
    g_w_dec_b = col_shard(g_wdb_pad[GLA_LOWRANK:LR], KEYW // N_CHIP)
    g_gla_norm = col_shard(g_gn.reshape(H, DV), DV // N_CHIP)
    small_w = [c_ctx, b_ada, pre1_g, post1_g, pre2_g, post2_g, w_dec_f, b_dec_f, w_dec_b, b_dec_b, gla_norm_g, sg_ln_g, sg_ln_b, w_s, b_s]
    small_m = [m_c_ctx, m_b_ada, m_pre1_g, m_post1_g, m_pre2_g, m_post2_g, m_w_dec_f, m_b_dec_f, m_w_dec_b, m_b_dec_b, m_gla_norm_g, m_sg_ln_g, m_sg_ln_b, m_w_s, m_b_s]
    small_v = [v_c_ctx, v_b_ada, v_pre1_g, v_post1_g, v_pre2_g, v_post2_g, v_w_dec_f, v_b_dec_f, v_w_dec_b, v_b_dec_b, v_gla_norm_g, v_sg_ln_g, v_sg_ln_b, v_w_s, v_b_s]
    small_g = [g_c_ctx, g_b_ada, g_pre1, g_post1, g_pre2, g_post2, g_w_dec_f, g_bdf, g_w_dec_b, g_bdb, g_gla_norm, g_lng, g_lnb, g_ws, g_bs]
    small_g = [g.reshape(w.shape) for g, w in zip(small_g, small_w)]
    shapes_w = [w.shape for w in small_w]
    upd = _adamw(_pack(small_w), _pack(small_g), _pack(small_m), _pack(small_v), "adamw_small")
    d_small, m_small, v_small = [_unpack(u.reshape(-1), shapes_w) for u in upd]

    def big(w, g, m, v, name):
        shp = w.shape
        res = _adamw(w.reshape(shp[-2:]), g.reshape(shp[-2:]), m.reshape(shp[-2:]), v.reshape(shp[-2:]), name)
        return [g.reshape(shp)] + [r.reshape(shp) for r in res]

    r_in = big(w_in, g_w_in, m_w_in, v_w_in, "adamw_w_in")
    r_o = big(w_o, g_w_o, m_w_o, v_w_o, "adamw_w_o")
    r_1 = big(w_1, g_w_1, m_w_1, v_w_1, "adamw_w_1")
    r_2 = big(w_2, g_w_2, m_w_2, v_w_2, "adamw_w_2")
    r_ada = [t.reshape(w_ada.shape) for t in (g_w_ada, d_w_ada, nm_w_ada, nv_w_ada)]

    def ordered(k):
        sm = [small_g, d_small, m_small, v_small][k]
        return [sm[0], r_ada[k], *sm[1:6], r_in[k], *sm[6:15], r_o[k], r_1[k], r_2[k]]

    return (loss, grad_x.reshape(x.shape), *ordered(0), *ordered(1), *ordered(2), *ordered(3))
```

```python
import functools
import math

import jax
import jax.numpy as jnp
from jax import lax
from jax.experimental import pallas as pl
from jax.experimental.pallas import tpu as pltpu

F32 = jnp.float32
BF16 = jnp.bfloat16
MESH = pl.DeviceIdType.MESH
ANY = pl.BlockSpec(memory_space=pl.ANY)

GLA_HEADS = 8
GLA_CHUNK = 64
GLA_LOWRANK = 16
GLA_TAU = 16.0
ROPE_BASE = 10000.0
GRID_W = 64
SG_GROUPS = 4
SG_CHUNK = 128
N_MOD = 6
EPS = 1e-6
ADAM_LR = 0.001
ADAM_B1 = 0.9
ADAM_B2 = 0.999
ADAM_EPS = 1e-08
ADAM_WD = 0.01
ADAM_STEP = 10

LANES = 128
VMEM_LIMIT = 56 << 20
N_DEV = 8
N_CHIP = 4


def _params(sem=None):
    return pltpu.CompilerParams(dimension_semantics=sem, vmem_limit_bytes=VMEM_LIMIT)


def _pick(dim, target, unit=LANES):
    best = None
    for t in range(unit, min(dim, target) + 1, unit):
        if dim % t == 0:
            best = t
    return dim if best is None else best


def _dg(a, b, ca, cb, precision=None):
    return lax.dot_general(a, b, (((ca,), (cb,)), ((), ())), preferred_element_type=F32,
                           precision=precision)


def _place():
    return lax.axis_index("x"), lax.axis_index("y"), lax.axis_index("c")


def _allgather_small(v, name):
    m_per, n = v.shape

    def body(x_ref, out_ref, send_sems, recv_sems, local_sem):
        x, y, c = _place()
        me, sibling = (x, y, c), (x, y, 1 - c)
        chips = [(1 - x, y), (x, 1 - y), (1 - x, 1 - y)]

        def rows(px, py, pc):
            return out_ref.at[pl.ds((4 * px + 2 * py + pc) * m_per, m_per), :]

        def copy(k, block, to, src=None):
            return pltpu.make_async_remote_copy(
                src_ref=rows(*block) if src is None else src, dst_ref=rows(*block),
                send_sem=send_sems.at[k], recv_sem=recv_sems.at[k],
                device_id=to, device_id_type=MESH)

        mine = pltpu.make_async_copy(x_ref, rows(*me), local_sem)
        mine.start()
        first = [copy(0, me, sibling, src=x_ref)]
        first += [copy(1 + j, me, (*chip, c), src=x_ref) for j, chip in enumerate(chips)]
        for cp in first:
            cp.start()
        passed = [copy(4 + j, (*chip, c), sibling) for j, chip in enumerate(chips)]
        for j, chip in enumerate(chips):
            copy(1 + j, (*chip, c), me).wait_recv()
            passed[j].start()
        copy(0, sibling, me).wait_recv()
        for j, chip in enumerate(chips):
            copy(4 + j, (*chip, 1 - c), me).wait_recv()
        for cp in first + passed:
            cp.wait_send()
        mine.wait()

    return pl.pallas_call(
        body, name=name,
        out_shape=jax.ShapeDtypeStruct((N_DEV * m_per, n), v.dtype),
        in_specs=[pl.BlockSpec(memory_space=pltpu.VMEM)],
        out_specs=pl.BlockSpec(memory_space=pltpu.VMEM),
        scratch_shapes=[pltpu.SemaphoreType.DMA((7,)), pltpu.SemaphoreType.DMA((7,)),
                        pltpu.SemaphoreType.DMA],
        compiler_params=pltpu.CompilerParams(vmem_limit_bytes=VMEM_LIMIT),
    )(v)


def _cast_blocks(w, s_me, name):
    _, r, cols = w.shape
    tr = _row_tile(r, cols, 4)

    def body(s_ref, w_ref, o_ref):
        o_ref[...] = w_ref[...].astype(BF16)

    return pl.pallas_call(
        body, name=name,
        out_shape=jax.ShapeDtypeStruct((N_DEV, r, cols), BF16),
        grid_spec=pltpu.PrefetchScalarGridSpec(
            num_scalar_prefetch=1, grid=(2, r // tr),
            in_specs=[pl.BlockSpec((None, tr, cols), lambda h, i, s: (h, i, 0))],
            out_specs=pl.BlockSpec((None, tr, cols), lambda h, i, s: (2 * s[0] + h, i, 0))),
        compiler_params=_params(("arbitrary", "arbitrary")),
    )(s_me, w)


def _gather_big(ws, name):
    nw = len(ws)

    def body(*refs):
        outs = refs[nw:2 * nw]
        send_sems, recv_sems = refs[2 * nw:]
        x, y, c = _place()
        me, sibling = (x, y, c), (x, y, 1 - c)
        chips = [(1 - x, y), (x, 1 - y), (1 - x, 1 - y)]

        def blk(px, py, pc):
            return 4 * px + 2 * py + pc

        def copy(w, k, block, to):
            return pltpu.make_async_remote_copy(
                src_ref=outs[w].at[block], dst_ref=outs[w].at[block],
                send_sem=send_sems.at[6 * w + k], recv_sem=recv_sems.at[6 * w + k],
                device_id=to, device_id_type=MESH)

        first = []
        for w in range(nw):
            for j, chip in enumerate(chips):
                cp = copy(w, j, blk(x, y, c), (*chip, c))
                cp.start()
                first.append(cp)
        passed = []
        for w in range(nw):
            for j, chip in enumerate(chips):
                copy(w, j, blk(*chip, c), me).wait_recv()
                cp = copy(w, 3 + j, blk(*chip, c), sibling)
                cp.start()
                passed.append(cp)
        for w in range(nw):
            for j, chip in enumerate(chips):
                copy(w, 3 + j, blk(*chip, 1 - c), me).wait_recv()
        for cp in first + passed:
            cp.wait_send()

    return pl.pallas_call(
        body, name=name,
        out_shape=[jax.ShapeDtypeStruct(w.shape, w.dtype) for w in ws],
        in_specs=[ANY] * nw, out_specs=[ANY] * nw,
        input_output_aliases={w: w for w in range(nw)},
        scratch_shapes=[pltpu.SemaphoreType.DMA((6 * nw,)), pltpu.SemaphoreType.DMA((6 * nw,))],
    )(*ws)


def _rs_sibling(gs, name):
    nw = len(gs)

    def body(*refs):
        ins, outs = refs[:nw], refs[nw:2 * nw]
        send_sems, recv_sems = refs[2 * nw:]
        x, y, c = _place()
        cps = []
        for w in range(nw):
            for s in range(N_CHIP):
                cp = pltpu.make_async_remote_copy(
                    src_ref=ins[w].at[2 * s + (1 - c)], dst_ref=outs[w].at[s],
                    send_sem=send_sems.at[N_CHIP * w + s], recv_sem=recv_sems.at[N_CHIP * w + s],
                    device_id=(x, y, 1 - c), device_id_type=MESH)
                cp.start()
                cps.append(cp)
        for cp in cps:
            cp.wait()

    return pl.pallas_call(
        body, name=name,
        out_shape=[jax.ShapeDtypeStruct((N_CHIP,) + g.shape[1:], g.dtype) for g in gs],
        in_specs=[ANY] * nw, out_specs=[ANY] * nw,
        scratch_shapes=[pltpu.SemaphoreType.DMA((N_CHIP * nw,)), pltpu.SemaphoreType.DMA((N_CHIP * nw,))],
    )(*gs)


def _rs_chips(ps, name):
    nw = len(ps)

    def body(*refs):
        ins, outs = refs[:nw], refs[nw:2 * nw]
        send_sems, recv_sems = refs[2 * nw:]
        x, y, c = _place()
        chips = [(1 - x, y), (x, 1 - y), (1 - x, 1 - y)]
        cps = []
        for w in range(nw):
            for j, chip in enumerate(chips):
                cp = pltpu.make_async_remote_copy(
                    src_ref=ins[w].at[2 * chip[0] + chip[1]], dst_ref=outs[w].at[j],
                    send_sem=send_sems.at[3 * w + j], recv_sem=recv_sems.at[3 * w + j],
                    device_id=(*chip, c), device_id_type=MESH)
                cp.start()
                cps.append(cp)
        for cp in cps:
            cp.wait()

    return pl.pallas_call(
        body, name=name,
        out_shape=[jax.ShapeDtypeStruct((3,) + p.shape[1:], p.dtype) for p in ps],
        in_specs=[ANY] * nw, out_specs=[ANY] * nw,
        scratch_shapes=[pltpu.SemaphoreType.DMA((3 * nw,)), pltpu.SemaphoreType.DMA((3 * nw,))],
    )(*ps)


def _rs_final(fs, name):
    nw = len(fs)

    def body(*refs):
        outs = refs[nw:2 * nw]
        send_sems, recv_sems = refs[2 * nw:]
        x, y, c = _place()
        cps = []
        for w in range(nw):
            cp = pltpu.make_async_remote_copy(
                src_ref=outs[w].at[c], dst_ref=outs[w].at[c],
                send_sem=send_sems.at[w], recv_sem=recv_sems.at[w],
                device_id=(x, y, 1 - c), device_id_type=MESH)
            cp.start()
            cps.append(cp)
        for cp in cps:
            cp.wait()

    return pl.pallas_call(
        body, name=name,
        out_shape=[jax.ShapeDtypeStruct(f.shape, f.dtype) for f in fs],
        in_specs=[ANY] * nw, out_specs=[ANY] * nw,
        input_output_aliases={w: w for w in range(nw)},
        scratch_shapes=[pltpu.SemaphoreType.DMA((nw,)), pltpu.SemaphoreType.DMA((nw,))],
    )(*fs)


_PHASE_COPIES = {"gather_ici": 3, "gather_d2d": 3, "gather_chain": 6, "rs_sibling": N_CHIP, "rs_chips": 3}
QUARTERS = 4


def _ph(buf, legs, src=None):
    return dict(buf=buf, src=src, legs=legs)


def _n_copies(ph):
    return sum(_PHASE_COPIES[kind] for kind, _, _ in ph["legs"])


def _phase_copies(ph, src, buf, send_sems, recv_sems, base):
    x, y, c = _place()
    sibling = (x, y, 1 - c)
    chips = [(1 - x, y), (x, 1 - y), (1 - x, 1 - y)]
    r = buf.shape[1]

    def make(k, trip):
        a, b, dev = trip
        return pltpu.make_async_remote_copy(src_ref=a, dst_ref=b, send_sem=send_sems.at[base + k], recv_sem=recv_sems.at[base + k],
                                            device_id=dev, device_id_type=MESH)

    out = []
    for kind, lo, hi in ph["legs"]:
        rows = pl.ds(lo * r // QUARTERS, (hi - lo) * r // QUARTERS)
        ici = [(buf.at[4 * x + 2 * y + c, rows], buf.at[4 * x + 2 * y + c, rows], (*chip, c)) for chip in chips]
        d2d = [(buf.at[4 * chip[0] + 2 * chip[1] + c, rows], buf.at[4 * chip[0] + 2 * chip[1] + c, rows], sibling) for chip in chips]
        if kind == "gather_ici":
            trips, later = ici, []
        elif kind == "gather_d2d":
            trips, later = d2d, []
        elif kind == "gather_chain":
            trips, later = ici, d2d
        elif kind == "rs_sibling":
            trips, later = [(src.at[2 * s + (1 - c), rows], buf.at[s, rows], sibling) for s in range(N_CHIP)], []
        else:
            trips, later = [(src.at[2 * chip[0] + chip[1], rows], buf.at[j, rows], (*chip, c)) for j, chip in enumerate(chips)], []
        out.append(([make(k, t) for k, t in enumerate(trips)], [make(len(trips) + k, t) for k, t in enumerate(later)]))
        base += _PHASE_COPIES[kind]
    return out


def _side_call(inner, grid, in_specs, out_specs, out_shape, scratch, args, phases, name, semantics, mid=0.8):
    n_in, n_out, n_ph = len(in_specs), len(out_specs), len(phases)
    if n_ph == 0:
        outs = pl.pallas_call(inner, name=name, grid=grid, in_specs=in_specs, out_specs=out_specs, out_shape=out_shape,
                              scratch_shapes=scratch, compiler_params=_params(semantics))(*args)
        return list(outs), []
    n_cp = sum(_n_copies(p) for p in phases)
    side_args, buf_pos, src_pos = [], [], []
    for p in phases:
        buf_pos.append(len(side_args))
        side_args.append(p["buf"])
        src_pos.append(len(side_args) if p["src"] is not None else None)
        if p["src"] is not None:
            side_args.append(p["src"])
    n_side = len(side_args)
    total = math.prod(grid)
    mid_lin = min(total - 1, int(total * mid))

    def body(*refs):
        b_in, s_in = refs[:n_in], refs[n_in:n_in + n_side]
        b_out, s_out = refs[n_in + n_side:n_in + n_side + n_out], refs[n_in + n_side + n_out:n_in + n_side + n_out + n_ph]
        rest = refs[n_in + n_side + n_out + n_ph:]
        send_sems, recv_sems = rest[-2:]
        lin = functools.reduce(lambda acc, ag: acc * ag[1] + pl.program_id(ag[0]), list(enumerate(grid))[1:], pl.program_id(0))

        def copies():
            out, base = [], 0
            for p, sp, so in zip(phases, src_pos, s_out):
                out += _phase_copies(p, None if sp is None else s_in[sp], so, send_sems, recv_sems, base)
                base += _n_copies(p)
            return out

        @pl.when(lin == 0)
        def _():
            for a, _ in copies():
                for cp in a:
                    cp.start()

        if any(kind == "gather_chain" for p in phases for kind, _, _ in p["legs"]):
            @pl.when(lin == mid_lin)
            def _():
                for a, b in copies():
                    if b:
                        for cp in a:
                            cp.wait()
                        for cp in b:
                            cp.start()

        inner(*b_in, *b_out, *rest[:-2])

        @pl.when(lin == total - 1)
        def _():
            for a, b in copies():
                for cp in (b if b else a):
                    cp.wait()

    outs = pl.pallas_call(
        body, name=name, grid=grid,
        in_specs=list(in_specs) + [ANY] * n_side, out_specs=list(out_specs) + [ANY] * n_ph,
        out_shape=list(out_shape) + [jax.ShapeDtypeStruct(p["buf"].shape, p["buf"].dtype) for p in phases],
        input_output_aliases={n_in + bp: n_out + k for k, bp in enumerate(buf_pos)},
        scratch_shapes=list(scratch) + [pltpu.SemaphoreType.DMA((n_cp,)), pltpu.SemaphoreType.DMA((n_cp,))],
        compiler_params=_params(("arbitrary",) * len(grid)),
    )(*args, *side_args)
    return list(outs[:n_out]), list(outs[n_out:])


def _row_tile(r, cols, itemsize):
    t = r
    while t * cols * itemsize > (2 << 20) and t % 16 == 0:
        t //= 2
    return t


def _sum_sibling(g, r1, c_me, name):
    _, r, cols = g.shape
    tr = _row_tile(r, cols, 4)

    def body(c_ref, g_ref, r_ref, o_ref):
        o_ref[...] = (g_ref[...].astype(F32) + r_ref[...].astype(F32)).astype(o_ref.dtype)

    return pl.pallas_call(
        body, name=name,
        out_shape=jax.ShapeDtypeStruct((N_CHIP, r, cols), g.dtype),
        grid_spec=pltpu.PrefetchScalarGridSpec(
            num_scalar_prefetch=1, grid=(N_CHIP, r // tr),
            in_specs=[pl.BlockSpec((None, tr, cols), lambda s, i, c: (2 * s + c[0], i, 0)),
                      pl.BlockSpec((None, tr, cols), lambda s, i, c: (s, i, 0))],
            out_specs=pl.BlockSpec((None, tr, cols), lambda s, i, c: (s, i, 0))),
        compiler_params=_params(("arbitrary", "arbitrary")),
    )(c_me, g, r1)


def _sum_chips(p, r2, sc_me, name):
    _, r, cols = p.shape
    tr = _row_tile(r, cols, 4)

    def body(s_ref, p_ref, a_ref, b_ref, c_ref, o_ref):
        o_ref[...] = ((p_ref[...].astype(F32) + a_ref[...].astype(F32)) + b_ref[...].astype(F32)) + c_ref[...].astype(F32)

    return pl.pallas_call(
        body, name=name,
        out_shape=jax.ShapeDtypeStruct((2, r, cols), F32),
        grid_spec=pltpu.PrefetchScalarGridSpec(
            num_scalar_prefetch=1, grid=(r // tr,),
            in_specs=[pl.BlockSpec((None, tr, cols), lambda i, s: (s[0], i, 0)),
                      pl.BlockSpec((None, tr, cols), lambda i, s: (0, i, 0)),
                      pl.BlockSpec((None, tr, cols), lambda i, s: (1, i, 0)),
                      pl.BlockSpec((None, tr, cols), lambda i, s: (2, i, 0))],
            out_specs=pl.BlockSpec((None, tr, cols), lambda i, s: (s[1], i, 0))),
        compiler_params=_params(("arbitrary",)),
    )(sc_me, p, r2, r2, r2)


def _shard_columns(g_al, shard_cols, bound, off_lo, off_hi, rh, name):
    d, acols = g_al.shape
    wp = -(-shard_cols // LANES) * LANES
    tr = min(LANES, rh)
    nt = acols // LANES

    def body(x_ref, o_ref):
        s = pl.program_id(0)
        lane = lax.broadcasted_iota(jnp.int32, (tr, LANES), 1)

        def tile(q):
            q = max(0, min(nt - 1, q))
            return x_ref[:, q * LANES:(q + 1) * LANES].astype(F32)

        def read(start):
            q, sh = divmod(start, LANES)
            if sh == 0:
                return tile(q)
            return jnp.where(lane < LANES - sh, pltpu.roll(tile(q), LANES - sh, 1), pltpu.roll(tile(q + 1), LANES - sh, 1))

        for k in range(N_CHIP):
            @pl.when(s == k)
            def _(k=k):
                for t in range(wp // LANES):
                    n0 = k * shard_cols + t * LANES
                    if n0 + LANES <= bound:
                        v = read(n0 + off_lo)
                    elif n0 >= bound:
                        v = read(n0 + off_hi)
                    else:
                        v = jnp.where(lane < bound - n0, read(n0 + off_lo), read(n0 + off_hi))
                    o_ref[:, t * LANES:(t + 1) * LANES] = v.astype(o_ref.dtype)

    return pl.pallas_call(
        body, name=name, grid=(N_CHIP, d // tr),
        out_shape=jax.ShapeDtypeStruct((N_DEV, rh, wp), BF16),
        in_specs=[pl.BlockSpec((tr, acols), lambda s, i: (i, 0))],
        out_specs=pl.BlockSpec((None, tr, wp), lambda s, i: (2 * s + (i * tr) // rh, ((i * tr) % rh) // tr, 0)),
        compiler_params=_params(("arbitrary", "arbitrary")),
    )(g_al)


def _mm(a, b, M, N, K, *, name, ta=False, tb=False, out_dtypes=(F32,), tm=1024, tn=1024, tk=2048,
        a_spec=None, b_spec=None, out_specs=None, out_shapes=None, epi=None, epi_in=(), epi_specs=(), side=(), side_mid=0.8):
    tm, tn, tk = min(tm, M), min(tn, N), min(tk, K)
    assert M % tm == 0 and N % tn == 0 and K % tk == 0, (name, M, N, K, tm, tn, tk)
    nk = K // tk
    n_epi, n_out = len(epi_in), len(out_dtypes)
    if a_spec is None:
        a_spec = pl.BlockSpec((tk, tm), lambda i, j, k: (k, i)) if ta else pl.BlockSpec((tm, tk), lambda i, j, k: (i, k))
    if b_spec is None:
        b_spec = pl.BlockSpec((tn, tk), lambda i, j, k: (j, k)) if tb else pl.BlockSpec((tk, tn), lambda i, j, k: (k, j))
    if out_specs is None:
        out_specs = [pl.BlockSpec((tm, tn), lambda i, j, k: (i, j))] * n_out
        out_shapes = [jax.ShapeDtypeStruct((M, N), dt) for dt in out_dtypes]

    def body(a_ref, b_ref, *rest):
        epi_refs, o_refs, acc = rest[:n_epi], rest[n_epi:n_epi + n_out], rest[-1]
        k = pl.program_id(2)

        @pl.when(k == 0)
        def _():
            acc[...] = jnp.zeros_like(acc)

        acc[...] += _dg(a_ref[...].astype(BF16), b_ref[...].astype(BF16), 0 if ta else 1, 1 if tb else 0)

        @pl.when(k == nk - 1)
        def _():
            r = acc[...]
            vals = (r,) if epi is None else epi(r, *[e[...] for e in epi_refs])
            for o_ref, v in zip(o_refs, vals):
                o_ref[...] = v.astype(o_ref.dtype)

    outs, side_outs = _side_call(body, (M // tm, N // tn, nk), [a_spec, b_spec, *epi_specs], out_specs, out_shapes,
                                 [pltpu.VMEM((tm, tn), F32)], (a, b, *epi_in), list(side), name,
                                 ("parallel", "parallel", "arbitrary"), mid=side_mid)
    if side:
        return outs + side_outs
    return outs[0] if n_out == 1 else outs


def _T(arr, width, col=0, lead=None):
    return ("tile", arr, width, col, lead)


def _W(arr):
    return ("whole", arr)


def _rows(fn, n_rows, tr, ins, tile_outs, acc_outs, name):
    tr = min(tr, n_rows)
    assert n_rows % tr == 0, (name, n_rows, tr)
    in_specs, args = [], []
    for d in ins:
        if d[0] == "tile":
            _, arr, width, col, lead = d
            if lead is None:
                in_specs.append(pl.BlockSpec((tr, width), lambda i, col=col: (i, col)))
            else:
                in_specs.append(pl.BlockSpec((None, tr, width), lambda i, col=col, lead=lead: (lead, i, col)))
            args.append(arr)
        else:
            arr = d[1]
            in_specs.append(pl.BlockSpec(arr.shape, lambda i, nd=arr.ndim: (0,) * nd))
            args.append(arr)
    n_in, n_t = len(ins), len(tile_outs)
    out_shape = [jax.ShapeDtypeStruct((n_rows, w), dt) for w, dt in tile_outs]
    out_specs = [pl.BlockSpec((tr, w), lambda i: (i, 0)) for w, _ in tile_outs]
    out_shape += [jax.ShapeDtypeStruct(s, F32) for s in acc_outs]
    out_specs += [pl.BlockSpec(s, lambda i, nd=len(s): (0,) * nd) for s in acc_outs]

    def body(*refs):
        in_refs, t_refs, a_refs = refs[:n_in], refs[n_in:n_in + n_t], refs[n_in + n_t:]
        vals = fn(*[r[...] for r in in_refs])
        for r, v in zip(t_refs, vals[:n_t]):
            r[...] = v.astype(r.dtype)
        first = pl.program_id(0) == 0
        for r, v in zip(a_refs, vals[n_t:]):
            @pl.when(first)
            def _(r=r, v=v):
                r[...] = v

            @pl.when(jnp.logical_not(first))
            def _(r=r, v=v):
                r[...] += v

    return pl.pallas_call(
        body, name=name, out_shape=out_shape, grid=(n_rows // tr,),
        in_specs=in_specs, out_specs=out_specs,
        compiler_params=_params(("arbitrary",)),
    )(*args)


def _colsum(t):
    return jnp.sum(t, axis=0, keepdims=True)


def _prenorm(x, g, sc, sh):
    xf = x.astype(F32)
    return xf * lax.rsqrt(jnp.mean(xf * xf, axis=-1, keepdims=True) + EPS) * g * (1.0 + sc) + sh


def _postnorm(x, y, gate, pg):
    return x + gate * (y * lax.rsqrt(jnp.mean(y * y, axis=-1, keepdims=True) + EPS) * pg)


def _gelu(t):
    return 0.5 * t * (1.0 + lax.erf(t * (2.0 ** -0.5)))


def _sg_pre(zu, zv, lng, lnb):
    u, vr = _gelu(zu), _gelu(zv)
    mu = jnp.mean(vr, axis=-1, keepdims=True)
    var = jnp.mean(jnp.square(vr - mu), axis=-1, keepdims=True)
    return u, (vr - mu) * lax.rsqrt(var + EPS) * lng + lnb


def _sg_mix(vv, ws_ref_vals, bs_vals, gw):
    parts = []
    for g in range(SG_GROUPS):
        s = _dg(ws_ref_vals[g].astype(BF16), vv[:, g * gw:(g + 1) * gw].astype(BF16), 1, 0)
        parts.append(s + bs_vals[g])
    return jnp.concatenate(parts, axis=1)


def _readout(o, r, g, heads, dv):
    parts = []
    for h in range(heads):
        oh = o[:, h * dv:(h + 1) * dv]
        parts.append(oh * lax.rsqrt(jnp.mean(oh * oh, axis=-1, keepdims=True) + EPS))
    return jnp.concatenate(parts, axis=1) * g * (r * jax.nn.sigmoid(r))


def _log_sigmoid(a):
    return jnp.minimum(a, 0.0) - jnp.log(1.0 + jnp.exp(-jnp.abs(a)))


def _rope_swap(t, m):
    lane = lax.broadcasted_iota(jnp.int32, t.shape, 1)
    return jnp.where((lane % (2 * m)) < m, pltpu.roll(t, 3 * m, 1), pltpu.roll(t, m, 1))


def _rope(t, cos, sin, heads, dk):
    parts = []
    for h in range(heads):
        th = t[:, h * dk:(h + 1) * dk]
        parts.append(th * cos + _rope_swap(th, dk // 4) * sin)
    return jnp.concatenate(parts, axis=1)


def _rope_t(dt, cos, sin, heads, dk):
    parts = []
    for h in range(heads):
        dh = dt[:, h * dk:(h + 1) * dk]
        parts.append(dh * cos + _rope_swap(dh * sin, dk // 4))
    return jnp.concatenate(parts, axis=1)


def _chunk_cumsum(t, upwards):
    n = t.shape[0]
    row = lax.broadcasted_iota(jnp.int32, (n, n), 0)
    col = lax.broadcasted_iota(jnp.int32, (n, n), 1)
    shift = GLA_CHUNK.bit_length() - 1
    same = jnp.right_shift(row, shift) == jnp.right_shift(col, shift)
    tri = jnp.logical_and(same, col <= row if upwards else col >= row)
    return _dg(tri.astype(F32), t, 1, 0, precision=lax.Precision.HIGHEST)


def _chunk_terms(d, qv, kv, b, C):
    row = lax.broadcasted_iota(jnp.int32, (C, C), 0)
    col = lax.broadcasted_iota(jnp.int32, (C, C), 1)
    tri = row >= col if d == 0 else row <= col
    end_row = lax.broadcasted_iota(jnp.int32, b.shape, 0) == (C - 1 if d == 0 else 0)
    btot = _colsum(jnp.where(end_row, b, 0.0))
    eb, enb, ebt = jnp.exp(b), jnp.exp(-b), jnp.exp(btot - b)
    return tri, btot, eb, enb, ebt, qv * eb, kv * enb, kv * ebt


def _gla_fwd(q, k, zv, v_col0, la, st0, heads, dk, dv, name, side=()):
    n, C, H = q.shape[0], GLA_CHUNK, heads
    nc = n // C

    def body(qf, kf, vf, laf, qb, kb, vb_, lab, st0_ref, of_ref, ob_ref, sf_ref, sb_ref, fin_ref, st):
        i = pl.program_id(1)

        @pl.when(i == 0)
        def _():
            st[...] = st0_ref[...]

        for d, (q_ref, k_ref, v_ref, la_ref, o_ref, save_ref) in enumerate(((qf, kf, vf, laf, of_ref, sf_ref), (qb, kb, vb_, lab, ob_ref, sb_ref))):
            tri, btot, _, _, _, qt, kt, kh = _chunk_terms(d, q_ref[...], k_ref[...], la_ref[...], C)
            s = st[d]
            vb = v_ref[...].astype(BF16)
            qtb = qt.astype(BF16)
            att = jnp.where(tri, _dg(qtb, kt.astype(BF16), 1, 1), 0.0)
            o_ref[...] = _dg(qtb, s.astype(BF16), 1, 1) + _dg(att.astype(BF16), vb, 1, 0)
            save_ref[...] = s
            s_new = s * jnp.exp(btot) + _dg(vb, kh.astype(BF16), 0, 0)
            st[d] = s_new

            @pl.when(i == nc - 1)
            def _(d=d, s_new=s_new):
                fin_ref[d] = s_new

    def seq(width, col0, rev, dir_cols=0):
        if rev:
            return pl.BlockSpec((C, width), lambda h, i: (nc - 1 - i, col0 + dir_cols + h))
        return pl.BlockSpec((C, width), lambda h, i: (i, col0 + h))

    both = pl.BlockSpec((2, None, dv, dk), lambda h, i: (0, h, 0, 0))
    outs, side_outs = _side_call(
        body, (H, nc),
        [seq(dk, 0, False), seq(dk, 0, False), seq(dv, v_col0, False), seq(dk, 0, False),
         seq(dk, 0, True), seq(dk, 0, True), seq(dv, v_col0, True), seq(dk, 0, True, H), both],
        [seq(dv, 0, False), seq(dv, 0, True),
         pl.BlockSpec((None, None, dv, dk), lambda h, i: (h, i, 0, 0)),
         pl.BlockSpec((None, None, dv, dk), lambda h, i: (h, nc - 1 - i, 0, 0)), both],
        [jax.ShapeDtypeStruct((n, H * dv), F32)] * 2 + [jax.ShapeDtypeStruct((H, nc, dv, dk), F32)] * 2
        + [jax.ShapeDtypeStruct((2, H, dv, dk), F32)],
        [pltpu.VMEM((2, dv, dk), F32)], (q, k, zv, la, q, k, zv, la, st0), list(side), name, ("arbitrary", "arbitrary"))
    return outs + side_outs


def _gla_bwd(q, k, zv, v_col0, la, saved_f, saved_b, do, dfin, heads, dk, dv, name, side=()):
    n, C, H = q.shape[0], GLA_CHUNK, heads
    nc = n // C

    def body(qf, kf, vf, laf, sf, dof, qb, kb, vb_, lab, sb, dob_, dfin_ref,
             dqf, dqb, dkf, dkb, dvf, dvb, dlaf, dlab, d0_ref, dst):
        i = pl.program_id(1)

        @pl.when(i == 0)
        def _():
            dst[...] = dfin_ref[...]

        dirs = ((qf, kf, vf, laf, sf, dof, dqf, dkf, dvf, dlaf), (qb, kb, vb_, lab, sb, dob_, dqb, dkb, dvb, dlab))
        for d, (q_ref, k_ref, v_ref, la_ref, save_ref, do_ref, dq_ref, dk_ref, dv_ref, dla_ref) in enumerate(dirs):
            tri, btot, eb, enb, ebt, qt, kt, kh = _chunk_terms(d, q_ref[...], k_ref[...], la_ref[...], C)
            s, dsn = save_ref[...], dst[d]
            vb, dob = v_ref[...].astype(BF16), do_ref[...].astype(BF16)
            qtb, ktb, khb, dsnb = qt.astype(BF16), kt.astype(BF16), kh.astype(BF16), dsn.astype(BF16)
            att = jnp.where(tri, _dg(qtb, ktb, 1, 1), 0.0).astype(BF16)
            datt = jnp.where(tri, _dg(dob, vb, 1, 1), 0.0).astype(BF16)
            dqt = _dg(dob, s.astype(BF16), 1, 0) + _dg(datt, ktb, 1, 0)
            dkt = _dg(datt, qtb, 0, 0)
            dkh = _dg(vb, dsnb, 1, 0)
            dv_ref[...] = _dg(att, dob, 0, 0) + _dg(khb, dsnb, 1, 1)
            ebtot = jnp.exp(btot)
            dbtot = ebtot * _colsum(s * dsn) + _colsum(dkh * kh)
            s0 = dsn * ebtot + _dg(dob, qtb, 0, 0)
            dst[d] = s0
            db = dqt * qt - dkt * kt - dkh * kh
            dq_ref[...] = dqt * eb
            dk_ref[...] = dkt * enb + dkh * ebt
            end_row = lax.broadcasted_iota(jnp.int32, db.shape, 0) == (C - 1 if d == 0 else 0)
            dla_ref[...] = db + jnp.where(end_row, dbtot, 0.0)

            @pl.when(i == nc - 1)
            def _(d=d, s0=s0):
                d0_ref[d] = s0

    def seq(width, col0, fwd_dir, dir_cols=0):
        if fwd_dir:
            return pl.BlockSpec((C, width), lambda h, i: (nc - 1 - i, col0 + h))
        return pl.BlockSpec((C, width), lambda h, i: (i, col0 + dir_cols + h))

    both = pl.BlockSpec((2, None, dv, dk), lambda h, i: (0, h, 0, 0))
    sav_f = pl.BlockSpec((None, None, dv, dk), lambda h, i: (h, nc - 1 - i, 0, 0))
    sav_b = pl.BlockSpec((None, None, dv, dk), lambda h, i: (h, i, 0, 0))
    outs, side_outs = _side_call(
        body, (H, nc),
        [seq(dk, 0, True), seq(dk, 0, True), seq(dv, v_col0, True), seq(dk, 0, True), sav_f, seq(dv, 0, True),
         seq(dk, 0, False), seq(dk, 0, False), seq(dv, v_col0, False), seq(dk, 0, False, H), sav_b, seq(dv, 0, False), both],
        [seq(dk, 0, True), seq(dk, 0, False), seq(dk, 0, True), seq(dk, 0, False), seq(dv, 0, True), seq(dv, 0, False),
         seq(dk, 0, True), seq(dk, 0, False), both],
        [jax.ShapeDtypeStruct((n, H * dk), F32)] * 4 + [jax.ShapeDtypeStruct((n, H * dv), F32)] * 2
        + [jax.ShapeDtypeStruct((n, H * dk), F32)] * 2 + [jax.ShapeDtypeStruct((2, H, dv, dk), F32)],
        [pltpu.VMEM((2, dv, dk), F32)], (q, k, zv, la, saved_f, do, q, k, zv, la, saved_b, do, dfin), list(side), name,
        ("arbitrary", "arbitrary"))
    return outs + side_outs


def _adamw_math(w, g, m, v):
    m2 = ADAM_B1 * m + (1.0 - ADAM_B1) * g
    v2 = ADAM_B2 * v + (1.0 - ADAM_B2) * jnp.square(g)
    m_hat = m2 / (1.0 - ADAM_B1 ** ADAM_STEP)
    v_hat = v2 / (1.0 - ADAM_B2 ** ADAM_STEP)
    delta = -ADAM_LR * (m_hat / (jnp.sqrt(v_hat) + ADAM_EPS) + ADAM_WD * w)
    return delta, m2, v2


def _adamw(w, g, m, v, name):
    r, cols = w.shape
    tr = _row_tile(r, cols, 4 * 4)
    spec = pl.BlockSpec((tr, cols), lambda i: (i, 0))

    def body(w_ref, g_ref, m_ref, v_ref, d_ref, m2_ref, v2_ref):
        d_ref[...], m2_ref[...], v2_ref[...] = _adamw_math(w_ref[...], g_ref[...], m_ref[...], v_ref[...])

    return pl.pallas_call(
        body, name=name, grid=(r // tr,), out_shape=[jax.ShapeDtypeStruct((r, cols), F32)] * 3,
        in_specs=[spec] * 4, out_specs=[spec] * 3, compiler_params=_params(("parallel",)),
    )(w, g, m, v)


def _ada_update(cond_t, dmod, w, m, v, name):
    r, cols = w.shape
    tr, tc = _pick(r, 512, 8), _pick(cols, 1024)
    spec = pl.BlockSpec((tr, tc), lambda i, j: (i, j))

    def body(c_ref, d_ref, w_ref, m_ref, v_ref, g_ref, dl_ref, m2_ref, v2_ref):
        g = _dg(c_ref[...].astype(BF16), d_ref[...].astype(BF16), 1, 0)
        g_ref[...] = g
        dl_ref[...], m2_ref[...], v2_ref[...] = _adamw_math(w_ref[...], g, m_ref[...], v_ref[...])

    return pl.pallas_call(
        body, name=name, grid=(r // tr, cols // tc), out_shape=[jax.ShapeDtypeStruct((r, cols), F32)] * 4,
        in_specs=[pl.BlockSpec((tr, cond_t.shape[1]), lambda i, j: (i, 0)),
                  pl.BlockSpec((dmod.shape[0], tc), lambda i, j: (0, j)), spec, spec, spec],
        out_specs=[spec] * 4, compiler_params=_params(("parallel", "parallel")),
    )(cond_t, dmod, w, m, v)


def _pack(parts, rows=8):
    flat = jnp.concatenate([p.reshape(-1).astype(F32) for p in parts])
    n = -(-flat.shape[0] // (rows * LANES)) * LANES
    return jnp.pad(flat, (0, rows * n - flat.shape[0])).reshape(rows, n)


def _unpack(flat, shapes):
    out, off = [], 0
    for s in shapes:
        size = math.prod(s)
        out.append(flat[off:off + size].reshape(s))
        off += size
    return out


def kernel(x, c, ctx, c_ctx, w_ada, b_ada, pre1_g, post1_g, pre2_g, post2_g, w_in, w_dec_f, b_dec_f, w_dec_b, b_dec_b, gla_norm_g, sg_ln_g, sg_ln_b, w_s, b_s, w_o, w_1, w_2, loss_target, m_c_ctx, m_w_ada, m_b_ada, m_pre1_g, m_post1_g, m_pre2_g, m_post2_g, m_w_in, m_w_dec_f, m_b_dec_f, m_w_dec_b, m_b_dec_b, m_gla_norm_g, m_sg_ln_g, m_sg_ln_b, m_w_s, m_b_s, m_w_o, m_w_1, m_w_2, v_c_ctx, v_w_ada, v_b_ada, v_pre1_g, v_post1_g, v_pre2_g, v_post2_g, v_w_in, v_w_dec_f, v_b_dec_f, v_w_dec_b, v_b_dec_b, v_gla_norm_g, v_sg_ln_g, v_sg_ln_b, v_w_s, v_b_s, v_w_o, v_w_1, v_w_2):
    N, D = x.shape[1], x.shape[2]
    NC = ctx.shape[1]
    H = GLA_HEADS
    VALW = D // 2
    DV = VALW // H
    DK = DV // 2
    KEYW = H * DK
    SGW = D - VALW
    GW = SGW // SG_GROUPS
    LR = 2 * GLA_LOWRANK
    F = w_1.shape[2] * N_CHIP
    FS = F // N_CHIP
    RH = D // 2
    MS = w_ada.shape[2]
    IN_COLS = w_in.shape[2] * N_CHIP
    K0, V0, R0, LF0 = KEYW, 2 * KEYW, 2 * KEYW + VALW, 2 * KEYW + 2 * VALW
    SG0 = LF0 + LR
    AQ, AK, AV, AR, ALR = 2 * SGW, 2 * SGW + KEYW, 2 * SGW + 2 * KEYW, 2 * SGW + 2 * KEYW + VALW, 2 * SGW + 2 * KEYW + 2 * VALW
    ACOLS = ALR + LANES
    assert IN_COLS == SG0 + 2 * SGW and N % SG_CHUNK == 0 and N % GLA_CHUNK == 0 and NC % GLA_CHUNK == 0

    ax, ay, ac = _place()
    s_me = (2 * ax + ay).astype(jnp.int32)
    b_me = (4 * ax + 2 * ay + ac).astype(jnp.int32)
    s_arr, c_arr = s_me.reshape(1), ac.astype(jnp.int32).reshape(1)
    sc_arr = jnp.concatenate([s_arr, c_arr])
    CS = IN_COLS // N_CHIP

    shards = [_cast_blocks(w_in[0].reshape(2, RH, CS), s_arr, "cast_w_in"), _cast_blocks(w_o[0].reshape(2, D // N_DEV, D), s_arr, "cast_w_o"),
              _cast_blocks(w_1[0].reshape(2, RH, FS), s_arr, "cast_w_1"), _cast_blocks(w_2[0].reshape(2, F // N_DEV, D), s_arr, "cast_w_2")]
    win_g, wo_g = _gather_big(shards[:2], "gather_weights")
    w1_buf, w2_buf = shards[2], shards[3]
    w_in_nat = win_g.reshape(N_CHIP, 2, RH, IN_COLS // N_CHIP).transpose(1, 2, 0, 3).reshape(D, IN_COLS)
    w_al = jnp.concatenate([w_in_nat[:, SG0:], w_in_nat[:, :LF0], w_in_nat[:, LF0:SG0],
                            jnp.zeros((D, LANES - LR), BF16)], axis=1)
    w_o_f = wo_g.reshape(D, D)

    n_dec = GLA_LOWRANK * (KEYW // N_CHIP)
    g0 = _allgather_small(_pack([c, w_dec_f, w_dec_b, gla_norm_g]), "gather_small0").reshape(N_DEV, -1)
    c_all = g0[:, :D]
    per_chip = g0[0::2]
    wdf = per_chip[:, D:D + n_dec].reshape(N_CHIP, GLA_LOWRANK, KEYW // N_CHIP).transpose(1, 0, 2).reshape(GLA_LOWRANK, KEYW)
    wdb = per_chip[:, D + n_dec:D + 2 * n_dec].reshape(N_CHIP, GLA_LOWRANK, KEYW // N_CHIP).transpose(1, 0, 2).reshape(GLA_LOWRANK, KEYW)
    gn_full = per_chip[:, D + 2 * n_dec:D + 2 * n_dec + H * (DV // N_CHIP)].reshape(N_CHIP, H, DV // N_CHIP).transpose(1, 0, 2).reshape(1, VALW)
    wd_f = jnp.zeros((LANES, KEYW), F32).at[:GLA_LOWRANK].set(wdf)
    wd_b = jnp.zeros((LANES, KEYW), F32).at[GLA_LOWRANK:LR].set(wdb)

    cond_in = jnp.zeros((16, D), F32).at[:N_DEV].set(c_all).at[N_DEV].set(c_ctx)
    b_ada_sh = lax.dynamic_slice(b_ada, (0, s_me * MS), (1, MS))

    def mod_epi(r, bias):
        return (r + bias,)

    def silu_rows(t):
        return (t * jax.nn.sigmoid(t),)

    cond = _rows(silu_rows, 16, 16, [_W(cond_in)], [(D, F32)], [], "cond_silu")[0]
    mod_sh = _mm(cond, w_ada[0], 16, MS, D, name="mod_matmul", tn=512, tk=D, epi=mod_epi, epi_in=(b_ada_sh,),
                 epi_specs=(pl.BlockSpec((1, min(512, MS)), lambda i, j, k: (0, j)),))
    g1m = _allgather_small(mod_sh, "gather_mod").reshape(N_DEV, 16, MS)[0::2]
    mod_all = g1m.transpose(1, 0, 2).reshape(16, N_CHIP * MS)
    mod_me = lax.dynamic_slice(mod_all, (b_me, 0), (1, N_MOD * D))
    sh1, sc1, gt1, sh2, sc2, gt2 = [mod_me[:, i * D:(i + 1) * D] for i in range(N_MOD)]
    csh1, csc1 = mod_all[N_DEV:N_DEV + 1, :D], mod_all[N_DEV:N_DEV + 1, D:2 * D]

    mq = DK // 4
    pos = jnp.arange(N)
    inv_freq = ROPE_BASE ** (-jnp.arange(mq, dtype=F32) / mq)
    ang_r = (pos // GRID_W).astype(F32)[:, None] * inv_freq[None, :]
    ang_c = (pos % GRID_W).astype(F32)[:, None] * inv_freq[None, :]
    cos_t = jnp.concatenate([jnp.cos(ang_r), jnp.cos(ang_r), jnp.cos(ang_c), jnp.cos(ang_c)], axis=1)
    sin_t = jnp.concatenate([-jnp.sin(ang_r), jnp.sin(ang_r), -jnp.sin(ang_c), jnp.sin(ang_c)], axis=1)

    x2, tgt, ctx2 = x[0], loss_target[0], ctx[0]
    TR = 128
    qscale = DK ** -0.5

    def prenorm_fwd(xa, g, sc, sh, n_rows, name):
        return _rows(lambda xv, gv, scv, shv: (_prenorm(xv, gv, scv, shv),), n_rows, TR,
                     [_T(xa, D), _W(g), _W(sc), _W(sh)], [(D, BF16)], [], name)[0]

    hx = prenorm_fwd(x2, pre1_g, sc1, sh1, N, "prenorm1_x")
    hc = prenorm_fwd(ctx2, pre1_g, csc1, csh1, NC, "prenorm1_ctx")
    tka = _pick(ACOLS, 1152)
    z_al, w1_buf = _mm(hx, w_al, N, ACOLS, D, name="in_proj_x", tn=tka, side=[_ph(w1_buf, [("gather_ici", 0, 2)])])
    zc_al = _mm(hc, w_al, NC, ACOLS, D, name="in_proj_ctx", tn=tka)

    def decay(lr, wdf_v, wdb_v, bf_v, bb_v):
        lrb = lr.astype(BF16)
        a_f = _dg(lrb, wdf_v.astype(BF16), 1, 0) + bf_v
        a_b = _dg(lrb, wdb_v.astype(BF16), 1, 0) + bb_v
        return a_f, a_b

    def running_decay(a_f, a_b):
        return jnp.concatenate([_chunk_cumsum(_log_sigmoid(a_f) / GLA_TAU, True), _chunk_cumsum(_log_sigmoid(a_b) / GLA_TAU, False)], axis=1)

    def prep_x(zq, zk, lr, cs, sn, wdf_v, wdb_v, bf_v, bb_v):
        return _rope(zq * qscale, cs, sn, H, DK), _rope(zk, cs, sn, H, DK), running_decay(*decay(lr, wdf_v, wdb_v, bf_v, bb_v))

    def prep_c(zk, lr, wdf_v, wdb_v, bf_v, bb_v):
        return zk, running_decay(*decay(lr, wdf_v, wdb_v, bf_v, bb_v))

    dec_w = [_W(wd_f), _W(wd_b), _W(b_dec_f), _W(b_dec_b)]
    q_r, k_r, la_x = _rows(prep_x, N, TR, [_T(z_al, KEYW, AQ // KEYW), _T(z_al, KEYW, AK // KEYW), _T(z_al, LANES, ALR // LANES),
                                           _T(cos_t, DK), _T(sin_t, DK)] + dec_w,
                           [(KEYW, F32), (KEYW, F32), (2 * KEYW, F32)], [], "gla_prep_x")
    k_c, la_c = _rows(prep_c, NC, TR, [_T(zc_al, KEYW, AK // KEYW), _T(zc_al, LANES, ALR // LANES)] + dec_w,
                      [(KEYW, F32), (2 * KEYW, F32)], [], "gla_prep_ctx")

    zero_state = jnp.zeros((2, H, DV, DK), F32)
    q_c = jnp.zeros((NC, KEYW), F32)
    _, _, savf_c, savb_c, st_c = _gla_fwd(q_c, k_c, zc_al, AV // DV, la_c, zero_state, H, DK, DV, "gla_fwd_ctx")
    o_f, o_b, savf_x, savb_x, _, w1_buf = _gla_fwd(q_r, k_r, z_al, AV // DV, la_x, st_c, H, DK, DV, "gla_fwd_x",
                                                   side=[_ph(w1_buf, [("gather_d2d", 0, 2), ("gather_ici", 2, 4)])])

    def readout_fwd(of, ob, r, g):
        return (_readout(of + ob, r, g, H, DV),)

    y_gla = _rows(readout_fwd, N, TR, [_T(o_f, VALW), _T(o_b, VALW), _T(z_al, VALW, AR // VALW), _W(gn_full)],
                  [(VALW, BF16)], [], "gla_readout")[0]

    bs_col = b_s[0].reshape(SG_GROUPS, SG_CHUNK, 1)

    def sg_fwd(zu, zv, lng, lnb, ws, bs):
        u, vv = _sg_pre(zu, zv, lng, lnb)
        return (u * _sg_mix(vv, ws, bs, GW),)

    y_sg = _rows(sg_fwd, N, SG_CHUNK, [_T(z_al, SGW, 0), _T(z_al, SGW, 1), _W(sg_ln_g), _W(sg_ln_b), _W(w_s[0]), _W(bs_col)],
                 [(SGW, BF16)], [], "sg_fwd")[0]
    ycat = jnp.concatenate([y_gla, y_sg], axis=1)
    y, w1_g, w2_buf = _mm(ycat, w_o_f, N, D, D, name="out_proj",
                          side=[_ph(w1_buf, [("gather_d2d", 2, 4)]), _ph(w2_buf, [("gather_ici", 0, 1)])])
    x1 = _rows(lambda xv, yv, gv, pv: (_postnorm(xv, yv, gv, pv),), N, TR,
               [_T(x2, D), _T(y, D), _W(gt1), _W(post1_g)], [(D, F32)], [], "postnorm1")[0]
    h2 = prenorm_fwd(x1, pre2_g, sc2, sh2, N, "prenorm2")

    tm1, tn1, tk1 = min(1024, N), min(1024, FS), min(2048, RH)
    w1_fwd_spec = pl.BlockSpec((None, tk1, tn1), lambda i, j, k: (2 * ((j * tn1) // FS) + (k * tk1) // RH, ((k * tk1) % RH) // tk1, ((j * tn1) % FS) // tn1))

    def relu2_epi(r):
        rf = jnp.maximum(r, 0.0)
        return rf * rf, rf

    act, rf, w2_g = _mm(h2, w1_g, N, F, D, name="mlp_up", out_dtypes=(BF16, BF16), tm=tm1, tn=tn1, tk=tk1, b_spec=w1_fwd_spec, epi=relu2_epi,
                        side=[_ph(w2_buf, [("gather_d2d", 0, 1), ("gather_chain", 1, 4)])], side_mid=0.88)
    w_2_f = w2_g.reshape(F, D)
    y2 = _mm(act, w_2_f, N, D, F, name="mlp_down")

    def final(x1v, y2v, gv, pv, tv):
        def loss_fn(x1a, y2a, ga, pa):
            err = _postnorm(x1a, y2a, ga, pa) - tv
            return 0.5 * jnp.sum(jnp.mean(err * err, axis=-1))
        loss, grads = jax.value_and_grad(loss_fn, argnums=(0, 1, 2, 3))(x1v, y2v, gv, pv)
        return grads[0], grads[1], jnp.full((1, LANES), loss, F32), _colsum(grads[2]), _colsum(grads[3])

    dx2, dy2, loss_acc, dgt2, dpost2 = _rows(final, N, TR, [_T(x1, D), _T(y2, D), _W(gt2), _W(post2_g), _T(tgt, D)],
                                             [(D, F32), (D, BF16)], [(1, LANES), (1, D), (1, D)], "loss_postnorm2_bwd")

    df = _mm(dy2, w_2_f, N, F, D, name="mlp_down_dx", tb=True, out_dtypes=(BF16,), epi=lambda r, rfv: (r * (2.0 * rfv.astype(F32)),),
             epi_in=(rf,), epi_specs=(pl.BlockSpec((min(1024, N), min(1024, F)), lambda i, j, k: (i, j)),))
    dw2 = _mm(act, dy2, F, D, N, name="mlp_down_dw", ta=True, out_dtypes=(BF16,)).reshape(N_DEV, F // N_DEV, D)
    tnb, tkb = min(1024, D, RH), min(2048, FS)
    w1_bwd_spec = pl.BlockSpec((None, tnb, tkb), lambda i, j, k: (2 * ((k * tkb) // FS) + (j * tnb) // RH, ((j * tnb) % RH) // tnb, ((k * tkb) % FS) // tkb))
    dh2, recv1_w2 = _mm(df, w1_g, N, D, F, name="mlp_up_dx", tb=True, tn=tnb, tk=tkb, b_spec=w1_bwd_spec,
                         side=[_ph(lax.empty((N_CHIP,) + dw2.shape[1:], BF16), [("rs_sibling", 0, 4)], src=dw2)])
    part_w2 = _sum_sibling(dw2, recv1_w2, c_arr, "rs_sum_sibling_w_2")
    tmw, tnw = min(1024, RH), min(1024, FS)
    dw1_spec = pl.BlockSpec((None, tmw, tnw), lambda i, j, k: (2 * ((j * tnw) // FS) + (i * tmw) // RH, ((i * tmw) % RH) // tmw, ((j * tnw) % FS) // tnw))
    dw1, recv2_w2 = _mm(h2, df, D, F, N, name="mlp_up_dw", ta=True, tm=tmw, tn=tnw, out_dtypes=(BF16,), out_specs=[dw1_spec],
                        out_shapes=[jax.ShapeDtypeStruct((N_DEV, RH, FS), BF16)],
                        side=[_ph(lax.empty((3,) + part_w2.shape[1:], BF16), [("rs_chips", 0, 3)], src=part_w2)])

    def prenorm_bwd(xv, gv, scv, shv, dh, dres):
        _, vjp = jax.vjp(_prenorm, xv, gv, scv, shv)
        dx, dg, dsc, dsh = vjp(dh)
        return dx + dres, _colsum(dg), _colsum(dsc), _colsum(dsh)

    dx1, dpre2, dsc2, dsh2 = _rows(prenorm_bwd, N, TR, [_T(x1, D), _W(pre2_g), _W(sc2), _W(sh2), _T(dh2, D), _T(dx2, D)],
                                   [(D, F32)], [(1, D)] * 3, "prenorm2_bwd")

    def postnorm_bwd(yv, gv, pv, dxv):
        _, vjp = jax.vjp(lambda ya, ga, pa: _postnorm(0.0, ya, ga, pa), yv, gv, pv)
        dy_, dg_, dp_ = vjp(dxv)
        return dy_, _colsum(dg_), _colsum(dp_)

    dy, dgt1, dpost1 = _rows(postnorm_bwd, N, TR, [_T(y, D), _W(gt1), _W(post1_g), _T(dx1, D)], [(D, BF16)], [(1, D)] * 2, "postnorm1_bwd")
    dycat, recv2_w2, recv1_w1 = _mm(dy, w_o_f, N, D, D, name="out_proj_dx", tb=True,
                                    side=[_ph(recv2_w2, [("rs_chips", 3, 4)], src=part_w2),
                                          _ph(lax.empty((N_CHIP,) + dw1.shape[1:], BF16), [("rs_sibling", 0, 4)], src=dw1)])
    part_w1 = _sum_sibling(dw1, recv1_w1, c_arr, "rs_sum_sibling_w_1")
    dwo = _mm(ycat, dy, D, D, N, name="out_proj_dw", ta=True, out_dtypes=(BF16,)).reshape(N_DEV, D // N_DEV, D)

    def readout_bwd(of, ob, r, g, dyv):
        _, vjp = jax.vjp(lambda o_, r_, g_: _readout(o_, r_, g_, H, DV), of + ob, r, g)
        do_, dr_, dg_ = vjp(dyv)
        return do_, dr_, _colsum(dg_)

    do_x, dz_r, dgn = _rows(readout_bwd, N, TR, [_T(o_f, VALW), _T(o_b, VALW), _T(z_al, VALW, AR // VALW), _W(gn_full), _T(dycat, VALW, 0)],
                            [(VALW, F32), (VALW, BF16)], [(1, VALW)], "gla_readout_bwd")

    def sg_bwd(zu, zv, lng, lnb, ws, bs, dyv):
        (u, vv), vjp = jax.vjp(_sg_pre, zu, zv, lng, lnb)
        s = _sg_mix(vv, ws, bs, GW)
        du, ds = dyv * s, dyv * u
        dws, dbs, dvv = [], [], []
        for g in range(SG_GROUPS):
            dsg = ds[:, g * GW:(g + 1) * GW]
            dsb = dsg.astype(BF16)
            dws.append(_dg(dsb, vv[:, g * GW:(g + 1) * GW].astype(BF16), 1, 1))
            dbs.append(jnp.sum(dsg, axis=1, keepdims=True))
            dvv.append(_dg(ws[g].astype(BF16), dsb, 0, 0))
        dzu, dzv, dlng, dlnb = vjp((du, jnp.concatenate(dvv, axis=1)))
        return jnp.concatenate([dzu, dzv], axis=1), _colsum(dlng), _colsum(dlnb), jnp.concatenate(dws, axis=0), jnp.concatenate(dbs, axis=0)

    dz_sg, dlng, dlnb, dws, dbs = _rows(sg_bwd, N, SG_CHUNK, [_T(z_al, SGW, 0), _T(z_al, SGW, 1), _W(sg_ln_g), _W(sg_ln_b), _W(w_s[0]), _W(bs_col), _T(dycat, SGW, VALW // SGW)],
                                        [(2 * SGW, BF16)], [(1, SGW), (1, SGW), (SG_GROUPS * SG_CHUNK, SG_CHUNK), (SG_GROUPS * SG_CHUNK, 1)], "sg_bwd")

    dq_f, dq_b, dk_f, dk_b, dv_f, dv_b, dla_f, dla_b, dst0, recv2_w1 = _gla_bwd(
        q_r, k_r, z_al, AV // DV, la_x, savf_x, savb_x, do_x, zero_state, H, DK, DV, "gla_bwd_x",
        side=[_ph(lax.empty((3,) + part_w1.shape[1:], BF16), [("rs_chips", 0, 3)], src=part_w1)])
    _, _, dkc_f, dkc_b, dvc_f, dvc_b, dlac_f, dlac_b, _ = _gla_bwd(
        q_c, k_c, zc_al, AV // DV, la_c, savf_c, savb_c, jnp.zeros((NC, VALW), F32), dst0, H, DK, DV, "gla_bwd_ctx")

    def decay_bwd(lr, dla_f_v, dla_b_v, wdf_v, wdb_v, bf_v, bb_v):
        a_f, a_b = decay(lr, wdf_v, wdb_v, bf_v, bb_v)
        da_f = _chunk_cumsum(dla_f_v, False) * jax.nn.sigmoid(-a_f) / GLA_TAU
        da_b = _chunk_cumsum(dla_b_v, True) * jax.nn.sigmoid(-a_b) / GLA_TAU
        lrb, dfb, dbb = lr.astype(BF16), da_f.astype(BF16), da_b.astype(BF16)
        dlr = _dg(dfb, wdf_v.astype(BF16), 1, 1) + _dg(dbb, wdb_v.astype(BF16), 1, 1)
        return dlr, _dg(lrb, dfb, 0, 0), _dg(lrb, dbb, 0, 0), _colsum(da_f), _colsum(da_b)

    def prep_x_bwd(dq0, dq1, dk0, dk1, dv0, dv1, lr, dla0, dla1, cs, sn, wdf_v, wdb_v, bf_v, bb_v):
        dlr, dwf, dwb, dbf, dbb = decay_bwd(lr, dla0, dla1, wdf_v, wdb_v, bf_v, bb_v)
        return (_rope_t(dq0 + dq1, cs, sn, H, DK) * qscale, _rope_t(dk0 + dk1, cs, sn, H, DK), dv0 + dv1, dlr, dwf, dwb, dbf, dbb)

    def prep_c_bwd(dk0, dk1, dv0, dv1, lr, dla0, dla1, wdf_v, wdb_v, bf_v, bb_v):
        dlr, dwf, dwb, dbf, dbb = decay_bwd(lr, dla0, dla1, wdf_v, wdb_v, bf_v, bb_v)
        return dk0 + dk1, dv0 + dv1, dlr, dwf, dwb, dbf, dbb

    dec_acc = [(LANES, KEYW), (LANES, KEYW), (1, KEYW), (1, KEYW)]
    dz_q, dz_k, dz_v, dz_lr, dwdf_x, dwdb_x, dbdf_x, dbdb_x = _rows(
        prep_x_bwd, N, TR, [_T(dq_f, KEYW), _T(dq_b, KEYW), _T(dk_f, KEYW), _T(dk_b, KEYW), _T(dv_f, VALW), _T(dv_b, VALW),
                            _T(z_al, LANES, ALR // LANES), _T(dla_f, KEYW), _T(dla_b, KEYW), _T(cos_t, DK), _T(sin_t, DK)] + dec_w,
        [(KEYW, BF16), (KEYW, BF16), (VALW, BF16), (LANES, BF16)], dec_acc, "gla_prep_x_bwd")
    dzc_k, dzc_v, dzc_lr, dwdf_c, dwdb_c, dbdf_c, dbdb_c = _rows(
        prep_c_bwd, NC, TR, [_T(dkc_f, KEYW), _T(dkc_b, KEYW), _T(dvc_f, VALW), _T(dvc_b, VALW),
                             _T(zc_al, LANES, ALR // LANES), _T(dlac_f, KEYW), _T(dlac_b, KEYW)] + dec_w,
        [(KEYW, BF16), (VALW, BF16), (LANES, BF16)], dec_acc, "gla_prep_ctx_bwd")

    dz_al = jnp.concatenate([dz_sg, dz_q, dz_k, dz_v, dz_r, dz_lr], axis=1)
    dzc_al = jnp.concatenate([jnp.zeros((NC, 2 * SGW + KEYW), BF16), dzc_k, dzc_v, jnp.zeros((NC, VALW), BF16), dzc_lr], axis=1)
    tkd = _pick(ACOLS, 3456)
    dhx, recv2_w1, recv1_wo = _mm(dz_al, w_al, N, D, ACOLS, name="in_proj_dx", tb=True, tk=tkd,
                                  side=[_ph(recv2_w1, [("rs_chips", 3, 4)], src=part_w1),
                                        _ph(lax.empty((N_CHIP,) + dwo.shape[1:], BF16), [("rs_sibling", 0, 4)], src=dwo)])
    part_wo = _sum_sibling(dwo, recv1_wo, c_arr, "rs_sum_sibling_w_o")
    dhc = _mm(dzc_al, w_al, NC, D, ACOLS, name="in_proj_dctx", tb=True, tk=tkd)
    h_cat = jnp.concatenate([hx, hc], axis=0)
    dz_cat = jnp.concatenate([dz_al, dzc_al], axis=0)
    tkt = _pick(N + NC, 2304)
    dw_al, recv2_wo = _mm(h_cat, dz_cat, D, ACOLS, N + NC, name="in_proj_dw", ta=True, tn=tka, tk=tkt, out_dtypes=(BF16,),
                          side=[_ph(lax.empty((3,) + part_wo.shape[1:], BF16), [("rs_chips", 0, 4)], src=part_wo)])

    grad_x, dpre1_x, dsc1, dsh1 = _rows(prenorm_bwd, N, TR, [_T(x2, D), _W(pre1_g), _W(sc1), _W(sh1), _T(dhx, D), _T(dx1, D)],
                                        [(D, F32)], [(1, D)] * 3, "prenorm1_x_bwd")

    def prenorm_bwd_ctx(xv, gv, scv, shv, dh):
        _, vjp = jax.vjp(_prenorm, xv, gv, scv, shv)
        _, dg, dsc, dsh = vjp(dh)
        return _colsum(dg), _colsum(dsc), _colsum(dsh)

    dpre1_c, dcsc1, dcsh1 = _rows(prenorm_bwd_ctx, NC, TR, [_T(ctx2, D), _W(pre1_g), _W(csc1), _W(csh1), _T(dhc, D)],
                                  [], [(1, D)] * 3, "prenorm1_ctx_bwd")

    g_in = _shard_columns(dw_al, CS, SG0, 2 * SGW, -SG0, RH, "w_in_grad_blocks")
    recv1_in = _rs_sibling([g_in], "rs_sibling_w_in")[0]
    part_in = _sum_sibling(g_in, recv1_in, c_arr, "rs_sum_sibling_w_in")
    recv2_in = _rs_chips([part_in], "rs_chips_w_in")[0]
    half = [_sum_chips(p, r, sc_arr, "rs_sum_chips_" + nm)
            for p, r, nm in zip((part_in, part_wo, part_w1, part_w2), (recv2_in, recv2_wo, recv2_w1, recv2_w2), ("w_in", "w_o", "w_1", "w_2"))]
    g_in_pad, g_w_o, g_w_1, g_w_2 = _rs_final(half, "rs_final")
    g_w_in = g_in_pad.reshape(D, -1)[:, :CS]
    g_w_o, g_w_1, g_w_2 = [g.reshape(w.shape[1:]) for g, w in zip((g_w_o, g_w_1, g_w_2), (w_o, w_1, w_2))]

    dmod_x = jnp.concatenate([dsh1, dsc1, dgt1, dsh2, dsc2, dgt2], axis=1)
    dmodc = jnp.concatenate([dcsh1, dcsc1], axis=1)
    small_parts = [loss_acc, dmod_x, dmodc, dpre1_x + dpre1_c, dpost1, dpre2, dpost2, dwdf_x + dwdf_c, dbdf_x + dbdf_c,
                   dwdb_x + dwdb_c, dbdb_x + dbdb_c, dgn, dlng, dlnb, dws, dbs]
    small_shapes = [p.shape for p in small_parts]
    packed = _pack(small_parts)
    n_sm = packed.shape[1]
    gathered = _allgather_small(packed, "gather_small_grads")

    def sum_devices(g):
        tot = g[0:8]
        for dev in range(1, N_DEV):
            tot = tot + g[8 * dev:8 * dev + 8]
        return (tot,)

    summed = _rows(sum_devices, N_DEV * 8, N_DEV * 8, [_W(gathered)], [], [(8, n_sm)], "sum_small_grads")[0]
    (loss_s, dmod_sum, dmodc_sum, g_pre1, g_post1, g_pre2, g_post2, g_wdf_pad, g_bdf, g_wdb_pad, g_bdb, g_gn, g_lng, g_lnb,
     g_ws, g_bs) = _unpack(summed.reshape(-1), small_shapes)
    loss = loss_s[0, 0]
    dmod_rows = gathered.reshape(N_DEV, -1)[:, LANES:LANES + N_MOD * D]
    g_b_ada = dmod_sum + jnp.pad(dmodc_sum, ((0, 0), (0, (N_MOD - 2) * D)))
    dmod16 = jnp.zeros((16, N_MOD * D), F32).at[:N_DEV].set(dmod_rows).at[N_DEV, :2 * D].set(dmodc_sum[0])
    dmod16_sh = lax.dynamic_slice(dmod16, (0, s_me * MS), (16, MS))
    g_w_ada, d_w_ada, nm_w_ada, nv_w_ada = _ada_update(cond.T, dmod16_sh, w_ada[0], m_w_ada[0], v_w_ada[0], "w_ada_update")

    dcond = _mm(dmod16_sh, w_ada[0], 16, D, MS, name="cond_bwd", tb=True, tk=min(512, MS))
    part_c = _allgather_small(dcond[N_DEV].reshape(8, D // 8), "gather_dcond").reshape(N_DEV, D)

    def cctx_grad(p, cv):
        sg = jax.nn.sigmoid(cv)
        tot = ((p[0:1] + p[2:3]) + p[4:5]) + p[6:7]
        return (jnp.broadcast_to(tot * (sg * (1.0 + cv * (1.0 - sg))), p.shape),)

    g_c_ctx = _rows(cctx_grad, N_DEV, N_DEV, [_W(part_c), _W(c_ctx.reshape(1, D))], [(D, F32)], [], "c_ctx_grad")[0][0:1]

    def col_shard(g_full, width):
        return lax.dynamic_slice_in_dim(g_full, s_me * width, width, axis=g_full.ndim - 1)

    g_w_dec_f = col_shard(g_wdf_pad[:GLA_LOWRANK], KEYW // N_CHIP)
    g_w_dec_b = col_shard(g_wdb_pad[GLA_LOWRANK:LR], KEYW // N_CHIP)
    g_gla_norm = col_shard(g_gn.reshape(H, DV), DV // N_CHIP)
    small_w = [c_ctx, b_ada, pre1_g, post1_g, pre2_g, post2_g, w_dec_f, b_dec_f, w_dec_b, b_dec_b, gla_norm_g, sg_ln_g, sg_ln_b, w_s, b_s]
    small_m = [m_c_ctx, m_b_ada, m_pre1_g, m_post1_g, m_pre2_g, m_post2_g, m_w_dec_f, m_b_dec_f, m_w_dec_b, m_b_dec_b, m_gla_norm_g, m_sg_ln_g, m_sg_ln_b, m_w_s, m_b_s]
    small_v = [v_c_ctx, v_b_ada, v_pre1_g, v_post1_g, v_pre2_g, v_post2_g, v_w_dec_f, v_b_dec_f, v_w_dec_b, v_b_dec_b, v_gla_norm_g, v_sg_ln_g, v_sg_ln_b, v_w_s, v_b_s]
    small_g = [g_c_ctx, g_b_ada, g_pre1, g_post1, g_pre2, g_post2, g_w_dec_f, g_bdf, g_w_dec_b, g_bdb, g_gla_norm, g_lng, g_lnb, g_ws, g_bs]
    small_g = [g.reshape(w.shape) for g, w in zip(small_g, small_w)]
    shapes_w = [w.shape for w in small_w]
    upd = _adamw(_pack(small_w), _pack(small_g), _pack(small_m), _pack(small_v), "adamw_small")
    d_small, m_small, v_small = [_unpack(u.reshape(-1), shapes_w) for u in upd]

    def big(w, g, m, v, name):
        shp = w.shape
        res = _adamw(w.reshape(shp[-2:]), g.reshape(shp[-2:]), m.reshape(shp[-2:]), v.reshape(shp[-2:]), name)
        return [g.reshape(shp)] + [r.reshape(shp) for r in res]

    r_in = big(w_in, g_w_in, m_w_in, v_w_in, "adamw_w_in")
    r_o = big(w_o, g_w_o, m_w_o, v_w_o, "adamw_w_o")
    r_1 = big(w_1, g_w_1, m_w_1, v_w_1, "adamw_w_1")
    r_2 = big(w_2, g_w_2, m_w_2, v_w_2, "adamw_w_2")
    r_ada = [t.reshape(w_ada.shape) for t in (g_w_ada, d_w_ada, nm_w_ada, nv_w_ada)]

    def ordered(k):
        sm = [small_g, d_small, m_small, v_small][k]
        return [sm[0], r_ada[k], *sm[1:6], r_in[k], *sm[6:15], r_o[k], r_1[k], r_2[k]]

    return (loss, grad_x.reshape(x.shape), *ordered(0), *ordered(1), *ordered(2), *ordered(3))
```

```python
import functools
import math

import jax
import jax.numpy as jnp
from jax import lax
from jax.experimental import pallas as pl
from jax.experimental.pallas import tpu as pltpu

F32 = jnp.float32
BF16 = jnp.bfloat16
MESH = pl.DeviceIdType.MESH
ANY = pl.BlockSpec(memory_space=pl.ANY)

GLA_HEADS = 8
GLA_CHUNK = 64
GLA_LOWRANK = 16
GLA_TAU = 16.0
ROPE_BASE = 10000.0
GRID_W = 64
SG_GROUPS = 4
SG_CHUNK = 128
N_MOD = 6
EPS = 1e-6
ADAM_LR = 0.001
ADAM_B1 = 0.9
ADAM_B2 = 0.999
ADAM_EPS = 1e-08
ADAM_WD = 0.01
ADAM_STEP = 10

LANES = 128
VMEM_LIMIT = 56 << 20
N_DEV = 8
N_CHIP = 4


def _params(sem=None):
    return pltpu.CompilerParams(dimension_semantics=sem, vmem_limit_bytes=VMEM_LIMIT)


def _pick(dim, target, unit=LANES):
    best = None
    for t in range(unit, min(dim, target) + 1, unit):
        if dim % t == 0:
            best = t
    return dim if best is None else best


def _dg(a, b, ca, cb, precision=None):
    return lax.dot_general(a, b, (((ca,), (cb,)), ((), ())), preferred_element_type=F32,
                           precision=precision)


def _place():
    return lax.axis_index("x"), lax.axis_index("y"), lax.axis_index("c")


def _allgather_small(v, name):
    m_per, n = v.shape

    def body(x_ref, out_ref, send_sems, recv_sems, local_sem):
        x, y, c = _place()
        me, sibling = (x, y, c), (x, y, 1 - c)
        chips = [(1 - x, y), (x, 1 - y), (1 - x, 1 - y)]

        def rows(px, py, pc):
            return out_ref.at[pl.ds((4 * px + 2 * py + pc) * m_per, m_per), :]

        def copy(k, block, to, src=None):
            return pltpu.make_async_remote_copy(
                src_ref=rows(*block) if src is None else src, dst_ref=rows(*block),
                send_sem=send_sems.at[k], recv_sem=recv_sems.at[k],
                device_id=to, device_id_type=MESH)

        mine = pltpu.make_async_copy(x_ref, rows(*me), local_sem)
        mine.start()
        first = [copy(0, me, sibling, src=x_ref)]
        first += [copy(1 + j, me, (*chip, c), src=x_ref) for j, chip in enumerate(chips)]
        for cp in first:
            cp.start()
        passed = [copy(4 + j, (*chip, c), sibling) for j, chip in enumerate(chips)]
        for j, chip in enumerate(chips):
            copy(1 + j, (*chip, c), me).wait_recv()
            passed[j].start()
        copy(0, sibling, me).wait_recv()
        for j, chip in enumerate(chips):
            copy(4 + j, (*chip, 1 - c), me).wait_recv()
        for cp in first + passed:
            cp.wait_send()
        mine.wait()

    return pl.pallas_call(
        body, name=name,
        out_shape=jax.ShapeDtypeStruct((N_DEV * m_per, n), v.dtype),
        in_specs=[pl.BlockSpec(memory_space=pltpu.VMEM)],
        out_specs=pl.BlockSpec(memory_space=pltpu.VMEM),
        scratch_shapes=[pltpu.SemaphoreType.DMA((7,)), pltpu.SemaphoreType.DMA((7,)),
                        pltpu.SemaphoreType.DMA],
        compiler_params=pltpu.CompilerParams(vmem_limit_bytes=VMEM_LIMIT),
    )(v)


def _cast_blocks(w, s_me, name):
    _, r, cols = w.shape
    tr = _row_tile(r, cols, 4)

    def body(s_ref, w_ref, o_ref):
        o_ref[...] = w_ref[...].astype(BF16)

    return pl.pallas_call(
        body, name=name,
        out_shape=jax.ShapeDtypeStruct((N_DEV, r, cols), BF16),
        grid_spec=pltpu.PrefetchScalarGridSpec(
            num_scalar_prefetch=1, grid=(2, r // tr),
            in_specs=[pl.BlockSpec((None, tr, cols), lambda h, i, s: (h, i, 0))],
            out_specs=pl.BlockSpec((None, tr, cols), lambda h, i, s: (2 * s[0] + h, i, 0))),
        compiler_params=_params(("arbitrary", "arbitrary")),
    )(s_me, w)


def _gather_big(ws, name):
    nw = len(ws)

    def body(*refs):
        outs = refs[nw:2 * nw]
        send_sems, recv_sems = refs[2 * nw:]
        x, y, c = _place()
        me, sibling = (x, y, c), (x, y, 1 - c)
        chips = [(1 - x, y), (x, 1 - y), (1 - x, 1 - y)]

        def blk(px, py, pc):
            return 4 * px + 2 * py + pc

        def copy(w, k, block, to):
            return pltpu.make_async_remote_copy(
                src_ref=outs[w].at[block], dst_ref=outs[w].at[block],
                send_sem=send_sems.at[6 * w + k], recv_sem=recv_sems.at[6 * w + k],
                device_id=to, device_id_type=MESH)

        first = []
        for w in range(nw):
            for j, chip in enumerate(chips):
                cp = copy(w, j, blk(x, y, c), (*chip, c))
                cp.start()
                first.append(cp)
        passed = []
        for w in range(nw):
            for j, chip in enumerate(chips):
                copy(w, j, blk(*chip, c), me).wait_recv()
                cp = copy(w, 3 + j, blk(*chip, c), sibling)
                cp.start()
                passed.append(cp)
        for w in range(nw):
            for j, chip in enumerate(chips):
                copy(w, 3 + j, blk(*chip, 1 - c), me).wait_recv()
        for cp in first + passed:
            cp.wait_send()

    return pl.pallas_call(
        body, name=name,
        out_shape=[jax.ShapeDtypeStruct(w.shape, w.dtype) for w in ws],
        in_specs=[ANY] * nw, out_specs=[ANY] * nw,
        input_output_aliases={w: w for w in range(nw)},
        scratch_shapes=[pltpu.SemaphoreType.DMA((6 * nw,)), pltpu.SemaphoreType.DMA((6 * nw,))],
    )(*ws)


def _rs_sibling(gs, name):
    nw = len(gs)

    def body(*refs):
        ins, outs = refs[:nw], refs[nw:2 * nw]
        send_sems, recv_sems = refs[2 * nw:]
        x, y, c = _place()
        cps = []
        for w in range(nw):
            for s in range(N_CHIP):
                cp = pltpu.make_async_remote_copy(
                    src_ref=ins[w].at[2 * s + (1 - c)], dst_ref=outs[w].at[s],
                    send_sem=send_sems.at[N_CHIP * w + s], recv_sem=recv_sems.at[N_CHIP * w + s],
                    device_id=(x, y, 1 - c), device_id_type=MESH)
                cp.start()
                cps.append(cp)
        for cp in cps:
            cp.wait()

    return pl.pallas_call(
        body, name=name,
        out_shape=[jax.ShapeDtypeStruct((N_CHIP,) + g.shape[1:], g.dtype) for g in gs],
        in_specs=[ANY] * nw, out_specs=[ANY] * nw,
        scratch_shapes=[pltpu.SemaphoreType.DMA((N_CHIP * nw,)), pltpu.SemaphoreType.DMA((N_CHIP * nw,))],
    )(*gs)


def _rs_final(fs, name):
    nw = len(fs)

    def body(*refs):
        outs = refs[nw:2 * nw]
        send_sems, recv_sems = refs[2 * nw:]
        x, y, c = _place()
        cps = []
        for w in range(nw):
            cp = pltpu.make_async_remote_copy(
                src_ref=outs[w].at[c], dst_ref=outs[w].at[c],
                send_sem=send_sems.at[w], recv_sem=recv_sems.at[w],
                device_id=(x, y, 1 - c), device_id_type=MESH)
            cp.start()
            cps.append(cp)
        for cp in cps:
            cp.wait()

    return pl.pallas_call(
        body, name=name,
        out_shape=[jax.ShapeDtypeStruct(f.shape, f.dtype) for f in fs],
        in_specs=[ANY] * nw, out_specs=[ANY] * nw,
        input_output_aliases={w: w for w in range(nw)},
        scratch_shapes=[pltpu.SemaphoreType.DMA((nw,)), pltpu.SemaphoreType.DMA((nw,))],
    )(*fs)


_PHASE_COPIES = {"gather_ici": 3, "gather_d2d": 3, "gather_chain": 6, "rs_sibling": N_CHIP, "rs_chips": 3}
QUARTERS = 4


def _ph(buf, legs, src=None):
    return dict(buf=buf, src=src, legs=legs)


def _n_copies(ph):
    return sum(_PHASE_COPIES[kind] for kind, _, _ in ph["legs"])


def _phase_copies(ph, src, buf, send_sems, recv_sems, base):
    x, y, c = _place()
    sibling = (x, y, 1 - c)
    chips = [(1 - x, y), (x, 1 - y), (1 - x, 1 - y)]
    r = buf.shape[1]

    def make(k, trip):
        a, b, dev = trip
        return pltpu.make_async_remote_copy(src_ref=a, dst_ref=b, send_sem=send_sems.at[base + k], recv_sem=recv_sems.at[base + k],
                                            device_id=dev, device_id_type=MESH)

    out = []
    for kind, lo, hi in ph["legs"]:
        rows = pl.ds(lo * r // QUARTERS, (hi - lo) * r // QUARTERS)
        ici = [(buf.at[4 * x + 2 * y + c, rows], buf.at[4 * x + 2 * y + c, rows], (*chip, c)) for chip in chips]
        d2d = [(buf.at[4 * chip[0] + 2 * chip[1] + c, rows], buf.at[4 * chip[0] + 2 * chip[1] + c, rows], sibling) for chip in chips]
        if kind == "gather_ici":
            trips, later = ici, []
        elif kind == "gather_d2d":
            trips, later = d2d, []
        elif kind == "gather_chain":
            trips, later = ici, d2d
        elif kind == "rs_sibling":
            trips, later = [(src.at[2 * s + (1 - c), rows], buf.at[s, rows], sibling) for s in range(N_CHIP)], []
        else:
            trips, later = [(src.at[2 * chip[0] + chip[1], rows], buf.at[j, rows], (*chip, c)) for j, chip in enumerate(chips)], []
        out.append(([make(k, t) for k, t in enumerate(trips)], [make(len(trips) + k, t) for k, t in enumerate(later)]))
        base += _PHASE_COPIES[kind]
    return out


def _side_call(inner, grid, in_specs, out_specs, out_shape, scratch, args, phases, name, semantics, mid=0.8):
    n_in, n_out, n_ph = len(in_specs), len(out_specs), len(phases)
    if n_ph == 0:
        outs = pl.pallas_call(inner, name=name, grid=grid, in_specs=in_specs, out_specs=out_specs, out_shape=out_shape,
                              scratch_shapes=scratch, compiler_params=_params(semantics))(*args)
        return list(outs), []
    n_cp = sum(_n_copies(p) for p in phases)
    side_args, buf_pos, src_pos = [], [], []
    for p in phases:
        buf_pos.append(len(side_args))
        side_args.append(p["buf"])
        src_pos.append(len(side_args) if p["src"] is not None else None)
        if p["src"] is not None:
            side_args.append(p["src"])
    n_side = len(side_args)
    total = math.prod(grid)
    mid_lin = min(total - 1, int(total * mid))

    def body(*refs):
        b_in, s_in = refs[:n_in], refs[n_in:n_in + n_side]
        b_out, s_out = refs[n_in + n_side:n_in + n_side + n_out], refs[n_in + n_side + n_out:n_in + n_side + n_out + n_ph]
        rest = refs[n_in + n_side + n_out + n_ph:]
        send_sems, recv_sems = rest[-2:]
        lin = functools.reduce(lambda acc, ag: acc * ag[1] + pl.program_id(ag[0]), list(enumerate(grid))[1:], pl.program_id(0))

        def copies():
            out, base = [], 0
            for p, sp, so in zip(phases, src_pos, s_out):
                out += _phase_copies(p, None if sp is None else s_in[sp], so, send_sems, recv_sems, base)
                base += _n_copies(p)
            return out

        @pl.when(lin == 0)
        def _():
            for a, _ in copies():
                for cp in a:
                    cp.start()

        if any(kind == "gather_chain" for p in phases for kind, _, _ in p["legs"]):
            @pl.when(lin == mid_lin)
            def _():
                for a, b in copies():
                    if b:
                        for cp in a:
                            cp.wait()
                        for cp in b:
                            cp.start()

        inner(*b_in, *b_out, *rest[:-2])

        @pl.when(lin == total - 1)
        def _():
            for a, b in copies():
                for cp in (b if b else a):
                    cp.wait()

    outs = pl.pallas_call(
        body, name=name, grid=grid,
        in_specs=list(in_specs) + [ANY] * n_side, out_specs=list(out_specs) + [ANY] * n_ph,
        out_shape=list(out_shape) + [jax.ShapeDtypeStruct(p["buf"].shape, p["buf"].dtype) for p in phases],
        input_output_aliases={n_in + bp: n_out + k for k, bp in enumerate(buf_pos)},
        scratch_shapes=list(scratch) + [pltpu.SemaphoreType.DMA((n_cp,)), pltpu.SemaphoreType.DMA((n_cp,))],
        compiler_params=_params(("arbitrary",) * len(grid)),
    )(*args, *side_args)
    return list(outs[:n_out]), list(outs[n_out:])


def _row_tile(r, cols, itemsize):
    t = r
    while t * cols * itemsize > (2 << 20) and t % 16 == 0:
        t //= 2
    return t


def _sum_sibling(g, r1, c_me, name):
    _, r, cols = g.shape
    tr = _row_tile(r, cols, 4)

    def body(c_ref, g_ref, r_ref, o_ref):
        o_ref[...] = (g_ref[...].astype(F32) + r_ref[...].astype(F32)).astype(o_ref.dtype)

    return pl.pallas_call(
        body, name=name,
        out_shape=jax.ShapeDtypeStruct((N_CHIP, r, cols), g.dtype),
        grid_spec=pltpu.PrefetchScalarGridSpec(
            num_scalar_prefetch=1, grid=(N_CHIP, r // tr),
            in_specs=[pl.BlockSpec((None, tr, cols), lambda s, i, c: (2 * s + c[0], i, 0)),
                      pl.BlockSpec((None, tr, cols), lambda s, i, c: (s, i, 0))],
            out_specs=pl.BlockSpec((None, tr, cols), lambda s, i, c: (s, i, 0))),
        compiler_params=_params(("arbitrary", "arbitrary")),
    )(c_me, g, r1)


def _sum_chips(p, r2, sc_me, name):
    _, r, cols = p.shape
    tr = _row_tile(r, cols, 4)

    def body(s_ref, p_ref, a_ref, b_ref, c_ref, o_ref):
        o_ref[...] = ((p_ref[...].astype(F32) + a_ref[...].astype(F32)) + b_ref[...].astype(F32)) + c_ref[...].astype(F32)

    return pl.pallas_call(
        body, name=name,
        out_shape=jax.ShapeDtypeStruct((2, r, cols), F32),
        grid_spec=pltpu.PrefetchScalarGridSpec(
            num_scalar_prefetch=1, grid=(r // tr,),
            in_specs=[pl.BlockSpec((None, tr, cols), lambda i, s: (s[0], i, 0)),
                      pl.BlockSpec((None, tr, cols), lambda i, s: (0, i, 0)),
                      pl.BlockSpec((None, tr, cols), lambda i, s: (1, i, 0)),
                      pl.BlockSpec((None, tr, cols), lambda i, s: (2, i, 0))],
            out_specs=pl.BlockSpec((None, tr, cols), lambda i, s: (s[1], i, 0))),
        compiler_params=_params(("arbitrary",)),
    )(sc_me, p, r2, r2, r2)


def _shard_columns(g_al, shard_cols, bound, off_lo, off_hi, rh, name):
    d, acols = g_al.shape
    wp = -(-shard_cols // LANES) * LANES
    tr = min(LANES, rh)
    nt = acols // LANES

    def body(x_ref, o_ref):
        s = pl.program_id(1)
        lane = lax.broadcasted_iota(jnp.int32, (tr, LANES), 1)

        def tile(q):
            q = max(0, min(nt - 1, q))
            return x_ref[:, q * LANES:(q + 1) * LANES].astype(F32)

        def read(start):
            q, sh = divmod(start, LANES)
            if sh == 0:
                return tile(q)
            return jnp.where(lane < LANES - sh, pltpu.roll(tile(q), LANES - sh, 1), pltpu.roll(tile(q + 1), LANES - sh, 1))

        for k in range(N_CHIP):
            @pl.when(s == k)
            def _(k=k):
                for t in range(wp // LANES):
                    n0 = k * shard_cols + t * LANES
                    if n0 + LANES <= bound:
                        v = read(n0 + off_lo)
                    elif n0 >= bound:
                        v = read(n0 + off_hi)
                    else:
                        v = jnp.where(lane < bound - n0, read(n0 + off_lo), read(n0 + off_hi))
                    o_ref[:, t * LANES:(t + 1) * LANES] = v.astype(o_ref.dtype)

    return pl.pallas_call(
        body, name=name, grid=(d // tr, N_CHIP),
        out_shape=jax.ShapeDtypeStruct((N_DEV, rh, wp), BF16),
        in_specs=[pl.BlockSpec((tr, acols), lambda i, s: (i, 0))],
        out_specs=pl.BlockSpec((None, tr, wp), lambda i, s: (2 * s + (i * tr) // rh, ((i * tr) % rh) // tr, 0)),
        compiler_params=_params(("arbitrary", "arbitrary")),
    )(g_al)


def _mm(a, b, M, N, K, *, name, ta=False, tb=False, out_dtypes=(F32,), tm=1024, tn=1024, tk=2048,
        a_spec=None, b_spec=None, out_specs=None, out_shapes=None, epi=None, epi_in=(), epi_specs=(), side=(), side_mid=0.8):
    tm, tn, tk = min(tm, M), min(tn, N), min(tk, K)
    assert M % tm == 0 and N % tn == 0 and K % tk == 0, (name, M, N, K, tm, tn, tk)
    nk = K // tk
    n_epi, n_out = len(epi_in), len(out_dtypes)
    if a_spec is None:
        a_spec = pl.BlockSpec((tk, tm), lambda i, j, k: (k, i)) if ta else pl.BlockSpec((tm, tk), lambda i, j, k: (i, k))
    if b_spec is None:
        b_spec = pl.BlockSpec((tn, tk), lambda i, j, k: (j, k)) if tb else pl.BlockSpec((tk, tn), lambda i, j, k: (k, j))
    if out_specs is None:
        out_specs = [pl.BlockSpec((tm, tn), lambda i, j, k: (i, j))] * n_out
        out_shapes = [jax.ShapeDtypeStruct((M, N), dt) for dt in out_dtypes]

    def body(a_ref, b_ref, *rest):
        epi_refs, o_refs, acc = rest[:n_epi], rest[n_epi:n_epi + n_out], rest[-1]
        k = pl.program_id(2)

        @pl.when(k == 0)
        def _():
            acc[...] = jnp.zeros_like(acc)

        acc[...] += _dg(a_ref[...].astype(BF16), b_ref[...].astype(BF16), 0 if ta else 1, 1 if tb else 0)

        @pl.when(k == nk - 1)
        def _():
            r = acc[...]
            vals = (r,) if epi is None else epi(r, *[e[...] for e in epi_refs])
            for o_ref, v in zip(o_refs, vals):
                o_ref[...] = v.astype(o_ref.dtype)

    outs, side_outs = _side_call(body, (M // tm, N // tn, nk), [a_spec, b_spec, *epi_specs], out_specs, out_shapes,
                                 [pltpu.VMEM((tm, tn), F32)], (a, b, *epi_in), list(side), name,
                                 ("parallel", "parallel", "arbitrary"), mid=side_mid)
    if side:
        return outs + side_outs
    return outs[0] if n_out == 1 else outs


def _T(arr, width, col=0, lead=None):
    return ("tile", arr, width, col, lead)


def _W(arr):
    return ("whole", arr)


def _rows(fn, n_rows, tr, ins, tile_outs, acc_outs, name):
    tr = min(tr, n_rows)
    assert n_rows % tr == 0, (name, n_rows, tr)
    in_specs, args = [], []
    for d in ins:
        if d[0] == "tile":
            _, arr, width, col, lead = d
            if lead is None:
                in_specs.append(pl.BlockSpec((tr, width), lambda i, col=col: (i, col)))
            else:
                in_specs.append(pl.BlockSpec((None, tr, width), lambda i, col=col, lead=lead: (lead, i, col)))
            args.append(arr)
        else:
            arr = d[1]
            in_specs.append(pl.BlockSpec(arr.shape, lambda i, nd=arr.ndim: (0,) * nd))
            args.append(arr)
    n_in, n_t = len(ins), len(tile_outs)
    out_shape = [jax.ShapeDtypeStruct((n_rows, w), dt) for w, dt in tile_outs]
    out_specs = [pl.BlockSpec((tr, w), lambda i: (i, 0)) for w, _ in tile_outs]
    out_shape += [jax.ShapeDtypeStruct(s, F32) for s in acc_outs]
    out_specs += [pl.BlockSpec(s, lambda i, nd=len(s): (0,) * nd) for s in acc_outs]

    def body(*refs):
        in_refs, t_refs, a_refs = refs[:n_in], refs[n_in:n_in + n_t], refs[n_in + n_t:]
        vals = fn(*[r[...] for r in in_refs])
        for r, v in zip(t_refs, vals[:n_t]):
            r[...] = v.astype(r.dtype)
        first = pl.program_id(0) == 0
        for r, v in zip(a_refs, vals[n_t:]):
            @pl.when(first)
            def _(r=r, v=v):
                r[...] = v

            @pl.when(jnp.logical_not(first))
            def _(r=r, v=v):
                r[...] += v

    return pl.pallas_call(
        body, name=name, out_shape=out_shape, grid=(n_rows // tr,),
        in_specs=in_specs, out_specs=out_specs,
        compiler_params=_params(("arbitrary",)),
    )(*args)


def _colsum(t):
    return jnp.sum(t, axis=0, keepdims=True)


def _prenorm(x, g, sc, sh):
    xf = x.astype(F32)
    return xf * lax.rsqrt(jnp.mean(xf * xf, axis=-1, keepdims=True) + EPS) * g * (1.0 + sc) + sh


def _postnorm(x, y, gate, pg):
    return x + gate * (y * lax.rsqrt(jnp.mean(y * y, axis=-1, keepdims=True) + EPS) * pg)


def _gelu(t):
    return 0.5 * t * (1.0 + lax.erf(t * (2.0 ** -0.5)))


def _sg_pre(zu, zv, lng, lnb):
    u, vr = _gelu(zu), _gelu(zv)
    mu = jnp.mean(vr, axis=-1, keepdims=True)
    var = jnp.mean(jnp.square(vr - mu), axis=-1, keepdims=True)
    return u, (vr - mu) * lax.rsqrt(var + EPS) * lng + lnb


def _sg_mix(vv, ws_ref_vals, bs_vals, gw):
    parts = []
    for g in range(SG_GROUPS):
        s = _dg(ws_ref_vals[g].astype(BF16), vv[:, g * gw:(g + 1) * gw].astype(BF16), 1, 0)
        parts.append(s + bs_vals[g])
    return jnp.concatenate(parts, axis=1)


def _readout(o, r, g, heads, dv):
    parts = []
    for h in range(heads):
        oh = o[:, h * dv:(h + 1) * dv]
        parts.append(oh * lax.rsqrt(jnp.mean(oh * oh, axis=-1, keepdims=True) + EPS))
    return jnp.concatenate(parts, axis=1) * g * (r * jax.nn.sigmoid(r))


def _log_sigmoid(a):
    return jnp.minimum(a, 0.0) - jnp.log(1.0 + jnp.exp(-jnp.abs(a)))


def _rope_swap(t, m):
    lane = lax.broadcasted_iota(jnp.int32, t.shape, 1)
    return jnp.where((lane % (2 * m)) < m, pltpu.roll(t, 3 * m, 1), pltpu.roll(t, m, 1))


def _rope(t, cos, sin, heads, dk):
    parts = []
    for h in range(heads):
        th = t[:, h * dk:(h + 1) * dk]
        parts.append(th * cos + _rope_swap(th, dk // 4) * sin)
    return jnp.concatenate(parts, axis=1)


def _rope_t(dt, cos, sin, heads, dk):
    parts = []
    for h in range(heads):
        dh = dt[:, h * dk:(h + 1) * dk]
        parts.append(dh * cos + _rope_swap(dh * sin, dk // 4))
    return jnp.concatenate(parts, axis=1)


def _chunk_cumsum(t, upwards):
    n = t.shape[0]
    row = lax.broadcasted_iota(jnp.int32, (n, n), 0)
    col = lax.broadcasted_iota(jnp.int32, (n, n), 1)
    shift = GLA_CHUNK.bit_length() - 1
    same = jnp.right_shift(row, shift) == jnp.right_shift(col, shift)
    tri = jnp.logical_and(same, col <= row if upwards else col >= row)
    return _dg(tri.astype(F32), t, 1, 0, precision=lax.Precision.HIGHEST)


def _chunk_terms(d, qv, kv, b, C):
    row = lax.broadcasted_iota(jnp.int32, (C, C), 0)
    col = lax.broadcasted_iota(jnp.int32, (C, C), 1)
    tri = row >= col if d == 0 else row <= col
    end_row = lax.broadcasted_iota(jnp.int32, b.shape, 0) == (C - 1 if d == 0 else 0)
    btot = _colsum(jnp.where(end_row, b, 0.0))
    eb, enb, ebt = jnp.exp(b), jnp.exp(-b), jnp.exp(btot - b)
    return tri, btot, eb, enb, ebt, qv * eb, kv * enb, kv * ebt


def _gla_fwd(q, k, zv, v_col0, la, st0, heads, dk, dv, name, side=(), side_mid=0.8):
    n, C, H = q.shape[0], GLA_CHUNK, heads
    nc = n // C

    def body(qf, kf, vf, laf, qb, kb, vb_, lab, st0_ref, of_ref, ob_ref, sf_ref, sb_ref, fin_ref, st):
        i = pl.program_id(1)

        @pl.when(i == 0)
        def _():
            st[...] = st0_ref[...]

        for d, (q_ref, k_ref, v_ref, la_ref, o_ref, save_ref) in enumerate(((qf, kf, vf, laf, of_ref, sf_ref), (qb, kb, vb_, lab, ob_ref, sb_ref))):
            tri, btot, _, _, _, qt, kt, kh = _chunk_terms(d, q_ref[...], k_ref[...], la_ref[...], C)
            s = st[d]
            vb = v_ref[...].astype(BF16)
            qtb = qt.astype(BF16)
            att = jnp.where(tri, _dg(qtb, kt.astype(BF16), 1, 1), 0.0)
            o_ref[...] = _dg(qtb, s.astype(BF16), 1, 1) + _dg(att.astype(BF16), vb, 1, 0)
            save_ref[...] = s
            s_new = s * jnp.exp(btot) + _dg(vb, kh.astype(BF16), 0, 0)
            st[d] = s_new

            @pl.when(i == nc - 1)
            def _(d=d, s_new=s_new):
                fin_ref[d] = s_new

    def seq(width, col0, rev, dir_cols=0):
        if rev:
            return pl.BlockSpec((C, width), lambda h, i: (nc - 1 - i, col0 + dir_cols + h))
        return pl.BlockSpec((C, width), lambda h, i: (i, col0 + h))

    both = pl.BlockSpec((2, None, dv, dk), lambda h, i: (0, h, 0, 0))
    outs, side_outs = _side_call(
        body, (H, nc),
        [seq(dk, 0, False), seq(dk, 0, False), seq(dv, v_col0, False), seq(dk, 0, False),
         seq(dk, 0, True), seq(dk, 0, True), seq(dv, v_col0, True), seq(dk, 0, True, H), both],
        [seq(dv, 0, False), seq(dv, 0, True),
         pl.BlockSpec((None, None, dv, dk), lambda h, i: (h, i, 0, 0)),
         pl.BlockSpec((None, None, dv, dk), lambda h, i: (h, nc - 1 - i, 0, 0)), both],
        [jax.ShapeDtypeStruct((n, H * dv), F32)] * 2 + [jax.ShapeDtypeStruct((H, nc, dv, dk), F32)] * 2
        + [jax.ShapeDtypeStruct((2, H, dv, dk), F32)],
        [pltpu.VMEM((2, dv, dk), F32)], (q, k, zv, la, q, k, zv, la, st0), list(side), name, ("arbitrary", "arbitrary"),
        mid=side_mid)
    return outs + side_outs


def _gla_bwd(q, k, zv, v_col0, la, saved_f, saved_b, do, dfin, heads, dk, dv, name, side=()):
    n, C, H = q.shape[0], GLA_CHUNK, heads
    nc = n // C

    def body(qf, kf, vf, laf, sf, dof, qb, kb, vb_, lab, sb, dob_, dfin_ref,
             dqf, dqb, dkf, dkb, dvf, dvb, dlaf, dlab, d0_ref, dst):
        i = pl.program_id(1)

        @pl.when(i == 0)
        def _():
            dst[...] = dfin_ref[...]

        dirs = ((qf, kf, vf, laf, sf, dof, dqf, dkf, dvf, dlaf), (qb, kb, vb_, lab, sb, dob_, dqb, dkb, dvb, dlab))
        for d, (q_ref, k_ref, v_ref, la_ref, save_ref, do_ref, dq_ref, dk_ref, dv_ref, dla_ref) in enumerate(dirs):
            tri, btot, eb, enb, ebt, qt, kt, kh = _chunk_terms(d, q_ref[...], k_ref[...], la_ref[...], C)
            s, dsn = save_ref[...], dst[d]
            vb, dob = v_ref[...].astype(BF16), do_ref[...].astype(BF16)
            qtb, ktb, khb, dsnb = qt.astype(BF16), kt.astype(BF16), kh.astype(BF16), dsn.astype(BF16)
            att = jnp.where(tri, _dg(qtb, ktb, 1, 1), 0.0).astype(BF16)
            datt = jnp.where(tri, _dg(dob, vb, 1, 1), 0.0).astype(BF16)
            dqt = _dg(dob, s.astype(BF16), 1, 0) + _dg(datt, ktb, 1, 0)
            dkt = _dg(datt, qtb, 0, 0)
            dkh = _dg(vb, dsnb, 1, 0)
            dv_ref[...] = _dg(att, dob, 0, 0) + _dg(khb, dsnb, 1, 1)
            ebtot = jnp.exp(btot)
            dbtot = ebtot * _colsum(s * dsn) + _colsum(dkh * kh)
            s0 = dsn * ebtot + _dg(dob, qtb, 0, 0)
            dst[d] = s0
            db = dqt * qt - dkt * kt - dkh * kh
            dq_ref[...] = dqt * eb
            dk_ref[...] = dkt * enb + dkh * ebt
            end_row = lax.broadcasted_iota(jnp.int32, db.shape, 0) == (C - 1 if d == 0 else 0)
            dla_ref[...] = db + jnp.where(end_row, dbtot, 0.0)

            @pl.when(i == nc - 1)
            def _(d=d, s0=s0):
                d0_ref[d] = s0

    def seq(width, col0, fwd_dir, dir_cols=0):
        if fwd_dir:
            return pl.BlockSpec((C, width), lambda h, i: (nc - 1 - i, col0 + h))
        return pl.BlockSpec((C, width), lambda h, i: (i, col0 + dir_cols + h))

    both = pl.BlockSpec((2, None, dv, dk), lambda h, i: (0, h, 0, 0))
    sav_f = pl.BlockSpec((None, None, dv, dk), lambda h, i: (h, nc - 1 - i, 0, 0))
    sav_b = pl.BlockSpec((None, None, dv, dk), lambda h, i: (h, i, 0, 0))
    outs, side_outs = _side_call(
        body, (H, nc),
        [seq(dk, 0, True), seq(dk, 0, True), seq(dv, v_col0, True), seq(dk, 0, True), sav_f, seq(dv, 0, True),
         seq(dk, 0, False), seq(dk, 0, False), seq(dv, v_col0, False), seq(dk, 0, False, H), sav_b, seq(dv, 0, False), both],
        [seq(dk, 0, True), seq(dk, 0, False), seq(dk, 0, True), seq(dk, 0, False), seq(dv, 0, True), seq(dv, 0, False),
         seq(dk, 0, True), seq(dk, 0, False), both],
        [jax.ShapeDtypeStruct((n, H * dk), F32)] * 4 + [jax.ShapeDtypeStruct((n, H * dv), F32)] * 2
        + [jax.ShapeDtypeStruct((n, H * dk), F32)] * 2 + [jax.ShapeDtypeStruct((2, H, dv, dk), F32)],
        [pltpu.VMEM((2, dv, dk), F32)], (q, k, zv, la, saved_f, do, q, k, zv, la, saved_b, do, dfin), list(side), name,
        ("arbitrary", "arbitrary"))
    return outs + side_outs


def _adamw_math(w, g, m, v):
    m2 = ADAM_B1 * m + (1.0 - ADAM_B1) * g
    v2 = ADAM_B2 * v + (1.0 - ADAM_B2) * jnp.square(g)
    m_hat = m2 / (1.0 - ADAM_B1 ** ADAM_STEP)
    v_hat = v2 / (1.0 - ADAM_B2 ** ADAM_STEP)
    delta = -ADAM_LR * (m_hat / (jnp.sqrt(v_hat) + ADAM_EPS) + ADAM_WD * w)
    return delta, m2, v2


def _adamw(w, g, m, v, name, side=()):
    r, cols = w.shape
    tr = _row_tile(r, cols, 4 * 4)
    spec = pl.BlockSpec((tr, cols), lambda i: (i, 0))

    def body(w_ref, g_ref, m_ref, v_ref, d_ref, m2_ref, v2_ref):
        d_ref[...], m2_ref[...], v2_ref[...] = _adamw_math(w_ref[...], g_ref[...], m_ref[...], v_ref[...])

    outs, side_outs = _side_call(body, (r // tr,), [spec] * 4, [spec] * 3, [jax.ShapeDtypeStruct((r, cols), F32)] * 3,
                                 [], (w, g, m, v), list(side), name, ("parallel",))
    return outs + side_outs


def _ada_update(cond_t, dmod, w, m, v, name, side=()):
    r, cols = w.shape
    tr, tc = _pick(r, 512, 8), _pick(cols, 1024)
    spec = pl.BlockSpec((tr, tc), lambda i, j: (i, j))

    def body(c_ref, d_ref, w_ref, m_ref, v_ref, g_ref, dl_ref, m2_ref, v2_ref):
        g = _dg(c_ref[...].astype(BF16), d_ref[...].astype(BF16), 1, 0)
        g_ref[...] = g
        dl_ref[...], m2_ref[...], v2_ref[...] = _adamw_math(w_ref[...], g, m_ref[...], v_ref[...])

    outs, side_outs = _side_call(
        body, (r // tr, cols // tc),
        [pl.BlockSpec((tr, cond_t.shape[1]), lambda i, j: (i, 0)), pl.BlockSpec((dmod.shape[0], tc), lambda i, j: (0, j)), spec, spec, spec],
        [spec] * 4, [jax.ShapeDtypeStruct((r, cols), F32)] * 4, [], (cond_t, dmod, w, m, v), list(side), name, ("parallel", "parallel"))
    return outs + side_outs


def _pack(parts, rows=8):
    flat = jnp.concatenate([p.reshape(-1).astype(F32) for p in parts])
    n = -(-flat.shape[0] // (rows * LANES)) * LANES
    return jnp.pad(flat, (0, rows * n - flat.shape[0])).reshape(rows, n)


def _unpack(flat, shapes):
    out, off = [], 0
    for s in shapes:
        size = math.prod(s)
        out.append(flat[off:off + size].reshape(s))
        off += size
    return out


def kernel(x, c, ctx, c_ctx, w_ada, b_ada, pre1_g, post1_g, pre2_g, post2_g, w_in, w_dec_f, b_dec_f, w_dec_b, b_dec_b, gla_norm_g, sg_ln_g, sg_ln_b, w_s, b_s, w_o, w_1, w_2, loss_target, m_c_ctx, m_w_ada, m_b_ada, m_pre1_g, m_post1_g, m_pre2_g, m_post2_g, m_w_in, m_w_dec_f, m_b_dec_f, m_w_dec_b, m_b_dec_b, m_gla_norm_g, m_sg_ln_g, m_sg_ln_b, m_w_s, m_b_s, m_w_o, m_w_1, m_w_2, v_c_ctx, v_w_ada, v_b_ada, v_pre1_g, v_post1_g, v_pre2_g, v_post2_g, v_w_in, v_w_dec_f, v_b_dec_f, v_w_dec_b, v_b_dec_b, v_gla_norm_g, v_sg_ln_g, v_sg_ln_b, v_w_s, v_b_s, v_w_o, v_w_1, v_w_2):
    N, D = x.shape[1], x.shape[2]
    NC = ctx.shape[1]
    H = GLA_HEADS
    VALW = D // 2
    DV = VALW // H
    DK = DV // 2
    KEYW = H * DK
    SGW = D - VALW
    GW = SGW // SG_GROUPS
    LR = 2 * GLA_LOWRANK
    F = w_1.shape[2] * N_CHIP
    FS = F // N_CHIP
    RH = D // 2
    MS = w_ada.shape[2]
    IN_COLS = w_in.shape[2] * N_CHIP
    K0, V0, R0, LF0 = KEYW, 2 * KEYW, 2 * KEYW + VALW, 2 * KEYW + 2 * VALW
    SG0 = LF0 + LR
    AQ, AK, AV, AR, ALR = 2 * SGW, 2 * SGW + KEYW, 2 * SGW + 2 * KEYW, 2 * SGW + 2 * KEYW + VALW, 2 * SGW + 2 * KEYW + 2 * VALW
    ACOLS = ALR + LANES
    assert IN_COLS == SG0 + 2 * SGW and N % SG_CHUNK == 0 and N % GLA_CHUNK == 0 and NC % GLA_CHUNK == 0

    ax, ay, ac = _place()
    s_me = (2 * ax + ay).astype(jnp.int32)
    b_me = (4 * ax + 2 * ay + ac).astype(jnp.int32)
    s_arr, c_arr = s_me.reshape(1), ac.astype(jnp.int32).reshape(1)
    sc_arr = jnp.concatenate([s_arr, c_arr])
    CS = IN_COLS // N_CHIP

    shards = [_cast_blocks(w_in[0].reshape(2, RH, CS), s_arr, "cast_w_in"), _cast_blocks(w_o[0].reshape(2, D // N_DEV, D), s_arr, "cast_w_o"),
              _cast_blocks(w_1[0].reshape(2, RH, FS), s_arr, "cast_w_1"), _cast_blocks(w_2[0].reshape(2, F // N_DEV, D), s_arr, "cast_w_2")]
    win_g, = _gather_big(shards[:1], "gather_weights")
    wo_buf, w1_buf, w2_buf = shards[1], shards[2], shards[3]
    w_in_nat = win_g.reshape(N_CHIP, 2, RH, IN_COLS // N_CHIP).transpose(1, 2, 0, 3).reshape(D, IN_COLS)
    w_al = jnp.concatenate([w_in_nat[:, SG0:], w_in_nat[:, :LF0], w_in_nat[:, LF0:SG0],
                            jnp.zeros((D, LANES - LR), BF16)], axis=1)

    n_dec = GLA_LOWRANK * (KEYW // N_CHIP)
    g0 = _allgather_small(_pack([c, w_dec_f, w_dec_b, gla_norm_g]), "gather_small0").reshape(N_DEV, -1)
    c_all = g0[:, :D]
    per_chip = g0[0::2]
    wdf = per_chip[:, D:D + n_dec].reshape(N_CHIP, GLA_LOWRANK, KEYW // N_CHIP).transpose(1, 0, 2).reshape(GLA_LOWRANK, KEYW)
    wdb = per_chip[:, D + n_dec:D + 2 * n_dec].reshape(N_CHIP, GLA_LOWRANK, KEYW // N_CHIP).transpose(1, 0, 2).reshape(GLA_LOWRANK, KEYW)
    gn_full = per_chip[:, D + 2 * n_dec:D + 2 * n_dec + H * (DV // N_CHIP)].reshape(N_CHIP, H, DV // N_CHIP).transpose(1, 0, 2).reshape(1, VALW)
    wd_f = jnp.zeros((LANES, KEYW), F32).at[:GLA_LOWRANK].set(wdf)
    wd_b = jnp.zeros((LANES, KEYW), F32).at[GLA_LOWRANK:LR].set(wdb)

    cond_in = jnp.zeros((16, D), F32).at[:N_DEV].set(c_all).at[N_DEV].set(c_ctx)
    b_ada_sh = lax.dynamic_slice(b_ada, (0, s_me * MS), (1, MS))

    def mod_epi(r, bias):
        return (r + bias,)

    def silu_rows(t):
        return (t * jax.nn.sigmoid(t),)

    cond = _rows(silu_rows, 16, 16, [_W(cond_in)], [(D, F32)], [], "cond_silu")[0]
    mod_sh = _mm(cond, w_ada[0], 16, MS, D, name="mod_matmul", tn=512, tk=D, epi=mod_epi, epi_in=(b_ada_sh,),
                 epi_specs=(pl.BlockSpec((1, min(512, MS)), lambda i, j, k: (0, j)),))
    g1m = _allgather_small(mod_sh, "gather_mod").reshape(N_DEV, 16, MS)[0::2]
    mod_all = g1m.transpose(1, 0, 2).reshape(16, N_CHIP * MS)
    mod_me = lax.dynamic_slice(mod_all, (b_me, 0), (1, N_MOD * D))
    sh1, sc1, gt1, sh2, sc2, gt2 = [mod_me[:, i * D:(i + 1) * D] for i in range(N_MOD)]
    csh1, csc1 = mod_all[N_DEV:N_DEV + 1, :D], mod_all[N_DEV:N_DEV + 1, D:2 * D]

    mq = DK // 4
    pos = jnp.arange(N)
    inv_freq = ROPE_BASE ** (-jnp.arange(mq, dtype=F32) / mq)
    ang_r = (pos // GRID_W).astype(F32)[:, None] * inv_freq[None, :]
    ang_c = (pos % GRID_W).astype(F32)[:, None] * inv_freq[None, :]
    cos_t = jnp.concatenate([jnp.cos(ang_r), jnp.cos(ang_r), jnp.cos(ang_c), jnp.cos(ang_c)], axis=1)
    sin_t = jnp.concatenate([-jnp.sin(ang_r), jnp.sin(ang_r), -jnp.sin(ang_c), jnp.sin(ang_c)], axis=1)

    x2, tgt, ctx2 = x[0], loss_target[0], ctx[0]
    TR = 128
    qscale = DK ** -0.5

    def prenorm_fwd(xa, g, sc, sh, n_rows, name):
        return _rows(lambda xv, gv, scv, shv: (_prenorm(xv, gv, scv, shv),), n_rows, TR,
                     [_T(xa, D), _W(g), _W(sc), _W(sh)], [(D, BF16)], [], name)[0]

    hx = prenorm_fwd(x2, pre1_g, sc1, sh1, N, "prenorm1_x")
    hc = prenorm_fwd(ctx2, pre1_g, csc1, csh1, NC, "prenorm1_ctx")
    tka = _pick(ACOLS, 1152)
    z_al, w1_buf = _mm(hx, w_al, N, ACOLS, D, name="in_proj_x", tn=tka, side=[_ph(w1_buf, [("gather_ici", 0, 2)])])
    zc_al = _mm(hc, w_al, NC, ACOLS, D, name="in_proj_ctx", tn=tka)

    def decay(lr, wdf_v, wdb_v, bf_v, bb_v):
        lrb = lr.astype(BF16)
        a_f = _dg(lrb, wdf_v.astype(BF16), 1, 0) + bf_v
        a_b = _dg(lrb, wdb_v.astype(BF16), 1, 0) + bb_v
        return a_f, a_b

    def running_decay(a_f, a_b):
        return jnp.concatenate([_chunk_cumsum(_log_sigmoid(a_f) / GLA_TAU, True), _chunk_cumsum(_log_sigmoid(a_b) / GLA_TAU, False)], axis=1)

    def prep_x(zq, zk, lr, cs, sn, wdf_v, wdb_v, bf_v, bb_v):
        return _rope(zq * qscale, cs, sn, H, DK), _rope(zk, cs, sn, H, DK), running_decay(*decay(lr, wdf_v, wdb_v, bf_v, bb_v))

    def prep_c(zk, lr, wdf_v, wdb_v, bf_v, bb_v):
        return zk, running_decay(*decay(lr, wdf_v, wdb_v, bf_v, bb_v))

    dec_w = [_W(wd_f), _W(wd_b), _W(b_dec_f), _W(b_dec_b)]
    q_r, k_r, la_x = _rows(prep_x, N, TR, [_T(z_al, KEYW, AQ // KEYW), _T(z_al, KEYW, AK // KEYW), _T(z_al, LANES, ALR // LANES),
                                           _T(cos_t, DK), _T(sin_t, DK)] + dec_w,
                           [(KEYW, F32), (KEYW, F32), (2 * KEYW, F32)], [], "gla_prep_x")
    k_c, la_c = _rows(prep_c, NC, TR, [_T(zc_al, KEYW, AK // KEYW), _T(zc_al, LANES, ALR // LANES)] + dec_w,
                      [(KEYW, F32), (2 * KEYW, F32)], [], "gla_prep_ctx")

    zero_state = jnp.zeros((2, H, DV, DK), F32)
    q_c = jnp.zeros((NC, KEYW), F32)
    _, _, savf_c, savb_c, st_c = _gla_fwd(q_c, k_c, zc_al, AV // DV, la_c, zero_state, H, DK, DV, "gla_fwd_ctx")
    o_f, o_b, savf_x, savb_x, _, wo_g, w1_buf = _gla_fwd(
        q_r, k_r, z_al, AV // DV, la_x, st_c, H, DK, DV, "gla_fwd_x",
        side=[_ph(wo_buf, [("gather_chain", 0, 4)]), _ph(w1_buf, [("gather_d2d", 0, 2), ("gather_ici", 2, 4)])], side_mid=0.55)
    w_o_f = wo_g.reshape(D, D)

    def readout_fwd(of, ob, r, g):
        return (_readout(of + ob, r, g, H, DV),)

    y_gla = _rows(readout_fwd, N, TR, [_T(o_f, VALW), _T(o_b, VALW), _T(z_al, VALW, AR // VALW), _W(gn_full)],
                  [(VALW, BF16)], [], "gla_readout")[0]

    bs_col = b_s[0].reshape(SG_GROUPS, SG_CHUNK, 1)

    def sg_fwd(zu, zv, lng, lnb, ws, bs):
        u, vv = _sg_pre(zu, zv, lng, lnb)
        return (u * _sg_mix(vv, ws, bs, GW),)

    y_sg = _rows(sg_fwd, N, SG_CHUNK, [_T(z_al, SGW, 0), _T(z_al, SGW, 1), _W(sg_ln_g), _W(sg_ln_b), _W(w_s[0]), _W(bs_col)],
                 [(SGW, BF16)], [], "sg_fwd")[0]
    ycat = jnp.concatenate([y_gla, y_sg], axis=1)
    y, w1_g, w2_buf = _mm(ycat, w_o_f, N, D, D, name="out_proj",
                          side=[_ph(w1_buf, [("gather_d2d", 2, 4)]), _ph(w2_buf, [("gather_ici", 0, 1)])])
    x1 = _rows(lambda xv, yv, gv, pv: (_postnorm(xv, yv, gv, pv),), N, TR,
               [_T(x2, D), _T(y, D), _W(gt1), _W(post1_g)], [(D, F32)], [], "postnorm1")[0]
    h2 = prenorm_fwd(x1, pre2_g, sc2, sh2, N, "prenorm2")

    tm1, tn1, tk1 = min(1024, N), min(1024, FS), min(2048, RH)
    w1_fwd_spec = pl.BlockSpec((None, tk1, tn1), lambda i, j, k: (2 * ((j * tn1) // FS) + (k * tk1) // RH, ((k * tk1) % RH) // tk1, ((j * tn1) % FS) // tn1))

    def relu2_epi(r):
        rf = jnp.maximum(r, 0.0)
        return rf * rf, rf

    act, rf, w2_g = _mm(h2, w1_g, N, F, D, name="mlp_up", out_dtypes=(BF16, BF16), tm=tm1, tn=tn1, tk=tk1, b_spec=w1_fwd_spec, epi=relu2_epi,
                        side=[_ph(w2_buf, [("gather_d2d", 0, 1), ("gather_chain", 1, 4)])], side_mid=0.88)
    w_2_f = w2_g.reshape(F, D)
    y2 = _mm(act, w_2_f, N, D, F, name="mlp_down")

    def final(x1v, y2v, gv, pv, tv):
        def loss_fn(x1a, y2a, ga, pa):
            err = _postnorm(x1a, y2a, ga, pa) - tv
            return 0.5 * jnp.sum(jnp.mean(err * err, axis=-1))
        loss, grads = jax.value_and_grad(loss_fn, argnums=(0, 1, 2, 3))(x1v, y2v, gv, pv)
        return grads[0], grads[1], jnp.full((1, LANES), loss, F32), _colsum(grads[2]), _colsum(grads[3])

    dx2, dy2, loss_acc, dgt2, dpost2 = _rows(final, N, TR, [_T(x1, D), _T(y2, D), _W(gt2), _W(post2_g), _T(tgt, D)],
                                             [(D, F32), (D, BF16)], [(1, LANES), (1, D), (1, D)], "loss_postnorm2_bwd")

    df = _mm(dy2, w_2_f, N, F, D, name="mlp_down_dx", tb=True, out_dtypes=(BF16,), epi=lambda r, rfv: (r * (2.0 * rfv.astype(F32)),),
             epi_in=(rf,), epi_specs=(pl.BlockSpec((min(1024, N), min(1024, F)), lambda i, j, k: (i, j)),))
    dw2 = _mm(act, dy2, F, D, N, name="mlp_down_dw", ta=True, out_dtypes=(BF16,)).reshape(N_DEV, F // N_DEV, D)
    tnb, tkb = min(1024, D, RH), min(2048, FS)
    w1_bwd_spec = pl.BlockSpec((None, tnb, tkb), lambda i, j, k: (2 * ((k * tkb) // FS) + (j * tnb) // RH, ((j * tnb) % RH) // tnb, ((k * tkb) % FS) // tkb))
    dh2, recv1_w2 = _mm(df, w1_g, N, D, F, name="mlp_up_dx", tb=True, tn=tnb, tk=tkb, b_spec=w1_bwd_spec,
                         side=[_ph(lax.empty((N_CHIP,) + dw2.shape[1:], BF16), [("rs_sibling", 0, 4)], src=dw2)])
    part_w2 = _sum_sibling(dw2, recv1_w2, c_arr, "rs_sum_sibling_w_2")
    tmw, tnw = min(1024, RH), min(1024, FS)
    dw1_spec = pl.BlockSpec((None, tmw, tnw), lambda i, j, k: (2 * ((j * tnw) // FS) + (i * tmw) // RH, ((i * tmw) % RH) // tmw, ((j * tnw) % FS) // tnw))
    dw1, recv2_w2 = _mm(h2, df, D, F, N, name="mlp_up_dw", ta=True, tm=tmw, tn=tnw, out_dtypes=(BF16,), out_specs=[dw1_spec],
                        out_shapes=[jax.ShapeDtypeStruct((N_DEV, RH, FS), BF16)],
                        side=[_ph(lax.empty((3,) + part_w2.shape[1:], BF16), [("rs_chips", 0, 3)], src=part_w2)])

    def prenorm_bwd(xv, gv, scv, shv, dh, dres):
        _, vjp = jax.vjp(_prenorm, xv, gv, scv, shv)
        dx, dg, dsc, dsh = vjp(dh)
        return dx + dres, _colsum(dg), _colsum(dsc), _colsum(dsh)

    dx1, dpre2, dsc2, dsh2 = _rows(prenorm_bwd, N, TR, [_T(x1, D), _W(pre2_g), _W(sc2), _W(sh2), _T(dh2, D), _T(dx2, D)],
                                   [(D, F32)], [(1, D)] * 3, "prenorm2_bwd")

    def postnorm_bwd(yv, gv, pv, dxv):
        _, vjp = jax.vjp(lambda ya, ga, pa: _postnorm(0.0, ya, ga, pa), yv, gv, pv)
        dy_, dg_, dp_ = vjp(dxv)
        return dy_, _colsum(dg_), _colsum(dp_)

    dy, dgt1, dpost1 = _rows(postnorm_bwd, N, TR, [_T(y, D), _W(gt1), _W(post1_g), _T(dx1, D)], [(D, BF16)], [(1, D)] * 2, "postnorm1_bwd")
    dycat, recv2_w2, recv1_w1 = _mm(dy, w_o_f, N, D, D, name="out_proj_dx", tb=True,
                                    side=[_ph(recv2_w2, [("rs_chips", 3, 4)], src=part_w2),
                                          _ph(lax.empty((N_CHIP,) + dw1.shape[1:], BF16), [("rs_sibling", 0, 4)], src=dw1)])
    part_w1 = _sum_sibling(dw1, recv1_w1, c_arr, "rs_sum_sibling_w_1")
    dwo = _mm(ycat, dy, D, D, N, name="out_proj_dw", ta=True, out_dtypes=(BF16,)).reshape(N_DEV, D // N_DEV, D)

    def readout_bwd(of, ob, r, g, dyv):
        _, vjp = jax.vjp(lambda o_, r_, g_: _readout(o_, r_, g_, H, DV), of + ob, r, g)
        do_, dr_, dg_ = vjp(dyv)
        return do_, dr_, _colsum(dg_)

    do_x, dz_r, dgn = _rows(readout_bwd, N, TR, [_T(o_f, VALW), _T(o_b, VALW), _T(z_al, VALW, AR // VALW), _W(gn_full), _T(dycat, VALW, 0)],
                            [(VALW, F32), (VALW, BF16)], [(1, VALW)], "gla_readout_bwd")

    def sg_bwd(zu, zv, lng, lnb, ws, bs, dyv):
        (u, vv), vjp = jax.vjp(_sg_pre, zu, zv, lng, lnb)
        s = _sg_mix(vv, ws, bs, GW)
        du, ds = dyv * s, dyv * u
        dws, dbs, dvv = [], [], []
        for g in range(SG_GROUPS):
            dsg = ds[:, g * GW:(g + 1) * GW]
            dsb = dsg.astype(BF16)
            dws.append(_dg(dsb, vv[:, g * GW:(g + 1) * GW].astype(BF16), 1, 1))
            dbs.append(jnp.sum(dsg, axis=1, keepdims=True))
            dvv.append(_dg(ws[g].astype(BF16), dsb, 0, 0))
        dzu, dzv, dlng, dlnb = vjp((du, jnp.concatenate(dvv, axis=1)))
        return jnp.concatenate([dzu, dzv], axis=1), _colsum(dlng), _colsum(dlnb), jnp.concatenate(dws, axis=0), jnp.concatenate(dbs, axis=0)

    dz_sg, dlng, dlnb, dws, dbs = _rows(sg_bwd, N, SG_CHUNK, [_T(z_al, SGW, 0), _T(z_al, SGW, 1), _W(sg_ln_g), _W(sg_ln_b), _W(w_s[0]), _W(bs_col), _T(dycat, SGW, VALW // SGW)],
                                        [(2 * SGW, BF16)], [(1, SGW), (1, SGW), (SG_GROUPS * SG_CHUNK, SG_CHUNK), (SG_GROUPS * SG_CHUNK, 1)], "sg_bwd")

    dq_f, dq_b, dk_f, dk_b, dv_f, dv_b, dla_f, dla_b, dst0, recv2_w1 = _gla_bwd(
        q_r, k_r, z_al, AV // DV, la_x, savf_x, savb_x, do_x, zero_state, H, DK, DV, "gla_bwd_x",
        side=[_ph(lax.empty((3,) + part_w1.shape[1:], BF16), [("rs_chips", 0, 3)], src=part_w1)])
    _, _, dkc_f, dkc_b, dvc_f, dvc_b, dlac_f, dlac_b, _ = _gla_bwd(
        q_c, k_c, zc_al, AV // DV, la_c, savf_c, savb_c, jnp.zeros((NC, VALW), F32), dst0, H, DK, DV, "gla_bwd_ctx")

    def decay_bwd(lr, dla_f_v, dla_b_v, wdf_v, wdb_v, bf_v, bb_v):
        a_f, a_b = decay(lr, wdf_v, wdb_v, bf_v, bb_v)
        da_f = _chunk_cumsum(dla_f_v, False) * jax.nn.sigmoid(-a_f) / GLA_TAU
        da_b = _chunk_cumsum(dla_b_v, True) * jax.nn.sigmoid(-a_b) / GLA_TAU
        lrb, dfb, dbb = lr.astype(BF16), da_f.astype(BF16), da_b.astype(BF16)
        dlr = _dg(dfb, wdf_v.astype(BF16), 1, 1) + _dg(dbb, wdb_v.astype(BF16), 1, 1)
        return dlr, _dg(lrb, dfb, 0, 0), _dg(lrb, dbb, 0, 0), _colsum(da_f), _colsum(da_b)

    def prep_x_bwd(dq0, dq1, dk0, dk1, dv0, dv1, lr, dla0, dla1, cs, sn, wdf_v, wdb_v, bf_v, bb_v):
        dlr, dwf, dwb, dbf, dbb = decay_bwd(lr, dla0, dla1, wdf_v, wdb_v, bf_v, bb_v)
        return (_rope_t(dq0 + dq1, cs, sn, H, DK) * qscale, _rope_t(dk0 + dk1, cs, sn, H, DK), dv0 + dv1, dlr, dwf, dwb, dbf, dbb)

    def prep_c_bwd(dk0, dk1, dv0, dv1, lr, dla0, dla1, wdf_v, wdb_v, bf_v, bb_v):
        dlr, dwf, dwb, dbf, dbb = decay_bwd(lr, dla0, dla1, wdf_v, wdb_v, bf_v, bb_v)
        return dk0 + dk1, dv0 + dv1, dlr, dwf, dwb, dbf, dbb

    dec_acc = [(LANES, KEYW), (LANES, KEYW), (1, KEYW), (1, KEYW)]
    dz_q, dz_k, dz_v, dz_lr, dwdf_x, dwdb_x, dbdf_x, dbdb_x = _rows(
        prep_x_bwd, N, TR, [_T(dq_f, KEYW), _T(dq_b, KEYW), _T(dk_f, KEYW), _T(dk_b, KEYW), _T(dv_f, VALW), _T(dv_b, VALW),
                            _T(z_al, LANES, ALR // LANES), _T(dla_f, KEYW), _T(dla_b, KEYW), _T(cos_t, DK), _T(sin_t, DK)] + dec_w,
        [(KEYW, BF16), (KEYW, BF16), (VALW, BF16), (LANES, BF16)], dec_acc, "gla_prep_x_bwd")
    dzc_k, dzc_v, dzc_lr, dwdf_c, dwdb_c, dbdf_c, dbdb_c = _rows(
        prep_c_bwd, NC, TR, [_T(dkc_f, KEYW), _T(dkc_b, KEYW), _T(dvc_f, VALW), _T(dvc_b, VALW),
                             _T(zc_al, LANES, ALR // LANES), _T(dlac_f, KEYW), _T(dlac_b, KEYW)] + dec_w,
        [(KEYW, BF16), (VALW, BF16), (LANES, BF16)], dec_acc, "gla_prep_ctx_bwd")

    dz_al = jnp.concatenate([dz_sg, dz_q, dz_k, dz_v, dz_r, dz_lr], axis=1)
    dzc_al = jnp.concatenate([jnp.zeros((NC, 2 * SGW + KEYW), BF16), dzc_k, dzc_v, jnp.zeros((NC, VALW), BF16), dzc_lr], axis=1)
    tkd = _pick(ACOLS, 3456)
    dhx, recv2_w1, recv1_wo = _mm(dz_al, w_al, N, D, ACOLS, name="in_proj_dx", tb=True, tk=tkd,
                                  side=[_ph(recv2_w1, [("rs_chips", 3, 4)], src=part_w1),
                                        _ph(lax.empty((N_CHIP,) + dwo.shape[1:], BF16), [("rs_sibling", 0, 4)], src=dwo)])
    part_wo = _sum_sibling(dwo, recv1_wo, c_arr, "rs_sum_sibling_w_o")
    dhc = _mm(dzc_al, w_al, NC, D, ACOLS, name="in_proj_dctx", tb=True, tk=tkd)
    h_cat = jnp.concatenate([hx, hc], axis=0)
    dz_cat = jnp.concatenate([dz_al, dzc_al], axis=0)
    tkt = _pick(N + NC, 2304)
    dw_al, recv2_wo = _mm(h_cat, dz_cat, D, ACOLS, N + NC, name="in_proj_dw", ta=True, tn=tka, tk=tkt, out_dtypes=(BF16,),
                          side=[_ph(lax.empty((3,) + part_wo.shape[1:], BF16), [("rs_chips", 0, 4)], src=part_wo)])

    grad_x, dpre1_x, dsc1, dsh1 = _rows(prenorm_bwd, N, TR, [_T(x2, D), _W(pre1_g), _W(sc1), _W(sh1), _T(dhx, D), _T(dx1, D)],
                                        [(D, F32)], [(1, D)] * 3, "prenorm1_x_bwd")

    def prenorm_bwd_ctx(xv, gv, scv, shv, dh):
        _, vjp = jax.vjp(_prenorm, xv, gv, scv, shv)
        _, dg, dsc, dsh = vjp(dh)
        return _colsum(dg), _colsum(dsc), _colsum(dsh)

    dpre1_c, dcsc1, dcsh1 = _rows(prenorm_bwd_ctx, NC, TR, [_T(ctx2, D), _W(pre1_g), _W(csc1), _W(csh1), _T(dhc, D)],
                                  [], [(1, D)] * 3, "prenorm1_ctx_bwd")

    g_in = _shard_columns(dw_al, CS, SG0, 2 * SGW, -SG0, RH, "w_in_grad_blocks")
    recv1_in = _rs_sibling([g_in], "rs_sibling_w_in")[0]
    part_in = _sum_sibling(g_in, recv1_in, c_arr, "rs_sum_sibling_w_in")
    half = [_sum_chips(p, r, sc_arr, "rs_sum_chips_" + nm)
            for p, r, nm in zip((part_wo, part_w1, part_w2), (recv2_wo, recv2_w1, recv2_w2), ("w_o", "w_1", "w_2"))]
    g_w_o, g_w_1, g_w_2 = [g.reshape(w.shape[1:]) for g, w in zip(_rs_final(half, "rs_final"), (w_o, w_1, w_2))]

    def big(w, g, m, v, name, side=()):
        shp = w.shape
        res = _adamw(w.reshape(shp[-2:]), g.reshape(shp[-2:]), m.reshape(shp[-2:]), v.reshape(shp[-2:]), name, side=side)
        return [g.reshape(shp)] + [r.reshape(shp) for r in res[:3]], res[3:]

    def w_in_leg(buf, lo, hi):
        return [_ph(buf, [("rs_chips", lo, hi)], src=part_in)]

    r_2, (recv2_in,) = big(w_2, g_w_2, m_w_2, v_w_2, "adamw_w_2", side=w_in_leg(lax.empty((3,) + part_in.shape[1:], BF16), 0, 1))
    r_1, (recv2_in,) = big(w_1, g_w_1, m_w_1, v_w_1, "adamw_w_1", side=w_in_leg(recv2_in, 1, 2))

    dmod_x = jnp.concatenate([dsh1, dsc1, dgt1, dsh2, dsc2, dgt2], axis=1)
    dmodc = jnp.concatenate([dcsh1, dcsc1], axis=1)
    small_parts = [loss_acc, dmod_x, dmodc, dpre1_x + dpre1_c, dpost1, dpre2, dpost2, dwdf_x + dwdf_c, dbdf_x + dbdf_c,
                   dwdb_x + dwdb_c, dbdb_x + dbdb_c, dgn, dlng, dlnb, dws, dbs]
    small_shapes = [p.shape for p in small_parts]
    packed = _pack(small_parts)
    n_sm = packed.shape[1]
    gathered = _allgather_small(packed, "gather_small_grads")

    def sum_devices(g):
        tot = g[0:8]
        for dev in range(1, N_DEV):
            tot = tot + g[8 * dev:8 * dev + 8]
        return (tot,)

    summed = _rows(sum_devices, N_DEV * 8, N_DEV * 8, [_W(gathered)], [], [(8, n_sm)], "sum_small_grads")[0]
    (loss_s, dmod_sum, dmodc_sum, g_pre1, g_post1, g_pre2, g_post2, g_wdf_pad, g_bdf, g_wdb_pad, g_bdb, g_gn, g_lng, g_lnb,
     g_ws, g_bs) = _unpack(summed.reshape(-1), small_shapes)
    loss = loss_s[0, 0]
    dmod_rows = gathered.reshape(N_DEV, -1)[:, LANES:LANES + N_MOD * D]
    g_b_ada = dmod_sum + jnp.pad(dmodc_sum, ((0, 0), (0, (N_MOD - 2) * D)))
    dmod16 = jnp.zeros((16, N_MOD * D), F32).at[:N_DEV].set(dmod_rows).at[N_DEV, :2 * D].set(dmodc_sum[0])
    dmod16_sh = lax.dynamic_slice(dmod16, (0, s_me * MS), (16, MS))
    g_w_ada, d_w_ada, nm_w_ada, nv_w_ada, recv2_in = _ada_update(cond.T, dmod16_sh, w_ada[0], m_w_ada[0], v_w_ada[0], "w_ada_update",
                                                                  side=w_in_leg(recv2_in, 2, 4))
    half_in = _sum_chips(part_in, recv2_in, sc_arr, "rs_sum_chips_w_in")
    g_w_in = _rs_final([half_in], "rs_final_w_in")[0].reshape(D, -1)[:, :CS]

    dcond = _mm(dmod16_sh, w_ada[0], 16, D, MS, name="cond_bwd", tb=True, tk=min(512, MS))
    part_c = _allgather_small(dcond[N_DEV].reshape(8, D // 8), "gather_dcond").reshape(N_DEV, D)

    def cctx_grad(p, cv):
        sg = jax.nn.sigmoid(cv)
        tot = ((p[0:1] + p[2:3]) + p[4:5]) + p[6:7]
        return (jnp.broadcast_to(tot * (sg * (1.0 + cv * (1.0 - sg))), p.shape),)

    g_c_ctx = _rows(cctx_grad, N_DEV, N_DEV, [_W(part_c), _W(c_ctx.reshape(1, D))], [(D, F32)], [], "c_ctx_grad")[0][0:1]

    def col_shard(g_full, width):
        return lax.dynamic_slice_in_dim(g_full, s_me * width, width, axis=g_full.ndim - 1)

    g_w_dec_f = col_shard(g_wdf_pad[:GLA_LOWRANK], KEYW // N_CHIP)
    g_w_dec_b = col_shard(g_wdb_pad[GLA_LOWRANK:LR], KEYW // N_CHIP)
    g_gla_norm = col_shard(g_gn.reshape(H, DV), DV // N_CHIP)
    small_w = [c_ctx, b_ada, pre1_g, post1_g, pre2_g, post2_g, w_dec_f, b_dec_f, w_dec_b, b_dec_b, gla_norm_g, sg_ln_g, sg_ln_b, w_s, b_s]
    small_m = [m_c_ctx, m_b_ada, m_pre1_g, m_post1_g, m_pre2_g, m_post2_g, m_w_dec_f, m_b_dec_f, m_w_dec_b, m_b_dec_b, m_gla_norm_g, m_sg_ln_g, m_sg_ln_b, m_w_s, m_b_s]
    small_v = [v_c_ctx, v_b_ada, v_pre1_g, v_post1_g, v_pre2_g, v_post2_g, v_w_dec_f, v_b_dec_f, v_w_dec_b, v_b_dec_b, v_gla_norm_g, v_sg_ln_g, v_sg_ln_b, v_w_s, v_b_s]
    small_g = [g_c_ctx, g_b_ada, g_pre1, g_post1, g_pre2, g_post2, g_w_dec_f, g_bdf, g_w_dec_b, g_bdb, g_gla_norm, g_lng, g_lnb, g_ws, g_bs]
    small_g = [g.reshape(w.shape) for g, w in zip(small_g, small_w)]
    shapes_w = [w.shape for w in small_w]
    upd = _adamw(_pack(small_w), _pack(small_g), _pack(small_m), _pack(small_v), "adamw_small")
    d_small, m_small, v_small = [_unpack(u.reshape(-1), shapes_w) for u in upd]
    r_in, _ = big(w_in, g_w_in, m_w_in, v_w_in, "adamw_w_in")
    r_o, _ = big(w_o, g_w_o, m_w_o, v_w_o, "adamw_w_o")
    r_ada = [t.reshape(w_ada.shape) for t in (g_w_ada, d_w_ada, nm_w_ada, nv_w_ada)]

    def ordered(k):
        sm = [small_g, d_small, m_small, v_small][k]
        return [sm[0], r_ada[k], *sm[1:6], r_in[k], *sm[6:15], r_o[k], r_1[k], r_2[k]]

    return (loss, grad_x.reshape(x.shape), *ordered(0), *ordered(1), *ordered(2), *ordered(3))
```

```python
import functools
import math

import jax
import jax.numpy as jnp
from jax import lax
from jax.experimental import pallas as pl
from jax.experimental.pallas import tpu as pltpu

F32 = jnp.float32
BF16 = jnp.bfloat16
MESH = pl.DeviceIdType.MESH
ANY = pl.BlockSpec(memory_space=pl.ANY)

GLA_HEADS = 8
GLA_CHUNK = 64
GLA_LOWRANK = 16
GLA_TAU = 16.0
ROPE_BASE = 10000.0
GRID_W = 64
SG_GROUPS = 4
SG_CHUNK = 128
N_MOD = 6
EPS = 1e-6
ADAM_LR = 0.001
ADAM_B1 = 0.9
ADAM_B2 = 0.999
ADAM_EPS = 1e-08
ADAM_WD = 0.01
ADAM_STEP = 10

LANES = 128
VMEM_LIMIT = 56 << 20
N_DEV = 8
N_CHIP = 4


def _params(sem=None):
    return pltpu.CompilerParams(dimension_semantics=sem, vmem_limit_bytes=VMEM_LIMIT)


def _pick(dim, target, unit=LANES):
    best = None
    for t in range(unit, min(dim, target) + 1, unit):
        if dim % t == 0:
            best = t
    return dim if best is None else best


def _dg(a, b, ca, cb, precision=None):
    return lax.dot_general(a, b, (((ca,), (cb,)), ((), ())), preferred_element_type=F32,
                           precision=precision)


def _place():
    return lax.axis_index("x"), lax.axis_index("y"), lax.axis_index("c")


def _allgather_small(v, name):
    m_per, n = v.shape

    def body(x_ref, out_ref, send_sems, recv_sems, local_sem):
        x, y, c = _place()
        me, sibling = (x, y, c), (x, y, 1 - c)
        chips = [(1 - x, y), (x, 1 - y), (1 - x, 1 - y)]

        def rows(px, py, pc):
            return out_ref.at[pl.ds((4 * px + 2 * py + pc) * m_per, m_per), :]

        def copy(k, block, to, src=None):
            return pltpu.make_async_remote_copy(
                src_ref=rows(*block) if src is None else src, dst_ref=rows(*block),
                send_sem=send_sems.at[k], recv_sem=recv_sems.at[k],
                device_id=to, device_id_type=MESH)

        mine = pltpu.make_async_copy(x_ref, rows(*me), local_sem)
        mine.start()
        first = [copy(0, me, sibling, src=x_ref)]
        first += [copy(1 + j, me, (*chip, c), src=x_ref) for j, chip in enumerate(chips)]
        for cp in first:
            cp.start()
        passed = [copy(4 + j, (*chip, c), sibling) for j, chip in enumerate(chips)]
        for j, chip in enumerate(chips):
            copy(1 + j, (*chip, c), me).wait_recv()
            passed[j].start()
        copy(0, sibling, me).wait_recv()
        for j, chip in enumerate(chips):
            copy(4 + j, (*chip, 1 - c), me).wait_recv()
        for cp in first + passed:
            cp.wait_send()
        mine.wait()

    return pl.pallas_call(
        body, name=name,
        out_shape=jax.ShapeDtypeStruct((N_DEV * m_per, n), v.dtype),
        in_specs=[pl.BlockSpec(memory_space=pltpu.VMEM)],
        out_specs=pl.BlockSpec(memory_space=pltpu.VMEM),
        scratch_shapes=[pltpu.SemaphoreType.DMA((7,)), pltpu.SemaphoreType.DMA((7,)),
                        pltpu.SemaphoreType.DMA],
        compiler_params=pltpu.CompilerParams(vmem_limit_bytes=VMEM_LIMIT),
    )(v)


def _cast_blocks(w, s_me, name):
    _, r, cols = w.shape
    tr = _row_tile(r, cols, 4)

    def body(s_ref, w_ref, o_ref):
        o_ref[...] = w_ref[...].astype(BF16)

    return pl.pallas_call(
        body, name=name,
        out_shape=jax.ShapeDtypeStruct((N_DEV, r, cols), BF16),
        grid_spec=pltpu.PrefetchScalarGridSpec(
            num_scalar_prefetch=1, grid=(2, r // tr),
            in_specs=[pl.BlockSpec((None, tr, cols), lambda h, i, s: (h, i, 0))],
            out_specs=pl.BlockSpec((None, tr, cols), lambda h, i, s: (2 * s[0] + h, i, 0))),
        compiler_params=_params(("arbitrary", "arbitrary")),
    )(s_me, w)


def _gather_big(ws, name):
    nw = len(ws)

    def body(*refs):
        outs = refs[nw:2 * nw]
        send_sems, recv_sems = refs[2 * nw:]
        x, y, c = _place()
        me, sibling = (x, y, c), (x, y, 1 - c)
        chips = [(1 - x, y), (x, 1 - y), (1 - x, 1 - y)]

        def blk(px, py, pc):
            return 4 * px + 2 * py + pc

        def copy(w, k, block, to):
            return pltpu.make_async_remote_copy(
                src_ref=outs[w].at[block], dst_ref=outs[w].at[block],
                send_sem=send_sems.at[6 * w + k], recv_sem=recv_sems.at[6 * w + k],
                device_id=to, device_id_type=MESH)

        first = []
        for w in range(nw):
            for j, chip in enumerate(chips):
                cp = copy(w, j, blk(x, y, c), (*chip, c))
                cp.start()
                first.append(cp)
        passed = []
        for w in range(nw):
            for j, chip in enumerate(chips):
                copy(w, j, blk(*chip, c), me).wait_recv()
                cp = copy(w, 3 + j, blk(*chip, c), sibling)
                cp.start()
                passed.append(cp)
        for w in range(nw):
            for j, chip in enumerate(chips):
                copy(w, 3 + j, blk(*chip, 1 - c), me).wait_recv()
        for cp in first + passed:
            cp.wait_send()

    return pl.pallas_call(
        body, name=name,
        out_shape=[jax.ShapeDtypeStruct(w.shape, w.dtype) for w in ws],
        in_specs=[ANY] * nw, out_specs=[ANY] * nw,
        input_output_aliases={w: w for w in range(nw)},
        scratch_shapes=[pltpu.SemaphoreType.DMA((6 * nw,)), pltpu.SemaphoreType.DMA((6 * nw,))],
    )(*ws)


def _rs_sibling(gs, name):
    nw = len(gs)

    def body(*refs):
        ins, outs = refs[:nw], refs[nw:2 * nw]
        send_sems, recv_sems = refs[2 * nw:]
        x, y, c = _place()
        cps = []
        for w in range(nw):
            for s in range(N_CHIP):
                cp = pltpu.make_async_remote_copy(
                    src_ref=ins[w].at[2 * s + (1 - c)], dst_ref=outs[w].at[s],
                    send_sem=send_sems.at[N_CHIP * w + s], recv_sem=recv_sems.at[N_CHIP * w + s],
                    device_id=(x, y, 1 - c), device_id_type=MESH)
                cp.start()
                cps.append(cp)
        for cp in cps:
            cp.wait()

    return pl.pallas_call(
        body, name=name,
        out_shape=[jax.ShapeDtypeStruct((N_CHIP,) + g.shape[1:], g.dtype) for g in gs],
        in_specs=[ANY] * nw, out_specs=[ANY] * nw,
        scratch_shapes=[pltpu.SemaphoreType.DMA((N_CHIP * nw,)), pltpu.SemaphoreType.DMA((N_CHIP * nw,))],
    )(*gs)


def _rs_final(fs, name):
    nw = len(fs)

    def body(*refs):
        outs = refs[nw:2 * nw]
        send_sems, recv_sems = refs[2 * nw:]
        x, y, c = _place()
        cps = []
        for w in range(nw):
            cp = pltpu.make_async_remote_copy(
                src_ref=outs[w].at[c], dst_ref=outs[w].at[c],
                send_sem=send_sems.at[w], recv_sem=recv_sems.at[w],
                device_id=(x, y, 1 - c), device_id_type=MESH)
            cp.start()
            cps.append(cp)
        for cp in cps:
            cp.wait()

    return pl.pallas_call(
        body, name=name,
        out_shape=[jax.ShapeDtypeStruct(f.shape, f.dtype) for f in fs],
        in_specs=[ANY] * nw, out_specs=[ANY] * nw,
        input_output_aliases={w: w for w in range(nw)},
        scratch_shapes=[pltpu.SemaphoreType.DMA((nw,)), pltpu.SemaphoreType.DMA((nw,))],
    )(*fs)


_PHASE_COPIES = {"gather_ici": 3, "gather_d2d": 3, "gather_chain": 6, "rs_sibling": N_CHIP, "rs_chips": 3}
QUARTERS = 4


def _ph(buf, legs, src=None):
    return dict(buf=buf, src=src, legs=legs)


def _n_copies(ph):
    return sum(_PHASE_COPIES[kind] for kind, _, _ in ph["legs"])


def _phase_copies(ph, src, buf, send_sems, recv_sems, base):
    x, y, c = _place()
    sibling = (x, y, 1 - c)
    chips = [(1 - x, y), (x, 1 - y), (1 - x, 1 - y)]
    r = buf.shape[1]

    def make(k, trip):
        a, b, dev = trip
        return pltpu.make_async_remote_copy(src_ref=a, dst_ref=b, send_sem=send_sems.at[base + k], recv_sem=recv_sems.at[base + k],
                                            device_id=dev, device_id_type=MESH)

    out = []
    for kind, lo, hi in ph["legs"]:
        rows = pl.ds(lo * r // QUARTERS, (hi - lo) * r // QUARTERS)
        ici = [(buf.at[4 * x + 2 * y + c, rows], buf.at[4 * x + 2 * y + c, rows], (*chip, c)) for chip in chips]
        d2d = [(buf.at[4 * chip[0] + 2 * chip[1] + c, rows], buf.at[4 * chip[0] + 2 * chip[1] + c, rows], sibling) for chip in chips]
        if kind == "gather_ici":
            trips, later = ici, []
        elif kind == "gather_d2d":
            trips, later = d2d, []
        elif kind == "gather_chain":
            trips, later = ici, d2d
        elif kind == "rs_sibling":
            trips, later = [(src.at[2 * s + (1 - c), rows], buf.at[s, rows], sibling) for s in range(N_CHIP)], []
        else:
            trips, later = [(src.at[2 * chip[0] + chip[1], rows], buf.at[j, rows], (*chip, c)) for j, chip in enumerate(chips)], []
        out.append(([make(k, t) for k, t in enumerate(trips)], [make(len(trips) + k, t) for k, t in enumerate(later)]))
        base += _PHASE_COPIES[kind]
    return out


def _side_call(inner, grid, in_specs, out_specs, out_shape, scratch, args, phases, name, semantics, mid=0.8):
    n_in, n_out, n_ph = len(in_specs), len(out_specs), len(phases)
    if n_ph == 0:
        outs = pl.pallas_call(inner, name=name, grid=grid, in_specs=in_specs, out_specs=out_specs, out_shape=out_shape,
                              scratch_shapes=scratch, compiler_params=_params(semantics))(*args)
        return list(outs), []
    n_cp = sum(_n_copies(p) for p in phases)
    side_args, buf_pos, src_pos = [], [], []
    for p in phases:
        buf_pos.append(len(side_args))
        side_args.append(p["buf"])
        src_pos.append(len(side_args) if p["src"] is not None else None)
        if p["src"] is not None:
            side_args.append(p["src"])
    n_side = len(side_args)
    total = math.prod(grid)
    mid_lin = min(total - 1, int(total * mid))

    def body(*refs):
        b_in, s_in = refs[:n_in], refs[n_in:n_in + n_side]
        b_out, s_out = refs[n_in + n_side:n_in + n_side + n_out], refs[n_in + n_side + n_out:n_in + n_side + n_out + n_ph]
        rest = refs[n_in + n_side + n_out + n_ph:]
        send_sems, recv_sems = rest[-2:]
        lin = functools.reduce(lambda acc, ag: acc * ag[1] + pl.program_id(ag[0]), list(enumerate(grid))[1:], pl.program_id(0))

        def copies():
            out, base = [], 0
            for p, sp, so in zip(phases, src_pos, s_out):
                out += _phase_copies(p, None if sp is None else s_in[sp], so, send_sems, recv_sems, base)
                base += _n_copies(p)
            return out

        @pl.when(lin == 0)
        def _():
            for a, _ in copies():
                for cp in a:
                    cp.start()

        if any(kind == "gather_chain" for p in phases for kind, _, _ in p["legs"]):
            @pl.when(lin == mid_lin)
            def _():
                for a, b in copies():
                    if b:
                        for cp in a:
                            cp.wait()
                        for cp in b:
                            cp.start()

        inner(*b_in, *b_out, *rest[:-2])

        @pl.when(lin == total - 1)
        def _():
            for a, b in copies():
                for cp in (b if b else a):
                    cp.wait()

    outs = pl.pallas_call(
        body, name=name, grid=grid,
        in_specs=list(in_specs) + [ANY] * n_side, out_specs=list(out_specs) + [ANY] * n_ph,
        out_shape=list(out_shape) + [jax.ShapeDtypeStruct(p["buf"].shape, p["buf"].dtype) for p in phases],
        input_output_aliases={n_in + bp: n_out + k for k, bp in enumerate(buf_pos)},
        scratch_shapes=list(scratch) + [pltpu.SemaphoreType.DMA((n_cp,)), pltpu.SemaphoreType.DMA((n_cp,))],
        compiler_params=_params(("arbitrary",) * len(grid)),
    )(*args, *side_args)
    return list(outs[:n_out]), list(outs[n_out:])


def _row_tile(r, cols, itemsize):
    t = r
    while t * cols * itemsize > (2 << 20) and t % 16 == 0:
        t //= 2
    return t


def _sum_sibling(g, r1, c_me, name):
    _, r, cols = g.shape
    tr = _row_tile(r, cols, 4)

    def body(c_ref, g_ref, r_ref, o_ref):
        o_ref[...] = (g_ref[...].astype(F32) + r_ref[...].astype(F32)).astype(o_ref.dtype)

    return pl.pallas_call(
        body, name=name,
        out_shape=jax.ShapeDtypeStruct((N_CHIP, r, cols), g.dtype),
        grid_spec=pltpu.PrefetchScalarGridSpec(
            num_scalar_prefetch=1, grid=(N_CHIP, r // tr),
            in_specs=[pl.BlockSpec((None, tr, cols), lambda s, i, c: (2 * s + c[0], i, 0)),
                      pl.BlockSpec((None, tr, cols), lambda s, i, c: (s, i, 0))],
            out_specs=pl.BlockSpec((None, tr, cols), lambda s, i, c: (s, i, 0))),
        compiler_params=_params(("arbitrary", "arbitrary")),
    )(c_me, g, r1)


def _sum_chips(p, r2, sc_me, name):
    _, r, cols = p.shape
    tr = _row_tile(r, cols, 4)

    def body(s_ref, p_ref, a_ref, b_ref, c_ref, o_ref):
        o_ref[...] = ((p_ref[...].astype(F32) + a_ref[...].astype(F32)) + b_ref[...].astype(F32)) + c_ref[...].astype(F32)

    return pl.pallas_call(
        body, name=name,
        out_shape=jax.ShapeDtypeStruct((2, r, cols), F32),
        grid_spec=pltpu.PrefetchScalarGridSpec(
            num_scalar_prefetch=1, grid=(r // tr,),
            in_specs=[pl.BlockSpec((None, tr, cols), lambda i, s: (s[0], i, 0)),
                      pl.BlockSpec((None, tr, cols), lambda i, s: (0, i, 0)),
                      pl.BlockSpec((None, tr, cols), lambda i, s: (1, i, 0)),
                      pl.BlockSpec((None, tr, cols), lambda i, s: (2, i, 0))],
            out_specs=pl.BlockSpec((None, tr, cols), lambda i, s: (s[1], i, 0))),
        compiler_params=_params(("arbitrary",)),
    )(sc_me, p, r2, r2, r2)


def _shard_columns(g_al, shard_cols, bound, off_lo, off_hi, rh, name):
    d, acols = g_al.shape
    wp = -(-shard_cols // LANES) * LANES
    tr = min(LANES, rh)
    nt = acols // LANES

    def body(x_ref, o_ref):
        s = pl.program_id(1)
        lane = lax.broadcasted_iota(jnp.int32, (tr, LANES), 1)

        def tile(q):
            q = max(0, min(nt - 1, q))
            return x_ref[:, q * LANES:(q + 1) * LANES].astype(F32)

        def read(start):
            q, sh = divmod(start, LANES)
            if sh == 0:
                return tile(q)
            return jnp.where(lane < LANES - sh, pltpu.roll(tile(q), LANES - sh, 1), pltpu.roll(tile(q + 1), LANES - sh, 1))

        for k in range(N_CHIP):
            @pl.when(s == k)
            def _(k=k):
                for t in range(wp // LANES):
                    n0 = k * shard_cols + t * LANES
                    if n0 + LANES <= bound:
                        v = read(n0 + off_lo)
                    elif n0 >= bound:
                        v = read(n0 + off_hi)
                    else:
                        v = jnp.where(lane < bound - n0, read(n0 + off_lo), read(n0 + off_hi))
                    o_ref[:, t * LANES:(t + 1) * LANES] = v.astype(o_ref.dtype)

    return pl.pallas_call(
        body, name=name, grid=(d // tr, N_CHIP),
        out_shape=jax.ShapeDtypeStruct((N_DEV, rh, wp), BF16),
        in_specs=[pl.BlockSpec((tr, acols), lambda i, s: (i, 0))],
        out_specs=pl.BlockSpec((None, tr, wp), lambda i, s: (2 * s + (i * tr) // rh, ((i * tr) % rh) // tr, 0)),
        compiler_params=_params(("arbitrary", "arbitrary")),
    )(g_al)


def _mm(a, b, M, N, K, *, name, ta=False, tb=False, out_dtypes=(F32,), tm=1024, tn=1024, tk=2048,
        a_spec=None, b_spec=None, out_specs=None, out_shapes=None, epi=None, epi_in=(), epi_specs=(), side=(), side_mid=0.8):
    tm, tn, tk = min(tm, M), min(tn, N), min(tk, K)
    assert M % tm == 0 and N % tn == 0 and K % tk == 0, (name, M, N, K, tm, tn, tk)
    nk = K // tk
    n_epi, n_out = len(epi_in), len(out_dtypes)
    if a_spec is None:
        a_spec = pl.BlockSpec((tk, tm), lambda i, j, k: (k, i)) if ta else pl.BlockSpec((tm, tk), lambda i, j, k: (i, k))
    if b_spec is None:
        b_spec = pl.BlockSpec((tn, tk), lambda i, j, k: (j, k)) if tb else pl.BlockSpec((tk, tn), lambda i, j, k: (k, j))
    if out_specs is None:
        out_specs = [pl.BlockSpec((tm, tn), lambda i, j, k: (i, j))] * n_out
        out_shapes = [jax.ShapeDtypeStruct((M, N), dt) for dt in out_dtypes]

    def body(a_ref, b_ref, *rest):
        epi_refs, o_refs, acc = rest[:n_epi], rest[n_epi:n_epi + n_out], rest[-1]
        k = pl.program_id(2)

        @pl.when(k == 0)
        def _():
            acc[...] = jnp.zeros_like(acc)

        acc[...] += _dg(a_ref[...].astype(BF16), b_ref[...].astype(BF16), 0 if ta else 1, 1 if tb else 0)

        @pl.when(k == nk - 1)
        def _():
            r = acc[...]
            vals = (r,) if epi is None else epi(r, *[e[...] for e in epi_refs])
            for o_ref, v in zip(o_refs, vals):
                o_ref[...] = v.astype(o_ref.dtype)

    outs, side_outs = _side_call(body, (M // tm, N // tn, nk), [a_spec, b_spec, *epi_specs], out_specs, out_shapes,
                                 [pltpu.VMEM((tm, tn), F32)], (a, b, *epi_in), list(side), name,
                                 ("parallel", "parallel", "arbitrary"), mid=side_mid)
    if side:
        return outs + side_outs
    return outs[0] if n_out == 1 else outs


def _T(arr, width, col=0, lead=None):
    return ("tile", arr, width, col, lead)


def _W(arr):
    return ("whole", arr)


def _rows(fn, n_rows, tr, ins, tile_outs, acc_outs, name):
    tr = min(tr, n_rows)
    assert n_rows % tr == 0, (name, n_rows, tr)
    in_specs, args = [], []
    for d in ins:
        if d[0] == "tile":
            _, arr, width, col, lead = d
            if lead is None:
                in_specs.append(pl.BlockSpec((tr, width), lambda i, col=col: (i, col)))
            else:
                in_specs.append(pl.BlockSpec((None, tr, width), lambda i, col=col, lead=lead: (lead, i, col)))
            args.append(arr)
        else:
            arr = d[1]
            in_specs.append(pl.BlockSpec(arr.shape, lambda i, nd=arr.ndim: (0,) * nd))
            args.append(arr)
    n_in, n_t = len(ins), len(tile_outs)
    out_shape = [jax.ShapeDtypeStruct((n_rows, w), dt) for w, dt in tile_outs]
    out_specs = [pl.BlockSpec((tr, w), lambda i: (i, 0)) for w, _ in tile_outs]
    out_shape += [jax.ShapeDtypeStruct(s, F32) for s in acc_outs]
    out_specs += [pl.BlockSpec(s, lambda i, nd=len(s): (0,) * nd) for s in acc_outs]

    def body(*refs):
        in_refs, t_refs, a_refs = refs[:n_in], refs[n_in:n_in + n_t], refs[n_in + n_t:]
        vals = fn(*[r[...] for r in in_refs])
        for r, v in zip(t_refs, vals[:n_t]):
            r[...] = v.astype(r.dtype)
        first = pl.program_id(0) == 0
        for r, v in zip(a_refs, vals[n_t:]):
            @pl.when(first)
            def _(r=r, v=v):
                r[...] = v

            @pl.when(jnp.logical_not(first))
            def _(r=r, v=v):
                r[...] += v

    return pl.pallas_call(
        body, name=name, out_shape=out_shape, grid=(n_rows // tr,),
        in_specs=in_specs, out_specs=out_specs,
        compiler_params=_params(("arbitrary",)),
    )(*args)


def _colsum(t):
    return jnp.sum(t, axis=0, keepdims=True)


def _prenorm(x, g, sc, sh):
    xf = x.astype(F32)
    return xf * lax.rsqrt(jnp.mean(xf * xf, axis=-1, keepdims=True) + EPS) * g * (1.0 + sc) + sh


def _postnorm(x, y, gate, pg):
    return x + gate * (y * lax.rsqrt(jnp.mean(y * y, axis=-1, keepdims=True) + EPS) * pg)


def _gelu(t):
    return 0.5 * t * (1.0 + lax.erf(t * (2.0 ** -0.5)))


def _sg_pre(zu, zv, lng, lnb):
    u, vr = _gelu(zu), _gelu(zv)
    mu = jnp.mean(vr, axis=-1, keepdims=True)
    var = jnp.mean(jnp.square(vr - mu), axis=-1, keepdims=True)
    return u, (vr - mu) * lax.rsqrt(var + EPS) * lng + lnb


def _sg_mix(vv, ws_ref_vals, bs_vals, gw):
    parts = []
    for g in range(SG_GROUPS):
        s = _dg(ws_ref_vals[g].astype(BF16), vv[:, g * gw:(g + 1) * gw].astype(BF16), 1, 0)
        parts.append(s + bs_vals[g])
    return jnp.concatenate(parts, axis=1)


def _readout(o, r, g, heads, dv):
    parts = []
    for h in range(heads):
        oh = o[:, h * dv:(h + 1) * dv]
        parts.append(oh * lax.rsqrt(jnp.mean(oh * oh, axis=-1, keepdims=True) + EPS))
    return jnp.concatenate(parts, axis=1) * g * (r * jax.nn.sigmoid(r))


def _log_sigmoid(a):
    return jnp.minimum(a, 0.0) - jnp.log(1.0 + jnp.exp(-jnp.abs(a)))


def _rope_swap(t, m):
    lane = lax.broadcasted_iota(jnp.int32, t.shape, 1)
    return jnp.where((lane % (2 * m)) < m, pltpu.roll(t, 3 * m, 1), pltpu.roll(t, m, 1))


def _rope(t, cos, sin, heads, dk):
    parts = []
    for h in range(heads):
        th = t[:, h * dk:(h + 1) * dk]
        parts.append(th * cos + _rope_swap(th, dk // 4) * sin)
    return jnp.concatenate(parts, axis=1)


def _rope_t(dt, cos, sin, heads, dk):
    parts = []
    for h in range(heads):
        dh = dt[:, h * dk:(h + 1) * dk]
        parts.append(dh * cos + _rope_swap(dh * sin, dk // 4))
    return jnp.concatenate(parts, axis=1)


def _chunk_cumsum(t, upwards):
    n = t.shape[0]
    row = lax.broadcasted_iota(jnp.int32, (n, n), 0)
    col = lax.broadcasted_iota(jnp.int32, (n, n), 1)
    shift = GLA_CHUNK.bit_length() - 1
    same = jnp.right_shift(row, shift) == jnp.right_shift(col, shift)
    tri = jnp.logical_and(same, col <= row if upwards else col >= row)
    return _dg(tri.astype(F32), t, 1, 0, precision=lax.Precision.HIGHEST)


def _chunk_terms(d, qv, kv, b, C):
    row = lax.broadcasted_iota(jnp.int32, (C, C), 0)
    col = lax.broadcasted_iota(jnp.int32, (C, C), 1)
    tri = row >= col if d == 0 else row <= col
    end_row = lax.broadcasted_iota(jnp.int32, b.shape, 0) == (C - 1 if d == 0 else 0)
    btot = _colsum(jnp.where(end_row, b, 0.0))
    eb, enb, ebt = jnp.exp(b), jnp.exp(-b), jnp.exp(btot - b)
    return tri, btot, eb, enb, ebt, qv * eb, kv * enb, kv * ebt


def _gla_fwd(q, k, zv, v_col0, la, st0, heads, dk, dv, name, side=(), side_mid=0.8):
    n, C, H = q.shape[0], GLA_CHUNK, heads
    nc = n // C

    def body(qf, kf, vf, laf, qb, kb, vb_, lab, st0_ref, of_ref, ob_ref, sf_ref, sb_ref, fin_ref, st):
        i = pl.program_id(1)

        @pl.when(i == 0)
        def _():
            st[...] = st0_ref[...]

        for d, (q_ref, k_ref, v_ref, la_ref, o_ref, save_ref) in enumerate(((qf, kf, vf, laf, of_ref, sf_ref), (qb, kb, vb_, lab, ob_ref, sb_ref))):
            tri, btot, _, _, _, qt, kt, kh = _chunk_terms(d, q_ref[...], k_ref[...], la_ref[...], C)
            s = st[d]
            vb = v_ref[...].astype(BF16)
            qtb = qt.astype(BF16)
            att = jnp.where(tri, _dg(qtb, kt.astype(BF16), 1, 1), 0.0)
            o_ref[...] = _dg(qtb, s.astype(BF16), 1, 1) + _dg(att.astype(BF16), vb, 1, 0)
            save_ref[...] = s
            s_new = s * jnp.exp(btot) + _dg(vb, kh.astype(BF16), 0, 0)
            st[d] = s_new

            @pl.when(i == nc - 1)
            def _(d=d, s_new=s_new):
                fin_ref[d] = s_new

    def seq(width, col0, rev, dir_cols=0):
        if rev:
            return pl.BlockSpec((C, width), lambda h, i: (nc - 1 - i, col0 + dir_cols + h))
        return pl.BlockSpec((C, width), lambda h, i: (i, col0 + h))

    both = pl.BlockSpec((2, None, dv, dk), lambda h, i: (0, h, 0, 0))
    outs, side_outs = _side_call(
        body, (H, nc),
        [seq(dk, 0, False), seq(dk, 0, False), seq(dv, v_col0, False), seq(dk, 0, False),
         seq(dk, 0, True), seq(dk, 0, True), seq(dv, v_col0, True), seq(dk, 0, True, H), both],
        [seq(dv, 0, False), seq(dv, 0, True),
         pl.BlockSpec((None, None, dv, dk), lambda h, i: (h, i, 0, 0)),
         pl.BlockSpec((None, None, dv, dk), lambda h, i: (h, nc - 1 - i, 0, 0)), both],
        [jax.ShapeDtypeStruct((n, H * dv), F32)] * 2 + [jax.ShapeDtypeStruct((H, nc, dv, dk), F32)] * 2
        + [jax.ShapeDtypeStruct((2, H, dv, dk), F32)],
        [pltpu.VMEM((2, dv, dk), F32)], (q, k, zv, la, q, k, zv, la, st0), list(side), name, ("arbitrary", "arbitrary"),
        mid=side_mid)
    return outs + side_outs


def _gla_bwd(q, k, zv, v_col0, la, saved_f, saved_b, do, dfin, heads, dk, dv, name, side=()):
    n, C, H = q.shape[0], GLA_CHUNK, heads
    nc = n // C

    def body(qf, kf, vf, laf, sf, dof, qb, kb, vb_, lab, sb, dob_, dfin_ref,
             dqf, dqb, dkf, dkb, dvf, dvb, dlaf, dlab, d0_ref, dst):
        i = pl.program_id(1)

        @pl.when(i == 0)
        def _():
            dst[...] = dfin_ref[...]

        dirs = ((qf, kf, vf, laf, sf, dof, dqf, dkf, dvf, dlaf), (qb, kb, vb_, lab, sb, dob_, dqb, dkb, dvb, dlab))
        for d, (q_ref, k_ref, v_ref, la_ref, save_ref, do_ref, dq_ref, dk_ref, dv_ref, dla_ref) in enumerate(dirs):
            tri, btot, eb, enb, ebt, qt, kt, kh = _chunk_terms(d, q_ref[...], k_ref[...], la_ref[...], C)
            s, dsn = save_ref[...], dst[d]
            vb, dob = v_ref[...].astype(BF16), do_ref[...].astype(BF16)
            qtb, ktb, khb, dsnb = qt.astype(BF16), kt.astype(BF16), kh.astype(BF16), dsn.astype(BF16)
            att = jnp.where(tri, _dg(qtb, ktb, 1, 1), 0.0).astype(BF16)
            datt = jnp.where(tri, _dg(dob, vb, 1, 1), 0.0).astype(BF16)
            dqt = _dg(dob, s.astype(BF16), 1, 0) + _dg(datt, ktb, 1, 0)
            dkt = _dg(datt, qtb, 0, 0)
            dkh = _dg(vb, dsnb, 1, 0)
            dv_ref[...] = _dg(att, dob, 0, 0) + _dg(khb, dsnb, 1, 1)
            ebtot = jnp.exp(btot)
            dbtot = ebtot * _colsum(s * dsn) + _colsum(dkh * kh)
            s0 = dsn * ebtot + _dg(dob, qtb, 0, 0)
            dst[d] = s0
            db = dqt * qt - dkt * kt - dkh * kh
            dq_ref[...] = dqt * eb
            dk_ref[...] = dkt * enb + dkh * ebt
            end_row = lax.broadcasted_iota(jnp.int32, db.shape, 0) == (C - 1 if d == 0 else 0)
            dla_ref[...] = db + jnp.where(end_row, dbtot, 0.0)

            @pl.when(i == nc - 1)
            def _(d=d, s0=s0):
                d0_ref[d] = s0

    def seq(width, col0, fwd_dir, dir_cols=0):
        if fwd_dir:
            return pl.BlockSpec((C, width), lambda h, i: (nc - 1 - i, col0 + h))
        return pl.BlockSpec((C, width), lambda h, i: (i, col0 + dir_cols + h))

    both = pl.BlockSpec((2, None, dv, dk), lambda h, i: (0, h, 0, 0))
    sav_f = pl.BlockSpec((None, None, dv, dk), lambda h, i: (h, nc - 1 - i, 0, 0))
    sav_b = pl.BlockSpec((None, None, dv, dk), lambda h, i: (h, i, 0, 0))
    outs, side_outs = _side_call(
        body, (H, nc),
        [seq(dk, 0, True), seq(dk, 0, True), seq(dv, v_col0, True), seq(dk, 0, True), sav_f, seq(dv, 0, True),
         seq(dk, 0, False), seq(dk, 0, False), seq(dv, v_col0, False), seq(dk, 0, False, H), sav_b, seq(dv, 0, False), both],
        [seq(dk, 0, True), seq(dk, 0, False), seq(dk, 0, True), seq(dk, 0, False), seq(dv, 0, True), seq(dv, 0, False),
         seq(dk, 0, True), seq(dk, 0, False), both],
        [jax.ShapeDtypeStruct((n, H * dk), F32)] * 4 + [jax.ShapeDtypeStruct((n, H * dv), F32)] * 2
        + [jax.ShapeDtypeStruct((n, H * dk), F32)] * 2 + [jax.ShapeDtypeStruct((2, H, dv, dk), F32)],
        [pltpu.VMEM((2, dv, dk), F32)], (q, k, zv, la, saved_f, do, q, k, zv, la, saved_b, do, dfin), list(side), name,
        ("arbitrary", "arbitrary"))
    return outs + side_outs


def _adamw_math(w, g, m, v):
    m2 = ADAM_B1 * m + (1.0 - ADAM_B1) * g
    v2 = ADAM_B2 * v + (1.0 - ADAM_B2) * jnp.square(g)
    m_hat = m2 / (1.0 - ADAM_B1 ** ADAM_STEP)
    v_hat = v2 / (1.0 - ADAM_B2 ** ADAM_STEP)
    delta = -ADAM_LR * (m_hat / (jnp.sqrt(v_hat) + ADAM_EPS) + ADAM_WD * w)
    return delta, m2, v2


def _adamw(w, g, m, v, name):
    r, cols = w.shape
    tr = _row_tile(r, cols, 4 * 4)
    spec = pl.BlockSpec((tr, cols), lambda i: (i, 0))

    def body(w_ref, g_ref, m_ref, v_ref, d_ref, m2_ref, v2_ref):
        d_ref[...], m2_ref[...], v2_ref[...] = _adamw_math(w_ref[...], g_ref[...], m_ref[...], v_ref[...])

    return _side_call(body, (r // tr,), [spec] * 4, [spec] * 3, [jax.ShapeDtypeStruct((r, cols), F32)] * 3,
                      [], (w, g, m, v), [], name, ("parallel",))[0]


def _ada_update(cond_t, dmod, w, m, v, name):
    r, cols = w.shape
    tr, tc = _pick(r, 512, 8), _pick(cols, 1024)
    spec = pl.BlockSpec((tr, tc), lambda i, j: (i, j))

    def body(c_ref, d_ref, w_ref, m_ref, v_ref, g_ref, dl_ref, m2_ref, v2_ref):
        g = _dg(c_ref[...].astype(BF16), d_ref[...].astype(BF16), 1, 0)
        g_ref[...] = g
        dl_ref[...], m2_ref[...], v2_ref[...] = _adamw_math(w_ref[...], g, m_ref[...], v_ref[...])

    return _side_call(
        body, (r // tr, cols // tc),
        [pl.BlockSpec((tr, cond_t.shape[1]), lambda i, j: (i, 0)), pl.BlockSpec((dmod.shape[0], tc), lambda i, j: (0, j)), spec, spec, spec],
        [spec] * 4, [jax.ShapeDtypeStruct((r, cols), F32)] * 4, [], (cond_t, dmod, w, m, v), [], name, ("parallel", "parallel"))[0]


def _pack(parts, rows=8):
    flat = jnp.concatenate([p.reshape(-1).astype(F32) for p in parts])
    n = -(-flat.shape[0] // (rows * LANES)) * LANES
    return jnp.pad(flat, (0, rows * n - flat.shape[0])).reshape(rows, n)


def _unpack(flat, shapes):
    out, off = [], 0
    for s in shapes:
        size = math.prod(s)
        out.append(flat[off:off + size].reshape(s))
        off += size
    return out


def kernel(x, c, ctx, c_ctx, w_ada, b_ada, pre1_g, post1_g, pre2_g, post2_g, w_in, w_dec_f, b_dec_f, w_dec_b, b_dec_b, gla_norm_g, sg_ln_g, sg_ln_b, w_s, b_s, w_o, w_1, w_2, loss_target, m_c_ctx, m_w_ada, m_b_ada, m_pre1_g, m_post1_g, m_pre2_g, m_post2_g, m_w_in, m_w_dec_f, m_b_dec_f, m_w_dec_b, m_b_dec_b, m_gla_norm_g, m_sg_ln_g, m_sg_ln_b, m_w_s, m_b_s, m_w_o, m_w_1, m_w_2, v_c_ctx, v_w_ada, v_b_ada, v_pre1_g, v_post1_g, v_pre2_g, v_post2_g, v_w_in, v_w_dec_f, v_b_dec_f, v_w_dec_b, v_b_dec_b, v_gla_norm_g, v_sg_ln_g, v_sg_ln_b, v_w_s, v_b_s, v_w_o, v_w_1, v_w_2):
    N, D = x.shape[1], x.shape[2]
    NC = ctx.shape[1]
    H = GLA_HEADS
    VALW = D // 2
    DV = VALW // H
    DK = DV // 2
    KEYW = H * DK
    SGW = D - VALW
    GW = SGW // SG_GROUPS
    LR = 2 * GLA_LOWRANK
    F = w_1.shape[2] * N_CHIP
    FS = F // N_CHIP
    RH = D // 2
    MS = w_ada.shape[2]
    IN_COLS = w_in.shape[2] * N_CHIP
    K0, V0, R0, LF0 = KEYW, 2 * KEYW, 2 * KEYW + VALW, 2 * KEYW + 2 * VALW
    SG0 = LF0 + LR
    AQ, AK, AV, AR, ALR = 2 * SGW, 2 * SGW + KEYW, 2 * SGW + 2 * KEYW, 2 * SGW + 2 * KEYW + VALW, 2 * SGW + 2 * KEYW + 2 * VALW
    ACOLS = ALR + LANES
    assert IN_COLS == SG0 + 2 * SGW and N % SG_CHUNK == 0 and N % GLA_CHUNK == 0 and NC % GLA_CHUNK == 0

    ax, ay, ac = _place()
    s_me = (2 * ax + ay).astype(jnp.int32)
    b_me = (4 * ax + 2 * ay + ac).astype(jnp.int32)
    s_arr, c_arr = s_me.reshape(1), ac.astype(jnp.int32).reshape(1)
    sc_arr = jnp.concatenate([s_arr, c_arr])
    CS = IN_COLS // N_CHIP

    shards = [_cast_blocks(w_in[0].reshape(2, RH, CS), s_arr, "cast_w_in"), _cast_blocks(w_o[0].reshape(2, D // N_DEV, D), s_arr, "cast_w_o"),
              _cast_blocks(w_1[0].reshape(2, RH, FS), s_arr, "cast_w_1"), _cast_blocks(w_2[0].reshape(2, F // N_DEV, D), s_arr, "cast_w_2")]
    win_g, = _gather_big(shards[:1], "gather_weights")
    wo_buf, w1_buf, w2_buf = shards[1], shards[2], shards[3]
    w_in_nat = win_g.reshape(N_CHIP, 2, RH, IN_COLS // N_CHIP).transpose(1, 2, 0, 3).reshape(D, IN_COLS)
    w_al = jnp.concatenate([w_in_nat[:, SG0:], w_in_nat[:, :LF0], w_in_nat[:, LF0:SG0],
                            jnp.zeros((D, LANES - LR), BF16)], axis=1)

    n_dec = GLA_LOWRANK * (KEYW // N_CHIP)
    g0 = _allgather_small(_pack([c, w_dec_f, w_dec_b, gla_norm_g]), "gather_small0").reshape(N_DEV, -1)
    c_all = g0[:, :D]
    per_chip = g0[0::2]
    wdf = per_chip[:, D:D + n_dec].reshape(N_CHIP, GLA_LOWRANK, KEYW // N_CHIP).transpose(1, 0, 2).reshape(GLA_LOWRANK, KEYW)
    wdb = per_chip[:, D + n_dec:D + 2 * n_dec].reshape(N_CHIP, GLA_LOWRANK, KEYW // N_CHIP).transpose(1, 0, 2).reshape(GLA_LOWRANK, KEYW)
    gn_full = per_chip[:, D + 2 * n_dec:D + 2 * n_dec + H * (DV // N_CHIP)].reshape(N_CHIP, H, DV // N_CHIP).transpose(1, 0, 2).reshape(1, VALW)
    wd_f = jnp.zeros((LANES, KEYW), F32).at[:GLA_LOWRANK].set(wdf)
    wd_b = jnp.zeros((LANES, KEYW), F32).at[GLA_LOWRANK:LR].set(wdb)

    cond_in = jnp.zeros((16, D), F32).at[:N_DEV].set(c_all).at[N_DEV].set(c_ctx)
    b_ada_sh = lax.dynamic_slice(b_ada, (0, s_me * MS), (1, MS))

    def mod_epi(r, bias):
        return (r + bias,)

    def silu_rows(t):
        return (t * jax.nn.sigmoid(t),)

    cond = _rows(silu_rows, 16, 16, [_W(cond_in)], [(D, F32)], [], "cond_silu")[0]
    mod_sh = _mm(cond, w_ada[0], 16, MS, D, name="mod_matmul", tn=512, tk=D, epi=mod_epi, epi_in=(b_ada_sh,),
                 epi_specs=(pl.BlockSpec((1, min(512, MS)), lambda i, j, k: (0, j)),))
    g1m = _allgather_small(mod_sh, "gather_mod").reshape(N_DEV, 16, MS)[0::2]
    mod_all = g1m.transpose(1, 0, 2).reshape(16, N_CHIP * MS)
    mod_me = lax.dynamic_slice(mod_all, (b_me, 0), (1, N_MOD * D))
    sh1, sc1, gt1, sh2, sc2, gt2 = [mod_me[:, i * D:(i + 1) * D] for i in range(N_MOD)]
    csh1, csc1 = mod_all[N_DEV:N_DEV + 1, :D], mod_all[N_DEV:N_DEV + 1, D:2 * D]

    mq = DK // 4
    pos = jnp.arange(N)
    inv_freq = ROPE_BASE ** (-jnp.arange(mq, dtype=F32) / mq)
    ang_r = (pos // GRID_W).astype(F32)[:, None] * inv_freq[None, :]
    ang_c = (pos % GRID_W).astype(F32)[:, None] * inv_freq[None, :]
    cos_t = jnp.concatenate([jnp.cos(ang_r), jnp.cos(ang_r), jnp.cos(ang_c), jnp.cos(ang_c)], axis=1)
    sin_t = jnp.concatenate([-jnp.sin(ang_r), jnp.sin(ang_r), -jnp.sin(ang_c), jnp.sin(ang_c)], axis=1)

    x2, tgt, ctx2 = x[0], loss_target[0], ctx[0]
    TR = 128
    qscale = DK ** -0.5

    def prenorm_fwd(xa, g, sc, sh, n_rows, name):
        return _rows(lambda xv, gv, scv, shv: (_prenorm(xv, gv, scv, shv),), n_rows, TR,
                     [_T(xa, D), _W(g), _W(sc), _W(sh)], [(D, BF16)], [], name)[0]

    hx = prenorm_fwd(x2, pre1_g, sc1, sh1, N, "prenorm1_x")
    hc = prenorm_fwd(ctx2, pre1_g, csc1, csh1, NC, "prenorm1_ctx")
    tka = _pick(ACOLS, 1152)
    z_al, w1_buf = _mm(hx, w_al, N, ACOLS, D, name="in_proj_x", tn=tka, side=[_ph(w1_buf, [("gather_ici", 0, 2)])])
    zc_al = _mm(hc, w_al, NC, ACOLS, D, name="in_proj_ctx", tn=tka)

    def decay(lr, wdf_v, wdb_v, bf_v, bb_v):
        lrb = lr.astype(BF16)
        a_f = _dg(lrb, wdf_v.astype(BF16), 1, 0) + bf_v
        a_b = _dg(lrb, wdb_v.astype(BF16), 1, 0) + bb_v
        return a_f, a_b

    def running_decay(a_f, a_b):
        return jnp.concatenate([_chunk_cumsum(_log_sigmoid(a_f) / GLA_TAU, True), _chunk_cumsum(_log_sigmoid(a_b) / GLA_TAU, False)], axis=1)

    def prep_x(zq, zk, lr, cs, sn, wdf_v, wdb_v, bf_v, bb_v):
        return _rope(zq * qscale, cs, sn, H, DK), _rope(zk, cs, sn, H, DK), running_decay(*decay(lr, wdf_v, wdb_v, bf_v, bb_v))

    def prep_c(zk, lr, wdf_v, wdb_v, bf_v, bb_v):
        return zk, running_decay(*decay(lr, wdf_v, wdb_v, bf_v, bb_v))

    dec_w = [_W(wd_f), _W(wd_b), _W(b_dec_f), _W(b_dec_b)]
    q_r, k_r, la_x = _rows(prep_x, N, TR, [_T(z_al, KEYW, AQ // KEYW), _T(z_al, KEYW, AK // KEYW), _T(z_al, LANES, ALR // LANES),
                                           _T(cos_t, DK), _T(sin_t, DK)] + dec_w,
                           [(KEYW, F32), (KEYW, F32), (2 * KEYW, F32)], [], "gla_prep_x")
    k_c, la_c = _rows(prep_c, NC, TR, [_T(zc_al, KEYW, AK // KEYW), _T(zc_al, LANES, ALR // LANES)] + dec_w,
                      [(KEYW, F32), (2 * KEYW, F32)], [], "gla_prep_ctx")

    zero_state = jnp.zeros((2, H, DV, DK), F32)
    q_c = jnp.zeros((NC, KEYW), F32)
    _, _, savf_c, savb_c, st_c = _gla_fwd(q_c, k_c, zc_al, AV // DV, la_c, zero_state, H, DK, DV, "gla_fwd_ctx")
    o_f, o_b, savf_x, savb_x, _, wo_g, w1_buf = _gla_fwd(
        q_r, k_r, z_al, AV // DV, la_x, st_c, H, DK, DV, "gla_fwd_x",
        side=[_ph(wo_buf, [("gather_chain", 0, 4)]), _ph(w1_buf, [("gather_d2d", 0, 2), ("gather_ici", 2, 4)])], side_mid=0.55)
    w_o_f = wo_g.reshape(D, D)

    def readout_fwd(of, ob, r, g):
        return (_readout(of + ob, r, g, H, DV),)

    y_gla = _rows(readout_fwd, N, TR, [_T(o_f, VALW), _T(o_b, VALW), _T(z_al, VALW, AR // VALW), _W(gn_full)],
                  [(VALW, BF16)], [], "gla_readout")[0]

    bs_col = b_s[0].reshape(SG_GROUPS, SG_CHUNK, 1)

    def sg_fwd(zu, zv, lng, lnb, ws, bs):
        u, vv = _sg_pre(zu, zv, lng, lnb)
        return (u * _sg_mix(vv, ws, bs, GW),)

    y_sg = _rows(sg_fwd, N, SG_CHUNK, [_T(z_al, SGW, 0), _T(z_al, SGW, 1), _W(sg_ln_g), _W(sg_ln_b), _W(w_s[0]), _W(bs_col)],
                 [(SGW, BF16)], [], "sg_fwd")[0]
    ycat = jnp.concatenate([y_gla, y_sg], axis=1)
    y, w1_g, w2_buf = _mm(ycat, w_o_f, N, D, D, name="out_proj",
                          side=[_ph(w1_buf, [("gather_d2d", 2, 4)]), _ph(w2_buf, [("gather_ici", 0, 1)])])
    x1 = _rows(lambda xv, yv, gv, pv: (_postnorm(xv, yv, gv, pv),), N, TR,
               [_T(x2, D), _T(y, D), _W(gt1), _W(post1_g)], [(D, F32)], [], "postnorm1")[0]
    h2 = prenorm_fwd(x1, pre2_g, sc2, sh2, N, "prenorm2")

    tm1, tn1, tk1 = min(1024, N), min(1024, FS), min(2048, RH)
    w1_fwd_spec = pl.BlockSpec((None, tk1, tn1), lambda i, j, k: (2 * ((j * tn1) // FS) + (k * tk1) // RH, ((k * tk1) % RH) // tk1, ((j * tn1) % FS) // tn1))

    def relu2_epi(r):
        rf = jnp.maximum(r, 0.0)
        return rf * rf, rf

    act, rf, w2_g = _mm(h2, w1_g, N, F, D, name="mlp_up", out_dtypes=(BF16, BF16), tm=tm1, tn=tn1, tk=tk1, b_spec=w1_fwd_spec, epi=relu2_epi,
                        side=[_ph(w2_buf, [("gather_d2d", 0, 1), ("gather_chain", 1, 4)])], side_mid=0.88)
    w_2_f = w2_g.reshape(F, D)
    y2 = _mm(act, w_2_f, N, D, F, name="mlp_down")

    def final(x1v, y2v, gv, pv, tv):
        def loss_fn(x1a, y2a, ga, pa):
            err = _postnorm(x1a, y2a, ga, pa) - tv
            return 0.5 * jnp.sum(jnp.mean(err * err, axis=-1))
        loss, grads = jax.value_and_grad(loss_fn, argnums=(0, 1, 2, 3))(x1v, y2v, gv, pv)
        return grads[0], grads[1], jnp.full((1, LANES), loss, F32), _colsum(grads[2]), _colsum(grads[3])

    dx2, dy2, loss_acc, dgt2, dpost2 = _rows(final, N, TR, [_T(x1, D), _T(y2, D), _W(gt2), _W(post2_g), _T(tgt, D)],
                                             [(D, F32), (D, BF16)], [(1, LANES), (1, D), (1, D)], "loss_postnorm2_bwd")

    df = _mm(dy2, w_2_f, N, F, D, name="mlp_down_dx", tb=True, out_dtypes=(BF16,), epi=lambda r, rfv: (r * (2.0 * rfv.astype(F32)),),
             epi_in=(rf,), epi_specs=(pl.BlockSpec((min(1024, N), min(1024, F)), lambda i, j, k: (i, j)),))
    dw2 = _mm(act, dy2, F, D, N, name="mlp_down_dw", ta=True, out_dtypes=(BF16,)).reshape(N_DEV, F // N_DEV, D)
    tnb, tkb = min(1024, D, RH), min(2048, FS)
    w1_bwd_spec = pl.BlockSpec((None, tnb, tkb), lambda i, j, k: (2 * ((k * tkb) // FS) + (j * tnb) // RH, ((j * tnb) % RH) // tnb, ((k * tkb) % FS) // tkb))
    dh2, recv1_w2 = _mm(df, w1_g, N, D, F, name="mlp_up_dx", tb=True, tn=tnb, tk=tkb, b_spec=w1_bwd_spec,
                         side=[_ph(lax.empty((N_CHIP,) + dw2.shape[1:], BF16), [("rs_sibling", 0, 4)], src=dw2)])
    part_w2 = _sum_sibling(dw2, recv1_w2, c_arr, "rs_sum_sibling_w_2")
    tmw, tnw = min(1024, RH), min(1024, FS)
    dw1_spec = pl.BlockSpec((None, tmw, tnw), lambda i, j, k: (2 * ((j * tnw) // FS) + (i * tmw) // RH, ((i * tmw) % RH) // tmw, ((j * tnw) % FS) // tnw))
    dw1, recv2_w2 = _mm(h2, df, D, F, N, name="mlp_up_dw", ta=True, tm=tmw, tn=tnw, out_dtypes=(BF16,), out_specs=[dw1_spec],
                        out_shapes=[jax.ShapeDtypeStruct((N_DEV, RH, FS), BF16)],
                        side=[_ph(lax.empty((3,) + part_w2.shape[1:], BF16), [("rs_chips", 0, 3)], src=part_w2)])

    def prenorm_bwd(xv, gv, scv, shv, dh, dres):
        _, vjp = jax.vjp(_prenorm, xv, gv, scv, shv)
        dx, dg, dsc, dsh = vjp(dh)
        return dx + dres, _colsum(dg), _colsum(dsc), _colsum(dsh)

    dx1, dpre2, dsc2, dsh2 = _rows(prenorm_bwd, N, TR, [_T(x1, D), _W(pre2_g), _W(sc2), _W(sh2), _T(dh2, D), _T(dx2, D)],
                                   [(D, F32)], [(1, D)] * 3, "prenorm2_bwd")

    def postnorm_bwd(yv, gv, pv, dxv):
        _, vjp = jax.vjp(lambda ya, ga, pa: _postnorm(0.0, ya, ga, pa), yv, gv, pv)
        dy_, dg_, dp_ = vjp(dxv)
        return dy_, _colsum(dg_), _colsum(dp_)

    dy, dgt1, dpost1 = _rows(postnorm_bwd, N, TR, [_T(y, D), _W(gt1), _W(post1_g), _T(dx1, D)], [(D, BF16)], [(1, D)] * 2, "postnorm1_bwd")
    dycat, recv2_w2, recv1_w1 = _mm(dy, w_o_f, N, D, D, name="out_proj_dx", tb=True,
                                    side=[_ph(recv2_w2, [("rs_chips", 3, 4)], src=part_w2),
                                          _ph(lax.empty((N_CHIP,) + dw1.shape[1:], BF16), [("rs_sibling", 0, 4)], src=dw1)])
    part_w1 = _sum_sibling(dw1, recv1_w1, c_arr, "rs_sum_sibling_w_1")
    dwo = _mm(ycat, dy, D, D, N, name="out_proj_dw", ta=True, out_dtypes=(BF16,)).reshape(N_DEV, D // N_DEV, D)

    def readout_bwd(of, ob, r, g, dyv):
        _, vjp = jax.vjp(lambda o_, r_, g_: _readout(o_, r_, g_, H, DV), of + ob, r, g)
        do_, dr_, dg_ = vjp(dyv)
        return do_, dr_, _colsum(dg_)

    do_x, dz_r, dgn = _rows(readout_bwd, N, TR, [_T(o_f, VALW), _T(o_b, VALW), _T(z_al, VALW, AR // VALW), _W(gn_full), _T(dycat, VALW, 0)],
                            [(VALW, F32), (VALW, BF16)], [(1, VALW)], "gla_readout_bwd")

    def sg_bwd(zu, zv, lng, lnb, ws, bs, dyv):
        (u, vv), vjp = jax.vjp(_sg_pre, zu, zv, lng, lnb)
        s = _sg_mix(vv, ws, bs, GW)
        du, ds = dyv * s, dyv * u
        dws, dbs, dvv = [], [], []
        for g in range(SG_GROUPS):
            dsg = ds[:, g * GW:(g + 1) * GW]
            dsb = dsg.astype(BF16)
            dws.append(_dg(dsb, vv[:, g * GW:(g + 1) * GW].astype(BF16), 1, 1))
            dbs.append(jnp.sum(dsg, axis=1, keepdims=True))
            dvv.append(_dg(ws[g].astype(BF16), dsb, 0, 0))
        dzu, dzv, dlng, dlnb = vjp((du, jnp.concatenate(dvv, axis=1)))
        return jnp.concatenate([dzu, dzv], axis=1), _colsum(dlng), _colsum(dlnb), jnp.concatenate(dws, axis=0), jnp.concatenate(dbs, axis=0)

    dz_sg, dlng, dlnb, dws, dbs = _rows(sg_bwd, N, SG_CHUNK, [_T(z_al, SGW, 0), _T(z_al, SGW, 1), _W(sg_ln_g), _W(sg_ln_b), _W(w_s[0]), _W(bs_col), _T(dycat, SGW, VALW // SGW)],
                                        [(2 * SGW, BF16)], [(1, SGW), (1, SGW), (SG_GROUPS * SG_CHUNK, SG_CHUNK), (SG_GROUPS * SG_CHUNK, 1)], "sg_bwd")

    dq_f, dq_b, dk_f, dk_b, dv_f, dv_b, dla_f, dla_b, dst0, recv2_w1, recv1_wo = _gla_bwd(
        q_r, k_r, z_al, AV // DV, la_x, savf_x, savb_x, do_x, zero_state, H, DK, DV, "gla_bwd_x",
        side=[_ph(lax.empty((3,) + part_w1.shape[1:], BF16), [("rs_chips", 0, 3)], src=part_w1),
              _ph(lax.empty((N_CHIP,) + dwo.shape[1:], BF16), [("rs_sibling", 0, 4)], src=dwo)])
    part_wo = _sum_sibling(dwo, recv1_wo, c_arr, "rs_sum_sibling_w_o")
    _, _, dkc_f, dkc_b, dvc_f, dvc_b, dlac_f, dlac_b, _ = _gla_bwd(
        q_c, k_c, zc_al, AV // DV, la_c, savf_c, savb_c, jnp.zeros((NC, VALW), F32), dst0, H, DK, DV, "gla_bwd_ctx")

    def decay_bwd(lr, dla_f_v, dla_b_v, wdf_v, wdb_v, bf_v, bb_v):
        a_f, a_b = decay(lr, wdf_v, wdb_v, bf_v, bb_v)
        da_f = _chunk_cumsum(dla_f_v, False) * jax.nn.sigmoid(-a_f) / GLA_TAU
        da_b = _chunk_cumsum(dla_b_v, True) * jax.nn.sigmoid(-a_b) / GLA_TAU
        lrb, dfb, dbb = lr.astype(BF16), da_f.astype(BF16), da_b.astype(BF16)
        dlr = _dg(dfb, wdf_v.astype(BF16), 1, 1) + _dg(dbb, wdb_v.astype(BF16), 1, 1)
        return dlr, _dg(lrb, dfb, 0, 0), _dg(lrb, dbb, 0, 0), _colsum(da_f), _colsum(da_b)

    def prep_x_bwd(dq0, dq1, dk0, dk1, dv0, dv1, lr, dla0, dla1, cs, sn, wdf_v, wdb_v, bf_v, bb_v):
        dlr, dwf, dwb, dbf, dbb = decay_bwd(lr, dla0, dla1, wdf_v, wdb_v, bf_v, bb_v)
        return (_rope_t(dq0 + dq1, cs, sn, H, DK) * qscale, _rope_t(dk0 + dk1, cs, sn, H, DK), dv0 + dv1, dlr, dwf, dwb, dbf, dbb)

    def prep_c_bwd(dk0, dk1, dv0, dv1, lr, dla0, dla1, wdf_v, wdb_v, bf_v, bb_v):
        dlr, dwf, dwb, dbf, dbb = decay_bwd(lr, dla0, dla1, wdf_v, wdb_v, bf_v, bb_v)
        return dk0 + dk1, dv0 + dv1, dlr, dwf, dwb, dbf, dbb

    dec_acc = [(LANES, KEYW), (LANES, KEYW), (1, KEYW), (1, KEYW)]
    dz_q, dz_k, dz_v, dz_lr, dwdf_x, dwdb_x, dbdf_x, dbdb_x = _rows(
        prep_x_bwd, N, TR, [_T(dq_f, KEYW), _T(dq_b, KEYW), _T(dk_f, KEYW), _T(dk_b, KEYW), _T(dv_f, VALW), _T(dv_b, VALW),
                            _T(z_al, LANES, ALR // LANES), _T(dla_f, KEYW), _T(dla_b, KEYW), _T(cos_t, DK), _T(sin_t, DK)] + dec_w,
        [(KEYW, BF16), (KEYW, BF16), (VALW, BF16), (LANES, BF16)], dec_acc, "gla_prep_x_bwd")
    dzc_k, dzc_v, dzc_lr, dwdf_c, dwdb_c, dbdf_c, dbdb_c = _rows(
        prep_c_bwd, NC, TR, [_T(dkc_f, KEYW), _T(dkc_b, KEYW), _T(dvc_f, VALW), _T(dvc_b, VALW),
                             _T(zc_al, LANES, ALR // LANES), _T(dlac_f, KEYW), _T(dlac_b, KEYW)] + dec_w,
        [(KEYW, BF16), (VALW, BF16), (LANES, BF16)], dec_acc, "gla_prep_ctx_bwd")

    dz_al = jnp.concatenate([dz_sg, dz_q, dz_k, dz_v, dz_r, dz_lr], axis=1)
    dzc_al = jnp.concatenate([jnp.zeros((NC, 2 * SGW + KEYW), BF16), dzc_k, dzc_v, jnp.zeros((NC, VALW), BF16), dzc_lr], axis=1)
    tkd = _pick(ACOLS, 3456)
    h_cat = jnp.concatenate([hx, hc], axis=0)
    dz_cat = jnp.concatenate([dz_al, dzc_al], axis=0)
    tkt = _pick(N + NC, 2304)
    dw_al, recv2_w1, recv2_wo = _mm(h_cat, dz_cat, D, ACOLS, N + NC, name="in_proj_dw", ta=True, tn=tka, tk=tkt, out_dtypes=(BF16,),
                                    side=[_ph(recv2_w1, [("rs_chips", 3, 4)], src=part_w1),
                                          _ph(lax.empty((3,) + part_wo.shape[1:], BF16), [("rs_chips", 0, 4)], src=part_wo)])
    g_in = _shard_columns(dw_al, CS, SG0, 2 * SGW, -SG0, RH, "w_in_grad_blocks")
    recv1_in = _rs_sibling([g_in], "rs_sibling_w_in")[0]
    part_in = _sum_sibling(g_in, recv1_in, c_arr, "rs_sum_sibling_w_in")
    dhx, recv2_in = _mm(dz_al, w_al, N, D, ACOLS, name="in_proj_dx", tb=True, tk=tkd,
                        side=[_ph(lax.empty((3,) + part_in.shape[1:], BF16), [("rs_chips", 0, 3)], src=part_in)])
    dhc = _mm(dzc_al, w_al, NC, D, ACOLS, name="in_proj_dctx", tb=True, tk=tkd)

    grad_x, dpre1_x, dsc1, dsh1 = _rows(prenorm_bwd, N, TR, [_T(x2, D), _W(pre1_g), _W(sc1), _W(sh1), _T(dhx, D), _T(dx1, D)],
                                        [(D, F32)], [(1, D)] * 3, "prenorm1_x_bwd")

    def prenorm_bwd_ctx(xv, gv, scv, shv, dh):
        _, vjp = jax.vjp(_prenorm, xv, gv, scv, shv)
        _, dg, dsc, dsh = vjp(dh)
        return _colsum(dg), _colsum(dsc), _colsum(dsh)

    dpre1_c, dcsc1, dcsh1 = _rows(prenorm_bwd_ctx, NC, TR, [_T(ctx2, D), _W(pre1_g), _W(csc1), _W(csh1), _T(dhc, D)],
                                  [], [(1, D)] * 3, "prenorm1_ctx_bwd")

    half = [_sum_chips(p, r, sc_arr, "rs_sum_chips_" + nm)
            for p, r, nm in zip((part_wo, part_w1, part_w2), (recv2_wo, recv2_w1, recv2_w2), ("w_o", "w_1", "w_2"))]
    g_w_o, g_w_1, g_w_2 = [g.reshape(w.shape[1:]) for g, w in zip(_rs_final(half, "rs_final"), (w_o, w_1, w_2))]

    dmod_x = jnp.concatenate([dsh1, dsc1, dgt1, dsh2, dsc2, dgt2], axis=1)
    dmodc = jnp.concatenate([dcsh1, dcsc1], axis=1)
    small_parts = [loss_acc, dmod_x, dmodc, dpre1_x + dpre1_c, dpost1, dpre2, dpost2, dwdf_x + dwdf_c, dbdf_x + dbdf_c,
                   dwdb_x + dwdb_c, dbdb_x + dbdb_c, dgn, dlng, dlnb, dws, dbs]
    small_shapes = [p.shape for p in small_parts]
    packed = _pack(small_parts)
    n_sm = packed.shape[1]
    gathered = _allgather_small(packed, "gather_small_grads")

    def sum_devices(g):
        tot = g[0:8]
        for dev in range(1, N_DEV):
            tot = tot + g[8 * dev:8 * dev + 8]
        return (tot,)

    summed = _rows(sum_devices, N_DEV * 8, N_DEV * 8, [_W(gathered)], [], [(8, n_sm)], "sum_small_grads")[0]
    (loss_s, dmod_sum, dmodc_sum, g_pre1, g_post1, g_pre2, g_post2, g_wdf_pad, g_bdf, g_wdb_pad, g_bdb, g_gn, g_lng, g_lnb,
     g_ws, g_bs) = _unpack(summed.reshape(-1), small_shapes)
    loss = loss_s[0, 0]
    dmod_rows = gathered.reshape(N_DEV, -1)[:, LANES:LANES + N_MOD * D]
    g_b_ada = dmod_sum + jnp.pad(dmodc_sum, ((0, 0), (0, (N_MOD - 2) * D)))
    dmod16 = jnp.zeros((16, N_MOD * D), F32).at[:N_DEV].set(dmod_rows).at[N_DEV, :2 * D].set(dmodc_sum[0])
    dmod16_sh = lax.dynamic_slice(dmod16, (0, s_me * MS), (16, MS))
    g_w_ada, d_w_ada, nm_w_ada, nv_w_ada = _ada_update(cond.T, dmod16_sh, w_ada[0], m_w_ada[0], v_w_ada[0], "w_ada_update")

    dcond, recv2_in = _mm(dmod16_sh, w_ada[0], 16, D, MS, name="cond_bwd", tb=True, tk=min(512, MS),
                          side=[_ph(recv2_in, [("rs_chips", 3, 4)], src=part_in)])
    half_in = _sum_chips(part_in, recv2_in, sc_arr, "rs_sum_chips_w_in")
    g_w_in = _rs_final([half_in], "rs_final_w_in")[0].reshape(D, -1)[:, :CS]
    part_c = _allgather_small(dcond[N_DEV].reshape(8, D // 8), "gather_dcond").reshape(N_DEV, D)

    def cctx_grad(p, cv):
        sg = jax.nn.sigmoid(cv)
        tot = ((p[0:1] + p[2:3]) + p[4:5]) + p[6:7]
        return (jnp.broadcast_to(tot * (sg * (1.0 + cv * (1.0 - sg))), p.shape),)

    g_c_ctx = _rows(cctx_grad, N_DEV, N_DEV, [_W(part_c), _W(c_ctx.reshape(1, D))], [(D, F32)], [], "c_ctx_grad")[0][0:1]

    def col_shard(g_full, width):
        return lax.dynamic_slice_in_dim(g_full, s_me * width, width, axis=g_full.ndim - 1)

    g_w_dec_f = col_shard(g_wdf_pad[:GLA_LOWRANK], KEYW // N_CHIP)
    g_w_dec_b = col_shard(g_wdb_pad[GLA_LOWRANK:LR], KEYW // N_CHIP)
    g_gla_norm = col_shard(g_gn.reshape(H, DV), DV // N_CHIP)
    small_w = [c_ctx, b_ada, pre1_g, post1_g, pre2_g, post2_g, w_dec_f, b_dec_f, w_dec_b, b_dec_b, gla_norm_g, sg_ln_g, sg_ln_b, w_s, b_s]
    small_m = [m_c_ctx, m_b_ada, m_pre1_g, m_post1_g, m_pre2_g, m_post2_g, m_w_dec_f, m_b_dec_f, m_w_dec_b, m_b_dec_b, m_gla_norm_g, m_sg_ln_g, m_sg_ln_b, m_w_s, m_b_s]
    small_v = [v_c_ctx, v_b_ada, v_pre1_g, v_post1_g, v_pre2_g, v_post2_g, v_w_dec_f, v_b_dec_f, v_w_dec_b, v_b_dec_b, v_gla_norm_g, v_sg_ln_g, v_sg_ln_b, v_w_s, v_b_s]
    small_g = [g_c_ctx, g_b_ada, g_pre1, g_post1, g_pre2, g_post2, g_w_dec_f, g_bdf, g_w_dec_b, g_bdb, g_gla_norm, g_lng, g_lnb, g_ws, g_bs]
    small_g = [g.reshape(w.shape) for g, w in zip(small_g, small_w)]
    shapes_w = [w.shape for w in small_w]
    upd = _adamw(_pack(small_w), _pack(small_g), _pack(small_m), _pack(small_v), "adamw_small")
    d_small, m_small, v_small = [_unpack(u.reshape(-1), shapes_w) for u in upd]

    def big(w, g, m, v, name):
        shp = w.shape
        res = _adamw(w.reshape(shp[-2:]), g.reshape(shp[-2:]), m.reshape(shp[-2:]), v.reshape(shp[-2:]), name)
        return [g.reshape(shp)] + [r.reshape(shp) for r in res]

    r_in = big(w_in, g_w_in, m_w_in, v_w_in, "adamw_w_in")
    r_o = big(w_o, g_w_o, m_w_o, v_w_o, "adamw_w_o")
    r_1 = big(w_1, g_w_1, m_w_1, v_w_1, "adamw_w_1")
    r_2 = big(w_2, g_w_2, m_w_2, v_w_2, "adamw_w_2")
    r_ada = [t.reshape(w_ada.shape) for t in (g_w_ada, d_w_ada, nm_w_ada, nv_w_ada)]

    def ordered(k):
        sm = [small_g, d_small, m_small, v_small][k]
        return [sm[0], r_ada[k], *sm[1:6], r_in[k], *sm[6:15], r_o[k], r_1[k], r_2[k]]

    return (loss, grad_x.reshape(x.shape), *ordered(0), *ordered(1), *ordered(2), *ordered(3))
```

```python
import functools
import math

import numpy as np
import jax
import jax.numpy as jnp
from jax import lax
from jax.experimental import pallas as pl
from jax.experimental.pallas import tpu as pltpu

F32 = jnp.float32
BF16 = jnp.bfloat16
MESH = pl.DeviceIdType.MESH
ANY = pl.BlockSpec(memory_space=pl.ANY)

GLA_HEADS = 8
GLA_CHUNK = 64
GLA_LOWRANK = 16
GLA_TAU = 16.0
ROPE_BASE = 10000.0
GRID_W = 64
SG_GROUPS = 4
SG_CHUNK = 128
N_MOD = 6
EPS = 1e-6
ADAM_LR = 0.001
ADAM_B1 = 0.9
ADAM_B2 = 0.999
ADAM_EPS = 1e-08
ADAM_WD = 0.01
ADAM_STEP = 10

LANES = 128
VMEM_LIMIT = 56 << 20
N_DEV = 8
N_CHIP = 4


def _params(sem=None):
    return pltpu.CompilerParams(dimension_semantics=sem, vmem_limit_bytes=VMEM_LIMIT)


def _pick(dim, target, unit=LANES):
    best = None
    for t in range(unit, min(dim, target) + 1, unit):
        if dim % t == 0:
            best = t
    return dim if best is None else best


def _dg(a, b, ca, cb, precision=None):
    return lax.dot_general(a, b, (((ca,), (cb,)), ((), ())), preferred_element_type=F32,
                           precision=precision)


def _place():
    return lax.axis_index("x"), lax.axis_index("y"), lax.axis_index("c")


def _allgather_small(v, name):
    m_per, n = v.shape

    def body(x_ref, out_ref, send_sems, recv_sems, local_sem):
        x, y, c = _place()
        me, sibling = (x, y, c), (x, y, 1 - c)
        chips = [(1 - x, y), (x, 1 - y), (1 - x, 1 - y)]

        def rows(px, py, pc):
            return out_ref.at[pl.ds((4 * px + 2 * py + pc) * m_per, m_per), :]

        def copy(k, block, to, src=None):
            return pltpu.make_async_remote_copy(
                src_ref=rows(*block) if src is None else src, dst_ref=rows(*block),
                send_sem=send_sems.at[k], recv_sem=recv_sems.at[k],
                device_id=to, device_id_type=MESH)

        mine = pltpu.make_async_copy(x_ref, rows(*me), local_sem)
        mine.start()
        first = [copy(0, me, sibling, src=x_ref)]
        first += [copy(1 + j, me, (*chip, c), src=x_ref) for j, chip in enumerate(chips)]
        for cp in first:
            cp.start()
        passed = [copy(4 + j, (*chip, c), sibling) for j, chip in enumerate(chips)]
        for j, chip in enumerate(chips):
            copy(1 + j, (*chip, c), me).wait_recv()
            passed[j].start()
        copy(0, sibling, me).wait_recv()
        for j, chip in enumerate(chips):
            copy(4 + j, (*chip, 1 - c), me).wait_recv()
        for cp in first + passed:
            cp.wait_send()
        mine.wait()

    return pl.pallas_call(
        body, name=name,
        out_shape=jax.ShapeDtypeStruct((N_DEV * m_per, n), v.dtype),
        in_specs=[pl.BlockSpec(memory_space=pltpu.VMEM)],
        out_specs=pl.BlockSpec(memory_space=pltpu.VMEM),
        scratch_shapes=[pltpu.SemaphoreType.DMA((7,)), pltpu.SemaphoreType.DMA((7,)),
                        pltpu.SemaphoreType.DMA],
        compiler_params=pltpu.CompilerParams(vmem_limit_bytes=VMEM_LIMIT),
    )(v)


def _cast_blocks(w, s_me, name):
    _, r, cols = w.shape
    tr = _row_tile(r, cols, 4)

    def body(s_ref, w_ref, o_ref):
        o_ref[...] = w_ref[...].astype(BF16)

    return pl.pallas_call(
        body, name=name,
        out_shape=jax.ShapeDtypeStruct((N_DEV, r, cols), BF16),
        grid_spec=pltpu.PrefetchScalarGridSpec(
            num_scalar_prefetch=1, grid=(2, r // tr),
            in_specs=[pl.BlockSpec((None, tr, cols), lambda h, i, s: (h, i, 0))],
            out_specs=pl.BlockSpec((None, tr, cols), lambda h, i, s: (2 * s[0] + h, i, 0))),
        compiler_params=_params(("arbitrary", "arbitrary")),
    )(s_me, w)


def _gather_big(ws, name):
    nw = len(ws)

    def body(*refs):
        outs = refs[nw:2 * nw]
        send_sems, recv_sems = refs[2 * nw:]
        x, y, c = _place()
        me, sibling = (x, y, c), (x, y, 1 - c)
        chips = [(1 - x, y), (x, 1 - y), (1 - x, 1 - y)]

        def blk(px, py, pc):
            return 4 * px + 2 * py + pc

        def copy(w, k, block, to):
            return pltpu.make_async_remote_copy(
                src_ref=outs[w].at[block], dst_ref=outs[w].at[block],
                send_sem=send_sems.at[6 * w + k], recv_sem=recv_sems.at[6 * w + k],
                device_id=to, device_id_type=MESH)

        first = []
        for w in range(nw):
            for j, chip in enumerate(chips):
                cp = copy(w, j, blk(x, y, c), (*chip, c))
                cp.start()
                first.append(cp)
        passed = []
        for w in range(nw):
            for j, chip in enumerate(chips):
                copy(w, j, blk(*chip, c), me).wait_recv()
                cp = copy(w, 3 + j, blk(*chip, c), sibling)
                cp.start()
                passed.append(cp)
        for w in range(nw):
            for j, chip in enumerate(chips):
                copy(w, 3 + j, blk(*chip, 1 - c), me).wait_recv()
        for cp in first + passed:
            cp.wait_send()

    return pl.pallas_call(
        body, name=name,
        out_shape=[jax.ShapeDtypeStruct(w.shape, w.dtype) for w in ws],
        in_specs=[ANY] * nw, out_specs=[ANY] * nw,
        input_output_aliases={w: w for w in range(nw)},
        scratch_shapes=[pltpu.SemaphoreType.DMA((6 * nw,)), pltpu.SemaphoreType.DMA((6 * nw,))],
    )(*ws)


def _rs_sibling(gs, name):
    nw = len(gs)

    def body(*refs):
        ins, outs = refs[:nw], refs[nw:2 * nw]
        send_sems, recv_sems = refs[2 * nw:]
        x, y, c = _place()
        cps = []
        for w in range(nw):
            for s in range(N_CHIP):
                cp = pltpu.make_async_remote_copy(
                    src_ref=ins[w].at[2 * s + (1 - c)], dst_ref=outs[w].at[s],
                    send_sem=send_sems.at[N_CHIP * w + s], recv_sem=recv_sems.at[N_CHIP * w + s],
                    device_id=(x, y, 1 - c), device_id_type=MESH)
                cp.start()
                cps.append(cp)
        for cp in cps:
            cp.wait()

    return pl.pallas_call(
        body, name=name,
        out_shape=[jax.ShapeDtypeStruct((N_CHIP,) + g.shape[1:], g.dtype) for g in gs],
        in_specs=[ANY] * nw, out_specs=[ANY] * nw,
        scratch_shapes=[pltpu.SemaphoreType.DMA((N_CHIP * nw,)), pltpu.SemaphoreType.DMA((N_CHIP * nw,))],
    )(*gs)


def _rs_final(fs, name):
    nw = len(fs)

    def body(*refs):
        outs = refs[nw:2 * nw]
        send_sems, recv_sems = refs[2 * nw:]
        x, y, c = _place()
        cps = []
        for w in range(nw):
            cp = pltpu.make_async_remote_copy(
                src_ref=outs[w].at[c], dst_ref=outs[w].at[c],
                send_sem=send_sems.at[w], recv_sem=recv_sems.at[w],
                device_id=(x, y, 1 - c), device_id_type=MESH)
            cp.start()
            cps.append(cp)
        for cp in cps:
            cp.wait()

    return pl.pallas_call(
        body, name=name,
        out_shape=[jax.ShapeDtypeStruct(f.shape, f.dtype) for f in fs],
        in_specs=[ANY] * nw, out_specs=[ANY] * nw,
        input_output_aliases={w: w for w in range(nw)},
        scratch_shapes=[pltpu.SemaphoreType.DMA((nw,)), pltpu.SemaphoreType.DMA((nw,))],
    )(*fs)


_PHASE_COPIES = {"gather_ici": 3, "gather_d2d": 3, "gather_chain": 6, "rs_sibling": N_CHIP, "rs_chips": 3}
QUARTERS = 4


def _ph(buf, legs, src=None):
    return dict(buf=buf, src=src, legs=legs)


def _n_copies(ph):
    return sum(_PHASE_COPIES[kind] for kind, _, _ in ph["legs"])


def _phase_copies(ph, src, buf, send_sems, recv_sems, base):
    x, y, c = _place()
    sibling = (x, y, 1 - c)
    chips = [(1 - x, y), (x, 1 - y), (1 - x, 1 - y)]
    r = buf.shape[1]

    def make(k, trip):
        a, b, dev = trip
        return pltpu.make_async_remote_copy(src_ref=a, dst_ref=b, send_sem=send_sems.at[base + k], recv_sem=recv_sems.at[base + k],
                                            device_id=dev, device_id_type=MESH)

    out = []
    for kind, lo, hi in ph["legs"]:
        rows = pl.ds(lo * r // QUARTERS, (hi - lo) * r // QUARTERS)
        ici = [(buf.at[4 * x + 2 * y + c, rows], buf.at[4 * x + 2 * y + c, rows], (*chip, c)) for chip in chips]
        d2d = [(buf.at[4 * chip[0] + 2 * chip[1] + c, rows], buf.at[4 * chip[0] + 2 * chip[1] + c, rows], sibling) for chip in chips]
        if kind == "gather_ici":
            trips, later = ici, []
        elif kind == "gather_d2d":
            trips, later = d2d, []
        elif kind == "gather_chain":
            trips, later = ici, d2d
        elif kind == "rs_sibling":
            trips, later = [(src.at[2 * s + (1 - c), rows], buf.at[s, rows], sibling) for s in range(N_CHIP)], []
        else:
            trips, later = [(src.at[2 * chip[0] + chip[1], rows], buf.at[j, rows], (*chip, c)) for j, chip in enumerate(chips)], []
        out.append(([make(k, t) for k, t in enumerate(trips)], [make(len(trips) + k, t) for k, t in enumerate(later)]))
        base += _PHASE_COPIES[kind]
    return out


def _side_call(inner, grid, in_specs, out_specs, out_shape, scratch, args, phases, name, semantics, mid=0.8):
    n_in, n_out, n_ph = len(in_specs), len(out_specs), len(phases)
    if n_ph == 0:
        outs = pl.pallas_call(inner, name=name, grid=grid, in_specs=in_specs, out_specs=out_specs, out_shape=out_shape,
                              scratch_shapes=scratch, compiler_params=_params(semantics))(*args)
        return list(outs), []
    n_cp = sum(_n_copies(p) for p in phases)
    side_args, buf_pos, src_pos = [], [], []
    for p in phases:
        buf_pos.append(len(side_args))
        side_args.append(p["buf"])
        src_pos.append(len(side_args) if p["src"] is not None else None)
        if p["src"] is not None:
            side_args.append(p["src"])
    n_side = len(side_args)
    total = math.prod(grid)
    mid_lin = min(total - 1, int(total * mid))

    def body(*refs):
        b_in, s_in = refs[:n_in], refs[n_in:n_in + n_side]
        b_out, s_out = refs[n_in + n_side:n_in + n_side + n_out], refs[n_in + n_side + n_out:n_in + n_side + n_out + n_ph]
        rest = refs[n_in + n_side + n_out + n_ph:]
        send_sems, recv_sems = rest[-2:]
        lin = functools.reduce(lambda acc, ag: acc * ag[1] + pl.program_id(ag[0]), list(enumerate(grid))[1:], pl.program_id(0))

        def copies():
            out, base = [], 0
            for p, sp, so in zip(phases, src_pos, s_out):
                out += _phase_copies(p, None if sp is None else s_in[sp], so, send_sems, recv_sems, base)
                base += _n_copies(p)
            return out

        @pl.when(lin == 0)
        def _():
            for a, _ in copies():
                for cp in a:
                    cp.start()

        if any(kind == "gather_chain" for p in phases for kind, _, _ in p["legs"]):
            @pl.when(lin == mid_lin)
            def _():
                for a, b in copies():
                    if b:
                        for cp in a:
                            cp.wait()
                        for cp in b:
                            cp.start()

        inner(*b_in, *b_out, *rest[:-2])

        @pl.when(lin == total - 1)
        def _():
            for a, b in copies():
                for cp in (b if b else a):
                    cp.wait()

    outs = pl.pallas_call(
        body, name=name, grid=grid,
        in_specs=list(in_specs) + [ANY] * n_side, out_specs=list(out_specs) + [ANY] * n_ph,
        out_shape=list(out_shape) + [jax.ShapeDtypeStruct(p["buf"].shape, p["buf"].dtype) for p in phases],
        input_output_aliases={n_in + bp: n_out + k for k, bp in enumerate(buf_pos)},
        scratch_shapes=list(scratch) + [pltpu.SemaphoreType.DMA((n_cp,)), pltpu.SemaphoreType.DMA((n_cp,))],
        compiler_params=_params(("arbitrary",) * len(grid)),
    )(*args, *side_args)
    return list(outs[:n_out]), list(outs[n_out:])


def _row_tile(r, cols, itemsize):
    t = r
    while t * cols * itemsize > (2 << 20) and t % 16 == 0:
        t //= 2
    return t


def _sum_sibling(g, r1, c_me, name):
    _, r, cols = g.shape
    tr = _row_tile(r, cols, 4)

    def body(c_ref, g_ref, r_ref, o_ref):
        o_ref[...] = (g_ref[...].astype(F32) + r_ref[...].astype(F32)).astype(o_ref.dtype)

    return pl.pallas_call(
        body, name=name,
        out_shape=jax.ShapeDtypeStruct((N_CHIP, r, cols), g.dtype),
        grid_spec=pltpu.PrefetchScalarGridSpec(
            num_scalar_prefetch=1, grid=(N_CHIP, r // tr),
            in_specs=[pl.BlockSpec((None, tr, cols), lambda s, i, c: (2 * s + c[0], i, 0)),
                      pl.BlockSpec((None, tr, cols), lambda s, i, c: (s, i, 0))],
            out_specs=pl.BlockSpec((None, tr, cols), lambda s, i, c: (s, i, 0))),
        compiler_params=_params(("arbitrary", "arbitrary")),
    )(c_me, g, r1)


def _sum_chips(p, r2, sc_me, name):
    _, r, cols = p.shape
    tr = _row_tile(r, cols, 4)

    def body(s_ref, p_ref, a_ref, b_ref, c_ref, o_ref):
        o_ref[...] = ((p_ref[...].astype(F32) + a_ref[...].astype(F32)) + b_ref[...].astype(F32)) + c_ref[...].astype(F32)

    return pl.pallas_call(
        body, name=name,
        out_shape=jax.ShapeDtypeStruct((2, r, cols), F32),
        grid_spec=pltpu.PrefetchScalarGridSpec(
            num_scalar_prefetch=1, grid=(r // tr,),
            in_specs=[pl.BlockSpec((None, tr, cols), lambda i, s: (s[0], i, 0)),
                      pl.BlockSpec((None, tr, cols), lambda i, s: (0, i, 0)),
                      pl.BlockSpec((None, tr, cols), lambda i, s: (1, i, 0)),
                      pl.BlockSpec((None, tr, cols), lambda i, s: (2, i, 0))],
            out_specs=pl.BlockSpec((None, tr, cols), lambda i, s: (s[1], i, 0))),
        compiler_params=_params(("arbitrary",)),
    )(sc_me, p, r2, r2, r2)


def _shard_columns(g_al, shard_cols, bound, off_lo, off_hi, rh, name):
    d, acols = g_al.shape
    wp = -(-shard_cols // LANES) * LANES
    tr = min(LANES, rh)
    nt = acols // LANES

    def body(x_ref, o_ref):
        s = pl.program_id(1)
        lane = lax.broadcasted_iota(jnp.int32, (tr, LANES), 1)

        def tile(q):
            q = max(0, min(nt - 1, q))
            return x_ref[:, q * LANES:(q + 1) * LANES].astype(F32)

        def read(start):
            q, sh = divmod(start, LANES)
            if sh == 0:
                return tile(q)
            return jnp.where(lane < LANES - sh, pltpu.roll(tile(q), LANES - sh, 1), pltpu.roll(tile(q + 1), LANES - sh, 1))

        for k in range(N_CHIP):
            @pl.when(s == k)
            def _(k=k):
                for t in range(wp // LANES):
                    n0 = k * shard_cols + t * LANES
                    if n0 + LANES <= bound:
                        v = read(n0 + off_lo)
                    elif n0 >= bound:
                        v = read(n0 + off_hi)
                    else:
                        v = jnp.where(lane < bound - n0, read(n0 + off_lo), read(n0 + off_hi))
                    o_ref[:, t * LANES:(t + 1) * LANES] = v.astype(o_ref.dtype)

    return pl.pallas_call(
        body, name=name, grid=(d // tr, N_CHIP),
        out_shape=jax.ShapeDtypeStruct((N_DEV, rh, wp), BF16),
        in_specs=[pl.BlockSpec((tr, acols), lambda i, s: (i, 0))],
        out_specs=pl.BlockSpec((None, tr, wp), lambda i, s: (2 * s + (i * tr) // rh, ((i * tr) % rh) // tr, 0)),
        compiler_params=_params(("arbitrary", "arbitrary")),
    )(g_al)


def _mm(a, b, M, N, K, *, name, ta=False, tb=False, out_dtypes=(F32,), tm=1024, tn=1024, tk=2048,
        a_spec=None, b_spec=None, out_specs=None, out_shapes=None, epi=None, epi_in=(), epi_specs=(), side=(), side_mid=0.8):
    tm, tn, tk = min(tm, M), min(tn, N), min(tk, K)
    assert M % tm == 0 and N % tn == 0 and K % tk == 0, (name, M, N, K, tm, tn, tk)
    nk = K // tk
    n_epi, n_out = len(epi_in), len(out_dtypes)
    if a_spec is None:
        a_spec = pl.BlockSpec((tk, tm), lambda i, j, k: (k, i)) if ta else pl.BlockSpec((tm, tk), lambda i, j, k: (i, k))
    if b_spec is None:
        b_spec = pl.BlockSpec((tn, tk), lambda i, j, k: (j, k)) if tb else pl.BlockSpec((tk, tn), lambda i, j, k: (k, j))
    if out_specs is None:
        out_specs = [pl.BlockSpec((tm, tn), lambda i, j, k: (i, j))] * n_out
        out_shapes = [jax.ShapeDtypeStruct((M, N), dt) for dt in out_dtypes]

    def body(a_ref, b_ref, *rest):
        epi_refs, o_refs, acc = rest[:n_epi], rest[n_epi:n_epi + n_out], rest[-1]
        k = pl.program_id(2)

        @pl.when(k == 0)
        def _():
            acc[...] = jnp.zeros_like(acc)

        acc[...] += _dg(a_ref[...].astype(BF16), b_ref[...].astype(BF16), 0 if ta else 1, 1 if tb else 0)

        @pl.when(k == nk - 1)
        def _():
            r = acc[...]
            vals = (r,) if epi is None else epi(r, *[e[...] for e in epi_refs])
            for o_ref, v in zip(o_refs, vals):
                o_ref[...] = v.astype(o_ref.dtype)

    outs, side_outs = _side_call(body, (M // tm, N // tn, nk), [a_spec, b_spec, *epi_specs], out_specs, out_shapes,
                                 [pltpu.VMEM((tm, tn), F32)], (a, b, *epi_in), list(side), name,
                                 ("parallel", "parallel", "arbitrary"), mid=side_mid)
    if side:
        return outs + side_outs
    return outs[0] if n_out == 1 else outs


def _T(arr, width, col=0, lead=None):
    return ("tile", arr, width, col, lead)


def _W(arr):
    return ("whole", arr)


def _rows(fn, n_rows, tr, ins, tile_outs, acc_outs, name):
    tr = min(tr, n_rows)
    assert n_rows % tr == 0, (name, n_rows, tr)
    in_specs, args = [], []
    for d in ins:
        if d[0] == "tile":
            _, arr, width, col, lead = d
            if lead is None:
                in_specs.append(pl.BlockSpec((tr, width), lambda i, col=col: (i, col)))
            else:
                in_specs.append(pl.BlockSpec((None, tr, width), lambda i, col=col, lead=lead: (lead, i, col)))
            args.append(arr)
        else:
            arr = d[1]
            in_specs.append(pl.BlockSpec(arr.shape, lambda i, nd=arr.ndim: (0,) * nd))
            args.append(arr)
    n_in, n_t = len(ins), len(tile_outs)
    out_shape = [jax.ShapeDtypeStruct((n_rows, w), dt) for w, dt in tile_outs]
    out_specs = [pl.BlockSpec((tr, w), lambda i: (i, 0)) for w, _ in tile_outs]
    out_shape += [jax.ShapeDtypeStruct(s, F32) for s in acc_outs]
    out_specs += [pl.BlockSpec(s, lambda i, nd=len(s): (0,) * nd) for s in acc_outs]

    def body(*refs):
        in_refs, t_refs, a_refs = refs[:n_in], refs[n_in:n_in + n_t], refs[n_in + n_t:]
        vals = fn(*[r[...] for r in in_refs])
        for r, v in zip(t_refs, vals[:n_t]):
            r[...] = v.astype(r.dtype)
        first = pl.program_id(0) == 0
        for r, v in zip(a_refs, vals[n_t:]):
            @pl.when(first)
            def _(r=r, v=v):
                r[...] = v

            @pl.when(jnp.logical_not(first))
            def _(r=r, v=v):
                r[...] += v

    return pl.pallas_call(
        body, name=name, out_shape=out_shape, grid=(n_rows // tr,),
        in_specs=in_specs, out_specs=out_specs,
        compiler_params=_params(("arbitrary",)),
    )(*args)


def _colsum(t):
    return jnp.sum(t, axis=0, keepdims=True)


def _prenorm(x, g, sc, sh):
    xf = x.astype(F32)
    return xf * lax.rsqrt(jnp.mean(xf * xf, axis=-1, keepdims=True) + EPS) * g * (1.0 + sc) + sh


def _postnorm(x, y, gate, pg):
    return x + gate * (y * lax.rsqrt(jnp.mean(y * y, axis=-1, keepdims=True) + EPS) * pg)


def _gelu(t):
    return 0.5 * t * (1.0 + lax.erf(t * (2.0 ** -0.5)))


def _sg_pre(zu, zv, lng, lnb):
    u, vr = _gelu(zu), _gelu(zv)
    mu = jnp.mean(vr, axis=-1, keepdims=True)
    var = jnp.mean(jnp.square(vr - mu), axis=-1, keepdims=True)
    return u, (vr - mu) * lax.rsqrt(var + EPS) * lng + lnb


def _sg_mix(vv, ws_ref_vals, bs_vals, gw):
    parts = []
    for g in range(SG_GROUPS):
        s = _dg(ws_ref_vals[g].astype(BF16), vv[:, g * gw:(g + 1) * gw].astype(BF16), 1, 0)
        parts.append(s + bs_vals[g])
    return jnp.concatenate(parts, axis=1)


def _readout(o, r, g, heads, dv):
    parts = []
    for h in range(heads):
        oh = o[:, h * dv:(h + 1) * dv]
        parts.append(oh * lax.rsqrt(jnp.mean(oh * oh, axis=-1, keepdims=True) + EPS))
    return jnp.concatenate(parts, axis=1) * g * (r * jax.nn.sigmoid(r))


def _log_sigmoid(a):
    return jnp.minimum(a, 0.0) - jnp.log(1.0 + jnp.exp(-jnp.abs(a)))


def _rope_swap(t, m):
    lane = lax.broadcasted_iota(jnp.int32, t.shape, 1)
    return jnp.where((lane % (2 * m)) < m, pltpu.roll(t, 3 * m, 1), pltpu.roll(t, m, 1))


def _rope(t, cos, sin, heads, dk):
    parts = []
    for h in range(heads):
        th = t[:, h * dk:(h + 1) * dk]
        parts.append(th * cos + _rope_swap(th, dk // 4) * sin)
    return jnp.concatenate(parts, axis=1)


def _rope_t(dt, cos, sin, heads, dk):
    parts = []
    for h in range(heads):
        dh = dt[:, h * dk:(h + 1) * dk]
        parts.append(dh * cos + _rope_swap(dh * sin, dk // 4))
    return jnp.concatenate(parts, axis=1)


def _chunk_cumsum(t, upwards):
    n = t.shape[0]
    row = lax.broadcasted_iota(jnp.int32, (n, n), 0)
    col = lax.broadcasted_iota(jnp.int32, (n, n), 1)
    shift = GLA_CHUNK.bit_length() - 1
    same = jnp.right_shift(row, shift) == jnp.right_shift(col, shift)
    tri = jnp.logical_and(same, col <= row if upwards else col >= row)
    return _dg(tri.astype(F32), t, 1, 0, precision=lax.Precision.HIGHEST)


def _chunk_terms(d, qv, kv, b, C):
    row = lax.broadcasted_iota(jnp.int32, (C, C), 0)
    col = lax.broadcasted_iota(jnp.int32, (C, C), 1)
    tri = row >= col if d == 0 else row <= col
    end_row = lax.broadcasted_iota(jnp.int32, b.shape, 0) == (C - 1 if d == 0 else 0)
    btot = _colsum(jnp.where(end_row, b, 0.0))
    eb, enb, ebt = jnp.exp(b), jnp.exp(-b), jnp.exp(btot - b)
    return tri, btot, eb, enb, ebt, qv * eb, kv * enb, kv * ebt


def _gla_fwd(q, k, zv, v_col0, la, st0, heads, dk, dv, name, side=(), side_mid=0.8):
    n, C, H = q.shape[0], GLA_CHUNK, heads
    nc = n // C

    def body(qf, kf, vf, laf, qb, kb, vb_, lab, st0_ref, of_ref, ob_ref, sf_ref, sb_ref, fin_ref, st):
        i = pl.program_id(1)

        @pl.when(i == 0)
        def _():
            st[...] = st0_ref[...]

        for d, (q_ref, k_ref, v_ref, la_ref, o_ref, save_ref) in enumerate(((qf, kf, vf, laf, of_ref, sf_ref), (qb, kb, vb_, lab, ob_ref, sb_ref))):
            tri, btot, _, _, _, qt, kt, kh = _chunk_terms(d, q_ref[...], k_ref[...], la_ref[...], C)
            s = st[d]
            vb = v_ref[...].astype(BF16)
            qtb = qt.astype(BF16)
            att = jnp.where(tri, _dg(qtb, kt.astype(BF16), 1, 1), 0.0)
            o_ref[...] = _dg(qtb, s.astype(BF16), 1, 1) + _dg(att.astype(BF16), vb, 1, 0)
            save_ref[...] = s
            s_new = s * jnp.exp(btot) + _dg(vb, kh.astype(BF16), 0, 0)
            st[d] = s_new

            @pl.when(i == nc - 1)
            def _(d=d, s_new=s_new):
                fin_ref[d] = s_new

    def seq(width, col0, rev, dir_cols=0):
        if rev:
            return pl.BlockSpec((C, width), lambda h, i: (nc - 1 - i, col0 + dir_cols + h))
        return pl.BlockSpec((C, width), lambda h, i: (i, col0 + h))

    both = pl.BlockSpec((2, None, dv, dk), lambda h, i: (0, h, 0, 0))
    outs, side_outs = _side_call(
        body, (H, nc),
        [seq(dk, 0, False), seq(dk, 0, False), seq(dv, v_col0, False), seq(dk, 0, False),
         seq(dk, 0, True), seq(dk, 0, True), seq(dv, v_col0, True), seq(dk, 0, True, H), both],
        [seq(dv, 0, False), seq(dv, 0, True),
         pl.BlockSpec((None, None, dv, dk), lambda h, i: (h, i, 0, 0)),
         pl.BlockSpec((None, None, dv, dk), lambda h, i: (h, nc - 1 - i, 0, 0)), both],
        [jax.ShapeDtypeStruct((n, H * dv), F32)] * 2 + [jax.ShapeDtypeStruct((H, nc, dv, dk), F32)] * 2
        + [jax.ShapeDtypeStruct((2, H, dv, dk), F32)],
        [pltpu.VMEM((2, dv, dk), F32)], (q, k, zv, la, q, k, zv, la, st0), list(side), name, ("arbitrary", "arbitrary"),
        mid=side_mid)
    return outs + side_outs


def _gla_bwd(q, k, zv, v_col0, la, saved_f, saved_b, do, dfin, heads, dk, dv, name, side=()):
    n, C, H = q.shape[0], GLA_CHUNK, heads
    nc = n // C

    def body(qf, kf, vf, laf, sf, dof, qb, kb, vb_, lab, sb, dob_, dfin_ref,
             dqf, dqb, dkf, dkb, dvf, dvb, dlaf, dlab, d0_ref, dst):
        i = pl.program_id(1)

        @pl.when(i == 0)
        def _():
            dst[...] = dfin_ref[...]

        dirs = ((qf, kf, vf, laf, sf, dof, dqf, dkf, dvf, dlaf), (qb, kb, vb_, lab, sb, dob_, dqb, dkb, dvb, dlab))
        for d, (q_ref, k_ref, v_ref, la_ref, save_ref, do_ref, dq_ref, dk_ref, dv_ref, dla_ref) in enumerate(dirs):
            tri, btot, eb, enb, ebt, qt, kt, kh = _chunk_terms(d, q_ref[...], k_ref[...], la_ref[...], C)
            s, dsn = save_ref[...], dst[d]
            vb, dob = v_ref[...].astype(BF16), do_ref[...].astype(BF16)
            qtb, ktb, khb, dsnb = qt.astype(BF16), kt.astype(BF16), kh.astype(BF16), dsn.astype(BF16)
            att = jnp.where(tri, _dg(qtb, ktb, 1, 1), 0.0).astype(BF16)
            datt = jnp.where(tri, _dg(dob, vb, 1, 1), 0.0).astype(BF16)
            dqt = _dg(dob, s.astype(BF16), 1, 0) + _dg(datt, ktb, 1, 0)
            dkt = _dg(datt, qtb, 0, 0)
            dkh = _dg(vb, dsnb, 1, 0)
            dv_ref[...] = _dg(att, dob, 0, 0) + _dg(khb, dsnb, 1, 1)
            ebtot = jnp.exp(btot)
            dbtot = ebtot * _colsum(s * dsn) + _colsum(dkh * kh)
            s0 = dsn * ebtot + _dg(dob, qtb, 0, 0)
            dst[d] = s0
            db = dqt * qt - dkt * kt - dkh * kh
            dq_ref[...] = dqt * eb
            dk_ref[...] = dkt * enb + dkh * ebt
            end_row = lax.broadcasted_iota(jnp.int32, db.shape, 0) == (C - 1 if d == 0 else 0)
            dla_ref[...] = db + jnp.where(end_row, dbtot, 0.0)

            @pl.when(i == nc - 1)
            def _(d=d, s0=s0):
                d0_ref[d] = s0

    def seq(width, col0, fwd_dir, dir_cols=0):
        if fwd_dir:
            return pl.BlockSpec((C, width), lambda h, i: (nc - 1 - i, col0 + h))
        return pl.BlockSpec((C, width), lambda h, i: (i, col0 + dir_cols + h))

    both = pl.BlockSpec((2, None, dv, dk), lambda h, i: (0, h, 0, 0))
    sav_f = pl.BlockSpec((None, None, dv, dk), lambda h, i: (h, nc - 1 - i, 0, 0))
    sav_b = pl.BlockSpec((None, None, dv, dk), lambda h, i: (h, i, 0, 0))
    outs, side_outs = _side_call(
        body, (H, nc),
        [seq(dk, 0, True), seq(dk, 0, True), seq(dv, v_col0, True), seq(dk, 0, True), sav_f, seq(dv, 0, True),
         seq(dk, 0, False), seq(dk, 0, False), seq(dv, v_col0, False), seq(dk, 0, False, H), sav_b, seq(dv, 0, False), both],
        [seq(dk, 0, True), seq(dk, 0, False), seq(dk, 0, True), seq(dk, 0, False), seq(dv, 0, True), seq(dv, 0, False),
         seq(dk, 0, True), seq(dk, 0, False), both],
        [jax.ShapeDtypeStruct((n, H * dk), F32)] * 4 + [jax.ShapeDtypeStruct((n, H * dv), F32)] * 2
        + [jax.ShapeDtypeStruct((n, H * dk), F32)] * 2 + [jax.ShapeDtypeStruct((2, H, dv, dk), F32)],
        [pltpu.VMEM((2, dv, dk), F32)], (q, k, zv, la, saved_f, do, q, k, zv, la, saved_b, do, dfin), list(side), name,
        ("arbitrary", "arbitrary"))
    return outs + side_outs


def _adamw_math(w, g, m, v):
    m2 = ADAM_B1 * m + (1.0 - ADAM_B1) * g
    v2 = ADAM_B2 * v + (1.0 - ADAM_B2) * jnp.square(g)
    m_hat = m2 / (1.0 - ADAM_B1 ** ADAM_STEP)
    v_hat = v2 / (1.0 - ADAM_B2 ** ADAM_STEP)
    delta = -ADAM_LR * (m_hat / (jnp.sqrt(v_hat) + ADAM_EPS) + ADAM_WD * w)
    return delta, m2, v2


def _adamw(w, g, m, v, name):
    r, cols = w.shape
    tr = _row_tile(r, cols, 4 * 4)
    spec = pl.BlockSpec((tr, cols), lambda i: (i, 0))

    def body(w_ref, g_ref, m_ref, v_ref, d_ref, m2_ref, v2_ref):
        d_ref[...], m2_ref[...], v2_ref[...] = _adamw_math(w_ref[...], g_ref[...], m_ref[...], v_ref[...])

    return _side_call(body, (r // tr,), [spec] * 4, [spec] * 3, [jax.ShapeDtypeStruct((r, cols), F32)] * 3,
                      [], (w, g, m, v), [], name, ("parallel",))[0]


def _adamw_many(ws, gs, ms, vs, name):
    n = len(ws)

    def body(*refs):
        ins, outs = refs[:4 * n], refs[4 * n:]
        for k in range(n):
            d, m2, v2 = _adamw_math(ins[k][...], ins[n + k][...], ins[2 * n + k][...], ins[3 * n + k][...])
            outs[k][...], outs[n + k][...], outs[2 * n + k][...] = d, m2, v2

    outs = pl.pallas_call(
        body, name=name, out_shape=[jax.ShapeDtypeStruct(w.shape, F32) for w in ws] * 3,
        in_specs=[pl.BlockSpec(memory_space=pltpu.VMEM)] * (4 * n), out_specs=[pl.BlockSpec(memory_space=pltpu.VMEM)] * (3 * n),
        compiler_params=_params(),
    )(*ws, *gs, *ms, *vs)
    return outs[:n], outs[n:2 * n], outs[2 * n:]


def _ada_update(cond_t, dmod, w, m, v, name):
    r, cols = w.shape
    tr, tc = _pick(r, 512, 8), _pick(cols, 1024)
    spec = pl.BlockSpec((tr, tc), lambda i, j: (i, j))

    def body(c_ref, d_ref, w_ref, m_ref, v_ref, g_ref, dl_ref, m2_ref, v2_ref):
        g = _dg(c_ref[...].astype(BF16), d_ref[...].astype(BF16), 1, 0)
        g_ref[...] = g
        dl_ref[...], m2_ref[...], v2_ref[...] = _adamw_math(w_ref[...], g, m_ref[...], v_ref[...])

    return _side_call(
        body, (r // tr, cols // tc),
        [pl.BlockSpec((tr, cond_t.shape[1]), lambda i, j: (i, 0)), pl.BlockSpec((dmod.shape[0], tc), lambda i, j: (0, j)), spec, spec, spec],
        [spec] * 4, [jax.ShapeDtypeStruct((r, cols), F32)] * 4, [], (cond_t, dmod, w, m, v), [], name, ("parallel", "parallel"))[0]


def _pack(parts, rows=8):
    flat = jnp.concatenate([p.reshape(-1).astype(F32) for p in parts])
    n = -(-flat.shape[0] // (rows * LANES)) * LANES
    return jnp.pad(flat, (0, rows * n - flat.shape[0])).reshape(rows, n)


def _unpack(flat, shapes):
    out, off = [], 0
    for s in shapes:
        size = math.prod(s)
        out.append(flat[off:off + size].reshape(s))
        off += size
    return out


def kernel(x, c, ctx, c_ctx, w_ada, b_ada, pre1_g, post1_g, pre2_g, post2_g, w_in, w_dec_f, b_dec_f, w_dec_b, b_dec_b, gla_norm_g, sg_ln_g, sg_ln_b, w_s, b_s, w_o, w_1, w_2, loss_target, m_c_ctx, m_w_ada, m_b_ada, m_pre1_g, m_post1_g, m_pre2_g, m_post2_g, m_w_in, m_w_dec_f, m_b_dec_f, m_w_dec_b, m_b_dec_b, m_gla_norm_g, m_sg_ln_g, m_sg_ln_b, m_w_s, m_b_s, m_w_o, m_w_1, m_w_2, v_c_ctx, v_w_ada, v_b_ada, v_pre1_g, v_post1_g, v_pre2_g, v_post2_g, v_w_in, v_w_dec_f, v_b_dec_f, v_w_dec_b, v_b_dec_b, v_gla_norm_g, v_sg_ln_g, v_sg_ln_b, v_w_s, v_b_s, v_w_o, v_w_1, v_w_2):
    N, D = x.shape[1], x.shape[2]
    NC = ctx.shape[1]
    H = GLA_HEADS
    VALW = D // 2
    DV = VALW // H
    DK = DV // 2
    KEYW = H * DK
    SGW = D - VALW
    GW = SGW // SG_GROUPS
    LR = 2 * GLA_LOWRANK
    F = w_1.shape[2] * N_CHIP
    FS = F // N_CHIP
    RH = D // 2
    MS = w_ada.shape[2]
    IN_COLS = w_in.shape[2] * N_CHIP
    K0, V0, R0, LF0 = KEYW, 2 * KEYW, 2 * KEYW + VALW, 2 * KEYW + 2 * VALW
    SG0 = LF0 + LR
    AQ, AK, AV, AR, ALR = 2 * SGW, 2 * SGW + KEYW, 2 * SGW + 2 * KEYW, 2 * SGW + 2 * KEYW + VALW, 2 * SGW + 2 * KEYW + 2 * VALW
    ACOLS = ALR + 4 * LANES
    LR_PAD = ACOLS - ALR
    assert IN_COLS == SG0 + 2 * SGW and N % SG_CHUNK == 0 and N % GLA_CHUNK == 0 and NC % GLA_CHUNK == 0

    ax, ay, ac = _place()
    s_me = (2 * ax + ay).astype(jnp.int32)
    b_me = (4 * ax + 2 * ay + ac).astype(jnp.int32)
    s_arr, c_arr = s_me.reshape(1), ac.astype(jnp.int32).reshape(1)
    sc_arr = jnp.concatenate([s_arr, c_arr])
    CS = IN_COLS // N_CHIP

    shards = [_cast_blocks(w_in[0].reshape(2, RH, CS), s_arr, "cast_w_in"), _cast_blocks(w_o[0].reshape(2, D // N_DEV, D), s_arr, "cast_w_o"),
              _cast_blocks(w_1[0].reshape(2, RH, FS), s_arr, "cast_w_1"), _cast_blocks(w_2[0].reshape(2, F // N_DEV, D), s_arr, "cast_w_2")]
    win_g, = _gather_big(shards[:1], "gather_weights")
    wo_buf, w1_buf, w2_buf = shards[1], shards[2], shards[3]
    w_in_nat = win_g.reshape(N_CHIP, 2, RH, IN_COLS // N_CHIP).transpose(1, 2, 0, 3).reshape(D, IN_COLS)
    w_al = jnp.concatenate([w_in_nat[:, SG0:], w_in_nat[:, :LF0], w_in_nat[:, LF0:SG0],
                            jnp.zeros((D, LR_PAD - LR), BF16)], axis=1)

    n_dec = GLA_LOWRANK * (KEYW // N_CHIP)
    g0 = _allgather_small(_pack([c, w_dec_f, w_dec_b, gla_norm_g]), "gather_small0").reshape(N_DEV, -1)
    c_all = g0[:, :D]
    per_chip = g0[0::2]
    wdf = per_chip[:, D:D + n_dec].reshape(N_CHIP, GLA_LOWRANK, KEYW // N_CHIP).transpose(1, 0, 2).reshape(GLA_LOWRANK, KEYW)
    wdb = per_chip[:, D + n_dec:D + 2 * n_dec].reshape(N_CHIP, GLA_LOWRANK, KEYW // N_CHIP).transpose(1, 0, 2).reshape(GLA_LOWRANK, KEYW)
    gn_full = per_chip[:, D + 2 * n_dec:D + 2 * n_dec + H * (DV // N_CHIP)].reshape(N_CHIP, H, DV // N_CHIP).transpose(1, 0, 2).reshape(1, VALW)
    wd_f = jnp.zeros((LANES, KEYW), F32).at[:GLA_LOWRANK].set(wdf)
    wd_b = jnp.zeros((LANES, KEYW), F32).at[GLA_LOWRANK:LR].set(wdb)

    cond_in = jnp.zeros((16, D), F32).at[:N_DEV].set(c_all).at[N_DEV].set(c_ctx)
    b_ada_sh = lax.dynamic_slice(b_ada, (0, s_me * MS), (1, MS))

    def mod_epi(r, bias):
        return (r + bias,)

    def silu_rows(t):
        return (t * jax.nn.sigmoid(t),)

    cond = _rows(silu_rows, 16, 16, [_W(cond_in)], [(D, F32)], [], "cond_silu")[0]
    mod_sh = _mm(cond, w_ada[0], 16, MS, D, name="mod_matmul", tn=512, tk=D, epi=mod_epi, epi_in=(b_ada_sh,),
                 epi_specs=(pl.BlockSpec((1, min(512, MS)), lambda i, j, k: (0, j)),))
    g1m = _allgather_small(mod_sh, "gather_mod").reshape(N_DEV, 16, MS)[0::2]
    mod_all = g1m.transpose(1, 0, 2).reshape(16, N_CHIP * MS)
    mod_me = lax.dynamic_slice(mod_all, (b_me, 0), (1, N_MOD * D))
    sh1, sc1, gt1, sh2, sc2, gt2 = [mod_me[:, i * D:(i + 1) * D] for i in range(N_MOD)]
    csh1, csc1 = mod_all[N_DEV:N_DEV + 1, :D], mod_all[N_DEV:N_DEV + 1, D:2 * D]

    mq = DK // 4
    pos = np.arange(N)
    inv_freq = (np.float32(ROPE_BASE) ** (-np.arange(mq, dtype=np.float32) / np.float32(mq))).astype(np.float32)
    ang_r = (pos // GRID_W).astype(np.float32)[:, None] * inv_freq[None, :]
    ang_c = (pos % GRID_W).astype(np.float32)[:, None] * inv_freq[None, :]
    cos_t = jnp.asarray(np.concatenate([np.cos(ang_r), np.cos(ang_r), np.cos(ang_c), np.cos(ang_c)], axis=1), F32)
    sin_t = jnp.asarray(np.concatenate([-np.sin(ang_r), np.sin(ang_r), -np.sin(ang_c), np.sin(ang_c)], axis=1), F32)

    x2, tgt, ctx2 = x[0], loss_target[0], ctx[0]
    TR = 128
    qscale = DK ** -0.5

    def prenorm_fwd(xa, g, sc, sh, n_rows, name):
        return _rows(lambda xv, gv, scv, shv: (_prenorm(xv, gv, scv, shv),), n_rows, TR,
                     [_T(xa, D), _W(g), _W(sc), _W(sh)], [(D, BF16)], [], name)[0]

    hx = prenorm_fwd(x2, pre1_g, sc1, sh1, N, "prenorm1_x")
    hc = prenorm_fwd(ctx2, pre1_g, csc1, csh1, NC, "prenorm1_ctx")
    tka = _pick(ACOLS, 1536, 2 * LANES)
    z_al, w1_buf = _mm(hx, w_al, N, ACOLS, D, name="in_proj_x", tn=tka, side=[_ph(w1_buf, [("gather_ici", 0, 2)])])
    zc_al = _mm(hc, w_al, NC, ACOLS, D, name="in_proj_ctx", tn=tka)

    def decay(lr, wdf_v, wdb_v, bf_v, bb_v):
        lrb = lr.astype(BF16)
        a_f = _dg(lrb, wdf_v.astype(BF16), 1, 0) + bf_v
        a_b = _dg(lrb, wdb_v.astype(BF16), 1, 0) + bb_v
        return a_f, a_b

    def running_decay(a_f, a_b):
        return jnp.concatenate([_chunk_cumsum(_log_sigmoid(a_f) / GLA_TAU, True), _chunk_cumsum(_log_sigmoid(a_b) / GLA_TAU, False)], axis=1)

    def prep_x(zq, zk, lr, cs, sn, wdf_v, wdb_v, bf_v, bb_v):
        return _rope(zq * qscale, cs, sn, H, DK), _rope(zk, cs, sn, H, DK), running_decay(*decay(lr, wdf_v, wdb_v, bf_v, bb_v))

    def prep_c(zk, lr, wdf_v, wdb_v, bf_v, bb_v):
        return zk, running_decay(*decay(lr, wdf_v, wdb_v, bf_v, bb_v))

    dec_w = [_W(wd_f), _W(wd_b), _W(b_dec_f), _W(b_dec_b)]
    q_r, k_r, la_x = _rows(prep_x, N, TR, [_T(z_al, KEYW, AQ // KEYW), _T(z_al, KEYW, AK // KEYW), _T(z_al, LANES, ALR // LANES),
                                           _T(cos_t, DK), _T(sin_t, DK)] + dec_w,
                           [(KEYW, F32), (KEYW, F32), (2 * KEYW, F32)], [], "gla_prep_x")
    k_c, la_c = _rows(prep_c, NC, TR, [_T(zc_al, KEYW, AK // KEYW), _T(zc_al, LANES, ALR // LANES)] + dec_w,
                      [(KEYW, F32), (2 * KEYW, F32)], [], "gla_prep_ctx")

    zero_state = jnp.zeros((2, H, DV, DK), F32)
    q_c = jnp.zeros((NC, KEYW), F32)
    _, _, savf_c, savb_c, st_c = _gla_fwd(q_c, k_c, zc_al, AV // DV, la_c, zero_state, H, DK, DV, "gla_fwd_ctx")
    o_f, o_b, savf_x, savb_x, _, wo_g, w1_buf = _gla_fwd(
        q_r, k_r, z_al, AV // DV, la_x, st_c, H, DK, DV, "gla_fwd_x",
        side=[_ph(wo_buf, [("gather_chain", 0, 4)]), _ph(w1_buf, [("gather_d2d", 0, 2), ("gather_ici", 2, 4)])], side_mid=0.55)
    w_o_f = wo_g.reshape(D, D)

    def readout_fwd(of, ob, r, g):
        return (_readout(of + ob, r, g, H, DV),)

    y_gla = _rows(readout_fwd, N, TR, [_T(o_f, VALW), _T(o_b, VALW), _T(z_al, VALW, AR // VALW), _W(gn_full)],
                  [(VALW, BF16)], [], "gla_readout")[0]

    bs_col = b_s[0].reshape(SG_GROUPS, SG_CHUNK, 1)

    def sg_fwd(zu, zv, lng, lnb, ws, bs):
        u, vv = _sg_pre(zu, zv, lng, lnb)
        return (u * _sg_mix(vv, ws, bs, GW),)

    y_sg = _rows(sg_fwd, N, SG_CHUNK, [_T(z_al, SGW, 0), _T(z_al, SGW, 1), _W(sg_ln_g), _W(sg_ln_b), _W(w_s[0]), _W(bs_col)],
                 [(SGW, BF16)], [], "sg_fwd")[0]
    ycat = jnp.concatenate([y_gla, y_sg], axis=1)
    y, w1_g, w2_buf = _mm(ycat, w_o_f, N, D, D, name="out_proj",
                          side=[_ph(w1_buf, [("gather_d2d", 2, 4)]), _ph(w2_buf, [("gather_ici", 0, 1)])])
    x1 = _rows(lambda xv, yv, gv, pv: (_postnorm(xv, yv, gv, pv),), N, TR,
               [_T(x2, D), _T(y, D), _W(gt1), _W(post1_g)], [(D, F32)], [], "postnorm1")[0]
    h2 = prenorm_fwd(x1, pre2_g, sc2, sh2, N, "prenorm2")

    tm1, tn1, tk1 = min(1024, N), min(1024, FS), min(2048, RH)
    w1_fwd_spec = pl.BlockSpec((None, tk1, tn1), lambda i, j, k: (2 * ((j * tn1) // FS) + (k * tk1) // RH, ((k * tk1) % RH) // tk1, ((j * tn1) % FS) // tn1))

    def relu2_epi(r):
        rf = jnp.maximum(r, 0.0)
        return rf * rf, rf

    act, rf, w2_g = _mm(h2, w1_g, N, F, D, name="mlp_up", out_dtypes=(BF16, BF16), tm=tm1, tn=tn1, tk=tk1, b_spec=w1_fwd_spec, epi=relu2_epi,
                        side=[_ph(w2_buf, [("gather_d2d", 0, 1), ("gather_chain", 1, 4)])], side_mid=0.88)
    w_2_f = w2_g.reshape(F, D)
    y2 = _mm(act, w_2_f, N, D, F, name="mlp_down")

    def final(x1v, y2v, gv, pv, tv):
        def loss_fn(x1a, y2a, ga, pa):
            err = _postnorm(x1a, y2a, ga, pa) - tv
            return 0.5 * jnp.sum(jnp.mean(err * err, axis=-1))
        loss, grads = jax.value_and_grad(loss_fn, argnums=(0, 1, 2, 3))(x1v, y2v, gv, pv)
        return grads[0], grads[1], jnp.full((1, LANES), loss, F32), _colsum(grads[2]), _colsum(grads[3])

    dx2, dy2, loss_acc, dgt2, dpost2 = _rows(final, N, TR, [_T(x1, D), _T(y2, D), _W(gt2), _W(post2_g), _T(tgt, D)],
                                             [(D, F32), (D, BF16)], [(1, LANES), (1, D), (1, D)], "loss_postnorm2_bwd")

    df = _mm(dy2, w_2_f, N, F, D, name="mlp_down_dx", tb=True, out_dtypes=(BF16,), epi=lambda r, rfv: (r * (2.0 * rfv.astype(F32)),),
             epi_in=(rf,), epi_specs=(pl.BlockSpec((min(1024, N), min(1024, F)), lambda i, j, k: (i, j)),))
    dw2 = _mm(act, dy2, F, D, N, name="mlp_down_dw", ta=True, out_dtypes=(BF16,)).reshape(N_DEV, F // N_DEV, D)
    tnb, tkb = min(1024, D, RH), min(2048, FS)
    w1_bwd_spec = pl.BlockSpec((None, tnb, tkb), lambda i, j, k: (2 * ((k * tkb) // FS) + (j * tnb) // RH, ((j * tnb) % RH) // tnb, ((k * tkb) % FS) // tkb))
    dh2, recv1_w2 = _mm(df, w1_g, N, D, F, name="mlp_up_dx", tb=True, tn=tnb, tk=tkb, b_spec=w1_bwd_spec,
                         side=[_ph(lax.empty((N_CHIP,) + dw2.shape[1:], BF16), [("rs_sibling", 0, 4)], src=dw2)])
    part_w2 = _sum_sibling(dw2, recv1_w2, c_arr, "rs_sum_sibling_w_2")
    tmw, tnw = min(1024, RH), min(1024, FS)
    dw1_spec = pl.BlockSpec((None, tmw, tnw), lambda i, j, k: (2 * ((j * tnw) // FS) + (i * tmw) // RH, ((i * tmw) % RH) // tmw, ((j * tnw) % FS) // tnw))
    dw1, recv2_w2 = _mm(h2, df, D, F, N, name="mlp_up_dw", ta=True, tm=tmw, tn=tnw, out_dtypes=(BF16,), out_specs=[dw1_spec],
                        out_shapes=[jax.ShapeDtypeStruct((N_DEV, RH, FS), BF16)],
                        side=[_ph(lax.empty((3,) + part_w2.shape[1:], BF16), [("rs_chips", 0, 3)], src=part_w2)])

    def prenorm_bwd(xv, gv, scv, shv, dh, dres):
        _, vjp = jax.vjp(_prenorm, xv, gv, scv, shv)
        dx, dg, dsc, dsh = vjp(dh)
        return dx + dres, _colsum(dg), _colsum(dsc), _colsum(dsh)

    dx1, dpre2, dsc2, dsh2 = _rows(prenorm_bwd, N, TR, [_T(x1, D), _W(pre2_g), _W(sc2), _W(sh2), _T(dh2, D), _T(dx2, D)],
                                   [(D, F32)], [(1, D)] * 3, "prenorm2_bwd")

    def postnorm_bwd(yv, gv, pv, dxv):
        _, vjp = jax.vjp(lambda ya, ga, pa: _postnorm(0.0, ya, ga, pa), yv, gv, pv)
        dy_, dg_, dp_ = vjp(dxv)
        return dy_, _colsum(dg_), _colsum(dp_)

    dy, dgt1, dpost1 = _rows(postnorm_bwd, N, TR, [_T(y, D), _W(gt1), _W(post1_g), _T(dx1, D)], [(D, BF16)], [(1, D)] * 2, "postnorm1_bwd")
    dycat, recv2_w2, recv1_w1 = _mm(dy, w_o_f, N, D, D, name="out_proj_dx", tb=True,
                                    side=[_ph(recv2_w2, [("rs_chips", 3, 4)], src=part_w2),
                                          _ph(lax.empty((N_CHIP,) + dw1.shape[1:], BF16), [("rs_sibling", 0, 4)], src=dw1)])
    part_w1 = _sum_sibling(dw1, recv1_w1, c_arr, "rs_sum_sibling_w_1")
    dwo = _mm(ycat, dy, D, D, N, name="out_proj_dw", ta=True, out_dtypes=(BF16,)).reshape(N_DEV, D // N_DEV, D)

    def readout_bwd(of, ob, r, g, dyv):
        _, vjp = jax.vjp(lambda o_, r_, g_: _readout(o_, r_, g_, H, DV), of + ob, r, g)
        do_, dr_, dg_ = vjp(dyv)
        return do_, dr_, _colsum(dg_)

    do_x, dz_r, dgn = _rows(readout_bwd, N, TR, [_T(o_f, VALW), _T(o_b, VALW), _T(z_al, VALW, AR // VALW), _W(gn_full), _T(dycat, VALW, 0)],
                            [(VALW, F32), (VALW, BF16)], [(1, VALW)], "gla_readout_bwd")

    def sg_bwd(zu, zv, lng, lnb, ws, bs, dyv):
        (u, vv), vjp = jax.vjp(_sg_pre, zu, zv, lng, lnb)
        s = _sg_mix(vv, ws, bs, GW)
        du, ds = dyv * s, dyv * u
        dws, dbs, dvv = [], [], []
        for g in range(SG_GROUPS):
            dsg = ds[:, g * GW:(g + 1) * GW]
            dsb = dsg.astype(BF16)
            dws.append(_dg(dsb, vv[:, g * GW:(g + 1) * GW].astype(BF16), 1, 1))
            dbs.append(jnp.sum(dsg, axis=1, keepdims=True))
            dvv.append(_dg(ws[g].astype(BF16), dsb, 0, 0))
        dzu, dzv, dlng, dlnb = vjp((du, jnp.concatenate(dvv, axis=1)))
        return jnp.concatenate([dzu, dzv], axis=1), _colsum(dlng), _colsum(dlnb), jnp.concatenate(dws, axis=0), jnp.concatenate(dbs, axis=0)

    dz_sg, dlng, dlnb, dws, dbs = _rows(sg_bwd, N, SG_CHUNK, [_T(z_al, SGW, 0), _T(z_al, SGW, 1), _W(sg_ln_g), _W(sg_ln_b), _W(w_s[0]), _W(bs_col), _T(dycat, SGW, VALW // SGW)],
                                        [(2 * SGW, BF16)], [(1, SGW), (1, SGW), (SG_GROUPS * SG_CHUNK, SG_CHUNK), (SG_GROUPS * SG_CHUNK, 1)], "sg_bwd")

    dq_f, dq_b, dk_f, dk_b, dv_f, dv_b, dla_f, dla_b, dst0, recv2_w1, recv1_wo = _gla_bwd(
        q_r, k_r, z_al, AV // DV, la_x, savf_x, savb_x, do_x, zero_state, H, DK, DV, "gla_bwd_x",
        side=[_ph(lax.empty((3,) + part_w1.shape[1:], BF16), [("rs_chips", 0, 3)], src=part_w1),
              _ph(lax.empty((N_CHIP,) + dwo.shape[1:], BF16), [("rs_sibling", 0, 4)], src=dwo)])
    part_wo = _sum_sibling(dwo, recv1_wo, c_arr, "rs_sum_sibling_w_o")
    _, _, dkc_f, dkc_b, dvc_f, dvc_b, dlac_f, dlac_b, _ = _gla_bwd(
        q_c, k_c, zc_al, AV // DV, la_c, savf_c, savb_c, jnp.zeros((NC, VALW), F32), dst0, H, DK, DV, "gla_bwd_ctx")

    def decay_bwd(lr, dla_f_v, dla_b_v, wdf_v, wdb_v, bf_v, bb_v):
        a_f, a_b = decay(lr, wdf_v, wdb_v, bf_v, bb_v)
        da_f = _chunk_cumsum(dla_f_v, False) * jax.nn.sigmoid(-a_f) / GLA_TAU
        da_b = _chunk_cumsum(dla_b_v, True) * jax.nn.sigmoid(-a_b) / GLA_TAU
        lrb, dfb, dbb = lr.astype(BF16), da_f.astype(BF16), da_b.astype(BF16)
        dlr = _dg(dfb, wdf_v.astype(BF16), 1, 1) + _dg(dbb, wdb_v.astype(BF16), 1, 1)
        return dlr, _dg(lrb, dfb, 0, 0), _dg(lrb, dbb, 0, 0), _colsum(da_f), _colsum(da_b)

    def prep_x_bwd(dq0, dq1, dk0, dk1, dv0, dv1, lr, dla0, dla1, cs, sn, wdf_v, wdb_v, bf_v, bb_v):
        dlr, dwf, dwb, dbf, dbb = decay_bwd(lr, dla0, dla1, wdf_v, wdb_v, bf_v, bb_v)
        return (_rope_t(dq0 + dq1, cs, sn, H, DK) * qscale, _rope_t(dk0 + dk1, cs, sn, H, DK), dv0 + dv1, dlr, dwf, dwb, dbf, dbb)

    def prep_c_bwd(dk0, dk1, dv0, dv1, lr, dla0, dla1, wdf_v, wdb_v, bf_v, bb_v):
        dlr, dwf, dwb, dbf, dbb = decay_bwd(lr, dla0, dla1, wdf_v, wdb_v, bf_v, bb_v)
        return dk0 + dk1, dv0 + dv1, dlr, dwf, dwb, dbf, dbb

    dec_acc = [(LANES, KEYW), (LANES, KEYW), (1, KEYW), (1, KEYW)]
    dz_q, dz_k, dz_v, dz_lr, dwdf_x, dwdb_x, dbdf_x, dbdb_x = _rows(
        prep_x_bwd, N, TR, [_T(dq_f, KEYW), _T(dq_b, KEYW), _T(dk_f, KEYW), _T(dk_b, KEYW), _T(dv_f, VALW), _T(dv_b, VALW),
                            _T(z_al, LANES, ALR // LANES), _T(dla_f, KEYW), _T(dla_b, KEYW), _T(cos_t, DK), _T(sin_t, DK)] + dec_w,
        [(KEYW, BF16), (KEYW, BF16), (VALW, BF16), (LANES, BF16)], dec_acc, "gla_prep_x_bwd")
    dzc_k, dzc_v, dzc_lr, dwdf_c, dwdb_c, dbdf_c, dbdb_c = _rows(
        prep_c_bwd, NC, TR, [_T(dkc_f, KEYW), _T(dkc_b, KEYW), _T(dvc_f, VALW), _T(dvc_b, VALW),
                             _T(zc_al, LANES, ALR // LANES), _T(dlac_f, KEYW), _T(dlac_b, KEYW)] + dec_w,
        [(KEYW, BF16), (VALW, BF16), (LANES, BF16)], dec_acc, "gla_prep_ctx_bwd")

    dz_al = jnp.concatenate([dz_sg, dz_q, dz_k, dz_v, dz_r, dz_lr, jnp.zeros((N, LR_PAD - LANES), BF16)], axis=1)
    dzc_al = jnp.concatenate([jnp.zeros((NC, 2 * SGW + KEYW), BF16), dzc_k, dzc_v, jnp.zeros((NC, VALW), BF16), dzc_lr,
                              jnp.zeros((NC, LR_PAD - LANES), BF16)], axis=1)
    tkd = _pick(ACOLS, 3584, 2 * LANES)
    h_cat = jnp.concatenate([hx, hc], axis=0)
    dz_cat = jnp.concatenate([dz_al, dzc_al], axis=0)
    tkt = _pick(N + NC, 2304)
    dw_al, recv2_w1, recv2_wo = _mm(h_cat, dz_cat, D, ACOLS, N + NC, name="in_proj_dw", ta=True, tn=tka, tk=tkt, out_dtypes=(BF16,),
                                    side=[_ph(recv2_w1, [("rs_chips", 3, 4)], src=part_w1),
                                          _ph(lax.empty((3,) + part_wo.shape[1:], BF16), [("rs_chips", 0, 4)], src=part_wo)])
    g_in = _shard_columns(dw_al, CS, SG0, 2 * SGW, -SG0, RH, "w_in_grad_blocks")
    recv1_in = _rs_sibling([g_in], "rs_sibling_w_in")[0]
    part_in = _sum_sibling(g_in, recv1_in, c_arr, "rs_sum_sibling_w_in")
    dhx, recv2_in = _mm(dz_al, w_al, N, D, ACOLS, name="in_proj_dx", tb=True, tk=tkd,
                        side=[_ph(lax.empty((3,) + part_in.shape[1:], BF16), [("rs_chips", 0, 3)], src=part_in)])
    dhc = _mm(dzc_al, w_al, NC, D, ACOLS, name="in_proj_dctx", tb=True, tk=tkd)

    grad_x, dpre1_x, dsc1, dsh1 = _rows(prenorm_bwd, N, TR, [_T(x2, D), _W(pre1_g), _W(sc1), _W(sh1), _T(dhx, D), _T(dx1, D)],
                                        [(D, F32)], [(1, D)] * 3, "prenorm1_x_bwd")

    def prenorm_bwd_ctx(xv, gv, scv, shv, dh):
        _, vjp = jax.vjp(_prenorm, xv, gv, scv, shv)
        _, dg, dsc, dsh = vjp(dh)
        return _colsum(dg), _colsum(dsc), _colsum(dsh)

    dpre1_c, dcsc1, dcsh1 = _rows(prenorm_bwd_ctx, NC, TR, [_T(ctx2, D), _W(pre1_g), _W(csc1), _W(csh1), _T(dhc, D)],
                                  [], [(1, D)] * 3, "prenorm1_ctx_bwd")

    half = [_sum_chips(p, r, sc_arr, "rs_sum_chips_" + nm)
            for p, r, nm in zip((part_wo, part_w1, part_w2), (recv2_wo, recv2_w1, recv2_w2), ("w_o", "w_1", "w_2"))]
    g_w_o, g_w_1, g_w_2 = [g.reshape(w.shape[1:]) for g, w in zip(_rs_final(half, "rs_final"), (w_o, w_1, w_2))]

    dmod_x = jnp.concatenate([dsh1, dsc1, dgt1, dsh2, dsc2, dgt2], axis=1)
    dmodc = jnp.concatenate([dcsh1, dcsc1], axis=1)
    small_parts = [loss_acc, dmod_x, dmodc, dpre1_x + dpre1_c, dpost1, dpre2, dpost2, dwdf_x + dwdf_c, dbdf_x + dbdf_c,
                   dwdb_x + dwdb_c, dbdb_x + dbdb_c, dgn, dlng, dlnb, dws, dbs]
    small_shapes = [p.shape for p in small_parts]
    packed = _pack(small_parts)
    n_sm = packed.shape[1]
    gathered = _allgather_small(packed, "gather_small_grads")

    def sum_devices(g):
        tot = g[0:8]
        for dev in range(1, N_DEV):
            tot = tot + g[8 * dev:8 * dev + 8]
        return (tot,)

    summed = _rows(sum_devices, N_DEV * 8, N_DEV * 8, [_W(gathered)], [], [(8, n_sm)], "sum_small_grads")[0]
    (loss_s, dmod_sum, dmodc_sum, g_pre1, g_post1, g_pre2, g_post2, g_wdf_pad, g_bdf, g_wdb_pad, g_bdb, g_gn, g_lng, g_lnb,
     g_ws, g_bs) = _unpack(summed.reshape(-1), small_shapes)
    loss = loss_s[0, 0]
    dmod_rows = gathered.reshape(N_DEV, -1)[:, LANES:LANES + N_MOD * D]
    g_b_ada = dmod_sum + jnp.pad(dmodc_sum, ((0, 0), (0, (N_MOD - 2) * D)))
    dmod16 = jnp.zeros((16, N_MOD * D), F32).at[:N_DEV].set(dmod_rows).at[N_DEV, :2 * D].set(dmodc_sum[0])
    dmod16_sh = lax.dynamic_slice(dmod16, (0, s_me * MS), (16, MS))
    g_w_ada, d_w_ada, nm_w_ada, nv_w_ada = _ada_update(cond.T, dmod16_sh, w_ada[0], m_w_ada[0], v_w_ada[0], "w_ada_update")

    dcond, recv2_in = _mm(dmod16_sh, w_ada[0], 16, D, MS, name="cond_bwd", tb=True, tk=min(512, MS),
                          side=[_ph(recv2_in, [("rs_chips", 3, 4)], src=part_in)])
    half_in = _sum_chips(part_in, recv2_in, sc_arr, "rs_sum_chips_w_in")
    g_w_in = _rs_final([half_in], "rs_final_w_in")[0].reshape(D, -1)[:, :CS]
    part_c = _allgather_small(dcond[N_DEV].reshape(8, D // 8), "gather_dcond").reshape(N_DEV, D)

    def cctx_grad(p, cv):
        sg = jax.nn.sigmoid(cv)
        tot = ((p[0:1] + p[2:3]) + p[4:5]) + p[6:7]
        return (jnp.broadcast_to(tot * (sg * (1.0 + cv * (1.0 - sg))), p.shape),)

    g_c_ctx = _rows(cctx_grad, N_DEV, N_DEV, [_W(part_c), _W(c_ctx.reshape(1, D))], [(D, F32)], [], "c_ctx_grad")[0][0:1]

    def col_shard(g_full, width):
        return lax.dynamic_slice_in_dim(g_full, s_me * width, width, axis=g_full.ndim - 1)

    g_w_dec_f = col_shard(g_wdf_pad[:GLA_LOWRANK], KEYW // N_CHIP)
    g_w_dec_b = col_shard(g_wdb_pad[GLA_LOWRANK:LR], KEYW // N_CHIP)
    g_gla_norm = col_shard(g_gn.reshape(H, DV), DV // N_CHIP)
    small_w = [c_ctx, b_ada, pre1_g, post1_g, pre2_g, post2_g, w_dec_f, b_dec_f, w_dec_b, b_dec_b, gla_norm_g, sg_ln_g, sg_ln_b, w_s, b_s]
    small_m = [m_c_ctx, m_b_ada, m_pre1_g, m_post1_g, m_pre2_g, m_post2_g, m_w_dec_f, m_b_dec_f, m_w_dec_b, m_b_dec_b, m_gla_norm_g, m_sg_ln_g, m_sg_ln_b, m_w_s, m_b_s]
    small_v = [v_c_ctx, v_b_ada, v_pre1_g, v_post1_g, v_pre2_g, v_post2_g, v_w_dec_f, v_b_dec_f, v_w_dec_b, v_b_dec_b, v_gla_norm_g, v_sg_ln_g, v_sg_ln_b, v_w_s, v_b_s]
    small_g = [g_c_ctx, g_b_ada, g_pre1, g_post1, g_pre2, g_post2, g_w_dec_f, g_bdf, g_w_dec_b, g_bdb, g_gla_norm, g_lng, g_lnb, g_ws, g_bs]
    small_g = [g.reshape(w.shape) for g, w in zip(small_g, small_w)]
    d_small, m_small, v_small = _adamw_many(small_w, small_g, small_m, small_v, "adamw_small")

    def big(w, g, m, v, name):
        shp = w.shape
        res = _adamw(w.reshape(shp[-2:]), g.reshape(shp[-2:]), m.reshape(shp[-2:]), v.reshape(shp[-2:]), name)
        return [g.reshape(shp)] + [r.reshape(shp) for r in res]

    r_in = big(w_in, g_w_in, m_w_in, v_w_in, "adamw_w_in")
    r_o = big(w_o, g_w_o, m_w_o, v_w_o, "adamw_w_o")
    r_1 = big(w_1, g_w_1, m_w_1, v_w_1, "adamw_w_1")
    r_2 = big(w_2, g_w_2, m_w_2, v_w_2, "adamw_w_2")
    r_ada = [t.reshape(w_ada.shape) for t in (g_w_ada, d_w_ada, nm_w_ada, nv_w_ada)]

    def ordered(k):
        sm = [small_g, d_small, m_small, v_small][k]
        return [sm[0], r_ada[k], *sm[1:6], r_in[k], *sm[6:15], r_o[k], r_1[k], r_2[k]]

    return (loss, grad_x.reshape(x.shape), *ordered(0), *ordered(1), *ordered(2), *ordered(3))
```

```python
import functools
import math

import numpy as np
import jax
import jax.numpy as jnp
from jax import lax
from jax.experimental import pallas as pl
from jax.experimental.pallas import tpu as pltpu

F32 = jnp.float32
BF16 = jnp.bfloat16
MESH = pl.DeviceIdType.MESH
ANY = pl.BlockSpec(memory_space=pl.ANY)

GLA_HEADS = 8
GLA_CHUNK = 64
GLA_LOWRANK = 16
GLA_TAU = 16.0
ROPE_BASE = 10000.0
GRID_W = 64
SG_GROUPS = 4
SG_CHUNK = 128
N_MOD = 6
EPS = 1e-6
ADAM_LR = 0.001
ADAM_B1 = 0.9
ADAM_B2 = 0.999
ADAM_EPS = 1e-08
ADAM_WD = 0.01
ADAM_STEP = 10

LANES = 128
VMEM_LIMIT = 56 << 20
N_DEV = 8
N_CHIP = 4


def _params(sem=None):
    return pltpu.CompilerParams(dimension_semantics=sem, vmem_limit_bytes=VMEM_LIMIT)


def _pick(dim, target, unit=LANES):
    best = None
    for t in range(unit, min(dim, target) + 1, unit):
        if dim % t == 0:
            best = t
    return dim if best is None else best


def _dg(a, b, ca, cb, precision=None):
    return lax.dot_general(a, b, (((ca,), (cb,)), ((), ())), preferred_element_type=F32,
                           precision=precision)


def _place():
    return lax.axis_index("x"), lax.axis_index("y"), lax.axis_index("c")


def _allgather_small(v, name):
    m_per, n = v.shape

    def body(x_ref, out_ref, send_sems, recv_sems, local_sem):
        x, y, c = _place()
        me, sibling = (x, y, c), (x, y, 1 - c)
        chips = [(1 - x, y), (x, 1 - y), (1 - x, 1 - y)]

        def rows(px, py, pc):
            return out_ref.at[pl.ds((4 * px + 2 * py + pc) * m_per, m_per), :]

        def copy(k, block, to, src=None):
            return pltpu.make_async_remote_copy(
                src_ref=rows(*block) if src is None else src, dst_ref=rows(*block),
                send_sem=send_sems.at[k], recv_sem=recv_sems.at[k],
                device_id=to, device_id_type=MESH)

        mine = pltpu.make_async_copy(x_ref, rows(*me), local_sem)
        mine.start()
        first = [copy(0, me, sibling, src=x_ref)]
        first += [copy(1 + j, me, (*chip, c), src=x_ref) for j, chip in enumerate(chips)]
        for cp in first:
            cp.start()
        passed = [copy(4 + j, (*chip, c), sibling) for j, chip in enumerate(chips)]
        for j, chip in enumerate(chips):
            copy(1 + j, (*chip, c), me).wait_recv()
            passed[j].start()
        copy(0, sibling, me).wait_recv()
        for j, chip in enumerate(chips):
            copy(4 + j, (*chip, 1 - c), me).wait_recv()
        for cp in first + passed:
            cp.wait_send()
        mine.wait()

    return pl.pallas_call(
        body, name=name,
        out_shape=jax.ShapeDtypeStruct((N_DEV * m_per, n), v.dtype),
        in_specs=[pl.BlockSpec(memory_space=pltpu.VMEM)],
        out_specs=pl.BlockSpec(memory_space=pltpu.VMEM),
        scratch_shapes=[pltpu.SemaphoreType.DMA((7,)), pltpu.SemaphoreType.DMA((7,)),
                        pltpu.SemaphoreType.DMA],
        compiler_params=pltpu.CompilerParams(vmem_limit_bytes=VMEM_LIMIT),
    )(v)


def _cast_blocks(w, s_me, name):
    _, r, cols = w.shape
    tr = _row_tile(r, cols, 4)

    def body(s_ref, w_ref, o_ref):
        o_ref[...] = w_ref[...].astype(BF16)

    return pl.pallas_call(
        body, name=name,
        out_shape=jax.ShapeDtypeStruct((N_DEV, r, cols), BF16),
        grid_spec=pltpu.PrefetchScalarGridSpec(
            num_scalar_prefetch=1, grid=(2, r // tr),
            in_specs=[pl.BlockSpec((None, tr, cols), lambda h, i, s: (h, i, 0))],
            out_specs=pl.BlockSpec((None, tr, cols), lambda h, i, s: (2 * s[0] + h, i, 0))),
        compiler_params=_params(("arbitrary", "arbitrary")),
    )(s_me, w)


def _gather_big(ws, name):
    nw = len(ws)

    def body(*refs):
        outs = refs[nw:2 * nw]
        send_sems, recv_sems = refs[2 * nw:]
        x, y, c = _place()
        me, sibling = (x, y, c), (x, y, 1 - c)
        chips = [(1 - x, y), (x, 1 - y), (1 - x, 1 - y)]

        def blk(px, py, pc):
            return 4 * px + 2 * py + pc

        def copy(w, k, block, to):
            return pltpu.make_async_remote_copy(
                src_ref=outs[w].at[block], dst_ref=outs[w].at[block],
                send_sem=send_sems.at[6 * w + k], recv_sem=recv_sems.at[6 * w + k],
                device_id=to, device_id_type=MESH)

        first = []
        for w in range(nw):
            for j, chip in enumerate(chips):
                cp = copy(w, j, blk(x, y, c), (*chip, c))
                cp.start()
                first.append(cp)
        passed = []
        for w in range(nw):
            for j, chip in enumerate(chips):
                copy(w, j, blk(*chip, c), me).wait_recv()
                cp = copy(w, 3 + j, blk(*chip, c), sibling)
                cp.start()
                passed.append(cp)
        for w in range(nw):
            for j, chip in enumerate(chips):
                copy(w, 3 + j, blk(*chip, 1 - c), me).wait_recv()
        for cp in first + passed:
            cp.wait_send()

    return pl.pallas_call(
        body, name=name,
        out_shape=[jax.ShapeDtypeStruct(w.shape, w.dtype) for w in ws],
        in_specs=[ANY] * nw, out_specs=[ANY] * nw,
        input_output_aliases={w: w for w in range(nw)},
        scratch_shapes=[pltpu.SemaphoreType.DMA((6 * nw,)), pltpu.SemaphoreType.DMA((6 * nw,))],
    )(*ws)


def _rs_sibling(gs, name):
    nw = len(gs)

    def body(*refs):
        ins, outs = refs[:nw], refs[nw:2 * nw]
        send_sems, recv_sems = refs[2 * nw:]
        x, y, c = _place()
        cps = []
        for w in range(nw):
            for s in range(N_CHIP):
                cp = pltpu.make_async_remote_copy(
                    src_ref=ins[w].at[2 * s + (1 - c)], dst_ref=outs[w].at[s],
                    send_sem=send_sems.at[N_CHIP * w + s], recv_sem=recv_sems.at[N_CHIP * w + s],
                    device_id=(x, y, 1 - c), device_id_type=MESH)
                cp.start()
                cps.append(cp)
        for cp in cps:
            cp.wait()

    return pl.pallas_call(
        body, name=name,
        out_shape=[jax.ShapeDtypeStruct((N_CHIP,) + g.shape[1:], g.dtype) for g in gs],
        in_specs=[ANY] * nw, out_specs=[ANY] * nw,
        scratch_shapes=[pltpu.SemaphoreType.DMA((N_CHIP * nw,)), pltpu.SemaphoreType.DMA((N_CHIP * nw,))],
    )(*gs)


def _rs_final(fs, name):
    nw = len(fs)

    def body(*refs):
        outs = refs[nw:2 * nw]
        send_sems, recv_sems = refs[2 * nw:]
        x, y, c = _place()
        cps = []
        for w in range(nw):
            cp = pltpu.make_async_remote_copy(
                src_ref=outs[w].at[c], dst_ref=outs[w].at[c],
                send_sem=send_sems.at[w], recv_sem=recv_sems.at[w],
                device_id=(x, y, 1 - c), device_id_type=MESH)
            cp.start()
            cps.append(cp)
        for cp in cps:
            cp.wait()

    return pl.pallas_call(
        body, name=name,
        out_shape=[jax.ShapeDtypeStruct(f.shape, f.dtype) for f in fs],
        in_specs=[ANY] * nw, out_specs=[ANY] * nw,
        input_output_aliases={w: w for w in range(nw)},
        scratch_shapes=[pltpu.SemaphoreType.DMA((nw,)), pltpu.SemaphoreType.DMA((nw,))],
    )(*fs)


_PHASE_COPIES = {"gather_ici": 3, "gather_d2d": 3, "gather_chain": 6, "rs_sibling": N_CHIP, "rs_chips": 3}
QUARTERS = 4


def _ph(buf, legs, src=None):
    return dict(buf=buf, src=src, legs=legs)


def _n_copies(ph):
    return sum(_PHASE_COPIES[kind] for kind, _, _ in ph["legs"])


def _phase_copies(ph, src, buf, send_sems, recv_sems, base):
    x, y, c = _place()
    sibling = (x, y, 1 - c)
    chips = [(1 - x, y), (x, 1 - y), (1 - x, 1 - y)]
    r = buf.shape[1]

    def make(k, trip):
        a, b, dev = trip
        return pltpu.make_async_remote_copy(src_ref=a, dst_ref=b, send_sem=send_sems.at[base + k], recv_sem=recv_sems.at[base + k],
                                            device_id=dev, device_id_type=MESH)

    out = []
    for kind, lo, hi in ph["legs"]:
        rows = pl.ds(lo * r // QUARTERS, (hi - lo) * r // QUARTERS)
        ici = [(buf.at[4 * x + 2 * y + c, rows], buf.at[4 * x + 2 * y + c, rows], (*chip, c)) for chip in chips]
        d2d = [(buf.at[4 * chip[0] + 2 * chip[1] + c, rows], buf.at[4 * chip[0] + 2 * chip[1] + c, rows], sibling) for chip in chips]
        if kind == "gather_ici":
            trips, later = ici, []
        elif kind == "gather_d2d":
            trips, later = d2d, []
        elif kind == "gather_chain":
            trips, later = ici, d2d
        elif kind == "rs_sibling":
            trips, later = [(src.at[2 * s + (1 - c), rows], buf.at[s, rows], sibling) for s in range(N_CHIP)], []
        else:
            trips, later = [(src.at[2 * chip[0] + chip[1], rows], buf.at[j, rows], (*chip, c)) for j, chip in enumerate(chips)], []
        out.append(([make(k, t) for k, t in enumerate(trips)], [make(len(trips) + k, t) for k, t in enumerate(later)]))
        base += _PHASE_COPIES[kind]
    return out


def _side_call(inner, grid, in_specs, out_specs, out_shape, scratch, args, phases, name, semantics, mid=0.8):
    n_in, n_out, n_ph = len(in_specs), len(out_specs), len(phases)
    if n_ph == 0:
        outs = pl.pallas_call(inner, name=name, grid=grid, in_specs=in_specs, out_specs=out_specs, out_shape=out_shape,
                              scratch_shapes=scratch, compiler_params=_params(semantics))(*args)
        return list(outs), []
    n_cp = sum(_n_copies(p) for p in phases)
    side_args, buf_pos, src_pos = [], [], []
    for p in phases:
        buf_pos.append(len(side_args))
        side_args.append(p["buf"])
        src_pos.append(len(side_args) if p["src"] is not None else None)
        if p["src"] is not None:
            side_args.append(p["src"])
    n_side = len(side_args)
    total = math.prod(grid)
    mid_lin = min(total - 1, int(total * mid))

    def body(*refs):
        b_in, s_in = refs[:n_in], refs[n_in:n_in + n_side]
        b_out, s_out = refs[n_in + n_side:n_in + n_side + n_out], refs[n_in + n_side + n_out:n_in + n_side + n_out + n_ph]
        rest = refs[n_in + n_side + n_out + n_ph:]
        send_sems, recv_sems = rest[-2:]
        lin = functools.reduce(lambda acc, ag: acc * ag[1] + pl.program_id(ag[0]), list(enumerate(grid))[1:], pl.program_id(0))

        def copies():
            out, base = [], 0
            for p, sp, so in zip(phases, src_pos, s_out):
                out += _phase_copies(p, None if sp is None else s_in[sp], so, send_sems, recv_sems, base)
                base += _n_copies(p)
            return out

        @pl.when(lin == 0)
        def _():
            for a, _ in copies():
                for cp in a:
                    cp.start()

        if any(kind == "gather_chain" for p in phases for kind, _, _ in p["legs"]):
            @pl.when(lin == mid_lin)
            def _():
                for a, b in copies():
                    if b:
                        for cp in a:
                            cp.wait()
                        for cp in b:
                            cp.start()

        inner(*b_in, *b_out, *rest[:-2])

        @pl.when(lin == total - 1)
        def _():
            for a, b in copies():
                for cp in (b if b else a):
                    cp.wait()

    outs = pl.pallas_call(
        body, name=name, grid=grid,
        in_specs=list(in_specs) + [ANY] * n_side, out_specs=list(out_specs) + [ANY] * n_ph,
        out_shape=list(out_shape) + [jax.ShapeDtypeStruct(p["buf"].shape, p["buf"].dtype) for p in phases],
        input_output_aliases={n_in + bp: n_out + k for k, bp in enumerate(buf_pos)},
        scratch_shapes=list(scratch) + [pltpu.SemaphoreType.DMA((n_cp,)), pltpu.SemaphoreType.DMA((n_cp,))],
        compiler_params=_params(("arbitrary",) * len(grid)),
    )(*args, *side_args)
    return list(outs[:n_out]), list(outs[n_out:])


def _row_tile(r, cols, itemsize):
    if r * cols * itemsize <= (2 << 20):
        return r
    fits = [t for t in range(8, r, 8) if r % t == 0 and t * cols * itemsize <= (2 << 20)]
    return max(fits) if fits else r


def _transpose_cols(g, n_cols, name):
    rows, wp = g.shape
    tr = min(rows, 512)

    def body(g_ref, o_ref):
        o_ref[...] = g_ref[...].T[:n_cols]

    return pl.pallas_call(
        body, name=name, grid=(rows // tr,), out_shape=jax.ShapeDtypeStruct((n_cols, rows), F32),
        in_specs=[pl.BlockSpec((tr, wp), lambda i: (i, 0))], out_specs=pl.BlockSpec((n_cols, tr), lambda i: (0, i)),
        compiler_params=_params(("parallel",)),
    )(g)


def _sum_sibling(g, r1, c_me, name):
    _, r, cols = g.shape
    tr = _row_tile(r, cols, 4)

    def body(c_ref, g_ref, r_ref, o_ref):
        o_ref[...] = (g_ref[...].astype(F32) + r_ref[...].astype(F32)).astype(o_ref.dtype)

    return pl.pallas_call(
        body, name=name,
        out_shape=jax.ShapeDtypeStruct((N_CHIP, r, cols), g.dtype),
        grid_spec=pltpu.PrefetchScalarGridSpec(
            num_scalar_prefetch=1, grid=(N_CHIP, r // tr),
            in_specs=[pl.BlockSpec((None, tr, cols), lambda s, i, c: (2 * s + c[0], i, 0)),
                      pl.BlockSpec((None, tr, cols), lambda s, i, c: (s, i, 0))],
            out_specs=pl.BlockSpec((None, tr, cols), lambda s, i, c: (s, i, 0))),
        compiler_params=_params(("arbitrary", "arbitrary")),
    )(c_me, g, r1)


def _sum_chips(p, r2, sc_me, name):
    _, r, cols = p.shape
    tr = _row_tile(r, cols, 4)

    def body(s_ref, p_ref, a_ref, b_ref, c_ref, o_ref):
        o_ref[...] = ((p_ref[...].astype(F32) + a_ref[...].astype(F32)) + b_ref[...].astype(F32)) + c_ref[...].astype(F32)

    return pl.pallas_call(
        body, name=name,
        out_shape=jax.ShapeDtypeStruct((2, r, cols), F32),
        grid_spec=pltpu.PrefetchScalarGridSpec(
            num_scalar_prefetch=1, grid=(r // tr,),
            in_specs=[pl.BlockSpec((None, tr, cols), lambda i, s: (s[0], i, 0)),
                      pl.BlockSpec((None, tr, cols), lambda i, s: (0, i, 0)),
                      pl.BlockSpec((None, tr, cols), lambda i, s: (1, i, 0)),
                      pl.BlockSpec((None, tr, cols), lambda i, s: (2, i, 0))],
            out_specs=pl.BlockSpec((None, tr, cols), lambda i, s: (s[1], i, 0))),
        compiler_params=_params(("arbitrary",)),
    )(sc_me, p, r2, r2, r2)


def _shard_columns(g_al, shard_cols, bound, off_lo, off_hi, rh, name):
    d, acols = g_al.shape
    wp = -(-shard_cols // LANES) * LANES
    tr = min(LANES, rh)
    nt = acols // LANES

    def body(x_ref, o_ref):
        s = pl.program_id(1)
        lane = lax.broadcasted_iota(jnp.int32, (tr, LANES), 1)

        def tile(q):
            q = max(0, min(nt - 1, q))
            return x_ref[:, q * LANES:(q + 1) * LANES].astype(F32)

        def read(start):
            q, sh = divmod(start, LANES)
            if sh == 0:
                return tile(q)
            return jnp.where(lane < LANES - sh, pltpu.roll(tile(q), LANES - sh, 1), pltpu.roll(tile(q + 1), LANES - sh, 1))

        for k in range(N_CHIP):
            @pl.when(s == k)
            def _(k=k):
                for t in range(wp // LANES):
                    n0 = k * shard_cols + t * LANES
                    if n0 + LANES <= bound:
                        v = read(n0 + off_lo)
                    elif n0 >= bound:
                        v = read(n0 + off_hi)
                    else:
                        v = jnp.where(lane < bound - n0, read(n0 + off_lo), read(n0 + off_hi))
                    o_ref[:, t * LANES:(t + 1) * LANES] = v.astype(o_ref.dtype)

    return pl.pallas_call(
        body, name=name, grid=(d // tr, N_CHIP),
        out_shape=jax.ShapeDtypeStruct((N_DEV, rh, wp), BF16),
        in_specs=[pl.BlockSpec((tr, acols), lambda i, s: (i, 0))],
        out_specs=pl.BlockSpec((None, tr, wp), lambda i, s: (2 * s + (i * tr) // rh, ((i * tr) % rh) // tr, 0)),
        compiler_params=_params(("arbitrary", "arbitrary")),
    )(g_al)


def _mm(a, b, M, N, K, *, name, ta=False, tb=False, out_dtypes=(F32,), tm=1024, tn=1024, tk=2048,
        a_spec=None, b_spec=None, out_specs=None, out_shapes=None, epi=None, epi_in=(), epi_specs=(), side=(), side_mid=0.8):
    tm, tn, tk = min(tm, M), min(tn, N), min(tk, K)
    assert M % tm == 0 and N % tn == 0 and K % tk == 0, (name, M, N, K, tm, tn, tk)
    nk = K // tk
    n_epi, n_out = len(epi_in), len(out_dtypes)
    if a_spec is None:
        a_spec = pl.BlockSpec((tk, tm), lambda i, j, k: (k, i)) if ta else pl.BlockSpec((tm, tk), lambda i, j, k: (i, k))
    if b_spec is None:
        b_spec = pl.BlockSpec((tn, tk), lambda i, j, k: (j, k)) if tb else pl.BlockSpec((tk, tn), lambda i, j, k: (k, j))
    if out_specs is None:
        out_specs = [pl.BlockSpec((tm, tn), lambda i, j, k: (i, j))] * n_out
        out_shapes = [jax.ShapeDtypeStruct((M, N), dt) for dt in out_dtypes]

    def body(a_ref, b_ref, *rest):
        epi_refs, o_refs, acc = rest[:n_epi], rest[n_epi:n_epi + n_out], rest[-1]
        k = pl.program_id(2)

        @pl.when(k == 0)
        def _():
            acc[...] = jnp.zeros_like(acc)

        acc[...] += _dg(a_ref[...].astype(BF16), b_ref[...].astype(BF16), 0 if ta else 1, 1 if tb else 0)

        @pl.when(k == nk - 1)
        def _():
            r = acc[...]
            vals = (r,) if epi is None else epi(r, *[e[...] for e in epi_refs])
            for o_ref, v in zip(o_refs, vals):
                o_ref[...] = v.astype(o_ref.dtype)

    outs, side_outs = _side_call(body, (M // tm, N // tn, nk), [a_spec, b_spec, *epi_specs], out_specs, out_shapes,
                                 [pltpu.VMEM((tm, tn), F32)], (a, b, *epi_in), list(side), name,
                                 ("parallel", "parallel", "arbitrary"), mid=side_mid)
    if side:
        return outs + side_outs
    return outs[0] if n_out == 1 else outs


def _T(arr, width, col=0, lead=None):
    return ("tile", arr, width, col, lead)


def _W(arr):
    return ("whole", arr)


def _rows(fn, n_rows, tr, ins, tile_outs, acc_outs, name):
    tr = min(tr, n_rows)
    assert n_rows % tr == 0, (name, n_rows, tr)
    in_specs, args = [], []
    for d in ins:
        if d[0] == "tile":
            _, arr, width, col, lead = d
            if lead is None:
                in_specs.append(pl.BlockSpec((tr, width), lambda i, col=col: (i, col)))
            else:
                in_specs.append(pl.BlockSpec((None, tr, width), lambda i, col=col, lead=lead: (lead, i, col)))
            args.append(arr)
        else:
            arr = d[1]
            in_specs.append(pl.BlockSpec(arr.shape, lambda i, nd=arr.ndim: (0,) * nd))
            args.append(arr)
    n_in, n_t = len(ins), len(tile_outs)
    out_shape = [jax.ShapeDtypeStruct((n_rows, w), dt) for w, dt in tile_outs]
    out_specs = [pl.BlockSpec((tr, w), lambda i: (i, 0)) for w, _ in tile_outs]
    out_shape += [jax.ShapeDtypeStruct(s, F32) for s in acc_outs]
    out_specs += [pl.BlockSpec(s, lambda i, nd=len(s): (0,) * nd) for s in acc_outs]

    def body(*refs):
        in_refs, t_refs, a_refs = refs[:n_in], refs[n_in:n_in + n_t], refs[n_in + n_t:]
        vals = fn(*[r[...] for r in in_refs])
        for r, v in zip(t_refs, vals[:n_t]):
            r[...] = v.astype(r.dtype)
        first = pl.program_id(0) == 0
        for r, v in zip(a_refs, vals[n_t:]):
            @pl.when(first)
            def _(r=r, v=v):
                r[...] = v

            @pl.when(jnp.logical_not(first))
            def _(r=r, v=v):
                r[...] += v

    return pl.pallas_call(
        body, name=name, out_shape=out_shape, grid=(n_rows // tr,),
        in_specs=in_specs, out_specs=out_specs,
        compiler_params=_params(("arbitrary",)),
    )(*args)


def _colsum(t):
    return jnp.sum(t, axis=0, keepdims=True)


def _prenorm(x, g, sc, sh):
    xf = x.astype(F32)
    return xf * lax.rsqrt(jnp.mean(xf * xf, axis=-1, keepdims=True) + EPS) * g * (1.0 + sc) + sh


def _postnorm(x, y, gate, pg):
    return x + gate * (y * lax.rsqrt(jnp.mean(y * y, axis=-1, keepdims=True) + EPS) * pg)


def _gelu(t):
    return 0.5 * t * (1.0 + lax.erf(t * (2.0 ** -0.5)))


def _sg_pre(zu, zv, lng, lnb):
    u, vr = _gelu(zu), _gelu(zv)
    mu = jnp.mean(vr, axis=-1, keepdims=True)
    var = jnp.mean(jnp.square(vr - mu), axis=-1, keepdims=True)
    return u, (vr - mu) * lax.rsqrt(var + EPS) * lng + lnb


def _sg_mix(vv, ws_ref_vals, bs_vals, gw):
    parts = []
    for g in range(SG_GROUPS):
        s = _dg(ws_ref_vals[g].astype(BF16), vv[:, g * gw:(g + 1) * gw].astype(BF16), 1, 0)
        parts.append(s + bs_vals[g])
    return jnp.concatenate(parts, axis=1)


def _readout(o, r, g, heads, dv):
    parts = []
    for h in range(heads):
        oh = o[:, h * dv:(h + 1) * dv]
        parts.append(oh * lax.rsqrt(jnp.mean(oh * oh, axis=-1, keepdims=True) + EPS))
    return jnp.concatenate(parts, axis=1) * g * (r * jax.nn.sigmoid(r))


def _log_sigmoid(a):
    return jnp.minimum(a, 0.0) - jnp.log(1.0 + jnp.exp(-jnp.abs(a)))


def _rope_swap(t, m):
    lane = lax.broadcasted_iota(jnp.int32, t.shape, 1)
    return jnp.where((lane % (2 * m)) < m, pltpu.roll(t, 3 * m, 1), pltpu.roll(t, m, 1))


def _rope(t, cos, sin, heads, dk):
    parts = []
    for h in range(heads):
        th = t[:, h * dk:(h + 1) * dk]
        parts.append(th * cos + _rope_swap(th, dk // 4) * sin)
    return jnp.concatenate(parts, axis=1)


def _rope_t(dt, cos, sin, heads, dk):
    parts = []
    for h in range(heads):
        dh = dt[:, h * dk:(h + 1) * dk]
        parts.append(dh * cos + _rope_swap(dh * sin, dk // 4))
    return jnp.concatenate(parts, axis=1)


def _chunk_cumsum(t, upwards):
    n = t.shape[0]
    row = lax.broadcasted_iota(jnp.int32, (n, n), 0)
    col = lax.broadcasted_iota(jnp.int32, (n, n), 1)
    shift = GLA_CHUNK.bit_length() - 1
    same = jnp.right_shift(row, shift) == jnp.right_shift(col, shift)
    tri = jnp.logical_and(same, col <= row if upwards else col >= row)
    return _dg(tri.astype(F32), t, 1, 0, precision=lax.Precision.HIGHEST)


def _chunk_terms(d, qv, kv, b, C):
    row = lax.broadcasted_iota(jnp.int32, (C, C), 0)
    col = lax.broadcasted_iota(jnp.int32, (C, C), 1)
    tri = row >= col if d == 0 else row <= col
    end_row = lax.broadcasted_iota(jnp.int32, b.shape, 0) == (C - 1 if d == 0 else 0)
    btot = _colsum(jnp.where(end_row, b, 0.0))
    eb, enb, ebt = jnp.exp(b), jnp.exp(-b), jnp.exp(btot - b)
    return tri, btot, eb, enb, ebt, qv * eb, kv * enb, kv * ebt


def _gla_fwd(q, k, zv, v_col0, la, st0, heads, dk, dv, name, side=(), side_mid=0.8):
    n, C, H = q.shape[0], GLA_CHUNK, heads
    nc = n // C

    def body(qf, kf, vf, laf, qb, kb, vb_, lab, st0_ref, of_ref, ob_ref, sf_ref, sb_ref, fin_ref, st):
        i = pl.program_id(1)

        @pl.when(i == 0)
        def _():
            st[...] = st0_ref[...]

        for d, (q_ref, k_ref, v_ref, la_ref, o_ref, save_ref) in enumerate(((qf, kf, vf, laf, of_ref, sf_ref), (qb, kb, vb_, lab, ob_ref, sb_ref))):
            tri, btot, _, _, _, qt, kt, kh = _chunk_terms(d, q_ref[...], k_ref[...], la_ref[...], C)
            s = st[d]
            vb = v_ref[...].astype(BF16)
            qtb = qt.astype(BF16)
            att = jnp.where(tri, _dg(qtb, kt.astype(BF16), 1, 1), 0.0)
            o_ref[...] = _dg(qtb, s.astype(BF16), 1, 1) + _dg(att.astype(BF16), vb, 1, 0)
            save_ref[...] = s
            s_new = s * jnp.exp(btot) + _dg(vb, kh.astype(BF16), 0, 0)
            st[d] = s_new

            @pl.when(i == nc - 1)
            def _(d=d, s_new=s_new):
                fin_ref[d] = s_new

    def seq(width, col0, rev, dir_cols=0):
        if rev:
            return pl.BlockSpec((C, width), lambda h, i: (nc - 1 - i, col0 + dir_cols + h))
        return pl.BlockSpec((C, width), lambda h, i: (i, col0 + h))

    both = pl.BlockSpec((2, None, dv, dk), lambda h, i: (0, h, 0, 0))
    outs, side_outs = _side_call(
        body, (H, nc),
        [seq(dk, 0, False), seq(dk, 0, False), seq(dv, v_col0, False), seq(dk, 0, False),
         seq(dk, 0, True), seq(dk, 0, True), seq(dv, v_col0, True), seq(dk, 0, True, H), both],
        [seq(dv, 0, False), seq(dv, 0, True),
         pl.BlockSpec((None, None, dv, dk), lambda h, i: (h, i, 0, 0)),
         pl.BlockSpec((None, None, dv, dk), lambda h, i: (h, nc - 1 - i, 0, 0)), both],
        [jax.ShapeDtypeStruct((n, H * dv), F32)] * 2 + [jax.ShapeDtypeStruct((H, nc, dv, dk), F32)] * 2
        + [jax.ShapeDtypeStruct((2, H, dv, dk), F32)],
        [pltpu.VMEM((2, dv, dk), F32)], (q, k, zv, la, q, k, zv, la, st0), list(side), name, ("arbitrary", "arbitrary"),
        mid=side_mid)
    return outs + side_outs


def _gla_bwd(q, k, zv, v_col0, la, saved_f, saved_b, do, dfin, heads, dk, dv, name, side=()):
    n, C, H = q.shape[0], GLA_CHUNK, heads
    nc = n // C

    def body(qf, kf, vf, laf, sf, dof, qb, kb, vb_, lab, sb, dob_, dfin_ref,
             dqf, dqb, dkf, dkb, dvf, dvb, dlaf, dlab, d0_ref, dst):
        i = pl.program_id(1)

        @pl.when(i == 0)
        def _():
            dst[...] = dfin_ref[...]

        dirs = ((qf, kf, vf, laf, sf, dof, dqf, dkf, dvf, dlaf), (qb, kb, vb_, lab, sb, dob_, dqb, dkb, dvb, dlab))
        for d, (q_ref, k_ref, v_ref, la_ref, save_ref, do_ref, dq_ref, dk_ref, dv_ref, dla_ref) in enumerate(dirs):
            tri, btot, eb, enb, ebt, qt, kt, kh = _chunk_terms(d, q_ref[...], k_ref[...], la_ref[...], C)
            s, dsn = save_ref[...], dst[d]
            vb, dob = v_ref[...].astype(BF16), do_ref[...].astype(BF16)
            qtb, ktb, khb, dsnb = qt.astype(BF16), kt.astype(BF16), kh.astype(BF16), dsn.astype(BF16)
            att = jnp.where(tri, _dg(qtb, ktb, 1, 1), 0.0).astype(BF16)
            datt = jnp.where(tri, _dg(dob, vb, 1, 1), 0.0).astype(BF16)
            dqt = _dg(dob, s.astype(BF16), 1, 0) + _dg(datt, ktb, 1, 0)
            dkt = _dg(datt, qtb, 0, 0)
            dkh = _dg(vb, dsnb, 1, 0)
            dv_ref[...] = _dg(att, dob, 0, 0) + _dg(khb, dsnb, 1, 1)
            ebtot = jnp.exp(btot)
            dbtot = ebtot * _colsum(s * dsn) + _colsum(dkh * kh)
            s0 = dsn * ebtot + _dg(dob, qtb, 0, 0)
            dst[d] = s0
            db = dqt * qt - dkt * kt - dkh * kh
            dq_ref[...] = dqt * eb
            dk_ref[...] = dkt * enb + dkh * ebt
            end_row = lax.broadcasted_iota(jnp.int32, db.shape, 0) == (C - 1 if d == 0 else 0)
            dla_ref[...] = db + jnp.where(end_row, dbtot, 0.0)

            @pl.when(i == nc - 1)
            def _(d=d, s0=s0):
                d0_ref[d] = s0

    def seq(width, col0, fwd_dir, dir_cols=0):
        if fwd_dir:
            return pl.BlockSpec((C, width), lambda h, i: (nc - 1 - i, col0 + h))
        return pl.BlockSpec((C, width), lambda h, i: (i, col0 + dir_cols + h))

    both = pl.BlockSpec((2, None, dv, dk), lambda h, i: (0, h, 0, 0))
    sav_f = pl.BlockSpec((None, None, dv, dk), lambda h, i: (h, nc - 1 - i, 0, 0))
    sav_b = pl.BlockSpec((None, None, dv, dk), lambda h, i: (h, i, 0, 0))
    outs, side_outs = _side_call(
        body, (H, nc),
        [seq(dk, 0, True), seq(dk, 0, True), seq(dv, v_col0, True), seq(dk, 0, True), sav_f, seq(dv, 0, True),
         seq(dk, 0, False), seq(dk, 0, False), seq(dv, v_col0, False), seq(dk, 0, False, H), sav_b, seq(dv, 0, False), both],
        [seq(dk, 0, True), seq(dk, 0, False), seq(dk, 0, True), seq(dk, 0, False), seq(dv, 0, True), seq(dv, 0, False),
         seq(dk, 0, True), seq(dk, 0, False), both],
        [jax.ShapeDtypeStruct((n, H * dk), F32)] * 4 + [jax.ShapeDtypeStruct((n, H * dv), F32)] * 2
        + [jax.ShapeDtypeStruct((n, H * dk), F32)] * 2 + [jax.ShapeDtypeStruct((2, H, dv, dk), F32)],
        [pltpu.VMEM((2, dv, dk), F32)], (q, k, zv, la, saved_f, do, q, k, zv, la, saved_b, do, dfin), list(side), name,
        ("arbitrary", "arbitrary"))
    return outs + side_outs


def _adamw_math(w, g, m, v):
    m2 = ADAM_B1 * m + (1.0 - ADAM_B1) * g
    v2 = ADAM_B2 * v + (1.0 - ADAM_B2) * jnp.square(g)
    m_hat = m2 / (1.0 - ADAM_B1 ** ADAM_STEP)
    v_hat = v2 / (1.0 - ADAM_B2 ** ADAM_STEP)
    delta = -ADAM_LR * (m_hat / (jnp.sqrt(v_hat) + ADAM_EPS) + ADAM_WD * w)
    return delta, m2, v2


def _adamw(w, g, m, v, name):
    r, cols = w.shape
    tr = _row_tile(r, cols, 4 * 4)
    spec = pl.BlockSpec((tr, cols), lambda i: (i, 0))

    def body(w_ref, g_ref, m_ref, v_ref, d_ref, m2_ref, v2_ref):
        d_ref[...], m2_ref[...], v2_ref[...] = _adamw_math(w_ref[...], g_ref[...], m_ref[...], v_ref[...])

    return _side_call(body, (r // tr,), [spec] * 4, [spec] * 3, [jax.ShapeDtypeStruct((r, cols), F32)] * 3,
                      [], (w, g, m, v), [], name, ("parallel",))[0]


def _adamw_many(ws, gs, ms, vs, name):
    n = len(ws)

    def body(*refs):
        ins, outs = refs[:4 * n], refs[4 * n:]
        for k in range(n):
            d, m2, v2 = _adamw_math(ins[k][...], ins[n + k][...], ins[2 * n + k][...], ins[3 * n + k][...])
            outs[k][...], outs[n + k][...], outs[2 * n + k][...] = d, m2, v2

    outs = pl.pallas_call(
        body, name=name, out_shape=[jax.ShapeDtypeStruct(w.shape, F32) for w in ws] * 3,
        in_specs=[pl.BlockSpec(memory_space=pltpu.VMEM)] * (4 * n), out_specs=[pl.BlockSpec(memory_space=pltpu.VMEM)] * (3 * n),
        compiler_params=_params(),
    )(*ws, *gs, *ms, *vs)
    return outs[:n], outs[n:2 * n], outs[2 * n:]


def _ada_update(cond_t, dmod, w, m, v, name):
    r, cols = w.shape
    tr, tc = _pick(r, 512, 8), _pick(cols, 1024)
    spec = pl.BlockSpec((tr, tc), lambda i, j: (i, j))

    def body(c_ref, d_ref, w_ref, m_ref, v_ref, g_ref, dl_ref, m2_ref, v2_ref):
        g = _dg(c_ref[...].astype(BF16), d_ref[...].astype(BF16), 1, 0)
        g_ref[...] = g
        dl_ref[...], m2_ref[...], v2_ref[...] = _adamw_math(w_ref[...], g, m_ref[...], v_ref[...])

    return _side_call(
        body, (r // tr, cols // tc),
        [pl.BlockSpec((tr, cond_t.shape[1]), lambda i, j: (i, 0)), pl.BlockSpec((dmod.shape[0], tc), lambda i, j: (0, j)), spec, spec, spec],
        [spec] * 4, [jax.ShapeDtypeStruct((r, cols), F32)] * 4, [], (cond_t, dmod, w, m, v), [], name, ("parallel", "parallel"))[0]


def _pack(parts, rows=8):
    flat = jnp.concatenate([p.reshape(-1).astype(F32) for p in parts])
    n = -(-flat.shape[0] // (rows * LANES)) * LANES
    return jnp.pad(flat, (0, rows * n - flat.shape[0])).reshape(rows, n)


def _unpack(flat, shapes):
    out, off = [], 0
    for s in shapes:
        size = math.prod(s)
        out.append(flat[off:off + size].reshape(s))
        off += size
    return out


def kernel(x, c, ctx, c_ctx, w_ada, b_ada, pre1_g, post1_g, pre2_g, post2_g, w_in, w_dec_f, b_dec_f, w_dec_b, b_dec_b, gla_norm_g, sg_ln_g, sg_ln_b, w_s, b_s, w_o, w_1, w_2, loss_target, m_c_ctx, m_w_ada, m_b_ada, m_pre1_g, m_post1_g, m_pre2_g, m_post2_g, m_w_in, m_w_dec_f, m_b_dec_f, m_w_dec_b, m_b_dec_b, m_gla_norm_g, m_sg_ln_g, m_sg_ln_b, m_w_s, m_b_s, m_w_o, m_w_1, m_w_2, v_c_ctx, v_w_ada, v_b_ada, v_pre1_g, v_post1_g, v_pre2_g, v_post2_g, v_w_in, v_w_dec_f, v_b_dec_f, v_w_dec_b, v_b_dec_b, v_gla_norm_g, v_sg_ln_g, v_sg_ln_b, v_w_s, v_b_s, v_w_o, v_w_1, v_w_2):
    N, D = x.shape[1], x.shape[2]
    NC = ctx.shape[1]
    H = GLA_HEADS
    VALW = D // 2
    DV = VALW // H
    DK = DV // 2
    KEYW = H * DK
    SGW = D - VALW
    GW = SGW // SG_GROUPS
    LR = 2 * GLA_LOWRANK
    F = w_1.shape[2] * N_CHIP
    FS = F // N_CHIP
    RH = D // 2
    MS = w_ada.shape[2]
    IN_COLS = w_in.shape[2] * N_CHIP
    K0, V0, R0, LF0 = KEYW, 2 * KEYW, 2 * KEYW + VALW, 2 * KEYW + 2 * VALW
    SG0 = LF0 + LR
    AQ, AK, AV, AR, ALR = 2 * SGW, 2 * SGW + KEYW, 2 * SGW + 2 * KEYW, 2 * SGW + 2 * KEYW + VALW, 2 * SGW + 2 * KEYW + 2 * VALW
    ACOLS = ALR + 4 * LANES
    LR_PAD = ACOLS - ALR
    assert IN_COLS == SG0 + 2 * SGW and N % SG_CHUNK == 0 and N % GLA_CHUNK == 0 and NC % GLA_CHUNK == 0

    ax, ay, ac = _place()
    s_me = (2 * ax + ay).astype(jnp.int32)
    b_me = (4 * ax + 2 * ay + ac).astype(jnp.int32)
    s_arr, c_arr = s_me.reshape(1), ac.astype(jnp.int32).reshape(1)
    sc_arr = jnp.concatenate([s_arr, c_arr])
    CS = IN_COLS // N_CHIP

    shards = [_cast_blocks(w_in[0].reshape(2, RH, CS), s_arr, "cast_w_in"), _cast_blocks(w_o[0].reshape(2, D // N_DEV, D), s_arr, "cast_w_o"),
              _cast_blocks(w_1[0].reshape(2, RH, FS), s_arr, "cast_w_1"), _cast_blocks(w_2[0].reshape(2, F // N_DEV, D), s_arr, "cast_w_2")]
    win_g, = _gather_big(shards[:1], "gather_weights")
    wo_buf, w1_buf, w2_buf = shards[1], shards[2], shards[3]
    w_in_nat = win_g.reshape(N_CHIP, 2, RH, IN_COLS // N_CHIP).transpose(1, 2, 0, 3).reshape(D, IN_COLS)
    w_al = jnp.concatenate([w_in_nat[:, SG0:], w_in_nat[:, :LF0], w_in_nat[:, LF0:SG0],
                            jnp.zeros((D, LR_PAD - LR), BF16)], axis=1)

    n_dec = GLA_LOWRANK * (KEYW // N_CHIP)
    g0 = _allgather_small(_pack([c, w_dec_f, w_dec_b, gla_norm_g]), "gather_small0").reshape(N_DEV, -1)
    c_all = g0[:, :D]
    per_chip = g0[0::2]
    wdf = per_chip[:, D:D + n_dec].reshape(N_CHIP, GLA_LOWRANK, KEYW // N_CHIP).transpose(1, 0, 2).reshape(GLA_LOWRANK, KEYW)
    wdb = per_chip[:, D + n_dec:D + 2 * n_dec].reshape(N_CHIP, GLA_LOWRANK, KEYW // N_CHIP).transpose(1, 0, 2).reshape(GLA_LOWRANK, KEYW)
    gn_full = per_chip[:, D + 2 * n_dec:D + 2 * n_dec + H * (DV // N_CHIP)].reshape(N_CHIP, H, DV // N_CHIP).transpose(1, 0, 2).reshape(1, VALW)
    wd_f = jnp.zeros((LANES, KEYW), F32).at[:GLA_LOWRANK].set(wdf)
    wd_b = jnp.zeros((LANES, KEYW), F32).at[GLA_LOWRANK:LR].set(wdb)

    cond_in = jnp.zeros((16, D), F32).at[:N_DEV].set(c_all).at[N_DEV].set(c_ctx)
    b_ada_sh = lax.dynamic_slice(b_ada, (0, s_me * MS), (1, MS))

    def mod_epi(r, bias):
        return (r + bias,)

    def silu_rows(t):
        return (t * jax.nn.sigmoid(t),)

    cond = _rows(silu_rows, 16, 16, [_W(cond_in)], [(D, F32)], [], "cond_silu")[0]
    mod_sh = _mm(cond, w_ada[0], 16, MS, D, name="mod_matmul", tn=512, tk=D, epi=mod_epi, epi_in=(b_ada_sh,),
                 epi_specs=(pl.BlockSpec((1, min(512, MS)), lambda i, j, k: (0, j)),))
    g1m = _allgather_small(mod_sh, "gather_mod").reshape(N_DEV, 16, MS)[0::2]
    mod_all = g1m.transpose(1, 0, 2).reshape(16, N_CHIP * MS)
    mod_me = lax.dynamic_slice(mod_all, (b_me, 0), (1, N_MOD * D))
    sh1, sc1, gt1, sh2, sc2, gt2 = [mod_me[:, i * D:(i + 1) * D] for i in range(N_MOD)]
    csh1, csc1 = mod_all[N_DEV:N_DEV + 1, :D], mod_all[N_DEV:N_DEV + 1, D:2 * D]

    mq = DK // 4
    pos = np.arange(N)
    inv_freq = (np.float32(ROPE_BASE) ** (-np.arange(mq, dtype=np.float32) / np.float32(mq))).astype(np.float32)
    ang_r = (pos // GRID_W).astype(np.float32)[:, None] * inv_freq[None, :]
    ang_c = (pos % GRID_W).astype(np.float32)[:, None] * inv_freq[None, :]
    cos_t = jnp.asarray(np.concatenate([np.cos(ang_r), np.cos(ang_r), np.cos(ang_c), np.cos(ang_c)], axis=1), F32)
    sin_t = jnp.asarray(np.concatenate([-np.sin(ang_r), np.sin(ang_r), -np.sin(ang_c), np.sin(ang_c)], axis=1), F32)

    x2, tgt, ctx2 = x[0], loss_target[0], ctx[0]
    TR = 128
    qscale = DK ** -0.5

    def prenorm_fwd(xa, g, sc, sh, n_rows, name):
        return _rows(lambda xv, gv, scv, shv: (_prenorm(xv, gv, scv, shv),), n_rows, TR,
                     [_T(xa, D), _W(g), _W(sc), _W(sh)], [(D, BF16)], [], name)[0]

    hx = prenorm_fwd(x2, pre1_g, sc1, sh1, N, "prenorm1_x")
    hc = prenorm_fwd(ctx2, pre1_g, csc1, csh1, NC, "prenorm1_ctx")
    tka = _pick(ACOLS, 1536, 2 * LANES)
    z_al, w1_buf = _mm(hx, w_al, N, ACOLS, D, name="in_proj_x", tn=tka, side=[_ph(w1_buf, [("gather_ici", 0, 2)])])
    zc_al = _mm(hc, w_al, NC, ACOLS, D, name="in_proj_ctx", tn=tka)

    def decay(lr, wdf_v, wdb_v, bf_v, bb_v):
        lrb = lr.astype(BF16)
        a_f = _dg(lrb, wdf_v.astype(BF16), 1, 0) + bf_v
        a_b = _dg(lrb, wdb_v.astype(BF16), 1, 0) + bb_v
        return a_f, a_b

    def running_decay(a_f, a_b):
        return jnp.concatenate([_chunk_cumsum(_log_sigmoid(a_f) / GLA_TAU, True), _chunk_cumsum(_log_sigmoid(a_b) / GLA_TAU, False)], axis=1)

    def prep_x(zq, zk, lr, cs, sn, wdf_v, wdb_v, bf_v, bb_v):
        return _rope(zq * qscale, cs, sn, H, DK), _rope(zk, cs, sn, H, DK), running_decay(*decay(lr, wdf_v, wdb_v, bf_v, bb_v))

    def prep_c(zk, lr, wdf_v, wdb_v, bf_v, bb_v):
        return zk, running_decay(*decay(lr, wdf_v, wdb_v, bf_v, bb_v))

    dec_w = [_W(wd_f), _W(wd_b), _W(b_dec_f), _W(b_dec_b)]
    q_r, k_r, la_x = _rows(prep_x, N, TR, [_T(z_al, KEYW, AQ // KEYW), _T(z_al, KEYW, AK // KEYW), _T(z_al, LANES, ALR // LANES),
                                           _T(cos_t, DK), _T(sin_t, DK)] + dec_w,
                           [(KEYW, F32), (KEYW, F32), (2 * KEYW, F32)], [], "gla_prep_x")
    k_c, la_c = _rows(prep_c, NC, TR, [_T(zc_al, KEYW, AK // KEYW), _T(zc_al, LANES, ALR // LANES)] + dec_w,
                      [(KEYW, F32), (2 * KEYW, F32)], [], "gla_prep_ctx")

    zero_state = jnp.zeros((2, H, DV, DK), F32)
    q_c = jnp.zeros((NC, KEYW), F32)
    _, _, savf_c, savb_c, st_c = _gla_fwd(q_c, k_c, zc_al, AV // DV, la_c, zero_state, H, DK, DV, "gla_fwd_ctx")
    o_f, o_b, savf_x, savb_x, _, wo_g, w1_buf = _gla_fwd(
        q_r, k_r, z_al, AV // DV, la_x, st_c, H, DK, DV, "gla_fwd_x",
        side=[_ph(wo_buf, [("gather_chain", 0, 4)]), _ph(w1_buf, [("gather_d2d", 0, 2), ("gather_ici", 2, 4)])], side_mid=0.55)
    w_o_f = wo_g.reshape(D, D)

    def readout_fwd(of, ob, r, g):
        return (_readout(of + ob, r, g, H, DV),)

    y_gla = _rows(readout_fwd, N, TR, [_T(o_f, VALW), _T(o_b, VALW), _T(z_al, VALW, AR // VALW), _W(gn_full)],
                  [(VALW, BF16)], [], "gla_readout")[0]

    bs_col = b_s[0].reshape(SG_GROUPS, SG_CHUNK, 1)

    def sg_fwd(zu, zv, lng, lnb, ws, bs):
        u, vv = _sg_pre(zu, zv, lng, lnb)
        return (u * _sg_mix(vv, ws, bs, GW),)

    y_sg = _rows(sg_fwd, N, SG_CHUNK, [_T(z_al, SGW, 0), _T(z_al, SGW, 1), _W(sg_ln_g), _W(sg_ln_b), _W(w_s[0]), _W(bs_col)],
                 [(SGW, BF16)], [], "sg_fwd")[0]
    ycat = jnp.concatenate([y_gla, y_sg], axis=1)
    y, w1_g, w2_buf = _mm(ycat, w_o_f, N, D, D, name="out_proj",
                          side=[_ph(w1_buf, [("gather_d2d", 2, 4)]), _ph(w2_buf, [("gather_ici", 0, 1)])])
    def between_sublayers(xv, yv, gv, pv, g2v, scv, shv):
        x1v = _postnorm(xv, yv, gv, pv)
        return x1v, _prenorm(x1v, g2v, scv, shv)

    x1, h2 = _rows(between_sublayers, N, TR, [_T(x2, D), _T(y, D), _W(gt1), _W(post1_g), _W(pre2_g), _W(sc2), _W(sh2)],
                   [(D, F32), (D, BF16)], [], "postnorm1_prenorm2")

    tm1, tn1, tk1 = min(1024, N), min(1024, FS), min(2048, RH)
    w1_fwd_spec = pl.BlockSpec((None, tk1, tn1), lambda i, j, k: (2 * ((j * tn1) // FS) + (k * tk1) // RH, ((k * tk1) % RH) // tk1, ((j * tn1) % FS) // tn1))

    def relu2_epi(r):
        rf = jnp.maximum(r, 0.0)
        return rf * rf, rf

    act, rf, w2_g = _mm(h2, w1_g, N, F, D, name="mlp_up", out_dtypes=(BF16, BF16), tm=tm1, tn=tn1, tk=tk1, b_spec=w1_fwd_spec, epi=relu2_epi,
                        side=[_ph(w2_buf, [("gather_d2d", 0, 1), ("gather_chain", 1, 4)])], side_mid=0.88)
    w_2_f = w2_g.reshape(F, D)
    y2 = _mm(act, w_2_f, N, D, F, name="mlp_down")

    def final(x1v, y2v, gv, pv, tv):
        def loss_fn(x1a, y2a, ga, pa):
            err = _postnorm(x1a, y2a, ga, pa) - tv
            return 0.5 * jnp.sum(jnp.mean(err * err, axis=-1))
        loss, grads = jax.value_and_grad(loss_fn, argnums=(0, 1, 2, 3))(x1v, y2v, gv, pv)
        return grads[0], grads[1], jnp.full((1, LANES), loss, F32), _colsum(grads[2]), _colsum(grads[3])

    dx2, dy2, loss_acc, dgt2, dpost2 = _rows(final, N, TR, [_T(x1, D), _T(y2, D), _W(gt2), _W(post2_g), _T(tgt, D)],
                                             [(D, F32), (D, BF16)], [(1, LANES), (1, D), (1, D)], "loss_postnorm2_bwd")

    df = _mm(dy2, w_2_f, N, F, D, name="mlp_down_dx", tb=True, out_dtypes=(BF16,), epi=lambda r, rfv: (r * (2.0 * rfv.astype(F32)),),
             epi_in=(rf,), epi_specs=(pl.BlockSpec((min(1024, N), min(1024, F)), lambda i, j, k: (i, j)),))
    dw2 = _mm(act, dy2, F, D, N, name="mlp_down_dw", ta=True, out_dtypes=(BF16,)).reshape(N_DEV, F // N_DEV, D)
    tnb, tkb = min(1024, D, RH), min(2048, FS)
    w1_bwd_spec = pl.BlockSpec((None, tnb, tkb), lambda i, j, k: (2 * ((k * tkb) // FS) + (j * tnb) // RH, ((j * tnb) % RH) // tnb, ((k * tkb) % FS) // tkb))
    dh2, recv1_w2 = _mm(df, w1_g, N, D, F, name="mlp_up_dx", tb=True, tn=tnb, tk=tkb, b_spec=w1_bwd_spec,
                         side=[_ph(lax.empty((N_CHIP,) + dw2.shape[1:], BF16), [("rs_sibling", 0, 4)], src=dw2)])
    part_w2 = _sum_sibling(dw2, recv1_w2, c_arr, "rs_sum_sibling_w_2")
    tmw, tnw = min(1024, RH), min(1024, FS)
    dw1_spec = pl.BlockSpec((None, tmw, tnw), lambda i, j, k: (2 * ((j * tnw) // FS) + (i * tmw) // RH, ((i * tmw) % RH) // tmw, ((j * tnw) % FS) // tnw))
    dw1, recv2_w2 = _mm(h2, df, D, F, N, name="mlp_up_dw", ta=True, tm=tmw, tn=tnw, out_dtypes=(BF16,), out_specs=[dw1_spec],
                        out_shapes=[jax.ShapeDtypeStruct((N_DEV, RH, FS), BF16)],
                        side=[_ph(lax.empty((3,) + part_w2.shape[1:], BF16), [("rs_chips", 0, 3)], src=part_w2)])

    def prenorm_bwd(xv, gv, scv, shv, dh, dres):
        _, vjp = jax.vjp(_prenorm, xv, gv, scv, shv)
        dx, dg, dsc, dsh = vjp(dh)
        return dx + dres, _colsum(dg), _colsum(dsc), _colsum(dsh)

    def between_sublayers_bwd(xv, gv, scv, shv, dh, dres, yv, g1v, pv):
        dx1v, dg, dsc, dsh = prenorm_bwd(xv, gv, scv, shv, dh, dres)
        _, vjp = jax.vjp(lambda ya, ga, pa: _postnorm(0.0, ya, ga, pa), yv, g1v, pv)
        dy_, dg1_, dp_ = vjp(dx1v)
        return dx1v, dy_, dg, dsc, dsh, _colsum(dg1_), _colsum(dp_)

    dx1, dy, dpre2, dsc2, dsh2, dgt1, dpost1 = _rows(
        between_sublayers_bwd, N, TR, [_T(x1, D), _W(pre2_g), _W(sc2), _W(sh2), _T(dh2, D), _T(dx2, D), _T(y, D), _W(gt1), _W(post1_g)],
        [(D, F32), (D, BF16)], [(1, D)] * 5, "prenorm2_postnorm1_bwd")
    dycat, recv2_w2, recv1_w1 = _mm(dy, w_o_f, N, D, D, name="out_proj_dx", tb=True,
                                    side=[_ph(recv2_w2, [("rs_chips", 3, 4)], src=part_w2),
                                          _ph(lax.empty((N_CHIP,) + dw1.shape[1:], BF16), [("rs_sibling", 0, 4)], src=dw1)])
    part_w1 = _sum_sibling(dw1, recv1_w1, c_arr, "rs_sum_sibling_w_1")
    dwo = _mm(ycat, dy, D, D, N, name="out_proj_dw", ta=True, out_dtypes=(BF16,)).reshape(N_DEV, D // N_DEV, D)

    def readout_bwd(of, ob, r, g, dyv):
        _, vjp = jax.vjp(lambda o_, r_, g_: _readout(o_, r_, g_, H, DV), of + ob, r, g)
        do_, dr_, dg_ = vjp(dyv)
        return do_, dr_, _colsum(dg_)

    do_x, dz_r, dgn = _rows(readout_bwd, N, TR, [_T(o_f, VALW), _T(o_b, VALW), _T(z_al, VALW, AR // VALW), _W(gn_full), _T(dycat, VALW, 0)],
                            [(VALW, F32), (VALW, BF16)], [(1, VALW)], "gla_readout_bwd")

    def sg_bwd(zu, zv, lng, lnb, ws, bs, dyv):
        (u, vv), vjp = jax.vjp(_sg_pre, zu, zv, lng, lnb)
        s = _sg_mix(vv, ws, bs, GW)
        du, ds = dyv * s, dyv * u
        dws, dbs, dvv = [], [], []
        for g in range(SG_GROUPS):
            dsg = ds[:, g * GW:(g + 1) * GW]
            dsb = dsg.astype(BF16)
            dws.append(_dg(dsb, vv[:, g * GW:(g + 1) * GW].astype(BF16), 1, 1))
            dbs.append(jnp.sum(dsg, axis=1, keepdims=True))
            dvv.append(_dg(ws[g].astype(BF16), dsb, 0, 0))
        dzu, dzv, dlng, dlnb = vjp((du, jnp.concatenate(dvv, axis=1)))
        return jnp.concatenate([dzu, dzv], axis=1), _colsum(dlng), _colsum(dlnb), jnp.concatenate(dws, axis=0), jnp.concatenate(dbs, axis=0)

    dz_sg, dlng, dlnb, dws, dbs = _rows(sg_bwd, N, SG_CHUNK, [_T(z_al, SGW, 0), _T(z_al, SGW, 1), _W(sg_ln_g), _W(sg_ln_b), _W(w_s[0]), _W(bs_col), _T(dycat, SGW, VALW // SGW)],
                                        [(2 * SGW, BF16)], [(1, SGW), (1, SGW), (SG_GROUPS * SG_CHUNK, SG_CHUNK), (SG_GROUPS * SG_CHUNK, 1)], "sg_bwd")

    dq_f, dq_b, dk_f, dk_b, dv_f, dv_b, dla_f, dla_b, dst0, recv2_w1, recv1_wo = _gla_bwd(
        q_r, k_r, z_al, AV // DV, la_x, savf_x, savb_x, do_x, zero_state, H, DK, DV, "gla_bwd_x",
        side=[_ph(lax.empty((3,) + part_w1.shape[1:], BF16), [("rs_chips", 0, 3)], src=part_w1),
              _ph(lax.empty((N_CHIP,) + dwo.shape[1:], BF16), [("rs_sibling", 0, 4)], src=dwo)])
    part_wo = _sum_sibling(dwo, recv1_wo, c_arr, "rs_sum_sibling_w_o")
    _, _, dkc_f, dkc_b, dvc_f, dvc_b, dlac_f, dlac_b, _ = _gla_bwd(
        q_c, k_c, zc_al, AV // DV, la_c, savf_c, savb_c, jnp.zeros((NC, VALW), F32), dst0, H, DK, DV, "gla_bwd_ctx")

    def decay_bwd(lr, dla_f_v, dla_b_v, wdf_v, wdb_v, bf_v, bb_v):
        a_f, a_b = decay(lr, wdf_v, wdb_v, bf_v, bb_v)
        da_f = _chunk_cumsum(dla_f_v, False) * jax.nn.sigmoid(-a_f) / GLA_TAU
        da_b = _chunk_cumsum(dla_b_v, True) * jax.nn.sigmoid(-a_b) / GLA_TAU
        lrb, dfb, dbb = lr.astype(BF16), da_f.astype(BF16), da_b.astype(BF16)
        dlr = _dg(dfb, wdf_v.astype(BF16), 1, 1) + _dg(dbb, wdb_v.astype(BF16), 1, 1)
        return dlr, _dg(lrb, dfb, 0, 0), _dg(lrb, dbb, 0, 0), _colsum(da_f), _colsum(da_b)

    def prep_x_bwd(dq0, dq1, dk0, dk1, dv0, dv1, lr, dla0, dla1, cs, sn, wdf_v, wdb_v, bf_v, bb_v):
        dlr, dwf, dwb, dbf, dbb = decay_bwd(lr, dla0, dla1, wdf_v, wdb_v, bf_v, bb_v)
        return (_rope_t(dq0 + dq1, cs, sn, H, DK) * qscale, _rope_t(dk0 + dk1, cs, sn, H, DK), dv0 + dv1, dlr, dwf, dwb, dbf, dbb)

    def prep_c_bwd(dk0, dk1, dv0, dv1, lr, dla0, dla1, wdf_v, wdb_v, bf_v, bb_v):
        dlr, dwf, dwb, dbf, dbb = decay_bwd(lr, dla0, dla1, wdf_v, wdb_v, bf_v, bb_v)
        return dk0 + dk1, dv0 + dv1, dlr, dwf, dwb, dbf, dbb

    dec_acc = [(LANES, KEYW), (LANES, KEYW), (1, KEYW), (1, KEYW)]
    dz_q, dz_k, dz_v, dz_lr, dwdf_x, dwdb_x, dbdf_x, dbdb_x = _rows(
        prep_x_bwd, N, TR, [_T(dq_f, KEYW), _T(dq_b, KEYW), _T(dk_f, KEYW), _T(dk_b, KEYW), _T(dv_f, VALW), _T(dv_b, VALW),
                            _T(z_al, LANES, ALR // LANES), _T(dla_f, KEYW), _T(dla_b, KEYW), _T(cos_t, DK), _T(sin_t, DK)] + dec_w,
        [(KEYW, BF16), (KEYW, BF16), (VALW, BF16), (LANES, BF16)], dec_acc, "gla_prep_x_bwd")
    dzc_k, dzc_v, dzc_lr, dwdf_c, dwdb_c, dbdf_c, dbdb_c = _rows(
        prep_c_bwd, NC, TR, [_T(dkc_f, KEYW), _T(dkc_b, KEYW), _T(dvc_f, VALW), _T(dvc_b, VALW),
                             _T(zc_al, LANES, ALR // LANES), _T(dlac_f, KEYW), _T(dlac_b, KEYW)] + dec_w,
        [(KEYW, BF16), (VALW, BF16), (LANES, BF16)], dec_acc, "gla_prep_ctx_bwd")

    dz_al = jnp.concatenate([dz_sg, dz_q, dz_k, dz_v, dz_r, dz_lr, jnp.zeros((N, LR_PAD - LANES), BF16)], axis=1)
    dzc_al = jnp.concatenate([jnp.zeros((NC, 2 * SGW + KEYW), BF16), dzc_k, dzc_v, jnp.zeros((NC, VALW), BF16), dzc_lr,
                              jnp.zeros((NC, LR_PAD - LANES), BF16)], axis=1)
    tkd = _pick(ACOLS, 3584, 2 * LANES)
    h_cat = jnp.concatenate([hx, hc], axis=0)
    dz_cat = jnp.concatenate([dz_al, dzc_al], axis=0)
    tkt = _pick(N + NC, 2304)
    dw_al, recv2_w1, recv2_wo = _mm(h_cat, dz_cat, D, ACOLS, N + NC, name="in_proj_dw", ta=True, tn=tka, tk=tkt, out_dtypes=(BF16,),
                                    side=[_ph(recv2_w1, [("rs_chips", 3, 4)], src=part_w1),
                                          _ph(lax.empty((3,) + part_wo.shape[1:], BF16), [("rs_chips", 0, 4)], src=part_wo)])
    g_in = _shard_columns(dw_al, CS, SG0, 2 * SGW, -SG0, RH, "w_in_grad_blocks")
    recv1_in = _rs_sibling([g_in], "rs_sibling_w_in")[0]
    part_in = _sum_sibling(g_in, recv1_in, c_arr, "rs_sum_sibling_w_in")
    dhx, recv2_in = _mm(dz_al, w_al, N, D, ACOLS, name="in_proj_dx", tb=True, tk=tkd,
                        side=[_ph(lax.empty((3,) + part_in.shape[1:], BF16), [("rs_chips", 0, 3)], src=part_in)])
    dhc = _mm(dzc_al, w_al, NC, D, ACOLS, name="in_proj_dctx", tb=True, tk=tkd)

    grad_x, dpre1_x, dsc1, dsh1 = _rows(prenorm_bwd, N, TR, [_T(x2, D), _W(pre1_g), _W(sc1), _W(sh1), _T(dhx, D), _T(dx1, D)],
                                        [(D, F32)], [(1, D)] * 3, "prenorm1_x_bwd")

    def prenorm_bwd_ctx(xv, gv, scv, shv, dh):
        _, vjp = jax.vjp(_prenorm, xv, gv, scv, shv)
        _, dg, dsc, dsh = vjp(dh)
        return _colsum(dg), _colsum(dsc), _colsum(dsh)

    dpre1_c, dcsc1, dcsh1 = _rows(prenorm_bwd_ctx, NC, TR, [_T(ctx2, D), _W(pre1_g), _W(csc1), _W(csh1), _T(dhc, D)],
                                  [], [(1, D)] * 3, "prenorm1_ctx_bwd")

    half = [_sum_chips(p, r, sc_arr, "rs_sum_chips_" + nm)
            for p, r, nm in zip((part_wo, part_w1, part_w2), (recv2_wo, recv2_w1, recv2_w2), ("w_o", "w_1", "w_2"))]
    g_w_o, g_w_1, g_w_2 = [g.reshape(w.shape[1:]) for g, w in zip(_rs_final(half, "rs_final"), (w_o, w_1, w_2))]

    dmod_x = jnp.concatenate([dsh1, dsc1, dgt1, dsh2, dsc2, dgt2], axis=1)
    dmodc = jnp.concatenate([dcsh1, dcsc1], axis=1)
    small_parts = [loss_acc, dmod_x, dmodc, dpre1_x + dpre1_c, dpost1, dpre2, dpost2, dwdf_x + dwdf_c, dbdf_x + dbdf_c,
                   dwdb_x + dwdb_c, dbdb_x + dbdb_c, dgn, dlng, dlnb, dws, dbs]
    small_shapes = [p.shape for p in small_parts]
    packed = _pack(small_parts)
    n_sm = packed.shape[1]
    gathered = _allgather_small(packed, "gather_small_grads")

    def sum_devices(g):
        tot = g[0:8]
        for dev in range(1, N_DEV):
            tot = tot + g[8 * dev:8 * dev + 8]
        return (tot,)

    summed = _rows(sum_devices, N_DEV * 8, N_DEV * 8, [_W(gathered)], [], [(8, n_sm)], "sum_small_grads")[0]
    (loss_s, dmod_sum, dmodc_sum, g_pre1, g_post1, g_pre2, g_post2, g_wdf_pad, g_bdf, g_wdb_pad, g_bdb, g_gn, g_lng, g_lnb,
     g_ws, g_bs) = _unpack(summed.reshape(-1), small_shapes)
    loss = loss_s[0, 0]
    dmod_rows = gathered.reshape(N_DEV, -1)[:, LANES:LANES + N_MOD * D]
    g_b_ada = dmod_sum + jnp.pad(dmodc_sum, ((0, 0), (0, (N_MOD - 2) * D)))
    dmod16 = jnp.zeros((16, N_MOD * D), F32).at[:N_DEV].set(dmod_rows).at[N_DEV, :2 * D].set(dmodc_sum[0])
    dmod16_sh = lax.dynamic_slice(dmod16, (0, s_me * MS), (16, MS))
    g_w_ada, d_w_ada, nm_w_ada, nv_w_ada = _ada_update(cond.T, dmod16_sh, w_ada[0], m_w_ada[0], v_w_ada[0], "w_ada_update")

    dcond, recv2_in = _mm(dmod16_sh, w_ada[0], 16, D, MS, name="cond_bwd", tb=True, tk=min(512, MS),
                          side=[_ph(recv2_in, [("rs_chips", 3, 4)], src=part_in)])
    half_in = _sum_chips(part_in, recv2_in, sc_arr, "rs_sum_chips_w_in")
    g_w_in_t = _transpose_cols(_rs_final([half_in], "rs_final_w_in")[0].reshape(D, -1), CS, "w_in_grad_t")
    part_c = _allgather_small(dcond[N_DEV].reshape(8, D // 8), "gather_dcond").reshape(N_DEV, D)

    def cctx_grad(p, cv):
        sg = jax.nn.sigmoid(cv)
        tot = ((p[0:1] + p[2:3]) + p[4:5]) + p[6:7]
        return (jnp.broadcast_to(tot * (sg * (1.0 + cv * (1.0 - sg))), p.shape),)

    g_c_ctx = _rows(cctx_grad, N_DEV, N_DEV, [_W(part_c), _W(c_ctx.reshape(1, D))], [(D, F32)], [], "c_ctx_grad")[0][0:1]

    def col_shard(g_full, width):
        return lax.dynamic_slice_in_dim(g_full, s_me * width, width, axis=g_full.ndim - 1)

    g_w_dec_f = col_shard(g_wdf_pad[:GLA_LOWRANK], KEYW // N_CHIP)
    g_w_dec_b = col_shard(g_wdb_pad[GLA_LOWRANK:LR], KEYW // N_CHIP)
    g_gla_norm = col_shard(g_gn.reshape(H, DV), DV // N_CHIP)
    small_w = [c_ctx, b_ada, pre1_g, post1_g, pre2_g, post2_g, w_dec_f, b_dec_f, w_dec_b, b_dec_b, gla_norm_g, sg_ln_g, sg_ln_b, w_s, b_s]
    small_m = [m_c_ctx, m_b_ada, m_pre1_g, m_post1_g, m_pre2_g, m_post2_g, m_w_dec_f, m_b_dec_f, m_w_dec_b, m_b_dec_b, m_gla_norm_g, m_sg_ln_g, m_sg_ln_b, m_w_s, m_b_s]
    small_v = [v_c_ctx, v_b_ada, v_pre1_g, v_post1_g, v_pre2_g, v_post2_g, v_w_dec_f, v_b_dec_f, v_w_dec_b, v_b_dec_b, v_gla_norm_g, v_sg_ln_g, v_sg_ln_b, v_w_s, v_b_s]
    small_g = [g_c_ctx, g_b_ada, g_pre1, g_post1, g_pre2, g_post2, g_w_dec_f, g_bdf, g_w_dec_b, g_bdb, g_gla_norm, g_lng, g_lnb, g_ws, g_bs]
    small_g = [g.reshape(w.shape) for g, w in zip(small_g, small_w)]
    d_small, m_small, v_small = _adamw_many(small_w, small_g, small_m, small_v, "adamw_small")

    def big(w, g, m, v, name):
        shp = w.shape
        res = _adamw(w.reshape(shp[-2:]), g.reshape(shp[-2:]), m.reshape(shp[-2:]), v.reshape(shp[-2:]), name)
        return [g.reshape(shp)] + [r.reshape(shp) for r in res]

    r_in = [jnp.swapaxes(t, 1, 2) for t in big(jnp.swapaxes(w_in, 1, 2), g_w_in_t[None], jnp.swapaxes(m_w_in, 1, 2),
                                               jnp.swapaxes(v_w_in, 1, 2), "adamw_w_in")]
    r_o = big(w_o, g_w_o, m_w_o, v_w_o, "adamw_w_o")
    r_1 = big(w_1, g_w_1, m_w_1, v_w_1, "adamw_w_1")
    r_2 = big(w_2, g_w_2, m_w_2, v_w_2, "adamw_w_2")
    r_ada = [t.reshape(w_ada.shape) for t in (g_w_ada, d_w_ada, nm_w_ada, nv_w_ada)]

    def ordered(k):
        sm = [small_g, d_small, m_small, v_small][k]
        return [sm[0], r_ada[k], *sm[1:6], r_in[k], *sm[6:15], r_o[k], r_1[k], r_2[k]]

    return (loss, grad_x.reshape(x.shape), *ordered(0), *ordered(1), *ordered(2), *ordered(3))
```

```python
import functools
import math

import numpy as np
import jax
import jax.numpy as jnp
from jax import lax
from jax.experimental import pallas as pl
from jax.experimental.pallas import tpu as pltpu

F32 = jnp.float32
BF16 = jnp.bfloat16
MESH = pl.DeviceIdType.MESH
ANY = pl.BlockSpec(memory_space=pl.ANY)

GLA_HEADS = 8
GLA_CHUNK = 64
GLA_LOWRANK = 16
GLA_TAU = 16.0
ROPE_BASE = 10000.0
GRID_W = 64
SG_GROUPS = 4
SG_CHUNK = 128
N_MOD = 6
EPS = 1e-6
ADAM_LR = 0.001
ADAM_B1 = 0.9
ADAM_B2 = 0.999
ADAM_EPS = 1e-08
ADAM_WD = 0.01
ADAM_STEP = 10

LANES = 128
VMEM_LIMIT = 56 << 20
N_DEV = 8
N_CHIP = 4


def _params(sem=None):
    return pltpu.CompilerParams(dimension_semantics=sem, vmem_limit_bytes=VMEM_LIMIT)


def _pick(dim, target, unit=LANES):
    best = None
    for t in range(unit, min(dim, target) + 1, unit):
        if dim % t == 0:
            best = t
    return dim if best is None else best


def _dg(a, b, ca, cb, precision=None):
    return lax.dot_general(a, b, (((ca,), (cb,)), ((), ())), preferred_element_type=F32,
                           precision=precision)


def _place():
    return lax.axis_index("x"), lax.axis_index("y"), lax.axis_index("c")


def _allgather_small(v, name):
    m_per, n = v.shape

    def body(x_ref, out_ref, send_sems, recv_sems, local_sem):
        x, y, c = _place()
        me, sibling = (x, y, c), (x, y, 1 - c)
        chips = [(1 - x, y), (x, 1 - y), (1 - x, 1 - y)]

        def rows(px, py, pc):
            return out_ref.at[pl.ds((4 * px + 2 * py + pc) * m_per, m_per), :]

        def copy(k, block, to, src=None):
            return pltpu.make_async_remote_copy(
                src_ref=rows(*block) if src is None else src, dst_ref=rows(*block),
                send_sem=send_sems.at[k], recv_sem=recv_sems.at[k],
                device_id=to, device_id_type=MESH)

        mine = pltpu.make_async_copy(x_ref, rows(*me), local_sem)
        mine.start()
        first = [copy(0, me, sibling, src=x_ref)]
        first += [copy(1 + j, me, (*chip, c), src=x_ref) for j, chip in enumerate(chips)]
        for cp in first:
            cp.start()
        passed = [copy(4 + j, (*chip, c), sibling) for j, chip in enumerate(chips)]
        for j, chip in enumerate(chips):
            copy(1 + j, (*chip, c), me).wait_recv()
            passed[j].start()
        copy(0, sibling, me).wait_recv()
        for j, chip in enumerate(chips):
            copy(4 + j, (*chip, 1 - c), me).wait_recv()
        for cp in first + passed:
            cp.wait_send()
        mine.wait()

    return pl.pallas_call(
        body, name=name,
        out_shape=jax.ShapeDtypeStruct((N_DEV * m_per, n), v.dtype),
        in_specs=[pl.BlockSpec(memory_space=pltpu.VMEM)],
        out_specs=pl.BlockSpec(memory_space=pltpu.VMEM),
        scratch_shapes=[pltpu.SemaphoreType.DMA((7,)), pltpu.SemaphoreType.DMA((7,)),
                        pltpu.SemaphoreType.DMA],
        compiler_params=pltpu.CompilerParams(vmem_limit_bytes=VMEM_LIMIT),
    )(v)


def _cast_blocks(w, s_me, name):
    _, r, cols = w.shape
    tr = _row_tile(r, cols, 4)

    def body(s_ref, w_ref, o_ref):
        o_ref[...] = w_ref[...].astype(BF16)

    return pl.pallas_call(
        body, name=name,
        out_shape=jax.ShapeDtypeStruct((N_DEV, r, cols), BF16),
        grid_spec=pltpu.PrefetchScalarGridSpec(
            num_scalar_prefetch=1, grid=(2, r // tr),
            in_specs=[pl.BlockSpec((None, tr, cols), lambda h, i, s: (h, i, 0))],
            out_specs=pl.BlockSpec((None, tr, cols), lambda h, i, s: (2 * s[0] + h, i, 0))),
        compiler_params=_params(("arbitrary", "arbitrary")),
    )(s_me, w)


def _gather_big(ws, name):
    nw = len(ws)

    def body(*refs):
        outs = refs[nw:2 * nw]
        send_sems, recv_sems = refs[2 * nw:]
        x, y, c = _place()
        me, sibling = (x, y, c), (x, y, 1 - c)
        chips = [(1 - x, y), (x, 1 - y), (1 - x, 1 - y)]

        def blk(px, py, pc):
            return 4 * px + 2 * py + pc

        def copy(w, k, block, to):
            return pltpu.make_async_remote_copy(
                src_ref=outs[w].at[block], dst_ref=outs[w].at[block],
                send_sem=send_sems.at[6 * w + k], recv_sem=recv_sems.at[6 * w + k],
                device_id=to, device_id_type=MESH)

        first = []
        for w in range(nw):
            for j, chip in enumerate(chips):
                cp = copy(w, j, blk(x, y, c), (*chip, c))
                cp.start()
                first.append(cp)
        passed = []
        for w in range(nw):
            for j, chip in enumerate(chips):
                copy(w, j, blk(*chip, c), me).wait_recv()
                cp = copy(w, 3 + j, blk(*chip, c), sibling)
                cp.start()
                passed.append(cp)
        for w in range(nw):
            for j, chip in enumerate(chips):
                copy(w, 3 + j, blk(*chip, 1 - c), me).wait_recv()
        for cp in first + passed:
            cp.wait_send()

    return pl.pallas_call(
        body, name=name,
        out_shape=[jax.ShapeDtypeStruct(w.shape, w.dtype) for w in ws],
        in_specs=[ANY] * nw, out_specs=[ANY] * nw,
        input_output_aliases={w: w for w in range(nw)},
        scratch_shapes=[pltpu.SemaphoreType.DMA((6 * nw,)), pltpu.SemaphoreType.DMA((6 * nw,))],
    )(*ws)


def _rs_sibling(gs, name):
    nw = len(gs)

    def body(*refs):
        ins, outs = refs[:nw], refs[nw:2 * nw]
        send_sems, recv_sems = refs[2 * nw:]
        x, y, c = _place()
        cps = []
        for w in range(nw):
            for s in range(N_CHIP):
                cp = pltpu.make_async_remote_copy(
                    src_ref=ins[w].at[2 * s + (1 - c)], dst_ref=outs[w].at[s],
                    send_sem=send_sems.at[N_CHIP * w + s], recv_sem=recv_sems.at[N_CHIP * w + s],
                    device_id=(x, y, 1 - c), device_id_type=MESH)
                cp.start()
                cps.append(cp)
        for cp in cps:
            cp.wait()

    return pl.pallas_call(
        body, name=name,
        out_shape=[jax.ShapeDtypeStruct((N_CHIP,) + g.shape[1:], g.dtype) for g in gs],
        in_specs=[ANY] * nw, out_specs=[ANY] * nw,
        scratch_shapes=[pltpu.SemaphoreType.DMA((N_CHIP * nw,)), pltpu.SemaphoreType.DMA((N_CHIP * nw,))],
    )(*gs)


def _rs_final(fs, name):
    nw = len(fs)

    def body(*refs):
        outs = refs[nw:2 * nw]
        send_sems, recv_sems = refs[2 * nw:]
        x, y, c = _place()
        cps = []
        for w in range(nw):
            cp = pltpu.make_async_remote_copy(
                src_ref=outs[w].at[c], dst_ref=outs[w].at[c],
                send_sem=send_sems.at[w], recv_sem=recv_sems.at[w],
                device_id=(x, y, 1 - c), device_id_type=MESH)
            cp.start()
            cps.append(cp)
        for cp in cps:
            cp.wait()

    return pl.pallas_call(
        body, name=name,
        out_shape=[jax.ShapeDtypeStruct(f.shape, f.dtype) for f in fs],
        in_specs=[ANY] * nw, out_specs=[ANY] * nw,
        input_output_aliases={w: w for w in range(nw)},
        scratch_shapes=[pltpu.SemaphoreType.DMA((nw,)), pltpu.SemaphoreType.DMA((nw,))],
    )(*fs)


_PHASE_COPIES = {"gather_ici": 3, "gather_d2d": 3, "gather_chain": 6, "rs_sibling": N_CHIP, "rs_chips": 3}
QUARTERS = 4


def _ph(buf, legs, src=None):
    return dict(buf=buf, src=src, legs=legs)


def _n_copies(ph):
    return sum(_PHASE_COPIES[kind] for kind, _, _ in ph["legs"])


def _phase_copies(ph, src, buf, send_sems, recv_sems, base):
    x, y, c = _place()
    sibling = (x, y, 1 - c)
    chips = [(1 - x, y), (x, 1 - y), (1 - x, 1 - y)]
    r = buf.shape[1]

    def make(k, trip):
        a, b, dev = trip
        return pltpu.make_async_remote_copy(src_ref=a, dst_ref=b, send_sem=send_sems.at[base + k], recv_sem=recv_sems.at[base + k],
                                            device_id=dev, device_id_type=MESH)

    out = []
    for kind, lo, hi in ph["legs"]:
        rows = pl.ds(lo * r // QUARTERS, (hi - lo) * r // QUARTERS)
        ici = [(buf.at[4 * x + 2 * y + c, rows], buf.at[4 * x + 2 * y + c, rows], (*chip, c)) for chip in chips]
        d2d = [(buf.at[4 * chip[0] + 2 * chip[1] + c, rows], buf.at[4 * chip[0] + 2 * chip[1] + c, rows], sibling) for chip in chips]
        if kind == "gather_ici":
            trips, later = ici, []
        elif kind == "gather_d2d":
            trips, later = d2d, []
        elif kind == "gather_chain":
            trips, later = ici, d2d
        elif kind == "rs_sibling":
            trips, later = [(src.at[2 * s + (1 - c), rows], buf.at[s, rows], sibling) for s in range(N_CHIP)], []
        else:
            trips, later = [(src.at[2 * chip[0] + chip[1], rows], buf.at[j, rows], (*chip, c)) for j, chip in enumerate(chips)], []
        out.append(([make(k, t) for k, t in enumerate(trips)], [make(len(trips) + k, t) for k, t in enumerate(later)]))
        base += _PHASE_COPIES[kind]
    return out


def _side_call(inner, grid, in_specs, out_specs, out_shape, scratch, args, phases, name, semantics, mid=0.8):
    n_in, n_out, n_ph = len(in_specs), len(out_specs), len(phases)
    if n_ph == 0:
        outs = pl.pallas_call(inner, name=name, grid=grid, in_specs=in_specs, out_specs=out_specs, out_shape=out_shape,
                              scratch_shapes=scratch, compiler_params=_params(semantics))(*args)
        return list(outs), []
    n_cp = sum(_n_copies(p) for p in phases)
    side_args, buf_pos, src_pos = [], [], []
    for p in phases:
        buf_pos.append(len(side_args))
        side_args.append(p["buf"])
        src_pos.append(len(side_args) if p["src"] is not None else None)
        if p["src"] is not None:
            side_args.append(p["src"])
    n_side = len(side_args)
    total = math.prod(grid)
    mid_lin = min(total - 1, int(total * mid))

    def body(*refs):
        b_in, s_in = refs[:n_in], refs[n_in:n_in + n_side]
        b_out, s_out = refs[n_in + n_side:n_in + n_side + n_out], refs[n_in + n_side + n_out:n_in + n_side + n_out + n_ph]
        rest = refs[n_in + n_side + n_out + n_ph:]
        send_sems, recv_sems = rest[-2:]
        lin = functools.reduce(lambda acc, ag: acc * ag[1] + pl.program_id(ag[0]), list(enumerate(grid))[1:], pl.program_id(0))

        def copies():
            out, base = [], 0
            for p, sp, so in zip(phases, src_pos, s_out):
                out += _phase_copies(p, None if sp is None else s_in[sp], so, send_sems, recv_sems, base)
                base += _n_copies(p)
            return out

        @pl.when(lin == 0)
        def _():
            for a, _ in copies():
                for cp in a:
                    cp.start()

        if any(kind == "gather_chain" for p in phases for kind, _, _ in p["legs"]):
            @pl.when(lin == mid_lin)
            def _():
                for a, b in copies():
                    if b:
                        for cp in a:
                            cp.wait()
                        for cp in b:
                            cp.start()

        inner(*b_in, *b_out, *rest[:-2])

        @pl.when(lin == total - 1)
        def _():
            for a, b in copies():
                for cp in (b if b else a):
                    cp.wait()

    outs = pl.pallas_call(
        body, name=name, grid=grid,
        in_specs=list(in_specs) + [ANY] * n_side, out_specs=list(out_specs) + [ANY] * n_ph,
        out_shape=list(out_shape) + [jax.ShapeDtypeStruct(p["buf"].shape, p["buf"].dtype) for p in phases],
        input_output_aliases={n_in + bp: n_out + k for k, bp in enumerate(buf_pos)},
        scratch_shapes=list(scratch) + [pltpu.SemaphoreType.DMA((n_cp,)), pltpu.SemaphoreType.DMA((n_cp,))],
        compiler_params=_params(("arbitrary",) * len(grid)),
    )(*args, *side_args)
    return list(outs[:n_out]), list(outs[n_out:])


def _row_tile(r, cols, itemsize):
    if r * cols * itemsize <= (2 << 20):
        return r
    fits = [t for t in range(8, r, 8) if r % t == 0 and t * cols * itemsize <= (2 << 20)]
    return max(fits) if fits else r


def _transpose_cols(g, n_cols, name):
    rows, wp = g.shape
    tr = min(rows, 512)

    def body(g_ref, o_ref):
        o_ref[...] = g_ref[...].T[:n_cols]

    return pl.pallas_call(
        body, name=name, grid=(rows // tr,), out_shape=jax.ShapeDtypeStruct((n_cols, rows), F32),
        in_specs=[pl.BlockSpec((tr, wp), lambda i: (i, 0))], out_specs=pl.BlockSpec((n_cols, tr), lambda i: (0, i)),
        compiler_params=_params(("parallel",)),
    )(g)


def _sum_sibling(g, r1, c_me, name):
    _, r, cols = g.shape
    tr = _row_tile(r, cols, 4)

    def body(c_ref, g_ref, r_ref, o_ref):
        o_ref[...] = (g_ref[...].astype(F32) + r_ref[...].astype(F32)).astype(o_ref.dtype)

    return pl.pallas_call(
        body, name=name,
        out_shape=jax.ShapeDtypeStruct((N_CHIP, r, cols), g.dtype),
        grid_spec=pltpu.PrefetchScalarGridSpec(
            num_scalar_prefetch=1, grid=(N_CHIP, r // tr),
            in_specs=[pl.BlockSpec((None, tr, cols), lambda s, i, c: (2 * s + c[0], i, 0)),
                      pl.BlockSpec((None, tr, cols), lambda s, i, c: (s, i, 0))],
            out_specs=pl.BlockSpec((None, tr, cols), lambda s, i, c: (s, i, 0))),
        compiler_params=_params(("arbitrary", "arbitrary")),
    )(c_me, g, r1)


def _sum_chips(p, r2, sc_me, name):
    _, r, cols = p.shape
    tr = _row_tile(r, cols, 4)

    def body(s_ref, p_ref, a_ref, b_ref, c_ref, o_ref):
        o_ref[...] = ((p_ref[...].astype(F32) + a_ref[...].astype(F32)) + b_ref[...].astype(F32)) + c_ref[...].astype(F32)

    return pl.pallas_call(
        body, name=name,
        out_shape=jax.ShapeDtypeStruct((2, r, cols), F32),
        grid_spec=pltpu.PrefetchScalarGridSpec(
            num_scalar_prefetch=1, grid=(r // tr,),
            in_specs=[pl.BlockSpec((None, tr, cols), lambda i, s: (s[0], i, 0)),
                      pl.BlockSpec((None, tr, cols), lambda i, s: (0, i, 0)),
                      pl.BlockSpec((None, tr, cols), lambda i, s: (1, i, 0)),
                      pl.BlockSpec((None, tr, cols), lambda i, s: (2, i, 0))],
            out_specs=pl.BlockSpec((None, tr, cols), lambda i, s: (s[1], i, 0))),
        compiler_params=_params(("arbitrary",)),
    )(sc_me, p, r2, r2, r2)


def _shard_columns(g_al, shard_cols, bound, off_lo, off_hi, rh, name):
    d, acols = g_al.shape
    wp = -(-shard_cols // LANES) * LANES
    tr = min(LANES, rh)
    nt = acols // LANES

    def body(x_ref, o_ref):
        s = pl.program_id(1)
        lane = lax.broadcasted_iota(jnp.int32, (tr, LANES), 1)

        def tile(q):
            q = max(0, min(nt - 1, q))
            return x_ref[:, q * LANES:(q + 1) * LANES].astype(F32)

        def read(start):
            q, sh = divmod(start, LANES)
            if sh == 0:
                return tile(q)
            return jnp.where(lane < LANES - sh, pltpu.roll(tile(q), LANES - sh, 1), pltpu.roll(tile(q + 1), LANES - sh, 1))

        for k in range(N_CHIP):
            @pl.when(s == k)
            def _(k=k):
                for t in range(wp // LANES):
                    n0 = k * shard_cols + t * LANES
                    if n0 + LANES <= bound:
                        v = read(n0 + off_lo)
                    elif n0 >= bound:
                        v = read(n0 + off_hi)
                    else:
                        v = jnp.where(lane < bound - n0, read(n0 + off_lo), read(n0 + off_hi))
                    o_ref[:, t * LANES:(t + 1) * LANES] = v.astype(o_ref.dtype)

    return pl.pallas_call(
        body, name=name, grid=(d // tr, N_CHIP),
        out_shape=jax.ShapeDtypeStruct((N_DEV, rh, wp), BF16),
        in_specs=[pl.BlockSpec((tr, acols), lambda i, s: (i, 0))],
        out_specs=pl.BlockSpec((None, tr, wp), lambda i, s: (2 * s + (i * tr) // rh, ((i * tr) % rh) // tr, 0)),
        compiler_params=_params(("arbitrary", "arbitrary")),
    )(g_al)


def _mm(a, b, M, N, K, *, name, ta=False, tb=False, out_dtypes=(F32,), tm=1024, tn=1024, tk=2048,
        a_spec=None, b_spec=None, out_specs=None, out_shapes=None, epi=None, epi_in=(), epi_specs=(), side=(), side_mid=0.8):
    tm, tn, tk = min(tm, M), min(tn, N), min(tk, K)
    assert M % tm == 0 and N % tn == 0 and K % tk == 0, (name, M, N, K, tm, tn, tk)
    nk = K // tk
    n_epi, n_out = len(epi_in), len(out_dtypes)
    if a_spec is None:
        a_spec = pl.BlockSpec((tk, tm), lambda i, j, k: (k, i)) if ta else pl.BlockSpec((tm, tk), lambda i, j, k: (i, k))
    if b_spec is None:
        b_spec = pl.BlockSpec((tn, tk), lambda i, j, k: (j, k)) if tb else pl.BlockSpec((tk, tn), lambda i, j, k: (k, j))
    if out_specs is None:
        out_specs = [pl.BlockSpec((tm, tn), lambda i, j, k: (i, j))] * n_out
        out_shapes = [jax.ShapeDtypeStruct((M, N), dt) for dt in out_dtypes]

    def body(a_ref, b_ref, *rest):
        epi_refs, o_refs, acc = rest[:n_epi], rest[n_epi:n_epi + n_out], rest[-1]
        k = pl.program_id(2)

        @pl.when(k == 0)
        def _():
            acc[...] = jnp.zeros_like(acc)

        acc[...] += _dg(a_ref[...].astype(BF16), b_ref[...].astype(BF16), 0 if ta else 1, 1 if tb else 0)

        @pl.when(k == nk - 1)
        def _():
            r = acc[...]
            vals = (r,) if epi is None else epi(r, *[e[...] for e in epi_refs])
            for o_ref, v in zip(o_refs, vals):
                o_ref[...] = v.astype(o_ref.dtype)

    outs, side_outs = _side_call(body, (M // tm, N // tn, nk), [a_spec, b_spec, *epi_specs], out_specs, out_shapes,
                                 [pltpu.VMEM((tm, tn), F32)], (a, b, *epi_in), list(side), name,
                                 ("parallel", "parallel", "arbitrary"), mid=side_mid)
    if side:
        return outs + side_outs
    return outs[0] if n_out == 1 else outs


def _T(arr, width, col=0, lead=None):
    return ("tile", arr, width, col, lead)


def _W(arr):
    return ("whole", arr)


def _rows(fn, n_rows, tr, ins, tile_outs, acc_outs, name, stacked=None):
    tr = min(tr, n_rows)
    assert n_rows % tr == 0, (name, n_rows, tr)
    total_rows, first_row, earlier = stacked if stacked is not None else (n_rows, 0, None)
    assert first_row % tr == 0
    in_specs, args = [], []
    for d in ins:
        if d[0] == "tile":
            _, arr, width, col, lead = d
            if lead is None:
                in_specs.append(pl.BlockSpec((tr, width), lambda i, col=col: (i, col)))
            else:
                in_specs.append(pl.BlockSpec((None, tr, width), lambda i, col=col, lead=lead: (lead, i, col)))
            args.append(arr)
        else:
            arr = d[1]
            in_specs.append(pl.BlockSpec(arr.shape, lambda i, nd=arr.ndim: (0,) * nd))
            args.append(arr)
    n_in, n_t = len(ins), len(tile_outs)
    out_shape = [jax.ShapeDtypeStruct((n_rows, w), dt) for w, dt in tile_outs]
    out_specs = [pl.BlockSpec((tr, w), lambda i: (i, 0)) for w, _ in tile_outs]
    if stacked is not None:
        out_shape[0] = jax.ShapeDtypeStruct((total_rows, tile_outs[0][0]), tile_outs[0][1])
        out_specs[0] = pl.BlockSpec((tr, tile_outs[0][0]), lambda i: (first_row // tr + i, 0))
    if earlier is not None:
        in_specs.append(ANY)
        args.append(earlier)
    out_shape += [jax.ShapeDtypeStruct(s, F32) for s in acc_outs]
    out_specs += [pl.BlockSpec(s, lambda i, nd=len(s): (0,) * nd) for s in acc_outs]

    n_args = len(args)

    def body(*refs):
        in_refs, t_refs, a_refs = refs[:n_in], refs[n_args:n_args + n_t], refs[n_args + n_t:]
        vals = fn(*[r[...] for r in in_refs])
        for r, v in zip(t_refs, vals[:n_t]):
            r[...] = v.astype(r.dtype)
        first = pl.program_id(0) == 0
        for r, v in zip(a_refs, vals[n_t:]):
            @pl.when(first)
            def _(r=r, v=v):
                r[...] = v

            @pl.when(jnp.logical_not(first))
            def _(r=r, v=v):
                r[...] += v

    return pl.pallas_call(
        body, name=name, out_shape=out_shape, grid=(n_rows // tr,),
        in_specs=in_specs, out_specs=out_specs,
        input_output_aliases={n_in: 0} if earlier is not None else {},
        compiler_params=_params(("arbitrary",)),
    )(*args)


def _colsum(t):
    return jnp.sum(t, axis=0, keepdims=True)


def _prenorm(x, g, sc, sh):
    xf = x.astype(F32)
    return xf * lax.rsqrt(jnp.mean(xf * xf, axis=-1, keepdims=True) + EPS) * g * (1.0 + sc) + sh


def _postnorm(x, y, gate, pg):
    return x + gate * (y * lax.rsqrt(jnp.mean(y * y, axis=-1, keepdims=True) + EPS) * pg)


def _gelu(t):
    return 0.5 * t * (1.0 + lax.erf(t * (2.0 ** -0.5)))


def _sg_pre(zu, zv, lng, lnb):
    u, vr = _gelu(zu), _gelu(zv)
    mu = jnp.mean(vr, axis=-1, keepdims=True)
    var = jnp.mean(jnp.square(vr - mu), axis=-1, keepdims=True)
    return u, (vr - mu) * lax.rsqrt(var + EPS) * lng + lnb


def _sg_mix(vv, ws_ref_vals, bs_vals, gw):
    parts = []
    for g in range(SG_GROUPS):
        s = _dg(ws_ref_vals[g].astype(BF16), vv[:, g * gw:(g + 1) * gw].astype(BF16), 1, 0)
        parts.append(s + bs_vals[g])
    return jnp.concatenate(parts, axis=1)


def _readout(o, r, g, heads, dv):
    parts = []
    for h in range(heads):
        oh = o[:, h * dv:(h + 1) * dv]
        parts.append(oh * lax.rsqrt(jnp.mean(oh * oh, axis=-1, keepdims=True) + EPS))
    return jnp.concatenate(parts, axis=1) * g * (r * jax.nn.sigmoid(r))


def _log_sigmoid(a):
    return jnp.minimum(a, 0.0) - jnp.log(1.0 + jnp.exp(-jnp.abs(a)))


def _rope_swap(t, m):
    lane = lax.broadcasted_iota(jnp.int32, t.shape, 1)
    return jnp.where((lane % (2 * m)) < m, pltpu.roll(t, 3 * m, 1), pltpu.roll(t, m, 1))


def _rope(t, cos, sin, heads, dk):
    parts = []
    for h in range(heads):
        th = t[:, h * dk:(h + 1) * dk]
        parts.append(th * cos + _rope_swap(th, dk // 4) * sin)
    return jnp.concatenate(parts, axis=1)


def _rope_t(dt, cos, sin, heads, dk):
    parts = []
    for h in range(heads):
        dh = dt[:, h * dk:(h + 1) * dk]
        parts.append(dh * cos + _rope_swap(dh * sin, dk // 4))
    return jnp.concatenate(parts, axis=1)


def _chunk_cumsum(t, upwards):
    n = t.shape[0]
    row = lax.broadcasted_iota(jnp.int32, (n, n), 0)
    col = lax.broadcasted_iota(jnp.int32, (n, n), 1)
    shift = GLA_CHUNK.bit_length() - 1
    same = jnp.right_shift(row, shift) == jnp.right_shift(col, shift)
    tri = jnp.logical_and(same, col <= row if upwards else col >= row)
    return _dg(tri.astype(F32), t, 1, 0, precision=lax.Precision.HIGHEST)


def _chunk_terms(d, qv, kv, b, C):
    row = lax.broadcasted_iota(jnp.int32, (C, C), 0)
    col = lax.broadcasted_iota(jnp.int32, (C, C), 1)
    tri = row >= col if d == 0 else row <= col
    end_row = lax.broadcasted_iota(jnp.int32, b.shape, 0) == (C - 1 if d == 0 else 0)
    btot = _colsum(jnp.where(end_row, b, 0.0))
    eb, enb, ebt = jnp.exp(b), jnp.exp(-b), jnp.exp(btot - b)
    return tri, btot, eb, enb, ebt, qv * eb, kv * enb, kv * ebt


def _gla_fwd(q, k, zv, v_col0, la, st0, heads, dk, dv, name, side=(), side_mid=0.8):
    n, C, H = q.shape[0], GLA_CHUNK, heads
    nc = n // C

    def body(qf, kf, vf, laf, qb, kb, vb_, lab, st0_ref, of_ref, ob_ref, sf_ref, sb_ref, fin_ref, st):
        i = pl.program_id(1)

        @pl.when(i == 0)
        def _():
            st[...] = st0_ref[...]

        for d, (q_ref, k_ref, v_ref, la_ref, o_ref, save_ref) in enumerate(((qf, kf, vf, laf, of_ref, sf_ref), (qb, kb, vb_, lab, ob_ref, sb_ref))):
            tri, btot, _, _, _, qt, kt, kh = _chunk_terms(d, q_ref[...], k_ref[...], la_ref[...], C)
            s = st[d]
            vb = v_ref[...].astype(BF16)
            qtb = qt.astype(BF16)
            att = jnp.where(tri, _dg(qtb, kt.astype(BF16), 1, 1), 0.0)
            o_ref[...] = _dg(qtb, s.astype(BF16), 1, 1) + _dg(att.astype(BF16), vb, 1, 0)
            save_ref[...] = s
            s_new = s * jnp.exp(btot) + _dg(vb, kh.astype(BF16), 0, 0)
            st[d] = s_new

            @pl.when(i == nc - 1)
            def _(d=d, s_new=s_new):
                fin_ref[d] = s_new

    def seq(width, col0, rev, dir_cols=0):
        if rev:
            return pl.BlockSpec((C, width), lambda h, i: (nc - 1 - i, col0 + dir_cols + h))
        return pl.BlockSpec((C, width), lambda h, i: (i, col0 + h))

    both = pl.BlockSpec((2, None, dv, dk), lambda h, i: (0, h, 0, 0))
    outs, side_outs = _side_call(
        body, (H, nc),
        [seq(dk, 0, False), seq(dk, 0, False), seq(dv, v_col0, False), seq(dk, 0, False),
         seq(dk, 0, True), seq(dk, 0, True), seq(dv, v_col0, True), seq(dk, 0, True, H), both],
        [seq(dv, 0, False), seq(dv, 0, True),
         pl.BlockSpec((None, None, dv, dk), lambda h, i: (h, i, 0, 0)),
         pl.BlockSpec((None, None, dv, dk), lambda h, i: (h, nc - 1 - i, 0, 0)), both],
        [jax.ShapeDtypeStruct((n, H * dv), F32)] * 2 + [jax.ShapeDtypeStruct((H, nc, dv, dk), F32)] * 2
        + [jax.ShapeDtypeStruct((2, H, dv, dk), F32)],
        [pltpu.VMEM((2, dv, dk), F32)], (q, k, zv, la, q, k, zv, la, st0), list(side), name, ("arbitrary", "arbitrary"),
        mid=side_mid)
    return outs + side_outs


def _gla_bwd(q, k, zv, v_col0, la, saved_f, saved_b, do, dfin, heads, dk, dv, name, side=()):
    n, C, H = q.shape[0], GLA_CHUNK, heads
    nc = n // C

    def body(qf, kf, vf, laf, sf, dof, qb, kb, vb_, lab, sb, dob_, dfin_ref,
             dqf, dqb, dkf, dkb, dvf, dvb, dlaf, dlab, d0_ref, dst):
        i = pl.program_id(1)

        @pl.when(i == 0)
        def _():
            dst[...] = dfin_ref[...]

        dirs = ((qf, kf, vf, laf, sf, dof, dqf, dkf, dvf, dlaf), (qb, kb, vb_, lab, sb, dob_, dqb, dkb, dvb, dlab))
        for d, (q_ref, k_ref, v_ref, la_ref, save_ref, do_ref, dq_ref, dk_ref, dv_ref, dla_ref) in enumerate(dirs):
            tri, btot, eb, enb, ebt, qt, kt, kh = _chunk_terms(d, q_ref[...], k_ref[...], la_ref[...], C)
            s, dsn = save_ref[...], dst[d]
            vb, dob = v_ref[...].astype(BF16), do_ref[...].astype(BF16)
            qtb, ktb, khb, dsnb = qt.astype(BF16), kt.astype(BF16), kh.astype(BF16), dsn.astype(BF16)
            att = jnp.where(tri, _dg(qtb, ktb, 1, 1), 0.0).astype(BF16)
            datt = jnp.where(tri, _dg(dob, vb, 1, 1), 0.0).astype(BF16)
            dqt = _dg(dob, s.astype(BF16), 1, 0) + _dg(datt, ktb, 1, 0)
            dkt = _dg(datt, qtb, 0, 0)
            dkh = _dg(vb, dsnb, 1, 0)
            dv_ref[...] = _dg(att, dob, 0, 0) + _dg(khb, dsnb, 1, 1)
            ebtot = jnp.exp(btot)
            dbtot = ebtot * _colsum(s * dsn) + _colsum(dkh * kh)
            s0 = dsn * ebtot + _dg(dob, qtb, 0, 0)
            dst[d] = s0
            db = dqt * qt - dkt * kt - dkh * kh
            dq_ref[...] = dqt * eb
            dk_ref[...] = dkt * enb + dkh * ebt
            end_row = lax.broadcasted_iota(jnp.int32, db.shape, 0) == (C - 1 if d == 0 else 0)
            dla_ref[...] = db + jnp.where(end_row, dbtot, 0.0)

            @pl.when(i == nc - 1)
            def _(d=d, s0=s0):
                d0_ref[d] = s0

    def seq(width, col0, fwd_dir, dir_cols=0):
        if fwd_dir:
            return pl.BlockSpec((C, width), lambda h, i: (nc - 1 - i, col0 + h))
        return pl.BlockSpec((C, width), lambda h, i: (i, col0 + dir_cols + h))

    both = pl.BlockSpec((2, None, dv, dk), lambda h, i: (0, h, 0, 0))
    sav_f = pl.BlockSpec((None, None, dv, dk), lambda h, i: (h, nc - 1 - i, 0, 0))
    sav_b = pl.BlockSpec((None, None, dv, dk), lambda h, i: (h, i, 0, 0))
    outs, side_outs = _side_call(
        body, (H, nc),
        [seq(dk, 0, True), seq(dk, 0, True), seq(dv, v_col0, True), seq(dk, 0, True), sav_f, seq(dv, 0, True),
         seq(dk, 0, False), seq(dk, 0, False), seq(dv, v_col0, False), seq(dk, 0, False, H), sav_b, seq(dv, 0, False), both],
        [seq(dk, 0, True), seq(dk, 0, False), seq(dk, 0, True), seq(dk, 0, False), seq(dv, 0, True), seq(dv, 0, False),
         seq(dk, 0, True), seq(dk, 0, False), both],
        [jax.ShapeDtypeStruct((n, H * dk), F32)] * 4 + [jax.ShapeDtypeStruct((n, H * dv), F32)] * 2
        + [jax.ShapeDtypeStruct((n, H * dk), F32)] * 2 + [jax.ShapeDtypeStruct((2, H, dv, dk), F32)],
        [pltpu.VMEM((2, dv, dk), F32)], (q, k, zv, la, saved_f, do, q, k, zv, la, saved_b, do, dfin), list(side), name,
        ("arbitrary", "arbitrary"))
    return outs + side_outs


def _adamw_math(w, g, m, v):
    m2 = ADAM_B1 * m + (1.0 - ADAM_B1) * g
    v2 = ADAM_B2 * v + (1.0 - ADAM_B2) * jnp.square(g)
    m_hat = m2 / (1.0 - ADAM_B1 ** ADAM_STEP)
    v_hat = v2 / (1.0 - ADAM_B2 ** ADAM_STEP)
    delta = -ADAM_LR * (m_hat / (jnp.sqrt(v_hat) + ADAM_EPS) + ADAM_WD * w)
    return delta, m2, v2


def _adamw(w, g, m, v, name):
    r, cols = w.shape
    tr = _row_tile(r, cols, 4 * 4)
    spec = pl.BlockSpec((tr, cols), lambda i: (i, 0))

    def body(w_ref, g_ref, m_ref, v_ref, d_ref, m2_ref, v2_ref):
        d_ref[...], m2_ref[...], v2_ref[...] = _adamw_math(w_ref[...], g_ref[...], m_ref[...], v_ref[...])

    return _side_call(body, (r // tr,), [spec] * 4, [spec] * 3, [jax.ShapeDtypeStruct((r, cols), F32)] * 3,
                      [], (w, g, m, v), [], name, ("parallel",))[0]


def _adamw_many(ws, gs, ms, vs, name):
    n = len(ws)

    def body(*refs):
        ins, outs = refs[:4 * n], refs[4 * n:]
        for k in range(n):
            d, m2, v2 = _adamw_math(ins[k][...], ins[n + k][...], ins[2 * n + k][...], ins[3 * n + k][...])
            outs[k][...], outs[n + k][...], outs[2 * n + k][...] = d, m2, v2

    outs = pl.pallas_call(
        body, name=name, out_shape=[jax.ShapeDtypeStruct(w.shape, F32) for w in ws] * 3,
        in_specs=[pl.BlockSpec(memory_space=pltpu.VMEM)] * (4 * n), out_specs=[pl.BlockSpec(memory_space=pltpu.VMEM)] * (3 * n),
        compiler_params=_params(),
    )(*ws, *gs, *ms, *vs)
    return outs[:n], outs[n:2 * n], outs[2 * n:]


def _ada_update(cond_t, dmod, w, m, v, name):
    r, cols = w.shape
    tr, tc = _pick(r, 512, 8), _pick(cols, 1024)
    spec = pl.BlockSpec((tr, tc), lambda i, j: (i, j))

    def body(c_ref, d_ref, w_ref, m_ref, v_ref, g_ref, dl_ref, m2_ref, v2_ref):
        g = _dg(c_ref[...].astype(BF16), d_ref[...].astype(BF16), 1, 0)
        g_ref[...] = g
        dl_ref[...], m2_ref[...], v2_ref[...] = _adamw_math(w_ref[...], g, m_ref[...], v_ref[...])

    return _side_call(
        body, (r // tr, cols // tc),
        [pl.BlockSpec((tr, cond_t.shape[1]), lambda i, j: (i, 0)), pl.BlockSpec((dmod.shape[0], tc), lambda i, j: (0, j)), spec, spec, spec],
        [spec] * 4, [jax.ShapeDtypeStruct((r, cols), F32)] * 4, [], (cond_t, dmod, w, m, v), [], name, ("parallel", "parallel"))[0]


def _pack(parts, rows=8):
    flat = jnp.concatenate([p.reshape(-1).astype(F32) for p in parts])
    n = -(-flat.shape[0] // (rows * LANES)) * LANES
    return jnp.pad(flat, (0, rows * n - flat.shape[0])).reshape(rows, n)


def _unpack(flat, shapes):
    out, off = [], 0
    for s in shapes:
        size = math.prod(s)
        out.append(flat[off:off + size].reshape(s))
        off += size
    return out


def kernel(x, c, ctx, c_ctx, w_ada, b_ada, pre1_g, post1_g, pre2_g, post2_g, w_in, w_dec_f, b_dec_f, w_dec_b, b_dec_b, gla_norm_g, sg_ln_g, sg_ln_b, w_s, b_s, w_o, w_1, w_2, loss_target, m_c_ctx, m_w_ada, m_b_ada, m_pre1_g, m_post1_g, m_pre2_g, m_post2_g, m_w_in, m_w_dec_f, m_b_dec_f, m_w_dec_b, m_b_dec_b, m_gla_norm_g, m_sg_ln_g, m_sg_ln_b, m_w_s, m_b_s, m_w_o, m_w_1, m_w_2, v_c_ctx, v_w_ada, v_b_ada, v_pre1_g, v_post1_g, v_pre2_g, v_post2_g, v_w_in, v_w_dec_f, v_b_dec_f, v_w_dec_b, v_b_dec_b, v_gla_norm_g, v_sg_ln_g, v_sg_ln_b, v_w_s, v_b_s, v_w_o, v_w_1, v_w_2):
    N, D = x.shape[1], x.shape[2]
    NC = ctx.shape[1]
    H = GLA_HEADS
    VALW = D // 2
    DV = VALW // H
    DK = DV // 2
    KEYW = H * DK
    SGW = D - VALW
    GW = SGW // SG_GROUPS
    LR = 2 * GLA_LOWRANK
    F = w_1.shape[2] * N_CHIP
    FS = F // N_CHIP
    RH = D // 2
    MS = w_ada.shape[2]
    IN_COLS = w_in.shape[2] * N_CHIP
    K0, V0, R0, LF0 = KEYW, 2 * KEYW, 2 * KEYW + VALW, 2 * KEYW + 2 * VALW
    SG0 = LF0 + LR
    AQ, AK, AV, AR, ALR = 2 * SGW, 2 * SGW + KEYW, 2 * SGW + 2 * KEYW, 2 * SGW + 2 * KEYW + VALW, 2 * SGW + 2 * KEYW + 2 * VALW
    ACOLS = ALR + 4 * LANES
    LR_PAD = ACOLS - ALR
    assert IN_COLS == SG0 + 2 * SGW and N % SG_CHUNK == 0 and N % GLA_CHUNK == 0 and NC % GLA_CHUNK == 0

    ax, ay, ac = _place()
    s_me = (2 * ax + ay).astype(jnp.int32)
    b_me = (4 * ax + 2 * ay + ac).astype(jnp.int32)
    s_arr, c_arr = s_me.reshape(1), ac.astype(jnp.int32).reshape(1)
    sc_arr = jnp.concatenate([s_arr, c_arr])
    CS = IN_COLS // N_CHIP

    shards = [_cast_blocks(w_in[0].reshape(2, RH, CS), s_arr, "cast_w_in"), _cast_blocks(w_o[0].reshape(2, D // N_DEV, D), s_arr, "cast_w_o"),
              _cast_blocks(w_1[0].reshape(2, RH, FS), s_arr, "cast_w_1"), _cast_blocks(w_2[0].reshape(2, F // N_DEV, D), s_arr, "cast_w_2")]
    win_g, = _gather_big(shards[:1], "gather_weights")
    wo_buf, w1_buf, w2_buf = shards[1], shards[2], shards[3]
    w_in_nat = win_g.reshape(N_CHIP, 2, RH, IN_COLS // N_CHIP).transpose(1, 2, 0, 3).reshape(D, IN_COLS)
    w_al = jnp.concatenate([w_in_nat[:, SG0:], w_in_nat[:, :LF0], w_in_nat[:, LF0:SG0],
                            jnp.zeros((D, LR_PAD - LR), BF16)], axis=1)

    n_dec = GLA_LOWRANK * (KEYW // N_CHIP)
    g0 = _allgather_small(_pack([c, w_dec_f, w_dec_b, gla_norm_g]), "gather_small0").reshape(N_DEV, -1)
    c_all = g0[:, :D]
    per_chip = g0[0::2]
    wdf = per_chip[:, D:D + n_dec].reshape(N_CHIP, GLA_LOWRANK, KEYW // N_CHIP).transpose(1, 0, 2).reshape(GLA_LOWRANK, KEYW)
    wdb = per_chip[:, D + n_dec:D + 2 * n_dec].reshape(N_CHIP, GLA_LOWRANK, KEYW // N_CHIP).transpose(1, 0, 2).reshape(GLA_LOWRANK, KEYW)
    gn_full = per_chip[:, D + 2 * n_dec:D + 2 * n_dec + H * (DV // N_CHIP)].reshape(N_CHIP, H, DV // N_CHIP).transpose(1, 0, 2).reshape(1, VALW)
    wd_f = jnp.zeros((LANES, KEYW), F32).at[:GLA_LOWRANK].set(wdf)
    wd_b = jnp.zeros((LANES, KEYW), F32).at[GLA_LOWRANK:LR].set(wdb)

    cond_in = jnp.zeros((16, D), F32).at[:N_DEV].set(c_all).at[N_DEV].set(c_ctx)
    b_ada_sh = lax.dynamic_slice(b_ada, (0, s_me * MS), (1, MS))

    def mod_epi(r, bias):
        return (r + bias,)

    def silu_rows(t):
        return (t * jax.nn.sigmoid(t),)

    cond = _rows(silu_rows, 16, 16, [_W(cond_in)], [(D, F32)], [], "cond_silu")[0]
    mod_sh = _mm(cond, w_ada[0], 16, MS, D, name="mod_matmul", tn=512, tk=D, epi=mod_epi, epi_in=(b_ada_sh,),
                 epi_specs=(pl.BlockSpec((1, min(512, MS)), lambda i, j, k: (0, j)),))
    g1m = _allgather_small(mod_sh, "gather_mod").reshape(N_DEV, 16, MS)[0::2]
    mod_all = g1m.transpose(1, 0, 2).reshape(16, N_CHIP * MS)
    mod_me = lax.dynamic_slice(mod_all, (b_me, 0), (1, N_MOD * D))
    sh1, sc1, gt1, sh2, sc2, gt2 = [mod_me[:, i * D:(i + 1) * D] for i in range(N_MOD)]
    csh1, csc1 = mod_all[N_DEV:N_DEV + 1, :D], mod_all[N_DEV:N_DEV + 1, D:2 * D]

    mq = DK // 4
    pos = np.arange(N)
    inv_freq = (np.float32(ROPE_BASE) ** (-np.arange(mq, dtype=np.float32) / np.float32(mq))).astype(np.float32)
    ang_r = (pos // GRID_W).astype(np.float32)[:, None] * inv_freq[None, :]
    ang_c = (pos % GRID_W).astype(np.float32)[:, None] * inv_freq[None, :]
    cos_t = jnp.asarray(np.concatenate([np.cos(ang_r), np.cos(ang_r), np.cos(ang_c), np.cos(ang_c)], axis=1), F32)
    sin_t = jnp.asarray(np.concatenate([-np.sin(ang_r), np.sin(ang_r), -np.sin(ang_c), np.sin(ang_c)], axis=1), F32)

    x2, tgt, ctx2 = x[0], loss_target[0], ctx[0]
    TR = 128
    qscale = DK ** -0.5

    def prenorm_fwd(xa, g, sc, sh, n_rows, name, stacked):
        return _rows(lambda xv, gv, scv, shv: (_prenorm(xv, gv, scv, shv),), n_rows, TR,
                     [_T(xa, D), _W(g), _W(sc), _W(sh)], [(D, BF16)], [], name, stacked=stacked)[0]

    h_cat = prenorm_fwd(x2, pre1_g, sc1, sh1, N, "prenorm1_x", (N + NC, 0, None))
    h_cat = prenorm_fwd(ctx2, pre1_g, csc1, csh1, NC, "prenorm1_ctx", (N + NC, N, h_cat))
    tka = _pick(ACOLS, 1536, 2 * LANES)
    tmc = min(1024, NC)
    z_al, w1_buf = _mm(h_cat, w_al, N, ACOLS, D, name="in_proj_x", tn=tka, side=[_ph(w1_buf, [("gather_ici", 0, 2)])])
    zc_al = _mm(h_cat, w_al, NC, ACOLS, D, name="in_proj_ctx", tn=tka,
                a_spec=pl.BlockSpec((tmc, min(2048, D)), lambda i, j, k: (N // tmc + i, k)))

    def decay(lr, wdf_v, wdb_v, bf_v, bb_v):
        lrb = lr.astype(BF16)
        a_f = _dg(lrb, wdf_v.astype(BF16), 1, 0) + bf_v
        a_b = _dg(lrb, wdb_v.astype(BF16), 1, 0) + bb_v
        return a_f, a_b

    def running_decay(a_f, a_b):
        return jnp.concatenate([_chunk_cumsum(_log_sigmoid(a_f) / GLA_TAU, True), _chunk_cumsum(_log_sigmoid(a_b) / GLA_TAU, False)], axis=1)

    def prep_x(zq, zk, lr, cs, sn, wdf_v, wdb_v, bf_v, bb_v):
        return _rope(zq * qscale, cs, sn, H, DK), _rope(zk, cs, sn, H, DK), running_decay(*decay(lr, wdf_v, wdb_v, bf_v, bb_v))

    def prep_c(zk, lr, wdf_v, wdb_v, bf_v, bb_v):
        return zk, running_decay(*decay(lr, wdf_v, wdb_v, bf_v, bb_v))

    dec_w = [_W(wd_f), _W(wd_b), _W(b_dec_f), _W(b_dec_b)]
    q_r, k_r, la_x = _rows(prep_x, N, TR, [_T(z_al, KEYW, AQ // KEYW), _T(z_al, KEYW, AK // KEYW), _T(z_al, LANES, ALR // LANES),
                                           _T(cos_t, DK), _T(sin_t, DK)] + dec_w,
                           [(KEYW, F32), (KEYW, F32), (2 * KEYW, F32)], [], "gla_prep_x")
    k_c, la_c = _rows(prep_c, NC, TR, [_T(zc_al, KEYW, AK // KEYW), _T(zc_al, LANES, ALR // LANES)] + dec_w,
                      [(KEYW, F32), (2 * KEYW, F32)], [], "gla_prep_ctx")

    zero_state = jnp.zeros((2, H, DV, DK), F32)
    q_c = jnp.zeros((NC, KEYW), F32)
    _, _, savf_c, savb_c, st_c = _gla_fwd(q_c, k_c, zc_al, AV // DV, la_c, zero_state, H, DK, DV, "gla_fwd_ctx")
    o_f, o_b, savf_x, savb_x, _, wo_g, w1_buf = _gla_fwd(
        q_r, k_r, z_al, AV // DV, la_x, st_c, H, DK, DV, "gla_fwd_x",
        side=[_ph(wo_buf, [("gather_chain", 0, 4)]), _ph(w1_buf, [("gather_d2d", 0, 2), ("gather_ici", 2, 4)])], side_mid=0.55)
    w_o_f = wo_g.reshape(D, D)

    def readout_fwd(of, ob, r, g):
        return (_readout(of + ob, r, g, H, DV),)

    y_gla = _rows(readout_fwd, N, TR, [_T(o_f, VALW), _T(o_b, VALW), _T(z_al, VALW, AR // VALW), _W(gn_full)],
                  [(VALW, BF16)], [], "gla_readout")[0]

    bs_col = b_s[0].reshape(SG_GROUPS, SG_CHUNK, 1)

    def sg_fwd(zu, zv, lng, lnb, ws, bs):
        u, vv = _sg_pre(zu, zv, lng, lnb)
        return (u * _sg_mix(vv, ws, bs, GW),)

    y_sg = _rows(sg_fwd, N, SG_CHUNK, [_T(z_al, SGW, 0), _T(z_al, SGW, 1), _W(sg_ln_g), _W(sg_ln_b), _W(w_s[0]), _W(bs_col)],
                 [(SGW, BF16)], [], "sg_fwd")[0]
    ycat = jnp.concatenate([y_gla, y_sg], axis=1)
    y, w1_g, w2_buf = _mm(ycat, w_o_f, N, D, D, name="out_proj",
                          side=[_ph(w1_buf, [("gather_d2d", 2, 4)]), _ph(w2_buf, [("gather_ici", 0, 1)])])
    def between_sublayers(xv, yv, gv, pv, g2v, scv, shv):
        x1v = _postnorm(xv, yv, gv, pv)
        return x1v, _prenorm(x1v, g2v, scv, shv)

    x1, h2 = _rows(between_sublayers, N, TR, [_T(x2, D), _T(y, D), _W(gt1), _W(post1_g), _W(pre2_g), _W(sc2), _W(sh2)],
                   [(D, F32), (D, BF16)], [], "postnorm1_prenorm2")

    tm1, tn1, tk1 = min(1024, N), min(1024, FS), min(2048, RH)
    w1_fwd_spec = pl.BlockSpec((None, tk1, tn1), lambda i, j, k: (2 * ((j * tn1) // FS) + (k * tk1) // RH, ((k * tk1) % RH) // tk1, ((j * tn1) % FS) // tn1))

    def relu2_epi(r):
        rf = jnp.maximum(r, 0.0)
        return rf * rf, rf

    act, rf, w2_g = _mm(h2, w1_g, N, F, D, name="mlp_up", out_dtypes=(BF16, BF16), tm=tm1, tn=tn1, tk=tk1, b_spec=w1_fwd_spec, epi=relu2_epi,
                        side=[_ph(w2_buf, [("gather_d2d", 0, 1), ("gather_chain", 1, 4)])], side_mid=0.88)
    w_2_f = w2_g.reshape(F, D)
    y2 = _mm(act, w_2_f, N, D, F, name="mlp_down")

    def final(x1v, y2v, gv, pv, tv):
        def loss_fn(x1a, y2a, ga, pa):
            err = _postnorm(x1a, y2a, ga, pa) - tv
            return 0.5 * jnp.sum(jnp.mean(err * err, axis=-1))
        loss, grads = jax.value_and_grad(loss_fn, argnums=(0, 1, 2, 3))(x1v, y2v, gv, pv)
        return grads[0], grads[1], jnp.full((1, LANES), loss, F32), _colsum(grads[2]), _colsum(grads[3])

    dx2, dy2, loss_acc, dgt2, dpost2 = _rows(final, N, TR, [_T(x1, D), _T(y2, D), _W(gt2), _W(post2_g), _T(tgt, D)],
                                             [(D, F32), (D, BF16)], [(1, LANES), (1, D), (1, D)], "loss_postnorm2_bwd")

    df = _mm(dy2, w_2_f, N, F, D, name="mlp_down_dx", tb=True, out_dtypes=(BF16,), epi=lambda r, rfv: (r * (2.0 * rfv.astype(F32)),),
             epi_in=(rf,), epi_specs=(pl.BlockSpec((min(1024, N), min(1024, F)), lambda i, j, k: (i, j)),))
    dw2 = _mm(act, dy2, F, D, N, name="mlp_down_dw", ta=True, out_dtypes=(BF16,)).reshape(N_DEV, F // N_DEV, D)
    tnb, tkb = min(1024, D, RH), min(2048, FS)
    w1_bwd_spec = pl.BlockSpec((None, tnb, tkb), lambda i, j, k: (2 * ((k * tkb) // FS) + (j * tnb) // RH, ((j * tnb) % RH) // tnb, ((k * tkb) % FS) // tkb))
    dh2, recv1_w2 = _mm(df, w1_g, N, D, F, name="mlp_up_dx", tb=True, tn=tnb, tk=tkb, b_spec=w1_bwd_spec,
                         side=[_ph(lax.empty((N_CHIP,) + dw2.shape[1:], BF16), [("rs_sibling", 0, 4)], src=dw2)])
    part_w2 = _sum_sibling(dw2, recv1_w2, c_arr, "rs_sum_sibling_w_2")
    tmw, tnw = min(1024, RH), min(1024, FS)
    dw1_spec = pl.BlockSpec((None, tmw, tnw), lambda i, j, k: (2 * ((j * tnw) // FS) + (i * tmw) // RH, ((i * tmw) % RH) // tmw, ((j * tnw) % FS) // tnw))
    dw1, recv2_w2 = _mm(h2, df, D, F, N, name="mlp_up_dw", ta=True, tm=tmw, tn=tnw, out_dtypes=(BF16,), out_specs=[dw1_spec],
                        out_shapes=[jax.ShapeDtypeStruct((N_DEV, RH, FS), BF16)],
                        side=[_ph(lax.empty((3,) + part_w2.shape[1:], BF16), [("rs_chips", 0, 3)], src=part_w2)])

    def prenorm_bwd(xv, gv, scv, shv, dh, dres):
        _, vjp = jax.vjp(_prenorm, xv, gv, scv, shv)
        dx, dg, dsc, dsh = vjp(dh)
        return dx + dres, _colsum(dg), _colsum(dsc), _colsum(dsh)

    def between_sublayers_bwd(xv, gv, scv, shv, dh, dres, yv, g1v, pv):
        dx1v, dg, dsc, dsh = prenorm_bwd(xv, gv, scv, shv, dh, dres)
        _, vjp = jax.vjp(lambda ya, ga, pa: _postnorm(0.0, ya, ga, pa), yv, g1v, pv)
        dy_, dg1_, dp_ = vjp(dx1v)
        return dx1v, dy_, dg, dsc, dsh, _colsum(dg1_), _colsum(dp_)

    dx1, dy, dpre2, dsc2, dsh2, dgt1, dpost1 = _rows(
        between_sublayers_bwd, N, TR, [_T(x1, D), _W(pre2_g), _W(sc2), _W(sh2), _T(dh2, D), _T(dx2, D), _T(y, D), _W(gt1), _W(post1_g)],
        [(D, F32), (D, BF16)], [(1, D)] * 5, "prenorm2_postnorm1_bwd")
    dycat, recv2_w2, recv1_w1 = _mm(dy, w_o_f, N, D, D, name="out_proj_dx", tb=True,
                                    side=[_ph(recv2_w2, [("rs_chips", 3, 4)], src=part_w2),
                                          _ph(lax.empty((N_CHIP,) + dw1.shape[1:], BF16), [("rs_sibling", 0, 4)], src=dw1)])
    part_w1 = _sum_sibling(dw1, recv1_w1, c_arr, "rs_sum_sibling_w_1")
    dwo = _mm(ycat, dy, D, D, N, name="out_proj_dw", ta=True, out_dtypes=(BF16,)).reshape(N_DEV, D // N_DEV, D)

    def readout_bwd(of, ob, r, g, dyv):
        _, vjp = jax.vjp(lambda o_, r_, g_: _readout(o_, r_, g_, H, DV), of + ob, r, g)
        do_, dr_, dg_ = vjp(dyv)
        return do_, dr_, _colsum(dg_)

    do_x, dz_r, dgn = _rows(readout_bwd, N, TR, [_T(o_f, VALW), _T(o_b, VALW), _T(z_al, VALW, AR // VALW), _W(gn_full), _T(dycat, VALW, 0)],
                            [(VALW, F32), (VALW, BF16)], [(1, VALW)], "gla_readout_bwd")

    def sg_bwd(zu, zv, lng, lnb, ws, bs, dyv):
        (u, vv), vjp = jax.vjp(_sg_pre, zu, zv, lng, lnb)
        s = _sg_mix(vv, ws, bs, GW)
        du, ds = dyv * s, dyv * u
        dws, dbs, dvv = [], [], []
        for g in range(SG_GROUPS):
            dsg = ds[:, g * GW:(g + 1) * GW]
            dsb = dsg.astype(BF16)
            dws.append(_dg(dsb, vv[:, g * GW:(g + 1) * GW].astype(BF16), 1, 1))
            dbs.append(jnp.sum(dsg, axis=1, keepdims=True))
            dvv.append(_dg(ws[g].astype(BF16), dsb, 0, 0))
        dzu, dzv, dlng, dlnb = vjp((du, jnp.concatenate(dvv, axis=1)))
        return jnp.concatenate([dzu, dzv], axis=1), _colsum(dlng), _colsum(dlnb), jnp.concatenate(dws, axis=0), jnp.concatenate(dbs, axis=0)

    dz_sg, dlng, dlnb, dws, dbs = _rows(sg_bwd, N, SG_CHUNK, [_T(z_al, SGW, 0), _T(z_al, SGW, 1), _W(sg_ln_g), _W(sg_ln_b), _W(w_s[0]), _W(bs_col), _T(dycat, SGW, VALW // SGW)],
                                        [(2 * SGW, BF16)], [(1, SGW), (1, SGW), (SG_GROUPS * SG_CHUNK, SG_CHUNK), (SG_GROUPS * SG_CHUNK, 1)], "sg_bwd")

    dq_f, dq_b, dk_f, dk_b, dv_f, dv_b, dla_f, dla_b, dst0, recv2_w1, recv1_wo = _gla_bwd(
        q_r, k_r, z_al, AV // DV, la_x, savf_x, savb_x, do_x, zero_state, H, DK, DV, "gla_bwd_x",
        side=[_ph(lax.empty((3,) + part_w1.shape[1:], BF16), [("rs_chips", 0, 3)], src=part_w1),
              _ph(lax.empty((N_CHIP,) + dwo.shape[1:], BF16), [("rs_sibling", 0, 4)], src=dwo)])
    part_wo = _sum_sibling(dwo, recv1_wo, c_arr, "rs_sum_sibling_w_o")
    _, _, dkc_f, dkc_b, dvc_f, dvc_b, dlac_f, dlac_b, _ = _gla_bwd(
        q_c, k_c, zc_al, AV // DV, la_c, savf_c, savb_c, jnp.zeros((NC, VALW), F32), dst0, H, DK, DV, "gla_bwd_ctx")

    def decay_bwd(lr, dla_f_v, dla_b_v, wdf_v, wdb_v, bf_v, bb_v):
        a_f, a_b = decay(lr, wdf_v, wdb_v, bf_v, bb_v)
        da_f = _chunk_cumsum(dla_f_v, False) * jax.nn.sigmoid(-a_f) / GLA_TAU
        da_b = _chunk_cumsum(dla_b_v, True) * jax.nn.sigmoid(-a_b) / GLA_TAU
        lrb, dfb, dbb = lr.astype(BF16), da_f.astype(BF16), da_b.astype(BF16)
        dlr = _dg(dfb, wdf_v.astype(BF16), 1, 1) + _dg(dbb, wdb_v.astype(BF16), 1, 1)
        return dlr, _dg(lrb, dfb, 0, 0), _dg(lrb, dbb, 0, 0), _colsum(da_f), _colsum(da_b)

    def prep_x_bwd(dq0, dq1, dk0, dk1, dv0, dv1, lr, dla0, dla1, cs, sn, dzsg, dzr, wdf_v, wdb_v, bf_v, bb_v):
        dlr, dwf, dwb, dbf, dbb = decay_bwd(lr, dla0, dla1, wdf_v, wdb_v, bf_v, bb_v)
        row = [dzsg, (_rope_t(dq0 + dq1, cs, sn, H, DK) * qscale).astype(BF16), _rope_t(dk0 + dk1, cs, sn, H, DK).astype(BF16),
               (dv0 + dv1).astype(BF16), dzr, dlr.astype(BF16), jnp.zeros((lr.shape[0], LR_PAD - LANES), BF16)]
        return jnp.concatenate(row, axis=1), dwf, dwb, dbf, dbb

    def prep_c_bwd(dk0, dk1, dv0, dv1, lr, dla0, dla1, wdf_v, wdb_v, bf_v, bb_v):
        dlr, dwf, dwb, dbf, dbb = decay_bwd(lr, dla0, dla1, wdf_v, wdb_v, bf_v, bb_v)
        n = lr.shape[0]
        row = [jnp.zeros((n, 2 * SGW + KEYW), BF16), (dk0 + dk1).astype(BF16), (dv0 + dv1).astype(BF16), jnp.zeros((n, VALW), BF16),
               dlr.astype(BF16), jnp.zeros((n, LR_PAD - LANES), BF16)]
        return jnp.concatenate(row, axis=1), dwf, dwb, dbf, dbb

    dec_acc = [(LANES, KEYW), (LANES, KEYW), (1, KEYW), (1, KEYW)]
    dz_cat, dwdf_x, dwdb_x, dbdf_x, dbdb_x = _rows(
        prep_x_bwd, N, TR, [_T(dq_f, KEYW), _T(dq_b, KEYW), _T(dk_f, KEYW), _T(dk_b, KEYW), _T(dv_f, VALW), _T(dv_b, VALW),
                            _T(z_al, LANES, ALR // LANES), _T(dla_f, KEYW), _T(dla_b, KEYW), _T(cos_t, DK), _T(sin_t, DK),
                            _T(dz_sg, 2 * SGW), _T(dz_r, VALW)] + dec_w,
        [(ACOLS, BF16)], dec_acc, "gla_prep_x_bwd", stacked=(N + NC, 0, None))
    dz_cat, dwdf_c, dwdb_c, dbdf_c, dbdb_c = _rows(
        prep_c_bwd, NC, TR, [_T(dkc_f, KEYW), _T(dkc_b, KEYW), _T(dvc_f, VALW), _T(dvc_b, VALW),
                             _T(zc_al, LANES, ALR // LANES), _T(dlac_f, KEYW), _T(dlac_b, KEYW)] + dec_w,
        [(ACOLS, BF16)], dec_acc, "gla_prep_ctx_bwd", stacked=(N + NC, N, dz_cat))
    tkd = _pick(ACOLS, 3584, 2 * LANES)
    tkt = _pick(N + NC, 2304)
    dw_al, recv2_w1, recv2_wo = _mm(h_cat, dz_cat, D, ACOLS, N + NC, name="in_proj_dw", ta=True, tn=tka, tk=tkt, out_dtypes=(BF16,),
                                    side=[_ph(recv2_w1, [("rs_chips", 3, 4)], src=part_w1),
                                          _ph(lax.empty((3,) + part_wo.shape[1:], BF16), [("rs_chips", 0, 4)], src=part_wo)])
    g_in = _shard_columns(dw_al, CS, SG0, 2 * SGW, -SG0, RH, "w_in_grad_blocks")
    recv1_in = _rs_sibling([g_in], "rs_sibling_w_in")[0]
    part_in = _sum_sibling(g_in, recv1_in, c_arr, "rs_sum_sibling_w_in")
    dhx, recv2_in = _mm(dz_cat, w_al, N, D, ACOLS, name="in_proj_dx", tb=True, tk=tkd,
                        side=[_ph(lax.empty((3,) + part_in.shape[1:], BF16), [("rs_chips", 0, 3)], src=part_in)])
    dhc = _mm(dz_cat, w_al, NC, D, ACOLS, name="in_proj_dctx", tb=True, tk=tkd,
              a_spec=pl.BlockSpec((tmc, tkd), lambda i, j, k: (N // tmc + i, k)))

    grad_x, dpre1_x, dsc1, dsh1 = _rows(prenorm_bwd, N, TR, [_T(x2, D), _W(pre1_g), _W(sc1), _W(sh1), _T(dhx, D), _T(dx1, D)],
                                        [(D, F32)], [(1, D)] * 3, "prenorm1_x_bwd")

    def prenorm_bwd_ctx(xv, gv, scv, shv, dh):
        _, vjp = jax.vjp(_prenorm, xv, gv, scv, shv)
        _, dg, dsc, dsh = vjp(dh)
        return _colsum(dg), _colsum(dsc), _colsum(dsh)

    dpre1_c, dcsc1, dcsh1 = _rows(prenorm_bwd_ctx, NC, TR, [_T(ctx2, D), _W(pre1_g), _W(csc1), _W(csh1), _T(dhc, D)],
                                  [], [(1, D)] * 3, "prenorm1_ctx_bwd")

    half = [_sum_chips(p, r, sc_arr, "rs_sum_chips_" + nm)
            for p, r, nm in zip((part_wo, part_w1, part_w2), (recv2_wo, recv2_w1, recv2_w2), ("w_o", "w_1", "w_2"))]
    g_w_o, g_w_1, g_w_2 = [g.reshape(w.shape[1:]) for g, w in zip(_rs_final(half, "rs_final"), (w_o, w_1, w_2))]

    dmod_x = jnp.concatenate([dsh1, dsc1, dgt1, dsh2, dsc2, dgt2], axis=1)
    dmodc = jnp.concatenate([dcsh1, dcsc1], axis=1)
    small_parts = [loss_acc, dmod_x, dmodc, dpre1_x + dpre1_c, dpost1, dpre2, dpost2, dwdf_x + dwdf_c, dbdf_x + dbdf_c,
                   dwdb_x + dwdb_c, dbdb_x + dbdb_c, dgn, dlng, dlnb, dws, dbs]
    small_shapes = [p.shape for p in small_parts]
    packed = _pack(small_parts)
    n_sm = packed.shape[1]
    gathered = _allgather_small(packed, "gather_small_grads")

    def sum_devices(g):
        tot = g[0:8]
        for dev in range(1, N_DEV):
            tot = tot + g[8 * dev:8 * dev + 8]
        return (tot,)

    summed = _rows(sum_devices, N_DEV * 8, N_DEV * 8, [_W(gathered)], [], [(8, n_sm)], "sum_small_grads")[0]
    (loss_s, dmod_sum, dmodc_sum, g_pre1, g_post1, g_pre2, g_post2, g_wdf_pad, g_bdf, g_wdb_pad, g_bdb, g_gn, g_lng, g_lnb,
     g_ws, g_bs) = _unpack(summed.reshape(-1), small_shapes)
    loss = loss_s[0, 0]
    dmod_rows = gathered.reshape(N_DEV, -1)[:, LANES:LANES + N_MOD * D]
    g_b_ada = dmod_sum + jnp.pad(dmodc_sum, ((0, 0), (0, (N_MOD - 2) * D)))
    dmod16 = jnp.zeros((16, N_MOD * D), F32).at[:N_DEV].set(dmod_rows).at[N_DEV, :2 * D].set(dmodc_sum[0])
    dmod16_sh = lax.dynamic_slice(dmod16, (0, s_me * MS), (16, MS))
    g_w_ada, d_w_ada, nm_w_ada, nv_w_ada = _ada_update(cond.T, dmod16_sh, w_ada[0], m_w_ada[0], v_w_ada[0], "w_ada_update")

    dcond, recv2_in = _mm(dmod16_sh, w_ada[0], 16, D, MS, name="cond_bwd", tb=True, tk=min(512, MS),
                          side=[_ph(recv2_in, [("rs_chips", 3, 4)], src=part_in)])
    half_in = _sum_chips(part_in, recv2_in, sc_arr, "rs_sum_chips_w_in")
    g_w_in_t = _transpose_cols(_rs_final([half_in], "rs_final_w_in")[0].reshape(D, -1), CS, "w_in_grad_t")
    part_c = _allgather_small(dcond[N_DEV].reshape(8, D // 8), "gather_dcond").reshape(N_DEV, D)

    def cctx_grad(p, cv):
        sg = jax.nn.sigmoid(cv)
        tot = ((p[0:1] + p[2:3]) + p[4:5]) + p[6:7]
        return (jnp.broadcast_to(tot * (sg * (1.0 + cv * (1.0 - sg))), p.shape),)

    g_c_ctx = _rows(cctx_grad, N_DEV, N_DEV, [_W(part_c), _W(c_ctx.reshape(1, D))], [(D, F32)], [], "c_ctx_grad")[0][0:1]

    def col_shard(g_full, width):
        return lax.dynamic_slice_in_dim(g_full, s_me * width, width, axis=g_full.ndim - 1)

    g_w_dec_f = col_shard(g_wdf_pad[:GLA_LOWRANK], KEYW // N_CHIP)
    g_w_dec_b = col_shard(g_wdb_pad[GLA_LOWRANK:LR], KEYW // N_CHIP)
    g_gla_norm = col_shard(g_gn.reshape(H, DV), DV // N_CHIP)
    small_w = [c_ctx, b_ada, pre1_g, post1_g, pre2_g, post2_g, w_dec_f, b_dec_f, w_dec_b, b_dec_b, gla_norm_g, sg_ln_g, sg_ln_b, w_s, b_s]
    small_m = [m_c_ctx, m_b_ada, m_pre1_g, m_post1_g, m_pre2_g, m_post2_g, m_w_dec_f, m_b_dec_f, m_w_dec_b, m_b_dec_b, m_gla_norm_g, m_sg_ln_g, m_sg_ln_b, m_w_s, m_b_s]
    small_v = [v_c_ctx, v_b_ada, v_pre1_g, v_post1_g, v_pre2_g, v_post2_g, v_w_dec_f, v_b_dec_f, v_w_dec_b, v_b_dec_b, v_gla_norm_g, v_sg_ln_g, v_sg_ln_b, v_w_s, v_b_s]
    small_g = [g_c_ctx, g_b_ada, g_pre1, g_post1, g_pre2, g_post2, g_w_dec_f, g_bdf, g_w_dec_b, g_bdb, g_gla_norm, g_lng, g_lnb, g_ws, g_bs]
    small_g = [g.reshape(w.shape) for g, w in zip(small_g, small_w)]
    d_small, m_small, v_small = _adamw_many(small_w, small_g, small_m, small_v, "adamw_small")

    def big(w, g, m, v, name):
        shp = w.shape
        res = _adamw(w.reshape(shp[-2:]), g.reshape(shp[-2:]), m.reshape(shp[-2:]), v.reshape(shp[-2:]), name)
        return [g.reshape(shp)] + [r.reshape(shp) for r in res]

    r_in = [jnp.swapaxes(t, 1, 2) for t in big(jnp.swapaxes(w_in, 1, 2), g_w_in_t[None], jnp.swapaxes(m_w_in, 1, 2),
                                               jnp.swapaxes(v_w_in, 1, 2), "adamw_w_in")]
    r_o = big(w_o, g_w_o, m_w_o, v_w_o, "adamw_w_o")
    r_1 = big(w_1, g_w_1, m_w_1, v_w_1, "adamw_w_1")
    r_2 = big(w_2, g_w_2, m_w_2, v_w_2, "adamw_w_2")
    r_ada = [t.reshape(w_ada.shape) for t in (g_w_ada, d_w_ada, nm_w_ada, nv_w_ada)]

    def ordered(k):
        sm = [small_g, d_small, m_small, v_small][k]
        return [sm[0], r_ada[k], *sm[1:6], r_in[k], *sm[6:15], r_o[k], r_1[k], r_2[k]]

    return (loss, grad_x.reshape(x.shape), *ordered(0), *ordered(1), *ordered(2), *ordered(3))
```

```python
import functools
import math

import numpy as np
import jax
import jax.numpy as jnp
from jax import lax
from jax.experimental import pallas as pl
from jax.experimental.pallas import tpu as pltpu

F32 = jnp.float32
BF16 = jnp.bfloat16
MESH = pl.DeviceIdType.MESH
ANY = pl.BlockSpec(memory_space=pl.ANY)

GLA_HEADS = 8
GLA_CHUNK = 64
GLA_LOWRANK = 16
GLA_TAU = 16.0
ROPE_BASE = 10000.0
GRID_W = 64
SG_GROUPS = 4
SG_CHUNK = 128
N_MOD = 6
EPS = 1e-6
ADAM_LR = 0.001
ADAM_B1 = 0.9
ADAM_B2 = 0.999
ADAM_EPS = 1e-08
ADAM_WD = 0.01
ADAM_STEP = 10

LANES = 128
VMEM_LIMIT = 56 << 20
N_DEV = 8
N_CHIP = 4


def _params(sem=None):
    return pltpu.CompilerParams(dimension_semantics=sem, vmem_limit_bytes=VMEM_LIMIT)


def _pick(dim, target, unit=LANES):
    best = None
    for t in range(unit, min(dim, target) + 1, unit):
        if dim % t == 0:
            best = t
    return dim if best is None else best


def _dg(a, b, ca, cb, precision=None):
    return lax.dot_general(a, b, (((ca,), (cb,)), ((), ())), preferred_element_type=F32,
                           precision=precision)


def _place():
    return lax.axis_index("x"), lax.axis_index("y"), lax.axis_index("c")


def _allgather_small(v, name):
    m_per, n = v.shape

    def body(x_ref, out_ref, send_sems, recv_sems, local_sem):
        x, y, c = _place()
        me, sibling = (x, y, c), (x, y, 1 - c)
        chips = [(1 - x, y), (x, 1 - y), (1 - x, 1 - y)]

        def rows(px, py, pc):
            return out_ref.at[pl.ds((4 * px + 2 * py + pc) * m_per, m_per), :]

        def copy(k, block, to, src=None):
            return pltpu.make_async_remote_copy(
                src_ref=rows(*block) if src is None else src, dst_ref=rows(*block),
                send_sem=send_sems.at[k], recv_sem=recv_sems.at[k],
                device_id=to, device_id_type=MESH)

        mine = pltpu.make_async_copy(x_ref, rows(*me), local_sem)
        mine.start()
        first = [copy(0, me, sibling, src=x_ref)]
        first += [copy(1 + j, me, (*chip, c), src=x_ref) for j, chip in enumerate(chips)]
        for cp in first:
            cp.start()
        passed = [copy(4 + j, (*chip, c), sibling) for j, chip in enumerate(chips)]
        for j, chip in enumerate(chips):
            copy(1 + j, (*chip, c), me).wait_recv()
            passed[j].start()
        copy(0, sibling, me).wait_recv()
        for j, chip in enumerate(chips):
            copy(4 + j, (*chip, 1 - c), me).wait_recv()
        for cp in first + passed:
            cp.wait_send()
        mine.wait()

    return pl.pallas_call(
        body, name=name,
        out_shape=jax.ShapeDtypeStruct((N_DEV * m_per, n), v.dtype),
        in_specs=[pl.BlockSpec(memory_space=pltpu.VMEM)],
        out_specs=pl.BlockSpec(memory_space=pltpu.VMEM),
        scratch_shapes=[pltpu.SemaphoreType.DMA((7,)), pltpu.SemaphoreType.DMA((7,)),
                        pltpu.SemaphoreType.DMA],
        compiler_params=pltpu.CompilerParams(vmem_limit_bytes=VMEM_LIMIT),
    )(v)


def _cast_blocks(w, s_me, name):
    _, r, cols = w.shape
    tr = _row_tile(r, cols, 4)

    def body(s_ref, w_ref, o_ref):
        o_ref[...] = w_ref[...].astype(BF16)

    return pl.pallas_call(
        body, name=name,
        out_shape=jax.ShapeDtypeStruct((N_DEV, r, cols), BF16),
        grid_spec=pltpu.PrefetchScalarGridSpec(
            num_scalar_prefetch=1, grid=(2, r // tr),
            in_specs=[pl.BlockSpec((None, tr, cols), lambda h, i, s: (h, i, 0))],
            out_specs=pl.BlockSpec((None, tr, cols), lambda h, i, s: (2 * s[0] + h, i, 0))),
        compiler_params=_params(("arbitrary", "arbitrary")),
    )(s_me, w)


def _gather_big(ws, name):
    nw = len(ws)

    def body(*refs):
        outs = refs[nw:2 * nw]
        send_sems, recv_sems = refs[2 * nw:]
        x, y, c = _place()
        me, sibling = (x, y, c), (x, y, 1 - c)
        chips = [(1 - x, y), (x, 1 - y), (1 - x, 1 - y)]

        def blk(px, py, pc):
            return 4 * px + 2 * py + pc

        def copy(w, k, block, to):
            return pltpu.make_async_remote_copy(
                src_ref=outs[w].at[block], dst_ref=outs[w].at[block],
                send_sem=send_sems.at[6 * w + k], recv_sem=recv_sems.at[6 * w + k],
                device_id=to, device_id_type=MESH)

        first = []
        for w in range(nw):
            for j, chip in enumerate(chips):
                cp = copy(w, j, blk(x, y, c), (*chip, c))
                cp.start()
                first.append(cp)
        passed = []
        for w in range(nw):
            for j, chip in enumerate(chips):
                copy(w, j, blk(*chip, c), me).wait_recv()
                cp = copy(w, 3 + j, blk(*chip, c), sibling)
                cp.start()
                passed.append(cp)
        for w in range(nw):
            for j, chip in enumerate(chips):
                copy(w, 3 + j, blk(*chip, 1 - c), me).wait_recv()
        for cp in first + passed:
            cp.wait_send()

    return pl.pallas_call(
        body, name=name,
        out_shape=[jax.ShapeDtypeStruct(w.shape, w.dtype) for w in ws],
        in_specs=[ANY] * nw, out_specs=[ANY] * nw,
        input_output_aliases={w: w for w in range(nw)},
        scratch_shapes=[pltpu.SemaphoreType.DMA((6 * nw,)), pltpu.SemaphoreType.DMA((6 * nw,))],
    )(*ws)


def _rs_sibling(gs, name):
    nw = len(gs)

    def body(*refs):
        ins, outs = refs[:nw], refs[nw:2 * nw]
        send_sems, recv_sems = refs[2 * nw:]
        x, y, c = _place()
        cps = []
        for w in range(nw):
            for s in range(N_CHIP):
                cp = pltpu.make_async_remote_copy(
                    src_ref=ins[w].at[2 * s + (1 - c)], dst_ref=outs[w].at[s],
                    send_sem=send_sems.at[N_CHIP * w + s], recv_sem=recv_sems.at[N_CHIP * w + s],
                    device_id=(x, y, 1 - c), device_id_type=MESH)
                cp.start()
                cps.append(cp)
        for cp in cps:
            cp.wait()

    return pl.pallas_call(
        body, name=name,
        out_shape=[jax.ShapeDtypeStruct((N_CHIP,) + g.shape[1:], g.dtype) for g in gs],
        in_specs=[ANY] * nw, out_specs=[ANY] * nw,
        scratch_shapes=[pltpu.SemaphoreType.DMA((N_CHIP * nw,)), pltpu.SemaphoreType.DMA((N_CHIP * nw,))],
    )(*gs)


def _rs_final(fs, name):
    nw = len(fs)

    def body(*refs):
        outs = refs[nw:2 * nw]
        send_sems, recv_sems = refs[2 * nw:]
        x, y, c = _place()
        cps = []
        for w in range(nw):
            cp = pltpu.make_async_remote_copy(
                src_ref=outs[w].at[c], dst_ref=outs[w].at[c],
                send_sem=send_sems.at[w], recv_sem=recv_sems.at[w],
                device_id=(x, y, 1 - c), device_id_type=MESH)
            cp.start()
            cps.append(cp)
        for cp in cps:
            cp.wait()

    return pl.pallas_call(
        body, name=name,
        out_shape=[jax.ShapeDtypeStruct(f.shape, f.dtype) for f in fs],
        in_specs=[ANY] * nw, out_specs=[ANY] * nw,
        input_output_aliases={w: w for w in range(nw)},
        scratch_shapes=[pltpu.SemaphoreType.DMA((nw,)), pltpu.SemaphoreType.DMA((nw,))],
    )(*fs)


_PHASE_COPIES = {"gather_ici": 3, "gather_d2d": 3, "gather_chain": 6, "rs_sibling": N_CHIP, "rs_chips": 3}
QUARTERS = 4


def _ph(buf, legs, src=None):
    return dict(buf=buf, src=src, legs=legs)


def _n_copies(ph):
    return sum(_PHASE_COPIES[kind] for kind, _, _ in ph["legs"])


def _phase_copies(ph, src, buf, send_sems, recv_sems, base):
    x, y, c = _place()
    sibling = (x, y, 1 - c)
    chips = [(1 - x, y), (x, 1 - y), (1 - x, 1 - y)]
    r = buf.shape[1]

    def make(k, trip):
        a, b, dev = trip
        return pltpu.make_async_remote_copy(src_ref=a, dst_ref=b, send_sem=send_sems.at[base + k], recv_sem=recv_sems.at[base + k],
                                            device_id=dev, device_id_type=MESH)

    out = []
    for kind, lo, hi in ph["legs"]:
        rows = pl.ds(int(lo * r) // QUARTERS, int((hi - lo) * r) // QUARTERS)
        ici = [(buf.at[4 * x + 2 * y + c, rows], buf.at[4 * x + 2 * y + c, rows], (*chip, c)) for chip in chips]
        d2d = [(buf.at[4 * chip[0] + 2 * chip[1] + c, rows], buf.at[4 * chip[0] + 2 * chip[1] + c, rows], sibling) for chip in chips]
        if kind == "gather_ici":
            trips, later = ici, []
        elif kind == "gather_d2d":
            trips, later = d2d, []
        elif kind == "gather_chain":
            trips, later = ici, d2d
        elif kind == "rs_sibling":
            trips, later = [(src.at[2 * s + (1 - c), rows], buf.at[s, rows], sibling) for s in range(N_CHIP)], []
        else:
            trips, later = [(src.at[2 * chip[0] + chip[1], rows], buf.at[j, rows], (*chip, c)) for j, chip in enumerate(chips)], []
        out.append(([make(k, t) for k, t in enumerate(trips)], [make(len(trips) + k, t) for k, t in enumerate(later)]))
        base += _PHASE_COPIES[kind]
    return out


def _side_call(inner, grid, in_specs, out_specs, out_shape, scratch, args, phases, name, semantics, mid=0.8):
    n_in, n_out, n_ph = len(in_specs), len(out_specs), len(phases)
    if n_ph == 0:
        outs = pl.pallas_call(inner, name=name, grid=grid, in_specs=in_specs, out_specs=out_specs, out_shape=out_shape,
                              scratch_shapes=scratch, compiler_params=_params(semantics))(*args)
        return list(outs), []
    n_cp = sum(_n_copies(p) for p in phases)
    side_args, buf_pos, src_pos = [], [], []
    for p in phases:
        buf_pos.append(len(side_args))
        side_args.append(p["buf"])
        src_pos.append(len(side_args) if p["src"] is not None else None)
        if p["src"] is not None:
            side_args.append(p["src"])
    n_side = len(side_args)
    total = math.prod(grid)
    mid_lin = min(total - 1, int(total * mid))

    def body(*refs):
        b_in, s_in = refs[:n_in], refs[n_in:n_in + n_side]
        b_out, s_out = refs[n_in + n_side:n_in + n_side + n_out], refs[n_in + n_side + n_out:n_in + n_side + n_out + n_ph]
        rest = refs[n_in + n_side + n_out + n_ph:]
        send_sems, recv_sems = rest[-2:]
        lin = functools.reduce(lambda acc, ag: acc * ag[1] + pl.program_id(ag[0]), list(enumerate(grid))[1:], pl.program_id(0))

        def copies():
            out, base = [], 0
            for p, sp, so in zip(phases, src_pos, s_out):
                out += _phase_copies(p, None if sp is None else s_in[sp], so, send_sems, recv_sems, base)
                base += _n_copies(p)
            return out

        @pl.when(lin == 0)
        def _():
            for a, _ in copies():
                for cp in a:
                    cp.start()

        if any(kind == "gather_chain" for p in phases for kind, _, _ in p["legs"]):
            @pl.when(lin == mid_lin)
            def _():
                for a, b in copies():
                    if b:
                        for cp in a:
                            cp.wait()
                        for cp in b:
                            cp.start()

        inner(*b_in, *b_out, *rest[:-2])

        @pl.when(lin == total - 1)
        def _():
            for a, b in copies():
                for cp in (b if b else a):
                    cp.wait()

    outs = pl.pallas_call(
        body, name=name, grid=grid,
        in_specs=list(in_specs) + [ANY] * n_side, out_specs=list(out_specs) + [ANY] * n_ph,
        out_shape=list(out_shape) + [jax.ShapeDtypeStruct(p["buf"].shape, p["buf"].dtype) for p in phases],
        input_output_aliases={n_in + bp: n_out + k for k, bp in enumerate(buf_pos)},
        scratch_shapes=list(scratch) + [pltpu.SemaphoreType.DMA((n_cp,)), pltpu.SemaphoreType.DMA((n_cp,))],
        compiler_params=_params(("arbitrary",) * len(grid)),
    )(*args, *side_args)
    return list(outs[:n_out]), list(outs[n_out:])


def _row_tile(r, cols, itemsize):
    if r * cols * itemsize <= (2 << 20):
        return r
    fits = [t for t in range(8, r, 8) if r % t == 0 and t * cols * itemsize <= (2 << 20)]
    return max(fits) if fits else r


def _transpose_cols(g, n_cols, name):
    rows, wp = g.shape
    tr = min(rows, 512)

    def body(g_ref, o_ref):
        o_ref[...] = g_ref[...].T[:n_cols]

    return pl.pallas_call(
        body, name=name, grid=(rows // tr,), out_shape=jax.ShapeDtypeStruct((n_cols, rows), F32),
        in_specs=[pl.BlockSpec((tr, wp), lambda i: (i, 0))], out_specs=pl.BlockSpec((n_cols, tr), lambda i: (0, i)),
        compiler_params=_params(("parallel",)),
    )(g)


def _sum_sibling(g, r1, c_me, name):
    _, r, cols = g.shape
    tr = _row_tile(r, cols, 4)

    def body(c_ref, g_ref, r_ref, o_ref):
        o_ref[...] = (g_ref[...].astype(F32) + r_ref[...].astype(F32)).astype(o_ref.dtype)

    return pl.pallas_call(
        body, name=name,
        out_shape=jax.ShapeDtypeStruct((N_CHIP, r, cols), g.dtype),
        grid_spec=pltpu.PrefetchScalarGridSpec(
            num_scalar_prefetch=1, grid=(N_CHIP, r // tr),
            in_specs=[pl.BlockSpec((None, tr, cols), lambda s, i, c: (2 * s + c[0], i, 0)),
                      pl.BlockSpec((None, tr, cols), lambda s, i, c: (s, i, 0))],
            out_specs=pl.BlockSpec((None, tr, cols), lambda s, i, c: (s, i, 0))),
        compiler_params=_params(("arbitrary", "arbitrary")),
    )(c_me, g, r1)


def _sum_chips(p, r2, sc_me, name):
    _, r, cols = p.shape
    tr = _row_tile(r, cols, 4)

    def body(s_ref, p_ref, a_ref, b_ref, c_ref, o_ref):
        o_ref[...] = ((p_ref[...].astype(F32) + a_ref[...].astype(F32)) + b_ref[...].astype(F32)) + c_ref[...].astype(F32)

    return pl.pallas_call(
        body, name=name,
        out_shape=jax.ShapeDtypeStruct((2, r, cols), F32),
        grid_spec=pltpu.PrefetchScalarGridSpec(
            num_scalar_prefetch=1, grid=(r // tr,),
            in_specs=[pl.BlockSpec((None, tr, cols), lambda i, s: (s[0], i, 0)),
                      pl.BlockSpec((None, tr, cols), lambda i, s: (0, i, 0)),
                      pl.BlockSpec((None, tr, cols), lambda i, s: (1, i, 0)),
                      pl.BlockSpec((None, tr, cols), lambda i, s: (2, i, 0))],
            out_specs=pl.BlockSpec((None, tr, cols), lambda i, s: (s[1], i, 0))),
        compiler_params=_params(("arbitrary",)),
    )(sc_me, p, r2, r2, r2)


def _shard_columns(g_al, shard_cols, bound, off_lo, off_hi, rh, name):
    d, acols = g_al.shape
    wp = -(-shard_cols // LANES) * LANES
    tr = min(LANES, rh)
    nt = acols // LANES

    def body(x_ref, o_ref):
        s = pl.program_id(1)
        lane = lax.broadcasted_iota(jnp.int32, (tr, LANES), 1)

        def tile(q):
            q = max(0, min(nt - 1, q))
            return x_ref[:, q * LANES:(q + 1) * LANES].astype(F32)

        def read(start):
            q, sh = divmod(start, LANES)
            if sh == 0:
                return tile(q)
            return jnp.where(lane < LANES - sh, pltpu.roll(tile(q), LANES - sh, 1), pltpu.roll(tile(q + 1), LANES - sh, 1))

        for k in range(N_CHIP):
            @pl.when(s == k)
            def _(k=k):
                for t in range(wp // LANES):
                    n0 = k * shard_cols + t * LANES
                    if n0 + LANES <= bound:
                        v = read(n0 + off_lo)
                    elif n0 >= bound:
                        v = read(n0 + off_hi)
                    else:
                        v = jnp.where(lane < bound - n0, read(n0 + off_lo), read(n0 + off_hi))
                    o_ref[:, t * LANES:(t + 1) * LANES] = v.astype(o_ref.dtype)

    return pl.pallas_call(
        body, name=name, grid=(d // tr, N_CHIP),
        out_shape=jax.ShapeDtypeStruct((N_DEV, rh, wp), BF16),
        in_specs=[pl.BlockSpec((tr, acols), lambda i, s: (i, 0))],
        out_specs=pl.BlockSpec((None, tr, wp), lambda i, s: (2 * s + (i * tr) // rh, ((i * tr) % rh) // tr, 0)),
        compiler_params=_params(("arbitrary", "arbitrary")),
    )(g_al)


def _mm(a, b, M, N, K, *, name, ta=False, tb=False, out_dtypes=(F32,), tm=1024, tn=1024, tk=2048,
        a_spec=None, b_spec=None, out_specs=None, out_shapes=None, epi=None, epi_in=(), epi_specs=(), side=(), side_mid=0.8):
    tm, tn, tk = min(tm, M), min(tn, N), min(tk, K)
    assert M % tm == 0 and N % tn == 0 and K % tk == 0, (name, M, N, K, tm, tn, tk)
    nk = K // tk
    n_epi, n_out = len(epi_in), len(out_dtypes)
    if a_spec is None:
        a_spec = pl.BlockSpec((tk, tm), lambda i, j, k: (k, i)) if ta else pl.BlockSpec((tm, tk), lambda i, j, k: (i, k))
    if b_spec is None:
        b_spec = pl.BlockSpec((tn, tk), lambda i, j, k: (j, k)) if tb else pl.BlockSpec((tk, tn), lambda i, j, k: (k, j))
    if out_specs is None:
        out_specs = [pl.BlockSpec((tm, tn), lambda i, j, k: (i, j))] * n_out
        out_shapes = [jax.ShapeDtypeStruct((M, N), dt) for dt in out_dtypes]

    def body(a_ref, b_ref, *rest):
        epi_refs, o_refs, acc = rest[:n_epi], rest[n_epi:n_epi + n_out], rest[-1]
        k = pl.program_id(2)

        @pl.when(k == 0)
        def _():
            acc[...] = jnp.zeros_like(acc)

        acc[...] += _dg(a_ref[...].astype(BF16), b_ref[...].astype(BF16), 0 if ta else 1, 1 if tb else 0)

        @pl.when(k == nk - 1)
        def _():
            r = acc[...]
            vals = (r,) if epi is None else epi(r, *[e[...] for e in epi_refs])
            for o_ref, v in zip(o_refs, vals):
                o_ref[...] = v.astype(o_ref.dtype)

    outs, side_outs = _side_call(body, (M // tm, N // tn, nk), [a_spec, b_spec, *epi_specs], out_specs, out_shapes,
                                 [pltpu.VMEM((tm, tn), F32)], (a, b, *epi_in), list(side), name,
                                 ("parallel", "parallel", "arbitrary"), mid=side_mid)
    if side:
        return outs + side_outs
    return outs[0] if n_out == 1 else outs


def _T(arr, width, col=0, lead=None):
    return ("tile", arr, width, col, lead)


def _W(arr):
    return ("whole", arr)


def _rows(fn, n_rows, tr, ins, tile_outs, acc_outs, name, stacked=None):
    tr = min(tr, n_rows)
    assert n_rows % tr == 0, (name, n_rows, tr)
    total_rows, first_row, earlier = stacked if stacked is not None else (n_rows, 0, None)
    assert first_row % tr == 0
    in_specs, args = [], []
    for d in ins:
        if d[0] == "tile":
            _, arr, width, col, lead = d
            if lead is None:
                in_specs.append(pl.BlockSpec((tr, width), lambda i, col=col: (i, col)))
            else:
                in_specs.append(pl.BlockSpec((None, tr, width), lambda i, col=col, lead=lead: (lead, i, col)))
            args.append(arr)
        else:
            arr = d[1]
            in_specs.append(pl.BlockSpec(arr.shape, lambda i, nd=arr.ndim: (0,) * nd))
            args.append(arr)
    n_in, n_t = len(ins), len(tile_outs)
    out_shape = [jax.ShapeDtypeStruct((n_rows, w), dt) for w, dt in tile_outs]
    out_specs = [pl.BlockSpec((tr, w), lambda i: (i, 0)) for w, _ in tile_outs]
    if stacked is not None:
        out_shape[0] = jax.ShapeDtypeStruct((total_rows, tile_outs[0][0]), tile_outs[0][1])
        out_specs[0] = pl.BlockSpec((tr, tile_outs[0][0]), lambda i: (first_row // tr + i, 0))
    if earlier is not None:
        in_specs.append(ANY)
        args.append(earlier)
    out_shape += [jax.ShapeDtypeStruct(s, F32) for s in acc_outs]
    out_specs += [pl.BlockSpec(s, lambda i, nd=len(s): (0,) * nd) for s in acc_outs]

    n_args = len(args)

    def body(*refs):
        in_refs, t_refs, a_refs = refs[:n_in], refs[n_args:n_args + n_t], refs[n_args + n_t:]
        vals = fn(*[r[...] for r in in_refs])
        for r, v in zip(t_refs, vals[:n_t]):
            r[...] = v.astype(r.dtype)
        first = pl.program_id(0) == 0
        for r, v in zip(a_refs, vals[n_t:]):
            @pl.when(first)
            def _(r=r, v=v):
                r[...] = v

            @pl.when(jnp.logical_not(first))
            def _(r=r, v=v):
                r[...] += v

    return pl.pallas_call(
        body, name=name, out_shape=out_shape, grid=(n_rows // tr,),
        in_specs=in_specs, out_specs=out_specs,
        input_output_aliases={n_in: 0} if earlier is not None else {},
        compiler_params=_params(("arbitrary",)),
    )(*args)


def _colsum(t):
    return jnp.sum(t, axis=0, keepdims=True)


def _prenorm(x, g, sc, sh):
    xf = x.astype(F32)
    return xf * lax.rsqrt(jnp.mean(xf * xf, axis=-1, keepdims=True) + EPS) * g * (1.0 + sc) + sh


def _postnorm(x, y, gate, pg):
    return x + gate * (y * lax.rsqrt(jnp.mean(y * y, axis=-1, keepdims=True) + EPS) * pg)


def _gelu(t):
    return 0.5 * t * (1.0 + lax.erf(t * (2.0 ** -0.5)))


def _sg_pre(zu, zv, lng, lnb):
    u, vr = _gelu(zu), _gelu(zv)
    mu = jnp.mean(vr, axis=-1, keepdims=True)
    var = jnp.mean(jnp.square(vr - mu), axis=-1, keepdims=True)
    return u, (vr - mu) * lax.rsqrt(var + EPS) * lng + lnb


def _sg_mix(vv, ws_ref_vals, bs_vals, gw):
    parts = []
    for g in range(SG_GROUPS):
        s = _dg(ws_ref_vals[g].astype(BF16), vv[:, g * gw:(g + 1) * gw].astype(BF16), 1, 0)
        parts.append(s + bs_vals[g])
    return jnp.concatenate(parts, axis=1)


def _readout(o, r, g, heads, dv):
    parts = []
    for h in range(heads):
        oh = o[:, h * dv:(h + 1) * dv]
        parts.append(oh * lax.rsqrt(jnp.mean(oh * oh, axis=-1, keepdims=True) + EPS))
    return jnp.concatenate(parts, axis=1) * g * (r * jax.nn.sigmoid(r))


def _log_sigmoid(a):
    return jnp.minimum(a, 0.0) - jnp.log(1.0 + jnp.exp(-jnp.abs(a)))


def _rope_swap(t, m):
    lane = lax.broadcasted_iota(jnp.int32, t.shape, 1)
    return jnp.where((lane % (2 * m)) < m, pltpu.roll(t, 3 * m, 1), pltpu.roll(t, m, 1))


def _rope(t, cos, sin, heads, dk):
    parts = []
    for h in range(heads):
        th = t[:, h * dk:(h + 1) * dk]
        parts.append(th * cos + _rope_swap(th, dk // 4) * sin)
    return jnp.concatenate(parts, axis=1)


def _rope_t(dt, cos, sin, heads, dk):
    parts = []
    for h in range(heads):
        dh = dt[:, h * dk:(h + 1) * dk]
        parts.append(dh * cos + _rope_swap(dh * sin, dk // 4))
    return jnp.concatenate(parts, axis=1)


def _chunk_cumsum(t, upwards):
    n = t.shape[0]
    row = lax.broadcasted_iota(jnp.int32, (n, n), 0)
    col = lax.broadcasted_iota(jnp.int32, (n, n), 1)
    shift = GLA_CHUNK.bit_length() - 1
    same = jnp.right_shift(row, shift) == jnp.right_shift(col, shift)
    tri = jnp.logical_and(same, col <= row if upwards else col >= row)
    return _dg(tri.astype(F32), t, 1, 0, precision=lax.Precision.HIGHEST)


def _chunk_terms(d, qv, kv, b, C):
    row = lax.broadcasted_iota(jnp.int32, (C, C), 0)
    col = lax.broadcasted_iota(jnp.int32, (C, C), 1)
    tri = row >= col if d == 0 else row <= col
    end_row = lax.broadcasted_iota(jnp.int32, b.shape, 0) == (C - 1 if d == 0 else 0)
    btot = _colsum(jnp.where(end_row, b, 0.0))
    eb, enb, ebt = jnp.exp(b), jnp.exp(-b), jnp.exp(btot - b)
    return tri, btot, eb, enb, ebt, qv * eb, kv * enb, kv * ebt


def _gla_fwd(q, k, zv, v_col0, la, st0, heads, dk, dv, name, side=(), side_mid=0.8):
    n, C, H = q.shape[0], GLA_CHUNK, heads
    nc = n // C

    def body(qf, kf, vf, laf, qb, kb, vb_, lab, st0_ref, of_ref, ob_ref, sf_ref, sb_ref, fin_ref, st):
        i = pl.program_id(1)

        @pl.when(i == 0)
        def _():
            st[...] = st0_ref[...]

        for d, (q_ref, k_ref, v_ref, la_ref, o_ref, save_ref) in enumerate(((qf, kf, vf, laf, of_ref, sf_ref), (qb, kb, vb_, lab, ob_ref, sb_ref))):
            tri, btot, _, _, _, qt, kt, kh = _chunk_terms(d, q_ref[...], k_ref[...], la_ref[...], C)
            s = st[d]
            vb = v_ref[...].astype(BF16)
            qtb = qt.astype(BF16)
            att = jnp.where(tri, _dg(qtb, kt.astype(BF16), 1, 1), 0.0)
            o_ref[...] = _dg(qtb, s.astype(BF16), 1, 1) + _dg(att.astype(BF16), vb, 1, 0)
            save_ref[...] = s
            s_new = s * jnp.exp(btot) + _dg(vb, kh.astype(BF16), 0, 0)
            st[d] = s_new

            @pl.when(i == nc - 1)
            def _(d=d, s_new=s_new):
                fin_ref[d] = s_new

    def seq(width, col0, rev, dir_cols=0):
        if rev:
            return pl.BlockSpec((C, width), lambda h, i: (nc - 1 - i, col0 + dir_cols + h))
        return pl.BlockSpec((C, width), lambda h, i: (i, col0 + h))

    both = pl.BlockSpec((2, None, dv, dk), lambda h, i: (0, h, 0, 0))
    outs, side_outs = _side_call(
        body, (H, nc),
        [seq(dk, 0, False), seq(dk, 0, False), seq(dv, v_col0, False), seq(dk, 0, False),
         seq(dk, 0, True), seq(dk, 0, True), seq(dv, v_col0, True), seq(dk, 0, True, H), both],
        [seq(dv, 0, False), seq(dv, 0, True),
         pl.BlockSpec((None, None, dv, dk), lambda h, i: (h, i, 0, 0)),
         pl.BlockSpec((None, None, dv, dk), lambda h, i: (h, nc - 1 - i, 0, 0)), both],
        [jax.ShapeDtypeStruct((n, H * dv), F32)] * 2 + [jax.ShapeDtypeStruct((H, nc, dv, dk), F32)] * 2
        + [jax.ShapeDtypeStruct((2, H, dv, dk), F32)],
        [pltpu.VMEM((2, dv, dk), F32)], (q, k, zv, la, q, k, zv, la, st0), list(side), name, ("arbitrary", "arbitrary"),
        mid=side_mid)
    return outs + side_outs


def _gla_bwd(q, k, zv, v_col0, la, saved_f, saved_b, do, dfin, heads, dk, dv, name, side=()):
    n, C, H = q.shape[0], GLA_CHUNK, heads
    nc = n // C

    def body(qf, kf, vf, laf, sf, dof, qb, kb, vb_, lab, sb, dob_, dfin_ref,
             dqf, dqb, dkf, dkb, dvf, dvb, dlaf, dlab, d0_ref, dst):
        i = pl.program_id(1)

        @pl.when(i == 0)
        def _():
            dst[...] = dfin_ref[...]

        dirs = ((qf, kf, vf, laf, sf, dof, dqf, dkf, dvf, dlaf), (qb, kb, vb_, lab, sb, dob_, dqb, dkb, dvb, dlab))
        for d, (q_ref, k_ref, v_ref, la_ref, save_ref, do_ref, dq_ref, dk_ref, dv_ref, dla_ref) in enumerate(dirs):
            tri, btot, eb, enb, ebt, qt, kt, kh = _chunk_terms(d, q_ref[...], k_ref[...], la_ref[...], C)
            s, dsn = save_ref[...], dst[d]
            vb, dob = v_ref[...].astype(BF16), do_ref[...].astype(BF16)
            qtb, ktb, khb, dsnb = qt.astype(BF16), kt.astype(BF16), kh.astype(BF16), dsn.astype(BF16)
            att = jnp.where(tri, _dg(qtb, ktb, 1, 1), 0.0).astype(BF16)
            datt = jnp.where(tri, _dg(dob, vb, 1, 1), 0.0).astype(BF16)
            dqt = _dg(dob, s.astype(BF16), 1, 0) + _dg(datt, ktb, 1, 0)
            dkt = _dg(datt, qtb, 0, 0)
            dkh = _dg(vb, dsnb, 1, 0)
            dv_ref[...] = _dg(att, dob, 0, 0) + _dg(khb, dsnb, 1, 1)
            ebtot = jnp.exp(btot)
            dbtot = ebtot * _colsum(s * dsn) + _colsum(dkh * kh)
            s0 = dsn * ebtot + _dg(dob, qtb, 0, 0)
            dst[d] = s0
            db = dqt * qt - dkt * kt - dkh * kh
            dq_ref[...] = dqt * eb
            dk_ref[...] = dkt * enb + dkh * ebt
            end_row = lax.broadcasted_iota(jnp.int32, db.shape, 0) == (C - 1 if d == 0 else 0)
            dla_ref[...] = db + jnp.where(end_row, dbtot, 0.0)

            @pl.when(i == nc - 1)
            def _(d=d, s0=s0):
                d0_ref[d] = s0

    def seq(width, col0, fwd_dir, dir_cols=0):
        if fwd_dir:
            return pl.BlockSpec((C, width), lambda h, i: (nc - 1 - i, col0 + h))
        return pl.BlockSpec((C, width), lambda h, i: (i, col0 + dir_cols + h))

    both = pl.BlockSpec((2, None, dv, dk), lambda h, i: (0, h, 0, 0))
    sav_f = pl.BlockSpec((None, None, dv, dk), lambda h, i: (h, nc - 1 - i, 0, 0))
    sav_b = pl.BlockSpec((None, None, dv, dk), lambda h, i: (h, i, 0, 0))
    outs, side_outs = _side_call(
        body, (H, nc),
        [seq(dk, 0, True), seq(dk, 0, True), seq(dv, v_col0, True), seq(dk, 0, True), sav_f, seq(dv, 0, True),
         seq(dk, 0, False), seq(dk, 0, False), seq(dv, v_col0, False), seq(dk, 0, False, H), sav_b, seq(dv, 0, False), both],
        [seq(dk, 0, True), seq(dk, 0, False), seq(dk, 0, True), seq(dk, 0, False), seq(dv, 0, True), seq(dv, 0, False),
         seq(dk, 0, True), seq(dk, 0, False), both],
        [jax.ShapeDtypeStruct((n, H * dk), F32)] * 4 + [jax.ShapeDtypeStruct((n, H * dv), F32)] * 2
        + [jax.ShapeDtypeStruct((n, H * dk), F32)] * 2 + [jax.ShapeDtypeStruct((2, H, dv, dk), F32)],
        [pltpu.VMEM((2, dv, dk), F32)], (q, k, zv, la, saved_f, do, q, k, zv, la, saved_b, do, dfin), list(side), name,
        ("arbitrary", "arbitrary"))
    return outs + side_outs


def _adamw_math(w, g, m, v):
    m2 = ADAM_B1 * m + (1.0 - ADAM_B1) * g
    v2 = ADAM_B2 * v + (1.0 - ADAM_B2) * jnp.square(g)
    m_hat = m2 / (1.0 - ADAM_B1 ** ADAM_STEP)
    v_hat = v2 / (1.0 - ADAM_B2 ** ADAM_STEP)
    delta = -ADAM_LR * (m_hat / (jnp.sqrt(v_hat) + ADAM_EPS) + ADAM_WD * w)
    return delta, m2, v2


def _adamw(w, g, m, v, name):
    r, cols = w.shape
    tr = _row_tile(r, cols, 4 * 4)
    spec = pl.BlockSpec((tr, cols), lambda i: (i, 0))

    def body(w_ref, g_ref, m_ref, v_ref, d_ref, m2_ref, v2_ref):
        d_ref[...], m2_ref[...], v2_ref[...] = _adamw_math(w_ref[...], g_ref[...], m_ref[...], v_ref[...])

    return _side_call(body, (r // tr,), [spec] * 4, [spec] * 3, [jax.ShapeDtypeStruct((r, cols), F32)] * 3,
                      [], (w, g, m, v), [], name, ("parallel",))[0]


def _adamw_many(ws, gs, ms, vs, name):
    n = len(ws)

    def body(*refs):
        ins, outs = refs[:4 * n], refs[4 * n:]
        for k in range(n):
            d, m2, v2 = _adamw_math(ins[k][...], ins[n + k][...], ins[2 * n + k][...], ins[3 * n + k][...])
            outs[k][...], outs[n + k][...], outs[2 * n + k][...] = d, m2, v2

    outs = pl.pallas_call(
        body, name=name, out_shape=[jax.ShapeDtypeStruct(w.shape, F32) for w in ws] * 3,
        in_specs=[pl.BlockSpec(memory_space=pltpu.VMEM)] * (4 * n), out_specs=[pl.BlockSpec(memory_space=pltpu.VMEM)] * (3 * n),
        compiler_params=_params(),
    )(*ws, *gs, *ms, *vs)
    return outs[:n], outs[n:2 * n], outs[2 * n:]


def _ada_update(cond_t, dmod, w, m, v, name):
    r, cols = w.shape
    tr, tc = _pick(r, 512, 8), _pick(cols, 1024)
    spec = pl.BlockSpec((tr, tc), lambda i, j: (i, j))

    def body(c_ref, d_ref, w_ref, m_ref, v_ref, g_ref, dl_ref, m2_ref, v2_ref):
        g = _dg(c_ref[...].astype(BF16), d_ref[...].astype(BF16), 1, 0)
        g_ref[...] = g
        dl_ref[...], m2_ref[...], v2_ref[...] = _adamw_math(w_ref[...], g, m_ref[...], v_ref[...])

    return _side_call(
        body, (r // tr, cols // tc),
        [pl.BlockSpec((tr, cond_t.shape[1]), lambda i, j: (i, 0)), pl.BlockSpec((dmod.shape[0], tc), lambda i, j: (0, j)), spec, spec, spec],
        [spec] * 4, [jax.ShapeDtypeStruct((r, cols), F32)] * 4, [], (cond_t, dmod, w, m, v), [], name, ("parallel", "parallel"))[0]


def _pack(parts, rows=8):
    flat = jnp.concatenate([p.reshape(-1).astype(F32) for p in parts])
    n = -(-flat.shape[0] // (rows * LANES)) * LANES
    return jnp.pad(flat, (0, rows * n - flat.shape[0])).reshape(rows, n)


def _unpack(flat, shapes):
    out, off = [], 0
    for s in shapes:
        size = math.prod(s)
        out.append(flat[off:off + size].reshape(s))
        off += size
    return out


def kernel(x, c, ctx, c_ctx, w_ada, b_ada, pre1_g, post1_g, pre2_g, post2_g, w_in, w_dec_f, b_dec_f, w_dec_b, b_dec_b, gla_norm_g, sg_ln_g, sg_ln_b, w_s, b_s, w_o, w_1, w_2, loss_target, m_c_ctx, m_w_ada, m_b_ada, m_pre1_g, m_post1_g, m_pre2_g, m_post2_g, m_w_in, m_w_dec_f, m_b_dec_f, m_w_dec_b, m_b_dec_b, m_gla_norm_g, m_sg_ln_g, m_sg_ln_b, m_w_s, m_b_s, m_w_o, m_w_1, m_w_2, v_c_ctx, v_w_ada, v_b_ada, v_pre1_g, v_post1_g, v_pre2_g, v_post2_g, v_w_in, v_w_dec_f, v_b_dec_f, v_w_dec_b, v_b_dec_b, v_gla_norm_g, v_sg_ln_g, v_sg_ln_b, v_w_s, v_b_s, v_w_o, v_w_1, v_w_2):
    N, D = x.shape[1], x.shape[2]
    NC = ctx.shape[1]
    H = GLA_HEADS
    VALW = D // 2
    DV = VALW // H
    DK = DV // 2
    KEYW = H * DK
    SGW = D - VALW
    GW = SGW // SG_GROUPS
    LR = 2 * GLA_LOWRANK
    F = w_1.shape[2] * N_CHIP
    FS = F // N_CHIP
    RH = D // 2
    MS = w_ada.shape[2]
    IN_COLS = w_in.shape[2] * N_CHIP
    K0, V0, R0, LF0 = KEYW, 2 * KEYW, 2 * KEYW + VALW, 2 * KEYW + 2 * VALW
    SG0 = LF0 + LR
    AQ, AK, AV, AR, ALR = 2 * SGW, 2 * SGW + KEYW, 2 * SGW + 2 * KEYW, 2 * SGW + 2 * KEYW + VALW, 2 * SGW + 2 * KEYW + 2 * VALW
    ACOLS = ALR + 4 * LANES
    LR_PAD = ACOLS - ALR
    assert IN_COLS == SG0 + 2 * SGW and N % SG_CHUNK == 0 and N % GLA_CHUNK == 0 and NC % GLA_CHUNK == 0

    ax, ay, ac = _place()
    s_me = (2 * ax + ay).astype(jnp.int32)
    b_me = (4 * ax + 2 * ay + ac).astype(jnp.int32)
    s_arr, c_arr = s_me.reshape(1), ac.astype(jnp.int32).reshape(1)
    sc_arr = jnp.concatenate([s_arr, c_arr])
    CS = IN_COLS // N_CHIP

    shards = [_cast_blocks(w_in[0].reshape(2, RH, CS), s_arr, "cast_w_in"), _cast_blocks(w_o[0].reshape(2, D // N_DEV, D), s_arr, "cast_w_o"),
              _cast_blocks(w_1[0].reshape(2, RH, FS), s_arr, "cast_w_1"), _cast_blocks(w_2[0].reshape(2, F // N_DEV, D), s_arr, "cast_w_2")]
    win_g, = _gather_big(shards[:1], "gather_weights")
    wo_buf, w1_buf, w2_buf = shards[1], shards[2], shards[3]
    w_in_nat = win_g.reshape(N_CHIP, 2, RH, IN_COLS // N_CHIP).transpose(1, 2, 0, 3).reshape(D, IN_COLS)
    w_al = jnp.concatenate([w_in_nat[:, SG0:], w_in_nat[:, :LF0], w_in_nat[:, LF0:SG0],
                            jnp.zeros((D, LR_PAD - LR), BF16)], axis=1)

    n_dec = GLA_LOWRANK * (KEYW // N_CHIP)
    g0 = _allgather_small(_pack([c, w_dec_f, w_dec_b, gla_norm_g]), "gather_small0").reshape(N_DEV, -1)
    c_all = g0[:, :D]
    per_chip = g0[0::2]
    wdf = per_chip[:, D:D + n_dec].reshape(N_CHIP, GLA_LOWRANK, KEYW // N_CHIP).transpose(1, 0, 2).reshape(GLA_LOWRANK, KEYW)
    wdb = per_chip[:, D + n_dec:D + 2 * n_dec].reshape(N_CHIP, GLA_LOWRANK, KEYW // N_CHIP).transpose(1, 0, 2).reshape(GLA_LOWRANK, KEYW)
    gn_full = per_chip[:, D + 2 * n_dec:D + 2 * n_dec + H * (DV // N_CHIP)].reshape(N_CHIP, H, DV // N_CHIP).transpose(1, 0, 2).reshape(1, VALW)
    wd_f = jnp.zeros((LANES, KEYW), F32).at[:GLA_LOWRANK].set(wdf)
    wd_b = jnp.zeros((LANES, KEYW), F32).at[GLA_LOWRANK:LR].set(wdb)

    cond_in = jnp.zeros((16, D), F32).at[:N_DEV].set(c_all).at[N_DEV].set(c_ctx)
    b_ada_sh = lax.dynamic_slice(b_ada, (0, s_me * MS), (1, MS))

    def mod_epi(r, bias):
        return (r + bias,)

    def silu_rows(t):
        return (t * jax.nn.sigmoid(t),)

    cond = _rows(silu_rows, 16, 16, [_W(cond_in)], [(D, F32)], [], "cond_silu")[0]
    mod_sh = _mm(cond, w_ada[0], 16, MS, D, name="mod_matmul", tn=512, tk=D, epi=mod_epi, epi_in=(b_ada_sh,),
                 epi_specs=(pl.BlockSpec((1, min(512, MS)), lambda i, j, k: (0, j)),))
    g1m = _allgather_small(mod_sh, "gather_mod").reshape(N_DEV, 16, MS)[0::2]
    mod_all = g1m.transpose(1, 0, 2).reshape(16, N_CHIP * MS)
    mod_me = lax.dynamic_slice(mod_all, (b_me, 0), (1, N_MOD * D))
    sh1, sc1, gt1, sh2, sc2, gt2 = [mod_me[:, i * D:(i + 1) * D] for i in range(N_MOD)]
    csh1, csc1 = mod_all[N_DEV:N_DEV + 1, :D], mod_all[N_DEV:N_DEV + 1, D:2 * D]

    mq = DK // 4
    pos = np.arange(N)
    inv_freq = (np.float32(ROPE_BASE) ** (-np.arange(mq, dtype=np.float32) / np.float32(mq))).astype(np.float32)
    ang_r = (pos // GRID_W).astype(np.float32)[:, None] * inv_freq[None, :]
    ang_c = (pos % GRID_W).astype(np.float32)[:, None] * inv_freq[None, :]
    cos_t = jnp.asarray(np.concatenate([np.cos(ang_r), np.cos(ang_r), np.cos(ang_c), np.cos(ang_c)], axis=1), F32)
    sin_t = jnp.asarray(np.concatenate([-np.sin(ang_r), np.sin(ang_r), -np.sin(ang_c), np.sin(ang_c)], axis=1), F32)

    x2, tgt, ctx2 = x[0], loss_target[0], ctx[0]
    TR = 128
    qscale = DK ** -0.5

    def prenorm_fwd(xa, g, sc, sh, n_rows, name, stacked):
        return _rows(lambda xv, gv, scv, shv: (_prenorm(xv, gv, scv, shv),), n_rows, TR,
                     [_T(xa, D), _W(g), _W(sc), _W(sh)], [(D, BF16)], [], name, stacked=stacked)[0]

    h_cat = prenorm_fwd(x2, pre1_g, sc1, sh1, N, "prenorm1_x", (N + NC, 0, None))
    h_cat = prenorm_fwd(ctx2, pre1_g, csc1, csh1, NC, "prenorm1_ctx", (N + NC, N, h_cat))
    tka = _pick(ACOLS, 1536, 2 * LANES)
    tmc = min(1024, NC)
    z_al, w1_buf = _mm(h_cat, w_al, N, ACOLS, D, name="in_proj_x", tn=tka, side=[_ph(w1_buf, [("gather_ici", 0, 2)])])
    zc_al = _mm(h_cat, w_al, NC, ACOLS, D, name="in_proj_ctx", tn=tka,
                a_spec=pl.BlockSpec((tmc, min(2048, D)), lambda i, j, k: (N // tmc + i, k)))

    def decay(lr, wdf_v, wdb_v, bf_v, bb_v):
        lrb = lr.astype(BF16)
        a_f = _dg(lrb, wdf_v.astype(BF16), 1, 0) + bf_v
        a_b = _dg(lrb, wdb_v.astype(BF16), 1, 0) + bb_v
        return a_f, a_b

    def running_decay(a_f, a_b):
        return jnp.concatenate([_chunk_cumsum(_log_sigmoid(a_f) / GLA_TAU, True), _chunk_cumsum(_log_sigmoid(a_b) / GLA_TAU, False)], axis=1)

    def prep_x(zq, zk, lr, cs, sn, wdf_v, wdb_v, bf_v, bb_v):
        return _rope(zq * qscale, cs, sn, H, DK), _rope(zk, cs, sn, H, DK), running_decay(*decay(lr, wdf_v, wdb_v, bf_v, bb_v))

    def prep_c(zk, lr, wdf_v, wdb_v, bf_v, bb_v):
        return zk, running_decay(*decay(lr, wdf_v, wdb_v, bf_v, bb_v))

    dec_w = [_W(wd_f), _W(wd_b), _W(b_dec_f), _W(b_dec_b)]
    q_r, k_r, la_x = _rows(prep_x, N, TR, [_T(z_al, KEYW, AQ // KEYW), _T(z_al, KEYW, AK // KEYW), _T(z_al, LANES, ALR // LANES),
                                           _T(cos_t, DK), _T(sin_t, DK)] + dec_w,
                           [(KEYW, F32), (KEYW, F32), (2 * KEYW, F32)], [], "gla_prep_x")
    k_c, la_c = _rows(prep_c, NC, TR, [_T(zc_al, KEYW, AK // KEYW), _T(zc_al, LANES, ALR // LANES)] + dec_w,
                      [(KEYW, F32), (2 * KEYW, F32)], [], "gla_prep_ctx")

    zero_state = jnp.zeros((2, H, DV, DK), F32)
    q_c = jnp.zeros((NC, KEYW), F32)
    _, _, savf_c, savb_c, st_c = _gla_fwd(q_c, k_c, zc_al, AV // DV, la_c, zero_state, H, DK, DV, "gla_fwd_ctx")
    o_f, o_b, savf_x, savb_x, _, wo_g, w1_buf = _gla_fwd(
        q_r, k_r, z_al, AV // DV, la_x, st_c, H, DK, DV, "gla_fwd_x",
        side=[_ph(wo_buf, [("gather_chain", 0, 4)]), _ph(w1_buf, [("gather_d2d", 0, 2), ("gather_ici", 2, 4)])], side_mid=0.55)
    w_o_f = wo_g.reshape(D, D)

    def readout_fwd(of, ob, r, g):
        return (_readout(of + ob, r, g, H, DV),)

    y_gla = _rows(readout_fwd, N, TR, [_T(o_f, VALW), _T(o_b, VALW), _T(z_al, VALW, AR // VALW), _W(gn_full)],
                  [(VALW, BF16)], [], "gla_readout")[0]

    bs_col = b_s[0].reshape(SG_GROUPS, SG_CHUNK, 1)

    def sg_fwd(zu, zv, lng, lnb, ws, bs):
        u, vv = _sg_pre(zu, zv, lng, lnb)
        return (u * _sg_mix(vv, ws, bs, GW),)

    y_sg = _rows(sg_fwd, N, SG_CHUNK, [_T(z_al, SGW, 0), _T(z_al, SGW, 1), _W(sg_ln_g), _W(sg_ln_b), _W(w_s[0]), _W(bs_col)],
                 [(SGW, BF16)], [], "sg_fwd")[0]
    ycat = jnp.concatenate([y_gla, y_sg], axis=1)
    y, w1_g, w2_buf = _mm(ycat, w_o_f, N, D, D, name="out_proj",
                          side=[_ph(w1_buf, [("gather_d2d", 2, 4)]), _ph(w2_buf, [("gather_ici", 0, 1)])])
    def between_sublayers(xv, yv, gv, pv, g2v, scv, shv):
        x1v = _postnorm(xv, yv, gv, pv)
        return x1v, _prenorm(x1v, g2v, scv, shv)

    x1, h2 = _rows(between_sublayers, N, TR, [_T(x2, D), _T(y, D), _W(gt1), _W(post1_g), _W(pre2_g), _W(sc2), _W(sh2)],
                   [(D, F32), (D, BF16)], [], "postnorm1_prenorm2")

    tm1, tn1, tk1 = min(1024, N), min(1024, FS), min(2048, RH)
    w1_fwd_spec = pl.BlockSpec((None, tk1, tn1), lambda i, j, k: (2 * ((j * tn1) // FS) + (k * tk1) // RH, ((k * tk1) % RH) // tk1, ((j * tn1) % FS) // tn1))

    def relu2_epi(r):
        rf = jnp.maximum(r, 0.0)
        return rf * rf, rf

    act, rf, w2_g = _mm(h2, w1_g, N, F, D, name="mlp_up", out_dtypes=(BF16, BF16), tm=tm1, tn=tn1, tk=tk1, b_spec=w1_fwd_spec, epi=relu2_epi,
                        side=[_ph(w2_buf, [("gather_d2d", 0, 1), ("gather_chain", 1, 4)])], side_mid=0.88)
    w_2_f = w2_g.reshape(F, D)
    y2 = _mm(act, w_2_f, N, D, F, name="mlp_down")

    def final(x1v, y2v, gv, pv, tv):
        def loss_fn(x1a, y2a, ga, pa):
            err = _postnorm(x1a, y2a, ga, pa) - tv
            return 0.5 * jnp.sum(jnp.mean(err * err, axis=-1))
        loss, grads = jax.value_and_grad(loss_fn, argnums=(0, 1, 2, 3))(x1v, y2v, gv, pv)
        return grads[0], grads[1], jnp.full((1, LANES), loss, F32), _colsum(grads[2]), _colsum(grads[3])

    dx2, dy2, loss_acc, dgt2, dpost2 = _rows(final, N, TR, [_T(x1, D), _T(y2, D), _W(gt2), _W(post2_g), _T(tgt, D)],
                                             [(D, F32), (D, BF16)], [(1, LANES), (1, D), (1, D)], "loss_postnorm2_bwd")

    df = _mm(dy2, w_2_f, N, F, D, name="mlp_down_dx", tb=True, out_dtypes=(BF16,), epi=lambda r, rfv: (r * (2.0 * rfv.astype(F32)),),
             epi_in=(rf,), epi_specs=(pl.BlockSpec((min(1024, N), min(1024, F)), lambda i, j, k: (i, j)),))
    dw2 = _mm(act, dy2, F, D, N, name="mlp_down_dw", ta=True, out_dtypes=(BF16,)).reshape(N_DEV, F // N_DEV, D)
    tnb, tkb = min(1024, D, RH), min(2048, FS)
    w1_bwd_spec = pl.BlockSpec((None, tnb, tkb), lambda i, j, k: (2 * ((k * tkb) // FS) + (j * tnb) // RH, ((j * tnb) % RH) // tnb, ((k * tkb) % FS) // tkb))
    dh2, recv1_w2 = _mm(df, w1_g, N, D, F, name="mlp_up_dx", tb=True, tn=tnb, tk=tkb, b_spec=w1_bwd_spec,
                         side=[_ph(lax.empty((N_CHIP,) + dw2.shape[1:], BF16), [("rs_sibling", 0, 4)], src=dw2)])
    part_w2 = _sum_sibling(dw2, recv1_w2, c_arr, "rs_sum_sibling_w_2")
    tmw, tnw = min(1024, RH), min(1024, FS)
    dw1_spec = pl.BlockSpec((None, tmw, tnw), lambda i, j, k: (2 * ((j * tnw) // FS) + (i * tmw) // RH, ((i * tmw) % RH) // tmw, ((j * tnw) % FS) // tnw))
    dw1, recv2_w2 = _mm(h2, df, D, F, N, name="mlp_up_dw", ta=True, tm=tmw, tn=tnw, out_dtypes=(BF16,), out_specs=[dw1_spec],
                        out_shapes=[jax.ShapeDtypeStruct((N_DEV, RH, FS), BF16)],
                        side=[_ph(lax.empty((3,) + part_w2.shape[1:], BF16), [("rs_chips", 0, 3)], src=part_w2)])

    def prenorm_bwd(xv, gv, scv, shv, dh, dres):
        _, vjp = jax.vjp(_prenorm, xv, gv, scv, shv)
        dx, dg, dsc, dsh = vjp(dh)
        return dx + dres, _colsum(dg), _colsum(dsc), _colsum(dsh)

    def between_sublayers_bwd(xv, gv, scv, shv, dh, dres, yv, g1v, pv):
        dx1v, dg, dsc, dsh = prenorm_bwd(xv, gv, scv, shv, dh, dres)
        _, vjp = jax.vjp(lambda ya, ga, pa: _postnorm(0.0, ya, ga, pa), yv, g1v, pv)
        dy_, dg1_, dp_ = vjp(dx1v)
        return dx1v, dy_, dg, dsc, dsh, _colsum(dg1_), _colsum(dp_)

    dx1, dy, dpre2, dsc2, dsh2, dgt1, dpost1 = _rows(
        between_sublayers_bwd, N, TR, [_T(x1, D), _W(pre2_g), _W(sc2), _W(sh2), _T(dh2, D), _T(dx2, D), _T(y, D), _W(gt1), _W(post1_g)],
        [(D, F32), (D, BF16)], [(1, D)] * 5, "prenorm2_postnorm1_bwd")
    dycat, recv2_w2, recv1_w1 = _mm(dy, w_o_f, N, D, D, name="out_proj_dx", tb=True,
                                    side=[_ph(recv2_w2, [("rs_chips", 3, 4)], src=part_w2),
                                          _ph(lax.empty((N_CHIP,) + dw1.shape[1:], BF16), [("rs_sibling", 0, 4)], src=dw1)])
    part_w1 = _sum_sibling(dw1, recv1_w1, c_arr, "rs_sum_sibling_w_1")
    dwo = _mm(ycat, dy, D, D, N, name="out_proj_dw", ta=True, out_dtypes=(BF16,)).reshape(N_DEV, D // N_DEV, D)

    def readout_bwd(of, ob, r, g, dyv):
        _, vjp = jax.vjp(lambda o_, r_, g_: _readout(o_, r_, g_, H, DV), of + ob, r, g)
        do_, dr_, dg_ = vjp(dyv)
        return do_, dr_, _colsum(dg_)

    do_x, dz_r, dgn = _rows(readout_bwd, N, TR, [_T(o_f, VALW), _T(o_b, VALW), _T(z_al, VALW, AR // VALW), _W(gn_full), _T(dycat, VALW, 0)],
                            [(VALW, F32), (VALW, BF16)], [(1, VALW)], "gla_readout_bwd")

    def sg_bwd(zu, zv, lng, lnb, ws, bs, dyv):
        (u, vv), vjp = jax.vjp(_sg_pre, zu, zv, lng, lnb)
        s = _sg_mix(vv, ws, bs, GW)
        du, ds = dyv * s, dyv * u
        dws, dbs, dvv = [], [], []
        for g in range(SG_GROUPS):
            dsg = ds[:, g * GW:(g + 1) * GW]
            dsb = dsg.astype(BF16)
            dws.append(_dg(dsb, vv[:, g * GW:(g + 1) * GW].astype(BF16), 1, 1))
            dbs.append(jnp.sum(dsg, axis=1, keepdims=True))
            dvv.append(_dg(ws[g].astype(BF16), dsb, 0, 0))
        dzu, dzv, dlng, dlnb = vjp((du, jnp.concatenate(dvv, axis=1)))
        return jnp.concatenate([dzu, dzv], axis=1), _colsum(dlng), _colsum(dlnb), jnp.concatenate(dws, axis=0), jnp.concatenate(dbs, axis=0)

    dz_sg, dlng, dlnb, dws, dbs = _rows(sg_bwd, N, SG_CHUNK, [_T(z_al, SGW, 0), _T(z_al, SGW, 1), _W(sg_ln_g), _W(sg_ln_b), _W(w_s[0]), _W(bs_col), _T(dycat, SGW, VALW // SGW)],
                                        [(2 * SGW, BF16)], [(1, SGW), (1, SGW), (SG_GROUPS * SG_CHUNK, SG_CHUNK), (SG_GROUPS * SG_CHUNK, 1)], "sg_bwd")

    dq_f, dq_b, dk_f, dk_b, dv_f, dv_b, dla_f, dla_b, dst0, recv2_w1, recv1_wo = _gla_bwd(
        q_r, k_r, z_al, AV // DV, la_x, savf_x, savb_x, do_x, zero_state, H, DK, DV, "gla_bwd_x",
        side=[_ph(lax.empty((3,) + part_w1.shape[1:], BF16), [("rs_chips", 0, 3)], src=part_w1),
              _ph(lax.empty((N_CHIP,) + dwo.shape[1:], BF16), [("rs_sibling", 0, 4)], src=dwo)])
    part_wo = _sum_sibling(dwo, recv1_wo, c_arr, "rs_sum_sibling_w_o")
    _, _, dkc_f, dkc_b, dvc_f, dvc_b, dlac_f, dlac_b, _ = _gla_bwd(
        q_c, k_c, zc_al, AV // DV, la_c, savf_c, savb_c, jnp.zeros((NC, VALW), F32), dst0, H, DK, DV, "gla_bwd_ctx")

    def decay_bwd(lr, dla_f_v, dla_b_v, wdf_v, wdb_v, bf_v, bb_v):
        a_f, a_b = decay(lr, wdf_v, wdb_v, bf_v, bb_v)
        da_f = _chunk_cumsum(dla_f_v, False) * jax.nn.sigmoid(-a_f) / GLA_TAU
        da_b = _chunk_cumsum(dla_b_v, True) * jax.nn.sigmoid(-a_b) / GLA_TAU
        lrb, dfb, dbb = lr.astype(BF16), da_f.astype(BF16), da_b.astype(BF16)
        dlr = _dg(dfb, wdf_v.astype(BF16), 1, 1) + _dg(dbb, wdb_v.astype(BF16), 1, 1)
        return dlr, _dg(lrb, dfb, 0, 0), _dg(lrb, dbb, 0, 0), _colsum(da_f), _colsum(da_b)

    def prep_x_bwd(dq0, dq1, dk0, dk1, dv0, dv1, lr, dla0, dla1, cs, sn, dzsg, dzr, wdf_v, wdb_v, bf_v, bb_v):
        dlr, dwf, dwb, dbf, dbb = decay_bwd(lr, dla0, dla1, wdf_v, wdb_v, bf_v, bb_v)
        row = [dzsg, (_rope_t(dq0 + dq1, cs, sn, H, DK) * qscale).astype(BF16), _rope_t(dk0 + dk1, cs, sn, H, DK).astype(BF16),
               (dv0 + dv1).astype(BF16), dzr, dlr.astype(BF16), jnp.zeros((lr.shape[0], LR_PAD - LANES), BF16)]
        return jnp.concatenate(row, axis=1), dwf, dwb, dbf, dbb

    def prep_c_bwd(dk0, dk1, dv0, dv1, lr, dla0, dla1, wdf_v, wdb_v, bf_v, bb_v):
        dlr, dwf, dwb, dbf, dbb = decay_bwd(lr, dla0, dla1, wdf_v, wdb_v, bf_v, bb_v)
        n = lr.shape[0]
        row = [jnp.zeros((n, 2 * SGW + KEYW), BF16), (dk0 + dk1).astype(BF16), (dv0 + dv1).astype(BF16), jnp.zeros((n, VALW), BF16),
               dlr.astype(BF16), jnp.zeros((n, LR_PAD - LANES), BF16)]
        return jnp.concatenate(row, axis=1), dwf, dwb, dbf, dbb

    dec_acc = [(LANES, KEYW), (LANES, KEYW), (1, KEYW), (1, KEYW)]
    dz_cat, dwdf_x, dwdb_x, dbdf_x, dbdb_x = _rows(
        prep_x_bwd, N, TR, [_T(dq_f, KEYW), _T(dq_b, KEYW), _T(dk_f, KEYW), _T(dk_b, KEYW), _T(dv_f, VALW), _T(dv_b, VALW),
                            _T(z_al, LANES, ALR // LANES), _T(dla_f, KEYW), _T(dla_b, KEYW), _T(cos_t, DK), _T(sin_t, DK),
                            _T(dz_sg, 2 * SGW), _T(dz_r, VALW)] + dec_w,
        [(ACOLS, BF16)], dec_acc, "gla_prep_x_bwd", stacked=(N + NC, 0, None))
    dz_cat, dwdf_c, dwdb_c, dbdf_c, dbdb_c = _rows(
        prep_c_bwd, NC, TR, [_T(dkc_f, KEYW), _T(dkc_b, KEYW), _T(dvc_f, VALW), _T(dvc_b, VALW),
                             _T(zc_al, LANES, ALR // LANES), _T(dlac_f, KEYW), _T(dlac_b, KEYW)] + dec_w,
        [(ACOLS, BF16)], dec_acc, "gla_prep_ctx_bwd", stacked=(N + NC, N, dz_cat))
    tkd = _pick(ACOLS, 3584, 2 * LANES)
    tkt = _pick(N + NC, 2304)
    dw_al, recv2_w1, recv2_wo = _mm(h_cat, dz_cat, D, ACOLS, N + NC, name="in_proj_dw", ta=True, tn=tka, tk=tkt, out_dtypes=(BF16,),
                                    side=[_ph(recv2_w1, [("rs_chips", 3, 4)], src=part_w1),
                                          _ph(lax.empty((3,) + part_wo.shape[1:], BF16), [("rs_chips", 0, 4)], src=part_wo)])
    g_in = _shard_columns(dw_al, CS, SG0, 2 * SGW, -SG0, RH, "w_in_grad_blocks")
    recv1_in = _rs_sibling([g_in], "rs_sibling_w_in")[0]
    part_in = _sum_sibling(g_in, recv1_in, c_arr, "rs_sum_sibling_w_in")
    dhx, recv2_in = _mm(dz_cat, w_al, N, D, ACOLS, name="in_proj_dx", tb=True, tk=tkd,
                        side=[_ph(lax.empty((3,) + part_in.shape[1:], BF16), [("rs_chips", 0, 3.5)], src=part_in)])
    dhc = _mm(dz_cat, w_al, NC, D, ACOLS, name="in_proj_dctx", tb=True, tk=tkd,
              a_spec=pl.BlockSpec((tmc, tkd), lambda i, j, k: (N // tmc + i, k)))

    grad_x, dpre1_x, dsc1, dsh1 = _rows(prenorm_bwd, N, TR, [_T(x2, D), _W(pre1_g), _W(sc1), _W(sh1), _T(dhx, D), _T(dx1, D)],
                                        [(D, F32)], [(1, D)] * 3, "prenorm1_x_bwd")

    def prenorm_bwd_ctx(xv, gv, scv, shv, dh):
        _, vjp = jax.vjp(_prenorm, xv, gv, scv, shv)
        _, dg, dsc, dsh = vjp(dh)
        return _colsum(dg), _colsum(dsc), _colsum(dsh)

    dpre1_c, dcsc1, dcsh1 = _rows(prenorm_bwd_ctx, NC, TR, [_T(ctx2, D), _W(pre1_g), _W(csc1), _W(csh1), _T(dhc, D)],
                                  [], [(1, D)] * 3, "prenorm1_ctx_bwd")

    half = [_sum_chips(p, r, sc_arr, "rs_sum_chips_" + nm)
            for p, r, nm in zip((part_wo, part_w1, part_w2), (recv2_wo, recv2_w1, recv2_w2), ("w_o", "w_1", "w_2"))]
    g_w_o, g_w_1, g_w_2 = [g.reshape(w.shape[1:]) for g, w in zip(_rs_final(half, "rs_final"), (w_o, w_1, w_2))]

    dmod_x = jnp.concatenate([dsh1, dsc1, dgt1, dsh2, dsc2, dgt2], axis=1)
    dmodc = jnp.concatenate([dcsh1, dcsc1], axis=1)
    small_parts = [loss_acc, dmod_x, dmodc, dpre1_x + dpre1_c, dpost1, dpre2, dpost2, dwdf_x + dwdf_c, dbdf_x + dbdf_c,
                   dwdb_x + dwdb_c, dbdb_x + dbdb_c, dgn, dlng, dlnb, dws, dbs]
    small_shapes = [p.shape for p in small_parts]
    packed = _pack(small_parts)
    n_sm = packed.shape[1]
    gathered = _allgather_small(packed, "gather_small_grads")

    def sum_devices(g):
        tot = g[0:8]
        for dev in range(1, N_DEV):
            tot = tot + g[8 * dev:8 * dev + 8]
        return (tot,)

    summed = _rows(sum_devices, N_DEV * 8, N_DEV * 8, [_W(gathered)], [], [(8, n_sm)], "sum_small_grads")[0]
    (loss_s, dmod_sum, dmodc_sum, g_pre1, g_post1, g_pre2, g_post2, g_wdf_pad, g_bdf, g_wdb_pad, g_bdb, g_gn, g_lng, g_lnb,
     g_ws, g_bs) = _unpack(summed.reshape(-1), small_shapes)
    loss = loss_s[0, 0]
    dmod_rows = gathered.reshape(N_DEV, -1)[:, LANES:LANES + N_MOD * D]
    g_b_ada = dmod_sum + jnp.pad(dmodc_sum, ((0, 0), (0, (N_MOD - 2) * D)))
    dmod16 = jnp.zeros((16, N_MOD * D), F32).at[:N_DEV].set(dmod_rows).at[N_DEV, :2 * D].set(dmodc_sum[0])
    dmod16_sh = lax.dynamic_slice(dmod16, (0, s_me * MS), (16, MS))
    g_w_ada, d_w_ada, nm_w_ada, nv_w_ada = _ada_update(cond.T, dmod16_sh, w_ada[0], m_w_ada[0], v_w_ada[0], "w_ada_update")

    dcond, recv2_in = _mm(dmod16_sh, w_ada[0], 16, D, MS, name="cond_bwd", tb=True, tk=min(512, MS),
                          side=[_ph(recv2_in, [("rs_chips", 3.5, 4)], src=part_in)])
    half_in = _sum_chips(part_in, recv2_in, sc_arr, "rs_sum_chips_w_in")
    g_w_in_t = _transpose_cols(_rs_final([half_in], "rs_final_w_in")[0].reshape(D, -1), CS, "w_in_grad_t")
    part_c = _allgather_small(dcond[N_DEV].reshape(8, D // 8), "gather_dcond").reshape(N_DEV, D)

    def cctx_grad(p, cv):
        sg = jax.nn.sigmoid(cv)
        tot = ((p[0:1] + p[2:3]) + p[4:5]) + p[6:7]
        return (jnp.broadcast_to(tot * (sg * (1.0 + cv * (1.0 - sg))), p.shape),)

    g_c_ctx = _rows(cctx_grad, N_DEV, N_DEV, [_W(part_c), _W(c_ctx.reshape(1, D))], [(D, F32)], [], "c_ctx_grad")[0][0:1]

    def col_shard(g_full, width):
        return lax.dynamic_slice_in_dim(g_full, s_me * width, width, axis=g_full.ndim - 1)

    g_w_dec_f = col_shard(g_wdf_pad[:GLA_LOWRANK], KEYW // N_CHIP)
    g_w_dec_b = col_shard(g_wdb_pad[GLA_LOWRANK:LR], KEYW // N_CHIP)
    g_gla_norm = col_shard(g_gn.reshape(H, DV), DV // N_CHIP)
    small_w = [c_ctx, b_ada, pre1_g, post1_g, pre2_g, post2_g, w_dec_f, b_dec_f, w_dec_b, b_dec_b, gla_norm_g, sg_ln_g, sg_ln_b, w_s, b_s]
    small_m = [m_c_ctx, m_b_ada, m_pre1_g, m_post1_g, m_pre2_g, m_post2_g, m_w_dec_f, m_b_dec_f, m_w_dec_b, m_b_dec_b, m_gla_norm_g, m_sg_ln_g, m_sg_ln_b, m_w_s, m_b_s]
    small_v = [v_c_ctx, v_b_ada, v_pre1_g, v_post1_g, v_pre2_g, v_post2_g, v_w_dec_f, v_b_dec_f, v_w_dec_b, v_b_dec_b, v_gla_norm_g, v_sg_ln_g, v_sg_ln_b, v_w_s, v_b_s]
    small_g = [g_c_ctx, g_b_ada, g_pre1, g_post1, g_pre2, g_post2, g_w_dec_f, g_bdf, g_w_dec_b, g_bdb, g_gla_norm, g_lng, g_lnb, g_ws, g_bs]
    small_g = [g.reshape(w.shape) for g, w in zip(small_g, small_w)]
    d_small, m_small, v_small = _adamw_many(small_w, small_g, small_m, small_v, "adamw_small")

    def big(w, g, m, v, name):
        shp = w.shape
        res = _adamw(w.reshape(shp[-2:]), g.reshape(shp[-2:]), m.reshape(shp[-2:]), v.reshape(shp[-2:]), name)
        return [g.reshape(shp)] + [r.reshape(shp) for r in res]

    r_in = [jnp.swapaxes(t, 1, 2) for t in big(jnp.swapaxes(w_in, 1, 2), g_w_in_t[None], jnp.swapaxes(m_w_in, 1, 2),
                                               jnp.swapaxes(v_w_in, 1, 2), "adamw_w_in")]
    r_o = big(w_o, g_w_o, m_w_o, v_w_o, "adamw_w_o")
    r_1 = big(w_1, g_w_1, m_w_1, v_w_1, "adamw_w_1")
    r_2 = big(w_2, g_w_2, m_w_2, v_w_2, "adamw_w_2")
    r_ada = [t.reshape(w_ada.shape) for t in (g_w_ada, d_w_ada, nm_w_ada, nv_w_ada)]

    def ordered(k):
        sm = [small_g, d_small, m_small, v_small][k]
        return [sm[0], r_ada[k], *sm[1:6], r_in[k], *sm[6:15], r_o[k], r_1[k], r_2[k]]

    return (loss, grad_x.reshape(x.shape), *ordered(0), *ordered(1), *ordered(2), *ordered(3))
```

```python
import functools
import math

import numpy as np
import jax
import jax.numpy as jnp
from jax import lax
from jax.experimental import pallas as pl
from jax.experimental.pallas import tpu as pltpu

F32 = jnp.float32
BF16 = jnp.bfloat16
MESH = pl.DeviceIdType.MESH
ANY = pl.BlockSpec(memory_space=pl.ANY)

GLA_HEADS = 8
GLA_CHUNK = 64
GLA_LOWRANK = 16
GLA_TAU = 16.0
ROPE_BASE = 10000.0
GRID_W = 64
SG_GROUPS = 4
SG_CHUNK = 128
N_MOD = 6
EPS = 1e-6
ADAM_LR = 0.001
ADAM_B1 = 0.9
ADAM_B2 = 0.999
ADAM_EPS = 1e-08
ADAM_WD = 0.01
ADAM_STEP = 10

LANES = 128
VMEM_LIMIT = 56 << 20
N_DEV = 8
N_CHIP = 4


def _params(sem=None):
    return pltpu.CompilerParams(dimension_semantics=sem, vmem_limit_bytes=VMEM_LIMIT)


def _pick(dim, target, unit=LANES):
    best = None
    for t in range(unit, min(dim, target) + 1, unit):
        if dim % t == 0:
            best = t
    return dim if best is None else best


def _dg(a, b, ca, cb, precision=None):
    return lax.dot_general(a, b, (((ca,), (cb,)), ((), ())), preferred_element_type=F32,
                           precision=precision)


def _place():
    return lax.axis_index("x"), lax.axis_index("y"), lax.axis_index("c")


def _allgather_small(v, name):
    m_per, n = v.shape

    def body(x_ref, out_ref, send_sems, recv_sems, local_sem):
        x, y, c = _place()
        me, sibling = (x, y, c), (x, y, 1 - c)
        chips = [(1 - x, y), (x, 1 - y), (1 - x, 1 - y)]

        def rows(px, py, pc):
            return out_ref.at[pl.ds((4 * px + 2 * py + pc) * m_per, m_per), :]

        def copy(k, block, to, src=None):
            return pltpu.make_async_remote_copy(
                src_ref=rows(*block) if src is None else src, dst_ref=rows(*block),
                send_sem=send_sems.at[k], recv_sem=recv_sems.at[k],
                device_id=to, device_id_type=MESH)

        mine = pltpu.make_async_copy(x_ref, rows(*me), local_sem)
        mine.start()
        first = [copy(0, me, sibling, src=x_ref)]
        first += [copy(1 + j, me, (*chip, c), src=x_ref) for j, chip in enumerate(chips)]
        for cp in first:
            cp.start()
        passed = [copy(4 + j, (*chip, c), sibling) for j, chip in enumerate(chips)]
        for j, chip in enumerate(chips):
            copy(1 + j, (*chip, c), me).wait_recv()
            passed[j].start()
        copy(0, sibling, me).wait_recv()
        for j, chip in enumerate(chips):
            copy(4 + j, (*chip, 1 - c), me).wait_recv()
        for cp in first + passed:
            cp.wait_send()
        mine.wait()

    return pl.pallas_call(
        body, name=name,
        out_shape=jax.ShapeDtypeStruct((N_DEV * m_per, n), v.dtype),
        in_specs=[pl.BlockSpec(memory_space=pltpu.VMEM)],
        out_specs=pl.BlockSpec(memory_space=pltpu.VMEM),
        scratch_shapes=[pltpu.SemaphoreType.DMA((7,)), pltpu.SemaphoreType.DMA((7,)),
                        pltpu.SemaphoreType.DMA],
        compiler_params=pltpu.CompilerParams(vmem_limit_bytes=VMEM_LIMIT),
    )(v)


def _cast_blocks(w, s_me, name):
    _, r, cols = w.shape
    tr = _row_tile(r, cols, 4)

    def body(s_ref, w_ref, o_ref):
        o_ref[...] = w_ref[...].astype(BF16)

    return pl.pallas_call(
        body, name=name,
        out_shape=jax.ShapeDtypeStruct((N_DEV, r, cols), BF16),
        grid_spec=pltpu.PrefetchScalarGridSpec(
            num_scalar_prefetch=1, grid=(2, r // tr),
            in_specs=[pl.BlockSpec((None, tr, cols), lambda h, i, s: (h, i, 0))],
            out_specs=pl.BlockSpec((None, tr, cols), lambda h, i, s: (2 * s[0] + h, i, 0))),
        compiler_params=_params(("arbitrary", "arbitrary")),
    )(s_me, w)


def _gather_big(ws, name):
    nw = len(ws)

    def body(*refs):
        outs = refs[nw:2 * nw]
        send_sems, recv_sems = refs[2 * nw:]
        x, y, c = _place()
        me, sibling = (x, y, c), (x, y, 1 - c)
        chips = [(1 - x, y), (x, 1 - y), (1 - x, 1 - y)]

        def blk(px, py, pc):
            return 4 * px + 2 * py + pc

        def copy(w, k, block, to):
            return pltpu.make_async_remote_copy(
                src_ref=outs[w].at[block], dst_ref=outs[w].at[block],
                send_sem=send_sems.at[6 * w + k], recv_sem=recv_sems.at[6 * w + k],
                device_id=to, device_id_type=MESH)

        first = []
        for w in range(nw):
            for j, chip in enumerate(chips):
                cp = copy(w, j, blk(x, y, c), (*chip, c))
                cp.start()
                first.append(cp)
        passed = []
        for w in range(nw):
            for j, chip in enumerate(chips):
                copy(w, j, blk(*chip, c), me).wait_recv()
                cp = copy(w, 3 + j, blk(*chip, c), sibling)
                cp.start()
                passed.append(cp)
        for w in range(nw):
            for j, chip in enumerate(chips):
                copy(w, 3 + j, blk(*chip, 1 - c), me).wait_recv()
        for cp in first + passed:
            cp.wait_send()

    return pl.pallas_call(
        body, name=name,
        out_shape=[jax.ShapeDtypeStruct(w.shape, w.dtype) for w in ws],
        in_specs=[ANY] * nw, out_specs=[ANY] * nw,
        input_output_aliases={w: w for w in range(nw)},
        scratch_shapes=[pltpu.SemaphoreType.DMA((6 * nw,)), pltpu.SemaphoreType.DMA((6 * nw,))],
    )(*ws)


def _rs_sibling(gs, name):
    nw = len(gs)

    def body(*refs):
        ins, outs = refs[:nw], refs[nw:2 * nw]
        send_sems, recv_sems = refs[2 * nw:]
        x, y, c = _place()
        cps = []
        for w in range(nw):
            for s in range(N_CHIP):
                cp = pltpu.make_async_remote_copy(
                    src_ref=ins[w].at[2 * s + (1 - c)], dst_ref=outs[w].at[s],
                    send_sem=send_sems.at[N_CHIP * w + s], recv_sem=recv_sems.at[N_CHIP * w + s],
                    device_id=(x, y, 1 - c), device_id_type=MESH)
                cp.start()
                cps.append(cp)
        for cp in cps:
            cp.wait()

    return pl.pallas_call(
        body, name=name,
        out_shape=[jax.ShapeDtypeStruct((N_CHIP,) + g.shape[1:], g.dtype) for g in gs],
        in_specs=[ANY] * nw, out_specs=[ANY] * nw,
        scratch_shapes=[pltpu.SemaphoreType.DMA((N_CHIP * nw,)), pltpu.SemaphoreType.DMA((N_CHIP * nw,))],
    )(*gs)


def _rs_final(fs, name):
    nw = len(fs)

    def body(*refs):
        outs = refs[nw:2 * nw]
        send_sems, recv_sems = refs[2 * nw:]
        x, y, c = _place()
        cps = []
        for w in range(nw):
            cp = pltpu.make_async_remote_copy(
                src_ref=outs[w].at[c], dst_ref=outs[w].at[c],
                send_sem=send_sems.at[w], recv_sem=recv_sems.at[w],
                device_id=(x, y, 1 - c), device_id_type=MESH)
            cp.start()
            cps.append(cp)
        for cp in cps:
            cp.wait()

    return pl.pallas_call(
        body, name=name,
        out_shape=[jax.ShapeDtypeStruct(f.shape, f.dtype) for f in fs],
        in_specs=[ANY] * nw, out_specs=[ANY] * nw,
        input_output_aliases={w: w for w in range(nw)},
        scratch_shapes=[pltpu.SemaphoreType.DMA((nw,)), pltpu.SemaphoreType.DMA((nw,))],
    )(*fs)


_PHASE_COPIES = {"gather_ici": 3, "gather_d2d": 3, "gather_chain": 6, "rs_sibling": N_CHIP, "rs_chips": 3}
QUARTERS = 4


def _ph(buf, legs, src=None):
    return dict(buf=buf, src=src, legs=legs)


def _n_copies(ph):
    return sum(_PHASE_COPIES[kind] for kind, _, _ in ph["legs"])


def _phase_copies(ph, src, buf, send_sems, recv_sems, base):
    x, y, c = _place()
    sibling = (x, y, 1 - c)
    chips = [(1 - x, y), (x, 1 - y), (1 - x, 1 - y)]
    r = buf.shape[1]

    def make(k, trip):
        a, b, dev = trip
        return pltpu.make_async_remote_copy(src_ref=a, dst_ref=b, send_sem=send_sems.at[base + k], recv_sem=recv_sems.at[base + k],
                                            device_id=dev, device_id_type=MESH)

    out = []
    for kind, lo, hi in ph["legs"]:
        rows = pl.ds(int(lo * r) // QUARTERS, int((hi - lo) * r) // QUARTERS)
        ici = [(buf.at[4 * x + 2 * y + c, rows], buf.at[4 * x + 2 * y + c, rows], (*chip, c)) for chip in chips]
        d2d = [(buf.at[4 * chip[0] + 2 * chip[1] + c, rows], buf.at[4 * chip[0] + 2 * chip[1] + c, rows], sibling) for chip in chips]
        if kind == "gather_ici":
            trips, later = ici, []
        elif kind == "gather_d2d":
            trips, later = d2d, []
        elif kind == "gather_chain":
            trips, later = ici, d2d
        elif kind == "rs_sibling":
            trips, later = [(src.at[2 * s + (1 - c), rows], buf.at[s, rows], sibling) for s in range(N_CHIP)], []
        else:
            trips, later = [(src.at[2 * chip[0] + chip[1], rows], buf.at[j, rows], (*chip, c)) for j, chip in enumerate(chips)], []
        out.append(([make(k, t) for k, t in enumerate(trips)], [make(len(trips) + k, t) for k, t in enumerate(later)]))
        base += _PHASE_COPIES[kind]
    return out


def _side_call(inner, grid, in_specs, out_specs, out_shape, scratch, args, phases, name, semantics, mid=0.8):
    n_in, n_out, n_ph = len(in_specs), len(out_specs), len(phases)
    if n_ph == 0:
        outs = pl.pallas_call(inner, name=name, grid=grid, in_specs=in_specs, out_specs=out_specs, out_shape=out_shape,
                              scratch_shapes=scratch, compiler_params=_params(semantics))(*args)
        return list(outs), []
    n_cp = sum(_n_copies(p) for p in phases)
    side_args, buf_pos, src_pos = [], [], []
    for p in phases:
        buf_pos.append(len(side_args))
        side_args.append(p["buf"])
        src_pos.append(len(side_args) if p["src"] is not None else None)
        if p["src"] is not None:
            side_args.append(p["src"])
    n_side = len(side_args)
    total = math.prod(grid)
    mid_lin = min(total - 1, int(total * mid))

    def body(*refs):
        b_in, s_in = refs[:n_in], refs[n_in:n_in + n_side]
        b_out, s_out = refs[n_in + n_side:n_in + n_side + n_out], refs[n_in + n_side + n_out:n_in + n_side + n_out + n_ph]
        rest = refs[n_in + n_side + n_out + n_ph:]
        send_sems, recv_sems = rest[-2:]
        lin = functools.reduce(lambda acc, ag: acc * ag[1] + pl.program_id(ag[0]), list(enumerate(grid))[1:], pl.program_id(0))

        def copies():
            out, base = [], 0
            for p, sp, so in zip(phases, src_pos, s_out):
                out += _phase_copies(p, None if sp is None else s_in[sp], so, send_sems, recv_sems, base)
                base += _n_copies(p)
            return out

        @pl.when(lin == 0)
        def _():
            for a, _ in copies():
                for cp in a:
                    cp.start()

        if any(kind == "gather_chain" for p in phases for kind, _, _ in p["legs"]):
            @pl.when(lin == mid_lin)
            def _():
                for a, b in copies():
                    if b:
                        for cp in a:
                            cp.wait()
                        for cp in b:
                            cp.start()

        inner(*b_in, *b_out, *rest[:-2])

        @pl.when(lin == total - 1)
        def _():
            for a, b in copies():
                for cp in (b if b else a):
                    cp.wait()

    outs = pl.pallas_call(
        body, name=name, grid=grid,
        in_specs=list(in_specs) + [ANY] * n_side, out_specs=list(out_specs) + [ANY] * n_ph,
        out_shape=list(out_shape) + [jax.ShapeDtypeStruct(p["buf"].shape, p["buf"].dtype) for p in phases],
        input_output_aliases={n_in + bp: n_out + k for k, bp in enumerate(buf_pos)},
        scratch_shapes=list(scratch) + [pltpu.SemaphoreType.DMA((n_cp,)), pltpu.SemaphoreType.DMA((n_cp,))],
        compiler_params=_params(("arbitrary",) * len(grid)),
    )(*args, *side_args)
    return list(outs[:n_out]), list(outs[n_out:])


def _row_tile(r, cols, itemsize):
    if r * cols * itemsize <= (2 << 20):
        return r
    fits = [t for t in range(8, r, 8) if r % t == 0 and t * cols * itemsize <= (2 << 20)]
    return max(fits) if fits else r


def _transpose_cols(g, n_cols, name):
    rows, wp = g.shape
    tr = min(rows, 512)

    def body(g_ref, o_ref):
        o_ref[...] = g_ref[...].T[:n_cols]

    return pl.pallas_call(
        body, name=name, grid=(rows // tr,), out_shape=jax.ShapeDtypeStruct((n_cols, rows), F32),
        in_specs=[pl.BlockSpec((tr, wp), lambda i: (i, 0))], out_specs=pl.BlockSpec((n_cols, tr), lambda i: (0, i)),
        compiler_params=_params(("parallel",)),
    )(g)


def _sum_sibling(g, r1, c_me, name):
    _, r, cols = g.shape
    tr = _row_tile(r, cols, 4)

    def body(c_ref, g_ref, r_ref, o_ref):
        o_ref[...] = (g_ref[...].astype(F32) + r_ref[...].astype(F32)).astype(o_ref.dtype)

    return pl.pallas_call(
        body, name=name,
        out_shape=jax.ShapeDtypeStruct((N_CHIP, r, cols), g.dtype),
        grid_spec=pltpu.PrefetchScalarGridSpec(
            num_scalar_prefetch=1, grid=(N_CHIP, r // tr),
            in_specs=[pl.BlockSpec((None, tr, cols), lambda s, i, c: (2 * s + c[0], i, 0)),
                      pl.BlockSpec((None, tr, cols), lambda s, i, c: (s, i, 0))],
            out_specs=pl.BlockSpec((None, tr, cols), lambda s, i, c: (s, i, 0))),
        compiler_params=_params(("arbitrary", "arbitrary")),
    )(c_me, g, r1)


def _sum_chips(p, r2, sc_me, name):
    _, r, cols = p.shape
    tr = _row_tile(r, cols, 4)

    def body(s_ref, p_ref, a_ref, b_ref, c_ref, o_ref):
        o_ref[...] = ((p_ref[...].astype(F32) + a_ref[...].astype(F32)) + b_ref[...].astype(F32)) + c_ref[...].astype(F32)

    return pl.pallas_call(
        body, name=name,
        out_shape=jax.ShapeDtypeStruct((2, r, cols), F32),
        grid_spec=pltpu.PrefetchScalarGridSpec(
            num_scalar_prefetch=1, grid=(r // tr,),
            in_specs=[pl.BlockSpec((None, tr, cols), lambda i, s: (s[0], i, 0)),
                      pl.BlockSpec((None, tr, cols), lambda i, s: (0, i, 0)),
                      pl.BlockSpec((None, tr, cols), lambda i, s: (1, i, 0)),
                      pl.BlockSpec((None, tr, cols), lambda i, s: (2, i, 0))],
            out_specs=pl.BlockSpec((None, tr, cols), lambda i, s: (s[1], i, 0))),
        compiler_params=_params(("arbitrary",)),
    )(sc_me, p, r2, r2, r2)


def _shard_columns(g_al, shard_cols, bound, off_lo, off_hi, rh, name):
    d, acols = g_al.shape
    wp = -(-shard_cols // LANES) * LANES
    tr = min(LANES, rh)
    nt = acols // LANES

    def body(x_ref, o_ref):
        s = pl.program_id(1)
        lane = lax.broadcasted_iota(jnp.int32, (tr, LANES), 1)

        def tile(q):
            q = max(0, min(nt - 1, q))
            return x_ref[:, q * LANES:(q + 1) * LANES].astype(F32)

        def read(start):
            q, sh = divmod(start, LANES)
            if sh == 0:
                return tile(q)
            return jnp.where(lane < LANES - sh, pltpu.roll(tile(q), LANES - sh, 1), pltpu.roll(tile(q + 1), LANES - sh, 1))

        for k in range(N_CHIP):
            @pl.when(s == k)
            def _(k=k):
                for t in range(wp // LANES):
                    n0 = k * shard_cols + t * LANES
                    if n0 + LANES <= bound:
                        v = read(n0 + off_lo)
                    elif n0 >= bound:
                        v = read(n0 + off_hi)
                    else:
                        v = jnp.where(lane < bound - n0, read(n0 + off_lo), read(n0 + off_hi))
                    o_ref[:, t * LANES:(t + 1) * LANES] = v.astype(o_ref.dtype)

    return pl.pallas_call(
        body, name=name, grid=(d // tr, N_CHIP),
        out_shape=jax.ShapeDtypeStruct((N_DEV, rh, wp), BF16),
        in_specs=[pl.BlockSpec((tr, acols), lambda i, s: (i, 0))],
        out_specs=pl.BlockSpec((None, tr, wp), lambda i, s: (2 * s + (i * tr) // rh, ((i * tr) % rh) // tr, 0)),
        compiler_params=_params(("arbitrary", "arbitrary")),
    )(g_al)


def _mm(a, b, M, N, K, *, name, ta=False, tb=False, out_dtypes=(F32,), tm=1024, tn=1024, tk=2048,
        a_spec=None, b_spec=None, out_specs=None, out_shapes=None, epi=None, epi_in=(), epi_specs=(), side=(), side_mid=0.8):
    tm, tn, tk = min(tm, M), min(tn, N), min(tk, K)
    assert M % tm == 0 and N % tn == 0 and K % tk == 0, (name, M, N, K, tm, tn, tk)
    nk = K // tk
    n_epi, n_out = len(epi_in), len(out_dtypes)
    if a_spec is None:
        a_spec = pl.BlockSpec((tk, tm), lambda i, j, k: (k, i)) if ta else pl.BlockSpec((tm, tk), lambda i, j, k: (i, k))
    if b_spec is None:
        b_spec = pl.BlockSpec((tn, tk), lambda i, j, k: (j, k)) if tb else pl.BlockSpec((tk, tn), lambda i, j, k: (k, j))
    if out_specs is None:
        out_specs = [pl.BlockSpec((tm, tn), lambda i, j, k: (i, j))] * n_out
        out_shapes = [jax.ShapeDtypeStruct((M, N), dt) for dt in out_dtypes]

    def body(a_ref, b_ref, *rest):
        epi_refs, o_refs, acc = rest[:n_epi], rest[n_epi:n_epi + n_out], rest[-1]
        k = pl.program_id(2)

        @pl.when(k == 0)
        def _():
            acc[...] = jnp.zeros_like(acc)

        acc[...] += _dg(a_ref[...].astype(BF16), b_ref[...].astype(BF16), 0 if ta else 1, 1 if tb else 0)

        @pl.when(k == nk - 1)
        def _():
            r = acc[...]
            vals = (r,) if epi is None else epi(r, *[e[...] for e in epi_refs])
            for o_ref, v in zip(o_refs, vals):
                o_ref[...] = v.astype(o_ref.dtype)

    outs, side_outs = _side_call(body, (M // tm, N // tn, nk), [a_spec, b_spec, *epi_specs], out_specs, out_shapes,
                                 [pltpu.VMEM((tm, tn), F32)], (a, b, *epi_in), list(side), name,
                                 ("parallel", "parallel", "arbitrary"), mid=side_mid)
    if side:
        return outs + side_outs
    return outs[0] if n_out == 1 else outs


def _T(arr, width, col=0, lead=None):
    return ("tile", arr, width, col, lead)


def _W(arr):
    return ("whole", arr)


def _rows(fn, n_rows, tr, ins, tile_outs, acc_outs, name, stacked=None):
    tr = min(tr, n_rows)
    assert n_rows % tr == 0, (name, n_rows, tr)
    total_rows, first_row, earlier = stacked if stacked is not None else (n_rows, 0, None)
    assert first_row % tr == 0
    in_specs, args = [], []
    for d in ins:
        if d[0] == "tile":
            _, arr, width, col, lead = d
            if lead is None:
                in_specs.append(pl.BlockSpec((tr, width), lambda i, col=col: (i, col)))
            else:
                in_specs.append(pl.BlockSpec((None, tr, width), lambda i, col=col, lead=lead: (lead, i, col)))
            args.append(arr)
        else:
            arr = d[1]
            in_specs.append(pl.BlockSpec(arr.shape, lambda i, nd=arr.ndim: (0,) * nd))
            args.append(arr)
    n_in, n_t = len(ins), len(tile_outs)
    out_shape = [jax.ShapeDtypeStruct((n_rows, w), dt) for w, dt in tile_outs]
    out_specs = [pl.BlockSpec((tr, w), lambda i: (i, 0)) for w, _ in tile_outs]
    if stacked is not None:
        out_shape[0] = jax.ShapeDtypeStruct((total_rows, tile_outs[0][0]), tile_outs[0][1])
        out_specs[0] = pl.BlockSpec((tr, tile_outs[0][0]), lambda i: (first_row // tr + i, 0))
    if earlier is not None:
        in_specs.append(ANY)
        args.append(earlier)
    out_shape += [jax.ShapeDtypeStruct(s, F32) for s in acc_outs]
    out_specs += [pl.BlockSpec(s, lambda i, nd=len(s): (0,) * nd) for s in acc_outs]

    n_args = len(args)

    def body(*refs):
        in_refs, t_refs, a_refs = refs[:n_in], refs[n_args:n_args + n_t], refs[n_args + n_t:]
        vals = fn(*[r[...] for r in in_refs])
        for r, v in zip(t_refs, vals[:n_t]):
            r[...] = v.astype(r.dtype)
        first = pl.program_id(0) == 0
        for r, v in zip(a_refs, vals[n_t:]):
            @pl.when(first)
            def _(r=r, v=v):
                r[...] = v

            @pl.when(jnp.logical_not(first))
            def _(r=r, v=v):
                r[...] += v

    return pl.pallas_call(
        body, name=name, out_shape=out_shape, grid=(n_rows // tr,),
        in_specs=in_specs, out_specs=out_specs,
        input_output_aliases={n_in: 0} if earlier is not None else {},
        compiler_params=_params(("arbitrary",)),
    )(*args)


def _colsum(t):
    return jnp.sum(t, axis=0, keepdims=True)


def _prenorm(x, g, sc, sh):
    xf = x.astype(F32)
    return xf * lax.rsqrt(jnp.mean(xf * xf, axis=-1, keepdims=True) + EPS) * g * (1.0 + sc) + sh


def _postnorm(x, y, gate, pg):
    return x + gate * (y * lax.rsqrt(jnp.mean(y * y, axis=-1, keepdims=True) + EPS) * pg)


def _gelu(t):
    return 0.5 * t * (1.0 + lax.erf(t * (2.0 ** -0.5)))


def _sg_pre(zu, zv, lng, lnb):
    u, vr = _gelu(zu), _gelu(zv)
    mu = jnp.mean(vr, axis=-1, keepdims=True)
    var = jnp.mean(jnp.square(vr - mu), axis=-1, keepdims=True)
    return u, (vr - mu) * lax.rsqrt(var + EPS) * lng + lnb


def _sg_mix(vv, ws_ref_vals, bs_vals, gw):
    parts = []
    for g in range(SG_GROUPS):
        s = _dg(ws_ref_vals[g].astype(BF16), vv[:, g * gw:(g + 1) * gw].astype(BF16), 1, 0)
        parts.append(s + bs_vals[g])
    return jnp.concatenate(parts, axis=1)


def _readout(o, r, g, heads, dv):
    parts = []
    for h in range(heads):
        oh = o[:, h * dv:(h + 1) * dv]
        parts.append(oh * lax.rsqrt(jnp.mean(oh * oh, axis=-1, keepdims=True) + EPS))
    return jnp.concatenate(parts, axis=1) * g * (r * jax.nn.sigmoid(r))


def _log_sigmoid(a):
    return jnp.minimum(a, 0.0) - jnp.log(1.0 + jnp.exp(-jnp.abs(a)))


def _rope_swap(t, m):
    lane = lax.broadcasted_iota(jnp.int32, t.shape, 1)
    return jnp.where((lane % (2 * m)) < m, pltpu.roll(t, 3 * m, 1), pltpu.roll(t, m, 1))


def _rope(t, cos, sin, heads, dk):
    parts = []
    for h in range(heads):
        th = t[:, h * dk:(h + 1) * dk]
        parts.append(th * cos + _rope_swap(th, dk // 4) * sin)
    return jnp.concatenate(parts, axis=1)


def _rope_t(dt, cos, sin, heads, dk):
    parts = []
    for h in range(heads):
        dh = dt[:, h * dk:(h + 1) * dk]
        parts.append(dh * cos + _rope_swap(dh * sin, dk // 4))
    return jnp.concatenate(parts, axis=1)


def _chunk_cumsum(t, upwards):
    n = t.shape[0]
    row = lax.broadcasted_iota(jnp.int32, (n, n), 0)
    col = lax.broadcasted_iota(jnp.int32, (n, n), 1)
    shift = GLA_CHUNK.bit_length() - 1
    same = jnp.right_shift(row, shift) == jnp.right_shift(col, shift)
    tri = jnp.logical_and(same, col <= row if upwards else col >= row)
    return _dg(tri.astype(F32), t, 1, 0, precision=lax.Precision.HIGHEST)


def _chunk_terms(d, qv, kv, b, C):
    row = lax.broadcasted_iota(jnp.int32, (C, C), 0)
    col = lax.broadcasted_iota(jnp.int32, (C, C), 1)
    tri = row >= col if d == 0 else row <= col
    end_row = lax.broadcasted_iota(jnp.int32, b.shape, 0) == (C - 1 if d == 0 else 0)
    btot = _colsum(jnp.where(end_row, b, 0.0))
    eb, enb, ebt = jnp.exp(b), jnp.exp(-b), jnp.exp(btot - b)
    return tri, btot, eb, enb, ebt, qv * eb, kv * enb, kv * ebt


def _gla_fwd(q, k, zv, v_col0, la, st0, heads, dk, dv, name, side=(), side_mid=0.8):
    n, C, H = q.shape[0], GLA_CHUNK, heads
    nc = n // C

    def body(qf, kf, vf, laf, qb, kb, vb_, lab, st0_ref, of_ref, ob_ref, sf_ref, sb_ref, fin_ref, st):
        i = pl.program_id(1)

        @pl.when(i == 0)
        def _():
            st[...] = st0_ref[...]

        for d, (q_ref, k_ref, v_ref, la_ref, o_ref, save_ref) in enumerate(((qf, kf, vf, laf, of_ref, sf_ref), (qb, kb, vb_, lab, ob_ref, sb_ref))):
            tri, btot, _, _, _, qt, kt, kh = _chunk_terms(d, q_ref[...], k_ref[...], la_ref[...], C)
            s = st[d]
            vb = v_ref[...].astype(BF16)
            qtb = qt.astype(BF16)
            att = jnp.where(tri, _dg(qtb, kt.astype(BF16), 1, 1), 0.0)
            o_ref[...] = _dg(qtb, s.astype(BF16), 1, 1) + _dg(att.astype(BF16), vb, 1, 0)
            save_ref[...] = s
            s_new = s * jnp.exp(btot) + _dg(vb, kh.astype(BF16), 0, 0)
            st[d] = s_new

            @pl.when(i == nc - 1)
            def _(d=d, s_new=s_new):
                fin_ref[d] = s_new

    def seq(width, col0, rev, dir_cols=0):
        if rev:
            return pl.BlockSpec((C, width), lambda h, i: (nc - 1 - i, col0 + dir_cols + h))
        return pl.BlockSpec((C, width), lambda h, i: (i, col0 + h))

    both = pl.BlockSpec((2, None, dv, dk), lambda h, i: (0, h, 0, 0))
    outs, side_outs = _side_call(
        body, (H, nc),
        [seq(dk, 0, False), seq(dk, 0, False), seq(dv, v_col0, False), seq(dk, 0, False),
         seq(dk, 0, True), seq(dk, 0, True), seq(dv, v_col0, True), seq(dk, 0, True, H), both],
        [seq(dv, 0, False), seq(dv, 0, True),
         pl.BlockSpec((None, None, dv, dk), lambda h, i: (h, i, 0, 0)),
         pl.BlockSpec((None, None, dv, dk), lambda h, i: (h, nc - 1 - i, 0, 0)), both],
        [jax.ShapeDtypeStruct((n, H * dv), F32)] * 2 + [jax.ShapeDtypeStruct((H, nc, dv, dk), F32)] * 2
        + [jax.ShapeDtypeStruct((2, H, dv, dk), F32)],
        [pltpu.VMEM((2, dv, dk), F32)], (q, k, zv, la, q, k, zv, la, st0), list(side), name, ("arbitrary", "arbitrary"),
        mid=side_mid)
    return outs + side_outs


def _gla_bwd(q, k, zv, v_col0, la, saved_f, saved_b, do, dfin, heads, dk, dv, name, side=()):
    n, C, H = q.shape[0], GLA_CHUNK, heads
    nc = n // C

    def body(qf, kf, vf, laf, sf, dof, qb, kb, vb_, lab, sb, dob_, dfin_ref,
             dqf, dqb, dkf, dkb, dvf, dvb, dlaf, dlab, d0_ref, dst):
        i = pl.program_id(1)

        @pl.when(i == 0)
        def _():
            dst[...] = dfin_ref[...]

        dirs = ((qf, kf, vf, laf, sf, dof, dqf, dkf, dvf, dlaf), (qb, kb, vb_, lab, sb, dob_, dqb, dkb, dvb, dlab))
        for d, hh in [(d, hh) for hh in range(HP) for d in range(2)]:
            q_ref, k_ref, v_ref, la_ref, save_ref, do_ref, dq_ref, dk_ref, dv_ref, dla_ref = dirs[d]
            kc, vc = slice(hh * dk, (hh + 1) * dk), slice(hh * dv, (hh + 1) * dv)
            tri, btot, eb, enb, ebt, qt, kt, kh = _chunk_terms(d, q_ref[:, kc], k_ref[:, kc], la_ref[:, kc], C)
            s, dsn = save_ref[hh], dst[d, hh]
            vb, dob = v_ref[:, vc].astype(BF16), do_ref[:, vc].astype(BF16)
            qtb, ktb, khb, dsnb = qt.astype(BF16), kt.astype(BF16), kh.astype(BF16), dsn.astype(BF16)
            att = jnp.where(tri, _dg(qtb, ktb, 1, 1), 0.0).astype(BF16)
            datt = jnp.where(tri, _dg(dob, vb, 1, 1), 0.0).astype(BF16)
            dqt = _dg(dob, s.astype(BF16), 1, 0) + _dg(datt, ktb, 1, 0)
            dkt = _dg(datt, qtb, 0, 0)
            dkh = _dg(vb, dsnb, 1, 0)
            dv_ref[:, vc] = _dg(att, dob, 0, 0) + _dg(khb, dsnb, 1, 1)
            ebtot = jnp.exp(btot)
            dbtot = ebtot * _colsum(s * dsn) + _colsum(dkh * kh)
            s0 = dsn * ebtot + _dg(dob, qtb, 0, 0)
            dst[d, hh] = s0
            db = dqt * qt - dkt * kt - dkh * kh
            dq_ref[:, kc] = dqt * eb
            dk_ref[:, kc] = dkt * enb + dkh * ebt
            end_row = lax.broadcasted_iota(jnp.int32, db.shape, 0) == (C - 1 if d == 0 else 0)
            dla_ref[:, kc] = db + jnp.where(end_row, dbtot, 0.0)

            @pl.when(i == nc - 1)
            def _(d=d, hh=hh, s0=s0):
                d0_ref[d, hh] = s0

    HP = 2 if H % 2 == 0 else 1

    def seq(width, col0, fwd_dir, dir_cols=0):
        first = (col0 + dir_cols) // HP
        if fwd_dir:
            return pl.BlockSpec((C, HP * width), lambda h, i: (nc - 1 - i, first + h))
        return pl.BlockSpec((C, HP * width), lambda h, i: (i, first + h))

    both = pl.BlockSpec((2, HP, dv, dk), lambda h, i: (0, h, 0, 0))
    sav_f = pl.BlockSpec((HP, None, dv, dk), lambda h, i: (h, nc - 1 - i, 0, 0))
    sav_b = pl.BlockSpec((HP, None, dv, dk), lambda h, i: (h, i, 0, 0))
    outs, side_outs = _side_call(
        body, (H // HP, nc),
        [seq(dk, 0, True), seq(dk, 0, True), seq(dv, v_col0, True), seq(dk, 0, True), sav_f, seq(dv, 0, True),
         seq(dk, 0, False), seq(dk, 0, False), seq(dv, v_col0, False), seq(dk, 0, False, H), sav_b, seq(dv, 0, False), both],
        [seq(dk, 0, True), seq(dk, 0, False), seq(dk, 0, True), seq(dk, 0, False), seq(dv, 0, True), seq(dv, 0, False),
         seq(dk, 0, True), seq(dk, 0, False), both],
        [jax.ShapeDtypeStruct((n, H * dk), F32)] * 4 + [jax.ShapeDtypeStruct((n, H * dv), F32)] * 2
        + [jax.ShapeDtypeStruct((n, H * dk), F32)] * 2 + [jax.ShapeDtypeStruct((2, H, dv, dk), F32)],
        [pltpu.VMEM((2, HP, dv, dk), F32)], (q, k, zv, la, saved_f, do, q, k, zv, la, saved_b, do, dfin), list(side), name,
        ("arbitrary", "arbitrary"))
    return outs + side_outs


def _adamw_math(w, g, m, v):
    m2 = ADAM_B1 * m + (1.0 - ADAM_B1) * g
    v2 = ADAM_B2 * v + (1.0 - ADAM_B2) * jnp.square(g)
    m_hat = m2 / (1.0 - ADAM_B1 ** ADAM_STEP)
    v_hat = v2 / (1.0 - ADAM_B2 ** ADAM_STEP)
    delta = -ADAM_LR * (m_hat / (jnp.sqrt(v_hat) + ADAM_EPS) + ADAM_WD * w)
    return delta, m2, v2


def _adamw(w, g, m, v, name):
    r, cols = w.shape
    tr = _row_tile(r, cols, 4 * 4)
    spec = pl.BlockSpec((tr, cols), lambda i: (i, 0))

    def body(w_ref, g_ref, m_ref, v_ref, d_ref, m2_ref, v2_ref):
        d_ref[...], m2_ref[...], v2_ref[...] = _adamw_math(w_ref[...], g_ref[...], m_ref[...], v_ref[...])

    return _side_call(body, (r // tr,), [spec] * 4, [spec] * 3, [jax.ShapeDtypeStruct((r, cols), F32)] * 3,
                      [], (w, g, m, v), [], name, ("parallel",))[0]


def _adamw_many(ws, gs, ms, vs, name):
    n = len(ws)

    def body(*refs):
        ins, outs = refs[:4 * n], refs[4 * n:]
        for k in range(n):
            d, m2, v2 = _adamw_math(ins[k][...], ins[n + k][...], ins[2 * n + k][...], ins[3 * n + k][...])
            outs[k][...], outs[n + k][...], outs[2 * n + k][...] = d, m2, v2

    outs = pl.pallas_call(
        body, name=name, out_shape=[jax.ShapeDtypeStruct(w.shape, F32) for w in ws] * 3,
        in_specs=[pl.BlockSpec(memory_space=pltpu.VMEM)] * (4 * n), out_specs=[pl.BlockSpec(memory_space=pltpu.VMEM)] * (3 * n),
        compiler_params=_params(),
    )(*ws, *gs, *ms, *vs)
    return outs[:n], outs[n:2 * n], outs[2 * n:]


def _ada_update(cond_t, dmod, w, m, v, name):
    r, cols = w.shape
    tr, tc = _pick(r, 512, 8), _pick(cols, 1024)
    spec = pl.BlockSpec((tr, tc), lambda i, j: (i, j))

    def body(c_ref, d_ref, w_ref, m_ref, v_ref, g_ref, dl_ref, m2_ref, v2_ref):
        g = _dg(c_ref[...].astype(BF16), d_ref[...].astype(BF16), 1, 0)
        g_ref[...] = g
        dl_ref[...], m2_ref[...], v2_ref[...] = _adamw_math(w_ref[...], g, m_ref[...], v_ref[...])

    return _side_call(
        body, (r // tr, cols // tc),
        [pl.BlockSpec((tr, cond_t.shape[1]), lambda i, j: (i, 0)), pl.BlockSpec((dmod.shape[0], tc), lambda i, j: (0, j)), spec, spec, spec],
        [spec] * 4, [jax.ShapeDtypeStruct((r, cols), F32)] * 4, [], (cond_t, dmod, w, m, v), [], name, ("parallel", "parallel"))[0]


def _pack(parts, rows=8):
    flat = jnp.concatenate([p.reshape(-1).astype(F32) for p in parts])
    n = -(-flat.shape[0] // (rows * LANES)) * LANES
    return jnp.pad(flat, (0, rows * n - flat.shape[0])).reshape(rows, n)


def _unpack(flat, shapes):
    out, off = [], 0
    for s in shapes:
        size = math.prod(s)
        out.append(flat[off:off + size].reshape(s))
        off += size
    return out


def kernel(x, c, ctx, c_ctx, w_ada, b_ada, pre1_g, post1_g, pre2_g, post2_g, w_in, w_dec_f, b_dec_f, w_dec_b, b_dec_b, gla_norm_g, sg_ln_g, sg_ln_b, w_s, b_s, w_o, w_1, w_2, loss_target, m_c_ctx, m_w_ada, m_b_ada, m_pre1_g, m_post1_g, m_pre2_g, m_post2_g, m_w_in, m_w_dec_f, m_b_dec_f, m_w_dec_b, m_b_dec_b, m_gla_norm_g, m_sg_ln_g, m_sg_ln_b, m_w_s, m_b_s, m_w_o, m_w_1, m_w_2, v_c_ctx, v_w_ada, v_b_ada, v_pre1_g, v_post1_g, v_pre2_g, v_post2_g, v_w_in, v_w_dec_f, v_b_dec_f, v_w_dec_b, v_b_dec_b, v_gla_norm_g, v_sg_ln_g, v_sg_ln_b, v_w_s, v_b_s, v_w_o, v_w_1, v_w_2):
    N, D = x.shape[1], x.shape[2]
    NC = ctx.shape[1]
    H = GLA_HEADS
    VALW = D // 2
    DV = VALW // H
    DK = DV // 2
    KEYW = H * DK
    SGW = D - VALW
    GW = SGW // SG_GROUPS
    LR = 2 * GLA_LOWRANK
    F = w_1.shape[2] * N_CHIP
    FS = F // N_CHIP
    RH = D // 2
    MS = w_ada.shape[2]
    IN_COLS = w_in.shape[2] * N_CHIP
    K0, V0, R0, LF0 = KEYW, 2 * KEYW, 2 * KEYW + VALW, 2 * KEYW + 2 * VALW
    SG0 = LF0 + LR
    AQ, AK, AV, AR, ALR = 2 * SGW, 2 * SGW + KEYW, 2 * SGW + 2 * KEYW, 2 * SGW + 2 * KEYW + VALW, 2 * SGW + 2 * KEYW + 2 * VALW
    ACOLS = ALR + 4 * LANES
    LR_PAD = ACOLS - ALR
    assert IN_COLS == SG0 + 2 * SGW and N % SG_CHUNK == 0 and N % GLA_CHUNK == 0 and NC % GLA_CHUNK == 0

    ax, ay, ac = _place()
    s_me = (2 * ax + ay).astype(jnp.int32)
    b_me = (4 * ax + 2 * ay + ac).astype(jnp.int32)
    s_arr, c_arr = s_me.reshape(1), ac.astype(jnp.int32).reshape(1)
    sc_arr = jnp.concatenate([s_arr, c_arr])
    CS = IN_COLS // N_CHIP

    shards = [_cast_blocks(w_in[0].reshape(2, RH, CS), s_arr, "cast_w_in"), _cast_blocks(w_o[0].reshape(2, D // N_DEV, D), s_arr, "cast_w_o"),
              _cast_blocks(w_1[0].reshape(2, RH, FS), s_arr, "cast_w_1"), _cast_blocks(w_2[0].reshape(2, F // N_DEV, D), s_arr, "cast_w_2")]
    win_g, = _gather_big(shards[:1], "gather_weights")
    wo_buf, w1_buf, w2_buf = shards[1], shards[2], shards[3]
    w_in_nat = win_g.reshape(N_CHIP, 2, RH, IN_COLS // N_CHIP).transpose(1, 2, 0, 3).reshape(D, IN_COLS)
    w_al = jnp.concatenate([w_in_nat[:, SG0:], w_in_nat[:, :LF0], w_in_nat[:, LF0:SG0],
                            jnp.zeros((D, LR_PAD - LR), BF16)], axis=1)

    n_dec = GLA_LOWRANK * (KEYW // N_CHIP)
    g0 = _allgather_small(_pack([c, w_dec_f, w_dec_b, gla_norm_g]), "gather_small0").reshape(N_DEV, -1)
    c_all = g0[:, :D]
    per_chip = g0[0::2]
    wdf = per_chip[:, D:D + n_dec].reshape(N_CHIP, GLA_LOWRANK, KEYW // N_CHIP).transpose(1, 0, 2).reshape(GLA_LOWRANK, KEYW)
    wdb = per_chip[:, D + n_dec:D + 2 * n_dec].reshape(N_CHIP, GLA_LOWRANK, KEYW // N_CHIP).transpose(1, 0, 2).reshape(GLA_LOWRANK, KEYW)
    gn_full = per_chip[:, D + 2 * n_dec:D + 2 * n_dec + H * (DV // N_CHIP)].reshape(N_CHIP, H, DV // N_CHIP).transpose(1, 0, 2).reshape(1, VALW)
    wd_f = jnp.zeros((LANES, KEYW), F32).at[:GLA_LOWRANK].set(wdf)
    wd_b = jnp.zeros((LANES, KEYW), F32).at[GLA_LOWRANK:LR].set(wdb)

    cond_in = jnp.zeros((16, D), F32).at[:N_DEV].set(c_all).at[N_DEV].set(c_ctx)
    b_ada_sh = lax.dynamic_slice(b_ada, (0, s_me * MS), (1, MS))

    def mod_epi(r, bias):
        return (r + bias,)

    def silu_rows(t):
        return (t * jax.nn.sigmoid(t),)

    cond = _rows(silu_rows, 16, 16, [_W(cond_in)], [(D, F32)], [], "cond_silu")[0]
    mod_sh = _mm(cond, w_ada[0], 16, MS, D, name="mod_matmul", tn=512, tk=D, epi=mod_epi, epi_in=(b_ada_sh,),
                 epi_specs=(pl.BlockSpec((1, min(512, MS)), lambda i, j, k: (0, j)),))
    g1m = _allgather_small(mod_sh, "gather_mod").reshape(N_DEV, 16, MS)[0::2]
    mod_all = g1m.transpose(1, 0, 2).reshape(16, N_CHIP * MS)
    mod_me = lax.dynamic_slice(mod_all, (b_me, 0), (1, N_MOD * D))
    sh1, sc1, gt1, sh2, sc2, gt2 = [mod_me[:, i * D:(i + 1) * D] for i in range(N_MOD)]
    csh1, csc1 = mod_all[N_DEV:N_DEV + 1, :D], mod_all[N_DEV:N_DEV + 1, D:2 * D]

    mq = DK // 4
    pos = np.arange(N)
    inv_freq = (np.float32(ROPE_BASE) ** (-np.arange(mq, dtype=np.float32) / np.float32(mq))).astype(np.float32)
    ang_r = (pos // GRID_W).astype(np.float32)[:, None] * inv_freq[None, :]
    ang_c = (pos % GRID_W).astype(np.float32)[:, None] * inv_freq[None, :]
    cos_t = jnp.asarray(np.concatenate([np.cos(ang_r), np.cos(ang_r), np.cos(ang_c), np.cos(ang_c)], axis=1), F32)
    sin_t = jnp.asarray(np.concatenate([-np.sin(ang_r), np.sin(ang_r), -np.sin(ang_c), np.sin(ang_c)], axis=1), F32)

    x2, tgt, ctx2 = x[0], loss_target[0], ctx[0]
    TR = 128
    qscale = DK ** -0.5

    def prenorm_fwd(xa, g, sc, sh, n_rows, name, stacked):
        return _rows(lambda xv, gv, scv, shv: (_prenorm(xv, gv, scv, shv),), n_rows, TR,
                     [_T(xa, D), _W(g), _W(sc), _W(sh)], [(D, BF16)], [], name, stacked=stacked)[0]

    h_cat = prenorm_fwd(x2, pre1_g, sc1, sh1, N, "prenorm1_x", (N + NC, 0, None))
    h_cat = prenorm_fwd(ctx2, pre1_g, csc1, csh1, NC, "prenorm1_ctx", (N + NC, N, h_cat))
    tka = _pick(ACOLS, 1536, 2 * LANES)
    tmc = min(1024, NC)
    z_al, w1_buf = _mm(h_cat, w_al, N, ACOLS, D, name="in_proj_x", tn=tka, side=[_ph(w1_buf, [("gather_ici", 0, 2)])])
    zc_al = _mm(h_cat, w_al, NC, ACOLS, D, name="in_proj_ctx", tn=tka,
                a_spec=pl.BlockSpec((tmc, min(2048, D)), lambda i, j, k: (N // tmc + i, k)))

    def decay(lr, wdf_v, wdb_v, bf_v, bb_v):
        lrb = lr.astype(BF16)
        a_f = _dg(lrb, wdf_v.astype(BF16), 1, 0) + bf_v
        a_b = _dg(lrb, wdb_v.astype(BF16), 1, 0) + bb_v
        return a_f, a_b

    def running_decay(a_f, a_b):
        return jnp.concatenate([_chunk_cumsum(_log_sigmoid(a_f) / GLA_TAU, True), _chunk_cumsum(_log_sigmoid(a_b) / GLA_TAU, False)], axis=1)

    def prep_x(zq, zk, lr, cs, sn, wdf_v, wdb_v, bf_v, bb_v):
        return _rope(zq * qscale, cs, sn, H, DK), _rope(zk, cs, sn, H, DK), running_decay(*decay(lr, wdf_v, wdb_v, bf_v, bb_v))

    def prep_c(zk, lr, wdf_v, wdb_v, bf_v, bb_v):
        return zk, running_decay(*decay(lr, wdf_v, wdb_v, bf_v, bb_v))

    dec_w = [_W(wd_f), _W(wd_b), _W(b_dec_f), _W(b_dec_b)]
    q_r, k_r, la_x = _rows(prep_x, N, TR, [_T(z_al, KEYW, AQ // KEYW), _T(z_al, KEYW, AK // KEYW), _T(z_al, LANES, ALR // LANES),
                                           _T(cos_t, DK), _T(sin_t, DK)] + dec_w,
                           [(KEYW, F32), (KEYW, F32), (2 * KEYW, F32)], [], "gla_prep_x")
    k_c, la_c = _rows(prep_c, NC, TR, [_T(zc_al, KEYW, AK // KEYW), _T(zc_al, LANES, ALR // LANES)] + dec_w,
                      [(KEYW, F32), (2 * KEYW, F32)], [], "gla_prep_ctx")

    zero_state = jnp.zeros((2, H, DV, DK), F32)
    q_c = jnp.zeros((NC, KEYW), F32)
    _, _, savf_c, savb_c, st_c = _gla_fwd(q_c, k_c, zc_al, AV // DV, la_c, zero_state, H, DK, DV, "gla_fwd_ctx")
    o_f, o_b, savf_x, savb_x, _, wo_g, w1_buf = _gla_fwd(
        q_r, k_r, z_al, AV // DV, la_x, st_c, H, DK, DV, "gla_fwd_x",
        side=[_ph(wo_buf, [("gather_chain", 0, 4)]), _ph(w1_buf, [("gather_d2d", 0, 2), ("gather_ici", 2, 4)])], side_mid=0.55)
    w_o_f = wo_g.reshape(D, D)

    def readout_fwd(of, ob, r, g):
        return (_readout(of + ob, r, g, H, DV),)

    y_gla = _rows(readout_fwd, N, TR, [_T(o_f, VALW), _T(o_b, VALW), _T(z_al, VALW, AR // VALW), _W(gn_full)],
                  [(VALW, BF16)], [], "gla_readout")[0]

    bs_col = b_s[0].reshape(SG_GROUPS, SG_CHUNK, 1)

    def sg_fwd(zu, zv, lng, lnb, ws, bs):
        u, vv = _sg_pre(zu, zv, lng, lnb)
        return (u * _sg_mix(vv, ws, bs, GW),)

    y_sg = _rows(sg_fwd, N, SG_CHUNK, [_T(z_al, SGW, 0), _T(z_al, SGW, 1), _W(sg_ln_g), _W(sg_ln_b), _W(w_s[0]), _W(bs_col)],
                 [(SGW, BF16)], [], "sg_fwd")[0]
    ycat = jnp.concatenate([y_gla, y_sg], axis=1)
    y, w1_g, w2_buf = _mm(ycat, w_o_f, N, D, D, name="out_proj",
                          side=[_ph(w1_buf, [("gather_d2d", 2, 4)]), _ph(w2_buf, [("gather_ici", 0, 1)])])
    def between_sublayers(xv, yv, gv, pv, g2v, scv, shv):
        x1v = _postnorm(xv, yv, gv, pv)
        return x1v, _prenorm(x1v, g2v, scv, shv)

    x1, h2 = _rows(between_sublayers, N, TR, [_T(x2, D), _T(y, D), _W(gt1), _W(post1_g), _W(pre2_g), _W(sc2), _W(sh2)],
                   [(D, F32), (D, BF16)], [], "postnorm1_prenorm2")

    tm1, tn1, tk1 = min(1024, N), min(1024, FS), min(2048, RH)
    w1_fwd_spec = pl.BlockSpec((None, tk1, tn1), lambda i, j, k: (2 * ((j * tn1) // FS) + (k * tk1) // RH, ((k * tk1) % RH) // tk1, ((j * tn1) % FS) // tn1))

    def relu2_epi(r):
        rf = jnp.maximum(r, 0.0)
        return rf * rf, rf

    act, rf, w2_g = _mm(h2, w1_g, N, F, D, name="mlp_up", out_dtypes=(BF16, BF16), tm=tm1, tn=tn1, tk=tk1, b_spec=w1_fwd_spec, epi=relu2_epi,
                        side=[_ph(w2_buf, [("gather_d2d", 0, 1), ("gather_chain", 1, 4)])], side_mid=0.88)
    w_2_f = w2_g.reshape(F, D)
    y2 = _mm(act, w_2_f, N, D, F, name="mlp_down")

    def final(x1v, y2v, gv, pv, tv):
        def loss_fn(x1a, y2a, ga, pa):
            err = _postnorm(x1a, y2a, ga, pa) - tv
            return 0.5 * jnp.sum(jnp.mean(err * err, axis=-1))
        loss, grads = jax.value_and_grad(loss_fn, argnums=(0, 1, 2, 3))(x1v, y2v, gv, pv)
        return grads[0], grads[1], jnp.full((1, LANES), loss, F32), _colsum(grads[2]), _colsum(grads[3])

    dx2, dy2, loss_acc, dgt2, dpost2 = _rows(final, N, TR, [_T(x1, D), _T(y2, D), _W(gt2), _W(post2_g), _T(tgt, D)],
                                             [(D, F32), (D, BF16)], [(1, LANES), (1, D), (1, D)], "loss_postnorm2_bwd")

    df = _mm(dy2, w_2_f, N, F, D, name="mlp_down_dx", tb=True, out_dtypes=(BF16,), epi=lambda r, rfv: (r * (2.0 * rfv.astype(F32)),),
             epi_in=(rf,), epi_specs=(pl.BlockSpec((min(1024, N), min(1024, F)), lambda i, j, k: (i, j)),))
    dw2 = _mm(act, dy2, F, D, N, name="mlp_down_dw", ta=True, out_dtypes=(BF16,)).reshape(N_DEV, F // N_DEV, D)
    tnb, tkb = min(1024, D, RH), min(2048, FS)
    w1_bwd_spec = pl.BlockSpec((None, tnb, tkb), lambda i, j, k: (2 * ((k * tkb) // FS) + (j * tnb) // RH, ((j * tnb) % RH) // tnb, ((k * tkb) % FS) // tkb))
    dh2, recv1_w2 = _mm(df, w1_g, N, D, F, name="mlp_up_dx", tb=True, tn=tnb, tk=tkb, b_spec=w1_bwd_spec,
                         side=[_ph(lax.empty((N_CHIP,) + dw2.shape[1:], BF16), [("rs_sibling", 0, 4)], src=dw2)])
    part_w2 = _sum_sibling(dw2, recv1_w2, c_arr, "rs_sum_sibling_w_2")
    tmw, tnw = min(1024, RH), min(1024, FS)
    dw1_spec = pl.BlockSpec((None, tmw, tnw), lambda i, j, k: (2 * ((j * tnw) // FS) + (i * tmw) // RH, ((i * tmw) % RH) // tmw, ((j * tnw) % FS) // tnw))
    dw1, recv2_w2 = _mm(h2, df, D, F, N, name="mlp_up_dw", ta=True, tm=tmw, tn=tnw, out_dtypes=(BF16,), out_specs=[dw1_spec],
                        out_shapes=[jax.ShapeDtypeStruct((N_DEV, RH, FS), BF16)],
                        side=[_ph(lax.empty((3,) + part_w2.shape[1:], BF16), [("rs_chips", 0, 3)], src=part_w2)])

    def prenorm_bwd(xv, gv, scv, shv, dh, dres):
        _, vjp = jax.vjp(_prenorm, xv, gv, scv, shv)
        dx, dg, dsc, dsh = vjp(dh)
        return dx + dres, _colsum(dg), _colsum(dsc), _colsum(dsh)

    def between_sublayers_bwd(xv, gv, scv, shv, dh, dres, yv, g1v, pv):
        dx1v, dg, dsc, dsh = prenorm_bwd(xv, gv, scv, shv, dh, dres)
        _, vjp = jax.vjp(lambda ya, ga, pa: _postnorm(0.0, ya, ga, pa), yv, g1v, pv)
        dy_, dg1_, dp_ = vjp(dx1v)
        return dx1v, dy_, dg, dsc, dsh, _colsum(dg1_), _colsum(dp_)

    dx1, dy, dpre2, dsc2, dsh2, dgt1, dpost1 = _rows(
        between_sublayers_bwd, N, TR, [_T(x1, D), _W(pre2_g), _W(sc2), _W(sh2), _T(dh2, D), _T(dx2, D), _T(y, D), _W(gt1), _W(post1_g)],
        [(D, F32), (D, BF16)], [(1, D)] * 5, "prenorm2_postnorm1_bwd")
    dycat, recv2_w2, recv1_w1 = _mm(dy, w_o_f, N, D, D, name="out_proj_dx", tb=True,
                                    side=[_ph(recv2_w2, [("rs_chips", 3, 4)], src=part_w2),
                                          _ph(lax.empty((N_CHIP,) + dw1.shape[1:], BF16), [("rs_sibling", 0, 4)], src=dw1)])
    part_w1 = _sum_sibling(dw1, recv1_w1, c_arr, "rs_sum_sibling_w_1")
    dwo, recv2_w1 = _mm(ycat, dy, D, D, N, name="out_proj_dw", ta=True, out_dtypes=(BF16,),
                        side=[_ph(lax.empty((3,) + part_w1.shape[1:], BF16), [("rs_chips", 0, 1)], src=part_w1)])
    dwo = dwo.reshape(N_DEV, D // N_DEV, D)

    def readout_bwd(of, ob, r, g, dyv):
        _, vjp = jax.vjp(lambda o_, r_, g_: _readout(o_, r_, g_, H, DV), of + ob, r, g)
        do_, dr_, dg_ = vjp(dyv)
        return do_, dr_, _colsum(dg_)

    do_x, dz_r, dgn = _rows(readout_bwd, N, TR, [_T(o_f, VALW), _T(o_b, VALW), _T(z_al, VALW, AR // VALW), _W(gn_full), _T(dycat, VALW, 0)],
                            [(VALW, F32), (VALW, BF16)], [(1, VALW)], "gla_readout_bwd")

    def sg_bwd(zu, zv, lng, lnb, ws, bs, dyv):
        (u, vv), vjp = jax.vjp(_sg_pre, zu, zv, lng, lnb)
        s = _sg_mix(vv, ws, bs, GW)
        du, ds = dyv * s, dyv * u
        dws, dbs, dvv = [], [], []
        for g in range(SG_GROUPS):
            dsg = ds[:, g * GW:(g + 1) * GW]
            dsb = dsg.astype(BF16)
            dws.append(_dg(dsb, vv[:, g * GW:(g + 1) * GW].astype(BF16), 1, 1))
            dbs.append(jnp.sum(dsg, axis=1, keepdims=True))
            dvv.append(_dg(ws[g].astype(BF16), dsb, 0, 0))
        dzu, dzv, dlng, dlnb = vjp((du, jnp.concatenate(dvv, axis=1)))
        return jnp.concatenate([dzu, dzv], axis=1), _colsum(dlng), _colsum(dlnb), jnp.concatenate(dws, axis=0), jnp.concatenate(dbs, axis=0)

    dz_sg, dlng, dlnb, dws, dbs = _rows(sg_bwd, N, SG_CHUNK, [_T(z_al, SGW, 0), _T(z_al, SGW, 1), _W(sg_ln_g), _W(sg_ln_b), _W(w_s[0]), _W(bs_col), _T(dycat, SGW, VALW // SGW)],
                                        [(2 * SGW, BF16)], [(1, SGW), (1, SGW), (SG_GROUPS * SG_CHUNK, SG_CHUNK), (SG_GROUPS * SG_CHUNK, 1)], "sg_bwd")

    dq_f, dq_b, dk_f, dk_b, dv_f, dv_b, dla_f, dla_b, dst0, recv2_w1, recv1_wo = _gla_bwd(
        q_r, k_r, z_al, AV // DV, la_x, savf_x, savb_x, do_x, zero_state, H, DK, DV, "gla_bwd_x",
        side=[_ph(recv2_w1, [("rs_chips", 1, 3)], src=part_w1),
              _ph(lax.empty((N_CHIP,) + dwo.shape[1:], BF16), [("rs_sibling", 0, 4)], src=dwo)])
    part_wo = _sum_sibling(dwo, recv1_wo, c_arr, "rs_sum_sibling_w_o")
    _, _, dkc_f, dkc_b, dvc_f, dvc_b, dlac_f, dlac_b, _ = _gla_bwd(
        q_c, k_c, zc_al, AV // DV, la_c, savf_c, savb_c, jnp.zeros((NC, VALW), F32), dst0, H, DK, DV, "gla_bwd_ctx")

    def decay_bwd(lr, dla_f_v, dla_b_v, wdf_v, wdb_v, bf_v, bb_v):
        a_f, a_b = decay(lr, wdf_v, wdb_v, bf_v, bb_v)
        da_f = _chunk_cumsum(dla_f_v, False) * jax.nn.sigmoid(-a_f) / GLA_TAU
        da_b = _chunk_cumsum(dla_b_v, True) * jax.nn.sigmoid(-a_b) / GLA_TAU
        lrb, dfb, dbb = lr.astype(BF16), da_f.astype(BF16), da_b.astype(BF16)
        dlr = _dg(dfb, wdf_v.astype(BF16), 1, 1) + _dg(dbb, wdb_v.astype(BF16), 1, 1)
        return dlr, _dg(lrb, dfb, 0, 0), _dg(lrb, dbb, 0, 0), _colsum(da_f), _colsum(da_b)

    def prep_x_bwd(dq0, dq1, dk0, dk1, dv0, dv1, lr, dla0, dla1, cs, sn, dzsg, dzr, wdf_v, wdb_v, bf_v, bb_v):
        dlr, dwf, dwb, dbf, dbb = decay_bwd(lr, dla0, dla1, wdf_v, wdb_v, bf_v, bb_v)
        row = [dzsg, (_rope_t(dq0 + dq1, cs, sn, H, DK) * qscale).astype(BF16), _rope_t(dk0 + dk1, cs, sn, H, DK).astype(BF16),
               (dv0 + dv1).astype(BF16), dzr, dlr.astype(BF16), jnp.zeros((lr.shape[0], LR_PAD - LANES), BF16)]
        return jnp.concatenate(row, axis=1), dwf, dwb, dbf, dbb

    def prep_c_bwd(dk0, dk1, dv0, dv1, lr, dla0, dla1, wdf_v, wdb_v, bf_v, bb_v):
        dlr, dwf, dwb, dbf, dbb = decay_bwd(lr, dla0, dla1, wdf_v, wdb_v, bf_v, bb_v)
        n = lr.shape[0]
        row = [jnp.zeros((n, 2 * SGW + KEYW), BF16), (dk0 + dk1).astype(BF16), (dv0 + dv1).astype(BF16), jnp.zeros((n, VALW), BF16),
               dlr.astype(BF16), jnp.zeros((n, LR_PAD - LANES), BF16)]
        return jnp.concatenate(row, axis=1), dwf, dwb, dbf, dbb

    dec_acc = [(LANES, KEYW), (LANES, KEYW), (1, KEYW), (1, KEYW)]
    dz_cat, dwdf_x, dwdb_x, dbdf_x, dbdb_x = _rows(
        prep_x_bwd, N, TR, [_T(dq_f, KEYW), _T(dq_b, KEYW), _T(dk_f, KEYW), _T(dk_b, KEYW), _T(dv_f, VALW), _T(dv_b, VALW),
                            _T(z_al, LANES, ALR // LANES), _T(dla_f, KEYW), _T(dla_b, KEYW), _T(cos_t, DK), _T(sin_t, DK),
                            _T(dz_sg, 2 * SGW), _T(dz_r, VALW)] + dec_w,
        [(ACOLS, BF16)], dec_acc, "gla_prep_x_bwd", stacked=(N + NC, 0, None))
    dz_cat, dwdf_c, dwdb_c, dbdf_c, dbdb_c = _rows(
        prep_c_bwd, NC, TR, [_T(dkc_f, KEYW), _T(dkc_b, KEYW), _T(dvc_f, VALW), _T(dvc_b, VALW),
                             _T(zc_al, LANES, ALR // LANES), _T(dlac_f, KEYW), _T(dlac_b, KEYW)] + dec_w,
        [(ACOLS, BF16)], dec_acc, "gla_prep_ctx_bwd", stacked=(N + NC, N, dz_cat))
    tkd = _pick(ACOLS, 3584, 2 * LANES)
    tkt = _pick(N + NC, 2304)
    dw_al, recv2_w1, recv2_wo = _mm(h_cat, dz_cat, D, ACOLS, N + NC, name="in_proj_dw", ta=True, tn=tka, tk=tkt, out_dtypes=(BF16,),
                                    side=[_ph(recv2_w1, [("rs_chips", 3, 4)], src=part_w1),
                                          _ph(lax.empty((3,) + part_wo.shape[1:], BF16), [("rs_chips", 0, 4)], src=part_wo)])
    g_in = _shard_columns(dw_al, CS, SG0, 2 * SGW, -SG0, RH, "w_in_grad_blocks")
    recv1_in = _rs_sibling([g_in], "rs_sibling_w_in")[0]
    part_in = _sum_sibling(g_in, recv1_in, c_arr, "rs_sum_sibling_w_in")
    dhx, recv2_in = _mm(dz_cat, w_al, N, D, ACOLS, name="in_proj_dx", tb=True, tk=tkd,
                        side=[_ph(lax.empty((3,) + part_in.shape[1:], BF16), [("rs_chips", 0, 3.5)], src=part_in)])
    dhc = _mm(dz_cat, w_al, NC, D, ACOLS, name="in_proj_dctx", tb=True, tk=tkd,
              a_spec=pl.BlockSpec((tmc, tkd), lambda i, j, k: (N // tmc + i, k)))

    grad_x, dpre1_x, dsc1, dsh1 = _rows(prenorm_bwd, N, TR, [_T(x2, D), _W(pre1_g), _W(sc1), _W(sh1), _T(dhx, D), _T(dx1, D)],
                                        [(D, F32)], [(1, D)] * 3, "prenorm1_x_bwd")

    def prenorm_bwd_ctx(xv, gv, scv, shv, dh):
        _, vjp = jax.vjp(_prenorm, xv, gv, scv, shv)
        _, dg, dsc, dsh = vjp(dh)
        return _colsum(dg), _colsum(dsc), _colsum(dsh)

    dpre1_c, dcsc1, dcsh1 = _rows(prenorm_bwd_ctx, NC, TR, [_T(ctx2, D), _W(pre1_g), _W(csc1), _W(csh1), _T(dhc, D)],
                                  [], [(1, D)] * 3, "prenorm1_ctx_bwd")

    half = [_sum_chips(p, r, sc_arr, "rs_sum_chips_" + nm)
            for p, r, nm in zip((part_wo, part_w1, part_w2), (recv2_wo, recv2_w1, recv2_w2), ("w_o", "w_1", "w_2"))]
    g_w_o, g_w_1, g_w_2 = [g.reshape(w.shape[1:]) for g, w in zip(_rs_final(half, "rs_final"), (w_o, w_1, w_2))]

    dmod_x = jnp.concatenate([dsh1, dsc1, dgt1, dsh2, dsc2, dgt2], axis=1)
    dmodc = jnp.concatenate([dcsh1, dcsc1], axis=1)
    small_parts = [loss_acc, dmod_x, dmodc, dpre1_x + dpre1_c, dpost1, dpre2, dpost2, dwdf_x + dwdf_c, dbdf_x + dbdf_c,
                   dwdb_x + dwdb_c, dbdb_x + dbdb_c, dgn, dlng, dlnb, dws, dbs]
    small_shapes = [p.shape for p in small_parts]
    packed = _pack(small_parts)
    n_sm = packed.shape[1]
    gathered = _allgather_small(packed, "gather_small_grads")

    def sum_devices(g):
        tot = g[0:8]
        for dev in range(1, N_DEV):
            tot = tot + g[8 * dev:8 * dev + 8]
        return (tot,)

    summed = _rows(sum_devices, N_DEV * 8, N_DEV * 8, [_W(gathered)], [], [(8, n_sm)], "sum_small_grads")[0]
    (loss_s, dmod_sum, dmodc_sum, g_pre1, g_post1, g_pre2, g_post2, g_wdf_pad, g_bdf, g_wdb_pad, g_bdb, g_gn, g_lng, g_lnb,
     g_ws, g_bs) = _unpack(summed.reshape(-1), small_shapes)
    loss = loss_s[0, 0]
    dmod_rows = gathered.reshape(N_DEV, -1)[:, LANES:LANES + N_MOD * D]
    g_b_ada = dmod_sum + jnp.pad(dmodc_sum, ((0, 0), (0, (N_MOD - 2) * D)))
    dmod16 = jnp.zeros((16, N_MOD * D), F32).at[:N_DEV].set(dmod_rows).at[N_DEV, :2 * D].set(dmodc_sum[0])
    dmod16_sh = lax.dynamic_slice(dmod16, (0, s_me * MS), (16, MS))
    g_w_ada, d_w_ada, nm_w_ada, nv_w_ada = _ada_update(cond.T, dmod16_sh, w_ada[0], m_w_ada[0], v_w_ada[0], "w_ada_update")

    dcond, recv2_in = _mm(dmod16_sh, w_ada[0], 16, D, MS, name="cond_bwd", tb=True, tk=min(512, MS),
                          side=[_ph(recv2_in, [("rs_chips", 3.5, 4)], src=part_in)])
    half_in = _sum_chips(part_in, recv2_in, sc_arr, "rs_sum_chips_w_in")
    g_w_in_t = _transpose_cols(_rs_final([half_in], "rs_final_w_in")[0].reshape(D, -1), CS, "w_in_grad_t")
    part_c = _allgather_small(dcond[N_DEV].reshape(8, D // 8), "gather_dcond").reshape(N_DEV, D)

    def cctx_grad(p, cv):
        sg = jax.nn.sigmoid(cv)
        tot = ((p[0:1] + p[2:3]) + p[4:5]) + p[6:7]
        return (jnp.broadcast_to(tot * (sg * (1.0 + cv * (1.0 - sg))), p.shape),)

    g_c_ctx = _rows(cctx_grad, N_DEV, N_DEV, [_W(part_c), _W(c_ctx.reshape(1, D))], [(D, F32)], [], "c_ctx_grad")[0][0:1]

    def col_shard(g_full, width):
        return lax.dynamic_slice_in_dim(g_full, s_me * width, width, axis=g_full.ndim - 1)

    g_w_dec_f = col_shard(g_wdf_pad[:GLA_LOWRANK], KEYW // N_CHIP)
    g_w_dec_b = col_shard(g_wdb_pad[GLA_LOWRANK:LR], KEYW // N_CHIP)
    g_gla_norm = col_shard(g_gn.reshape(H, DV), DV // N_CHIP)
    small_w = [c_ctx, b_ada, pre1_g, post1_g, pre2_g, post2_g, w_dec_f, b_dec_f, w_dec_b, b_dec_b, gla_norm_g, sg_ln_g, sg_ln_b, w_s, b_s]
    small_m = [m_c_ctx, m_b_ada, m_pre1_g, m_post1_g, m_pre2_g, m_post2_g, m_w_dec_f, m_b_dec_f, m_w_dec_b, m_b_dec_b, m_gla_norm_g, m_sg_ln_g, m_sg_ln_b, m_w_s, m_b_s]
    small_v = [v_c_ctx, v_b_ada, v_pre1_g, v_post1_g, v_pre2_g, v_post2_g, v_w_dec_f, v_b_dec_f, v_w_dec_b, v_b_dec_b, v_gla_norm_g, v_sg_ln_g, v_sg_ln_b, v_w_s, v_b_s]
    small_g = [g_c_ctx, g_b_ada, g_pre1, g_post1, g_pre2, g_post2, g_w_dec_f, g_bdf, g_w_dec_b, g_bdb, g_gla_norm, g_lng, g_lnb, g_ws, g_bs]
    small_g = [g.reshape(w.shape) for g, w in zip(small_g, small_w)]
    d_small, m_small, v_small = _adamw_many(small_w, small_g, small_m, small_v, "adamw_small")

    def big(w, g, m, v, name):
        shp = w.shape
        res = _adamw(w.reshape(shp[-2:]), g.reshape(shp[-2:]), m.reshape(shp[-2:]), v.reshape(shp[-2:]), name)
        return [g.reshape(shp)] + [r.reshape(shp) for r in res]

    r_in = [jnp.swapaxes(t, 1, 2) for t in big(jnp.swapaxes(w_in, 1, 2), g_w_in_t[None], jnp.swapaxes(m_w_in, 1, 2),
                                               jnp.swapaxes(v_w_in, 1, 2), "adamw_w_in")]
    r_o = big(w_o, g_w_o, m_w_o, v_w_o, "adamw_w_o")
    r_1 = big(w_1, g_w_1, m_w_1, v_w_1, "adamw_w_1")
    r_2 = big(w_2, g_w_2, m_w_2, v_w_2, "adamw_w_2")
    r_ada = [t.reshape(w_ada.shape) for t in (g_w_ada, d_w_ada, nm_w_ada, nv_w_ada)]

    def ordered(k):
        sm = [small_g, d_small, m_small, v_small][k]
        return [sm[0], r_ada[k], *sm[1:6], r_in[k], *sm[6:15], r_o[k], r_1[k], r_2[k]]

    return (loss, grad_x.reshape(x.shape), *ordered(0), *ordered(1), *ordered(2), *ordered(3))
```
